```python
import jax, jax.numpy as jnp
from jax import lax
import numpy as np

D_MODEL = 4096
BATCH = 8
SEQ = 2048
DEPTH = 1
DEC_BATCH = 8
DEC_SEQ = 64
PAST_LEN = 1024

CHUNK = 64
Q_BLOCK = 128
H_FOX = D_MODEL // 256
DH_FOX = 128
W_FOX = H_FOX * DH_FOX
H_ML = D_MODEL // 512
DK_ML = 128
DV_ML = 256
W_ML = H_ML * DV_ML
QK_ML = 2 * H_ML * DK_ML
CONV_W = 4
N_GROUPS = 4
EXP_PER_GROUP = 4
N_EXPERTS = N_GROUPS * EXP_PER_GROUP
TOP_K_IN_GROUP = 2
D_EXPERT = 1024
EPS = 1e-6
IN_SPLITS = (W_FOX, W_FOX, W_FOX, H_FOX, QK_ML, W_ML, W_ML, H_ML, H_ML)
N_IN = sum(IN_SPLITS)

kernel_name = "fox_mlstm_hiermoe_stream_step"


def rms_norm(x, g):
    xf = x.astype(jnp.float32)
    y = xf * lax.rsqrt(jnp.mean(xf * xf, axis=-1, keepdims=True) + EPS)
    return (y * g.astype(jnp.float32)).astype(x.dtype)


def in_projection(xn, w_in, b_fox_f, b_ml_i, b_ml_f):
    B, T, _ = xn.shape
    f32 = jnp.float32
    p = jnp.einsum('btd,dn->btn', xn, w_in)
    offs = np.cumsum(IN_SPLITS)[:-1].tolist()
    fq, fk, fv, ff, qk_raw, mv, mo, mi, mf = jnp.split(p, offs, axis=-1)
    fox = (fq.reshape(B, T, H_FOX, DH_FOX), fk.reshape(B, T, H_FOX, DH_FOX), fv.reshape(B, T, H_FOX, DH_FOX),
           jax.nn.log_sigmoid(ff.astype(f32) + b_fox_f.astype(f32)))
    ml = (qk_raw, mv.reshape(B, T, H_ML, DV_ML),
          jax.nn.sigmoid(mo.astype(f32)).reshape(B, T, H_ML, DV_ML),
          mi.astype(f32) + b_ml_i.astype(f32),
          jax.nn.log_sigmoid(mf.astype(f32) + b_ml_f.astype(f32)))
    return fox, ml


def fox_block(q, q_pos, Fq, k, v, Fk, k_pos):
    s = jnp.einsum('bqhd,bkhd->bhqk', q, k).astype(jnp.float32) * (DH_FOX ** -0.5)
    logits = s + (Fq[..., :, None] - Fk[..., None, :])
    mask = k_pos[None, :] <= q_pos[:, None]
    p = jax.nn.softmax(jnp.where(mask, logits, -jnp.inf), axis=-1)
    return jnp.einsum('bhqk,bkhd->bqhd', p.astype(v.dtype), v)


def fox_prompt(q, k, v, logf):
    B, S, H, dh = q.shape
    nb = S // Q_BLOCK
    F = jnp.cumsum(logf, axis=1).transpose(0, 2, 1)
    qb = q.reshape(B, nb, Q_BLOCK, H, dh).transpose(1, 0, 2, 3, 4)
    Fb = F.reshape(B, H, nb, Q_BLOCK).transpose(2, 0, 1, 3)
    pos = jnp.arange(S, dtype=jnp.int32)
    qpos = pos.reshape(nb, Q_BLOCK)
    out = lax.map(lambda a: fox_block(a[0], a[2], a[1], k, v, F, pos), (qb, Fb, qpos))
    return out.transpose(1, 0, 2, 3, 4).reshape(B, S, H * dh)


def fox_sample(q, k_new, v_new, logf_new, cache_k, cache_v, cache_logf):
    B, T, H, dh = q.shape
    P = cache_k.shape[1]
    k = jnp.concatenate([cache_k, k_new.astype(cache_k.dtype)], axis=1)
    v = jnp.concatenate([cache_v, v_new.astype(cache_v.dtype)], axis=1)
    logf = jnp.concatenate([cache_logf.astype(jnp.float32), logf_new], axis=1)
    F = jnp.cumsum(logf, axis=1).transpose(0, 2, 1)
    k_pos = jnp.arange(P + T, dtype=jnp.int32)
    out = fox_block(q, k_pos[P:], F[..., P:], k, v, F, k_pos)
    return out.reshape(B, T, H * dh)


def short_conv_qk(qk_hist, w_conv):
    T = qk_hist.shape[1] - (CONV_W - 1)
    y = qk_hist[:, 0:T] * w_conv[0]
    for j in range(1, CONV_W):
        y = y + qk_hist[:, j:j + T] * w_conv[j]
    y = jax.nn.silu(y)
    B = y.shape[0]
    q = y[..., :QK_ML // 2].reshape(B, T, H_ML, DK_ML)
    k = y[..., QK_ML // 2:].reshape(B, T, H_ML, DK_ML) * (DK_ML ** -0.5)
    return q, k


def to_chunks(a, L):
    B, T, H = a.shape[:3]
    a = a.reshape((B, T // L, L, H) + a.shape[3:])
    return jnp.moveaxis(a, (1, 3), (0, 2))


def from_chunks(a):
    a = jnp.moveaxis(a, (0, 2), (1, 3))
    B, nc, L, H = a.shape[:4]
    return a.reshape((B, nc * L, H) + a.shape[4:])


def mlstm_chunkwise(q, k, v, i_pre, logf, C0, n0, m0, L):
    f32 = jnp.float32
    xs = tuple(to_chunks(a.astype(f32), L) for a in (q, k, v, i_pre, logf))
    causal = jnp.tril(jnp.ones((L, L), dtype=bool))

    def step(carry, chunk):
        C, n, m = carry
        qc, kc, vc, ic, fc = chunk
        b = jnp.cumsum(fc, axis=-1)
        D = jnp.where(causal, b[..., :, None] - b[..., None, :] + ic[..., None, :], -jnp.inf)
        inter = b + m[..., None]
        m_t = jnp.maximum(inter, jnp.max(D, axis=-1))
        w_inter = jnp.exp(inter - m_t)
        S = jnp.einsum('bhtk,bhsk->bhts', qc, kc) * jnp.exp(D - m_t[..., None])
        num = w_inter[..., None] * jnp.einsum('bhvk,bhtk->bhtv', C, qc) + jnp.einsum('bhts,bhsv->bhtv', S, vc)
        den = w_inter * jnp.einsum('bhk,bhtk->bht', n, qc) + jnp.sum(S, axis=-1)
        h = num / jnp.maximum(jnp.abs(den), jnp.exp(-m_t))[..., None]
        bL = b[..., -1]
        dec = bL[..., None] - b + ic
        m_new = jnp.maximum(bL + m, jnp.max(dec, axis=-1))
        a = jnp.exp(bL + m - m_new)
        w_s = jnp.exp(dec - m_new[..., None])
        C_new = a[..., None, None] * C + jnp.einsum('bhs,bhsv,bhsk->bhvk', w_s, vc, kc)
        n_new = a[..., None] * n + jnp.einsum('bhs,bhsk->bhk', w_s, kc)
        return (C_new, n_new, m_new), h

    (C, n, m), h = lax.scan(step, (C0.astype(f32), n0.astype(f32), m0.astype(f32)), xs)
    return from_chunks(h), C, n, m


def mlstm_output(h, o, g_head):
    B, T = h.shape[:2]
    hn = h * lax.rsqrt(jnp.mean(h * h, axis=-1, keepdims=True) + EPS)
    return (o * hn * g_head.astype(jnp.float32).reshape(H_ML, DV_ML)).reshape(B, T, W_ML)


def token_mixer(xn, hist, w_in, b_fox_f, b_ml_i, b_ml_f, w_conv, g_ml_head, w_out):
    B, T, _ = xn.shape
    (fq, fk, fv, flogf), (qk_raw, mv, mo, mi, mlogf) = in_projection(xn, w_in, b_fox_f, b_ml_i, b_ml_f)
    if hist is None:
        fox_out = fox_prompt(fq, fk, fv, flogf)
        conv_hist = jnp.zeros((B, CONV_W - 1, QK_ML), qk_raw.dtype)
        C0 = jnp.zeros((B, H_ML, DV_ML, DK_ML), jnp.float32)
        n0 = jnp.zeros((B, H_ML, DK_ML), jnp.float32)
        m0 = jnp.zeros((B, H_ML), jnp.float32)
        L = CHUNK
    else:
        cache_k, cache_v, cache_logf, C0, n0, m0, conv_hist = hist
        fox_out = fox_sample(fq, fk, fv, flogf, cache_k, cache_v, cache_logf)
        L = T
    qk_hist = jnp.concatenate([conv_hist.astype(qk_raw.dtype), qk_raw], axis=1)
    q, k = short_conv_qk(qk_hist, w_conv)
    h, C, n, m = mlstm_chunkwise(q, k, mv, mi, mlogf, C0, n0, m0, L)
    ml_out = mlstm_output(h, mo, g_ml_head).astype(xn.dtype)
    heads = jnp.concatenate([fox_out, ml_out], axis=-1)
    mix = jnp.einsum('btc,cd->btd', heads, w_out)
    new_state = (fk, fv, flogf, C, n, m, qk_hist[:, -(CONV_W - 1):])
    return mix, new_state


def hier_moe(xn, w_rg, b_rg, w_re, b_re, w_gate, w_up, w_down):
    B, T, D = xn.shape
    f32 = jnp.float32
    x2 = xn.reshape(B * T, D)
    g_logits = jnp.dot(x2, w_rg).astype(f32) + b_rg.astype(f32)
    _, g_idx = lax.top_k(g_logits, 1)
    g_prob = jnp.take_along_axis(jax.nn.softmax(g_logits, axis=-1), g_idx, axis=-1)
    e_logits = (jnp.dot(x2, w_re).astype(f32) + b_re.astype(f32)).reshape(-1, N_GROUPS, EXP_PER_GROUP)
    e_in = e_logits[jnp.arange(e_logits.shape[0]), g_idx[:, 0]]
    e_val, e_idx = lax.top_k(e_in, TOP_K_IN_GROUP)
    gates = g_prob * jax.nn.softmax(e_val, axis=-1)
    expert_id = g_idx * EXP_PER_GROUP + e_idx
    combine = jnp.sum(jax.nn.one_hot(expert_id, N_EXPERTS, dtype=f32) * gates[..., None], axis=1)
    y = jnp.zeros_like(x2)
    for e in range(N_EXPERTS):
        hdn = jax.nn.silu(x2 @ w_gate[e]) * (x2 @ w_up[e])
        y = y + combine[:, e:e + 1].astype(x2.dtype) * (hdn @ w_down[e])
    return y.reshape(B, T, D)


def stack_layers(per_layer):
    return tuple(jnp.stack([st[i] for st in per_layer], axis=0) for i in range(len(per_layer[0])))


def setup_inputs(seed: int = 0) -> dict:
    key = jax.random.key(seed)
    ks = jax.random.split(key, 26)
    f32 = jnp.float32

    def nrm(k, shape, scale):
        return scale * jax.random.normal(k, shape, f32)

    return {
        "x_prompt": nrm(ks[0], (BATCH, SEQ, D_MODEL), 1.0),
        "x_sample": nrm(ks[1], (DEC_BATCH, DEC_SEQ, D_MODEL), 1.0),
        "cache_fox_k": nrm(ks[2], (DEPTH, DEC_BATCH, PAST_LEN, H_FOX, DH_FOX), 1.0),
        "cache_fox_v": nrm(ks[3], (DEPTH, DEC_BATCH, PAST_LEN, H_FOX, DH_FOX), 1.0),
        "cache_fox_logf": jax.nn.log_sigmoid(4.0 + nrm(ks[4], (DEPTH, DEC_BATCH, PAST_LEN, H_FOX), 1.0)),
        "state_mlstm_C": nrm(ks[5], (DEPTH, DEC_BATCH, H_ML, DV_ML, DK_ML), 0.1),
        "state_mlstm_n": nrm(ks[6], (DEPTH, DEC_BATCH, H_ML, DK_ML), 0.3),
        "state_mlstm_m": nrm(ks[7], (DEPTH, DEC_BATCH, H_ML), 1.0),
        "state_mlstm_conv": nrm(ks[8], (DEPTH, DEC_BATCH, CONV_W - 1, QK_ML), 1.0),
        "g_norm_mix": 1.0 + nrm(ks[9], (DEPTH, D_MODEL), 0.02),
        "w_in": nrm(ks[10], (DEPTH, D_MODEL, N_IN), D_MODEL ** -0.5),
        "b_fox_f": jnp.linspace(2.0, 6.0, H_FOX, dtype=f32)[None, :] + nrm(ks[11], (DEPTH, H_FOX), 0.1),
        "b_mlstm_i": nrm(ks[12], (DEPTH, H_ML), 0.1),
        "b_mlstm_f": jnp.linspace(3.0, 6.0, H_ML, dtype=f32)[None, :] + nrm(ks[13], (DEPTH, H_ML), 0.1),
        "w_conv": nrm(ks[14], (DEPTH, CONV_W, QK_ML), CONV_W ** -0.5),
        "g_mlstm_head": 1.0 + nrm(ks[15], (DEPTH, W_ML), 0.02),
        "w_out": nrm(ks[16], (DEPTH, D_MODEL, D_MODEL), D_MODEL ** -0.5),
        "g_norm_ffn": 1.0 + nrm(ks[17], (DEPTH, D_MODEL), 0.02),
        "w_router_group": nrm(ks[18], (DEPTH, D_MODEL, N_GROUPS), D_MODEL ** -0.5),
        "b_router_group": nrm(ks[19], (DEPTH, N_GROUPS), 0.01),
        "w_router_expert": nrm(ks[20], (DEPTH, D_MODEL, N_EXPERTS), D_MODEL ** -0.5),
        "b_router_expert": nrm(ks[21], (DEPTH, N_EXPERTS), 0.01),
        "w_exp_gate": nrm(ks[22], (DEPTH, N_EXPERTS, D_MODEL, D_EXPERT), D_MODEL ** -0.5),
        "w_exp_up": nrm(ks[23], (DEPTH, N_EXPERTS, D_MODEL, D_EXPERT), D_MODEL ** -0.5),
        "w_exp_down": nrm(ks[24], (DEPTH, N_EXPERTS, D_EXPERT, D_MODEL), D_EXPERT ** -0.5),
        "g_norm_final": 1.0 + nrm(ks[25], (D_MODEL,), 0.02),
    }


def reference(x_prompt, x_sample, cache_fox_k, cache_fox_v, cache_fox_logf, state_mlstm_C, state_mlstm_n,
              state_mlstm_m, state_mlstm_conv, g_norm_mix, w_in, b_fox_f, b_mlstm_i, b_mlstm_f, w_conv,
              g_mlstm_head, w_out, g_norm_ffn, w_router_group, b_router_group, w_router_expert,
              b_router_expert, w_exp_gate, w_exp_up, w_exp_down, g_norm_final):
    xp, xs = x_prompt, x_sample
    st_p, st_s = [], []
    for l in range(DEPTH):
        mixer_w = (w_in[l], b_fox_f[l], b_mlstm_i[l], b_mlstm_f[l], w_conv[l], g_mlstm_head[l], w_out[l])
        moe_w = (w_router_group[l], b_router_group[l], w_router_expert[l], b_router_expert[l],
                 w_exp_gate[l], w_exp_up[l], w_exp_down[l])
        hist = (cache_fox_k[l], cache_fox_v[l], cache_fox_logf[l], state_mlstm_C[l], state_mlstm_n[l],
                state_mlstm_m[l], state_mlstm_conv[l])
        mix_p, new_p = token_mixer(rms_norm(xp, g_norm_mix[l]), None, *mixer_w)
        mix_s, new_s = token_mixer(rms_norm(xs, g_norm_mix[l]), hist, *mixer_w)
        xp = xp + mix_p
        xs = xs + mix_s
        xp = xp + hier_moe(rms_norm(xp, g_norm_ffn[l]), *moe_w)
        xs = xs + hier_moe(rms_norm(xs, g_norm_ffn[l]), *moe_w)
        st_p.append(new_p)
        st_s.append(new_s)
    y_prompt = rms_norm(xp, g_norm_final)
    y_sample = rms_norm(xs, g_norm_final)
    p_fox_k, p_fox_v, p_fox_logf, p_mlstm_C, p_mlstm_n, p_mlstm_m, p_mlstm_conv = stack_layers(st_p)
    s_fox_k, s_fox_v, s_fox_logf, s_mlstm_C, s_mlstm_n, s_mlstm_m, s_mlstm_conv = stack_layers(st_s)
    return (y_prompt, y_sample, p_fox_k, p_fox_v, p_fox_logf, p_mlstm_C, p_mlstm_n, p_mlstm_m, p_mlstm_conv,
            s_fox_k, s_fox_v, s_fox_logf, s_mlstm_C, s_mlstm_n, s_mlstm_m, s_mlstm_conv)
```

```python
import functools

import jax
import jax.numpy as jnp
from jax import lax
from jax.experimental import pallas as pl
from jax.experimental.pallas import tpu as pltpu

F32 = jnp.float32
BF16 = jnp.bfloat16
EPS = 1e-6
LANES = 128
V7X_VMEM_LIMIT = 56 * 1024 * 1024
TOP_K = 2
MLSTM_CHUNK = 256
HIGHEST = lax.Precision.HIGHEST
NEG_INF = float("-inf")


def _params(*sem):
    return pltpu.CompilerParams(dimension_semantics=sem, vmem_limit_bytes=V7X_VMEM_LIMIT)


def _tile(n, pref):
    if n <= pref:
        return n
    for t in range(pref - pref % 8, 7, -8):
        if n % t == 0:
            return t
    return n


def _log_sigmoid(x):
    return jnp.minimum(x, 0.0) - jnp.log1p(jnp.exp(-jnp.abs(x)))


def _rmsnorm_kernel(x_ref, g_ref, o_ref):
    x = x_ref[...]
    y = x * lax.rsqrt(jnp.mean(x * x, axis=-1, keepdims=True) + EPS)
    o_ref[...] = (y * g_ref[...]).astype(o_ref.dtype)


def rmsnorm_cast(x, g, out_dtype):
    m, d = x.shape
    tm = _tile(m, 256)
    return pl.pallas_call(
        _rmsnorm_kernel,
        grid=(m // tm,),
        in_specs=[pl.BlockSpec((tm, d), lambda i: (i, 0)), pl.BlockSpec((1, d), lambda i: (0, 0))],
        out_specs=pl.BlockSpec((tm, d), lambda i: (i, 0)),
        out_shape=jax.ShapeDtypeStruct((m, d), out_dtype),
        compiler_params=_params("parallel"),
        name="rmsnorm_cast",
    )(x, g.reshape(1, d))


def _inproj_kernel(x_ref, w_ref, ws_ref, b_ref, *out_refs, nseg, n_fox, n_ml):
    seg_refs, gate_ref = out_refs[:-1], out_refs[-1]
    j = pl.program_id(1)
    x = x_ref[...]
    acc = jnp.dot(x, w_ref[...], preferred_element_type=F32)
    for s, o_ref in enumerate(seg_refs):
        @pl.when((j >= s * nseg) & (j < (s + 1) * nseg))
        def _(o_ref=o_ref):
            o_ref[...] = acc.astype(o_ref.dtype)

    @pl.when(j == 0)
    def _():
        p = jnp.dot(x, ws_ref[...], preferred_element_type=F32) + b_ref[...]
        lane = lax.broadcasted_iota(jnp.int32, p.shape, 1)
        is_id = (lane >= n_fox) & (lane < n_fox + n_ml)
        gate_ref[...] = jnp.where(is_id, p, _log_sigmoid(p))


def in_projection(xn, w_main, w_gate, b_gate, seg_dtypes, n_fox, n_ml):
    m, d = xn.shape
    nsegs = len(seg_dtypes)
    w = w_main.shape[1] // nsegs
    tm = _tile(m, 512)
    tn = _tile(w, 512)
    nseg = w // tn

    def seg_map(s):
        return lambda i, j: (i, jnp.clip(j - s * nseg, 0, nseg - 1))

    out_specs = [pl.BlockSpec((tm, tn), seg_map(s)) for s in range(nsegs)]
    out_specs.append(pl.BlockSpec((tm, LANES), lambda i, j: (i, 0)))
    out_shape = [jax.ShapeDtypeStruct((m, w), dt) for dt in seg_dtypes]
    out_shape.append(jax.ShapeDtypeStruct((m, LANES), F32))
    return pl.pallas_call(
        functools.partial(_inproj_kernel, nseg=nseg, n_fox=n_fox, n_ml=n_ml),
        grid=(m // tm, nsegs * nseg),
        in_specs=[
            pl.BlockSpec((tm, d), lambda i, j: (i, 0)),
            pl.BlockSpec((d, tn), lambda i, j: (0, j)),
            pl.BlockSpec((d, LANES), lambda i, j: (0, 0)),
            pl.BlockSpec((1, LANES), lambda i, j: (0, 0)),
        ],
        out_specs=out_specs,
        out_shape=out_shape,
        compiler_params=_params("arbitrary", "arbitrary"),
        name="in_projection",
    )(xn, w_main, w_gate, b_gate)


def _cumsum_kernel(g_ref, f_ref, *, chunk):
    s = g_ref.shape[0]
    r = lax.broadcasted_iota(jnp.int32, (chunk, chunk), 0)
    c = lax.broadcasted_iota(jnp.int32, (chunk, chunk), 1)
    tri = (c <= r).astype(F32)
    carry = jnp.zeros((1, g_ref.shape[1]), F32)
    for k in range(s // chunk):
        blk = g_ref[k * chunk:(k + 1) * chunk, :]
        loc = jnp.dot(tri, blk, precision=HIGHEST, preferred_element_type=F32)
        f_ref[k * chunk:(k + 1) * chunk, :] = loc + carry
        carry = carry + loc[chunk - 1:chunk, :]


def cumsum_time(g):
    b, s, n = g.shape
    chunk = next(c for c in (256, 128, 64, 32, 16, 8) if s % c == 0)
    return pl.pallas_call(
        functools.partial(_cumsum_kernel, chunk=chunk),
        grid=(b,),
        in_specs=[pl.BlockSpec((None, s, n), lambda i: (i, 0, 0))],
        out_specs=pl.BlockSpec((None, s, n), lambda i: (i, 0, 0)),
        out_shape=jax.ShapeDtypeStruct((b, s, n), F32),
        compiler_params=_params("parallel"),
        name="cumsum_time",
    )(g)


def _fox_kernel(*refs, past, t, tq, scale):
    if past:
        q_ref, k_ref, v_ref, pk_ref, pv_ref, fcol_ref, frow_ref, o_ref = refs
    else:
        q_ref, k_ref, v_ref, fcol_ref, frow_ref, o_ref = refs
    h = pl.program_id(1)
    fc = fcol_ref[past:past + t, :]
    lane = lax.broadcasted_iota(jnp.int32, fc.shape, 1)
    fq_all = jnp.sum(jnp.where(lane == h, fc, 0.0), axis=1, keepdims=True)
    fk_all = frow_ref[...]
    kb = k_ref[...].astype(BF16)
    vb = v_ref[...].astype(BF16)
    if past:
        pkb = pk_ref[...].astype(BF16)
        pvb = pv_ref[...].astype(BF16)
    dn_t = (((1,), (1,)), ((), ()))
    for qi in range(t // tq):
        q = q_ref[qi * tq:(qi + 1) * tq, :]
        fq = fq_all[qi * tq:(qi + 1) * tq, :]
        n_k = (qi + 1) * tq
        s = lax.dot_general(q, kb[0:n_k, :], dn_t, preferred_element_type=F32)
        logits = s * scale + (fq - fk_all[:, past:past + n_k])
        row = lax.broadcasted_iota(jnp.int32, (tq, n_k), 0) + qi * tq
        col = lax.broadcasted_iota(jnp.int32, (tq, n_k), 1)
        logits = jnp.where(col <= row, logits, NEG_INF)
        mx = jnp.max(logits, axis=1, keepdims=True)
        if past:
            sp = lax.dot_general(q, pkb, dn_t, preferred_element_type=F32)
            lp = sp * scale + (fq - fk_all[:, 0:past])
            mx = jnp.maximum(mx, jnp.max(lp, axis=1, keepdims=True))
        p = jnp.exp(logits - mx)
        den = jnp.sum(p, axis=1, keepdims=True)
        acc = jnp.dot(p.astype(BF16), vb[0:n_k, :], preferred_element_type=F32)
        if past:
            pp = jnp.exp(lp - mx)
            den = den + jnp.sum(pp, axis=1, keepdims=True)
            acc = acc + jnp.dot(pp.astype(BF16), pvb, preferred_element_type=F32)
        o_ref[qi * tq:(qi + 1) * tq, :] = (acc / den).astype(o_ref.dtype)


def fox_attention(q, k, v, f_col, f_row, n_heads, past_k=None, past_v=None):
    b, t, w = q.shape
    dh = w // n_heads
    past = 0 if past_k is None else past_k.shape[1]
    tq = _tile(t, 256)

    def head_blk(rows):
        return pl.BlockSpec((None, rows, dh), lambda i, h: (i, 0, h))

    in_specs = [head_blk(t), head_blk(t), head_blk(t)]
    args = [q, k, v]
    if past:
        in_specs += [head_blk(past), head_blk(past)]
        args += [past_k, past_v]
    in_specs += [
        pl.BlockSpec((None, past + t, LANES), lambda i, h: (i, 0, 0)),
        pl.BlockSpec((None, None, 1, past + t), lambda i, h: (i, h, 0, 0)),
    ]
    args += [f_col, f_row]
    return pl.pallas_call(
        functools.partial(_fox_kernel, past=past, t=t, tq=tq, scale=float(dh) ** -0.5),
        grid=(b, n_heads),
        in_specs=in_specs,
        out_specs=head_blk(t),
        out_shape=jax.ShapeDtypeStruct((b, t, w), BF16),
        compiler_params=_params("parallel", "parallel"),
        name="fox_attention",
    )(*args)


def _mlstm_kernel(qraw_ref, kraw_ref, hq_ref, hk_ref, wq_ref, wk_ref, v_ref, og_ref, g_ref, f_ref,
                  irow_ref, frow_ref, c0_ref, n0_ref, m0_ref, gh_ref,
                  out_ref, c_ref, n_ref, m_ref,
                  histq_ref, histk_ref, qc_ref, kc_ref, *, t, chunk, conv_w, i_off, f_off, k_scale):
    h = pl.program_id(1)
    hpad = hq_ref.shape[0]

    def conv_silu(raw_ref, hist_in_ref, w_ref, hist_ref, scale, dst_ref):
        hist_ref[0:hpad, :] = hist_in_ref[...]
        hist_ref[hpad:hpad + t, :] = raw_ref[...]
        y = None
        for j in range(conv_w):
            start = hpad - (conv_w - 1) + j
            term = hist_ref[start:start + t, :] * w_ref[j:j + 1, :]
            y = term if y is None else y + term
        y = y * jax.nn.sigmoid(y)
        if scale != 1.0:
            y = y * scale
        dst_ref[...] = y.astype(dst_ref.dtype)

    conv_silu(qraw_ref, hq_ref, wq_ref, histq_ref, 1.0, qc_ref)
    conv_silu(kraw_ref, hk_ref, wk_ref, histk_ref, k_scale, kc_ref)

    c_ref[...] = c0_ref[...]
    n_ref[...] = n0_ref[...]
    m_ref[...] = m0_ref[...]
    gh = gh_ref[...]
    rr = lax.broadcasted_iota(jnp.int32, (chunk, chunk), 0)
    cc = lax.broadcasted_iota(jnp.int32, (chunk, chunk), 1)
    causal = cc <= rr
    lane = lax.broadcasted_iota(jnp.int32, (chunk, LANES), 1)

    def body(c, f_prev):
        r0 = pl.multiple_of(c * chunk, chunk)
        rows = pl.ds(r0, chunk)
        i_col = jnp.sum(jnp.where(lane == i_off + h, g_ref[rows, :], 0.0), axis=1, keepdims=True)
        b_col = jnp.sum(jnp.where(lane == f_off + h, f_ref[rows, :], 0.0), axis=1, keepdims=True) - f_prev
        i_row = irow_ref[pl.ds(c, 1), :]
        b_row = frow_ref[pl.ds(c, 1), :] - f_prev
        m_prev = m_ref[:, 0:1]
        q = qc_ref[rows, :]
        k = kc_ref[rows, :]
        v = v_ref[rows, :]
        cmat = c_ref[...]
        nvec = n_ref[...]

        d = jnp.where(causal, b_col - b_row + i_row, NEG_INF)
        inter = b_col + m_prev
        m_t = jnp.maximum(inter, jnp.max(d, axis=1, keepdims=True))
        w_inter = jnp.exp(inter - m_t)
        qk = lax.dot_general(q, k, (((1,), (1,)), ((), ())), preferred_element_type=F32)
        s = qk * jnp.exp(d - m_t)
        qc_state = lax.dot_general(q, cmat.astype(BF16), (((1,), (1,)), ((), ())),
                                   preferred_element_type=F32)
        num = w_inter * qc_state + jnp.dot(s.astype(BF16), v, preferred_element_type=F32)
        qn = jnp.sum(q.astype(F32) * nvec, axis=1, keepdims=True)
        den = w_inter * qn + jnp.sum(s, axis=1, keepdims=True)
        hval = num / jnp.maximum(jnp.abs(den), jnp.exp(-m_t))

        hn = hval * lax.rsqrt(jnp.mean(hval * hval, axis=1, keepdims=True) + EPS)
        out = jax.nn.sigmoid(og_ref[rows, :]) * hn * gh
        out_ref[rows, :] = out.astype(out_ref.dtype)

        b_last = b_col[chunk - 1:chunk, :]
        dec_row = b_last - b_row + i_row
        dec_col = b_last - b_col + i_col
        m_new = jnp.maximum(b_last + m_prev, jnp.max(dec_row, axis=1, keepdims=True))
        a = jnp.exp(b_last + m_prev - m_new)
        w_col = jnp.exp(dec_col - m_new)
        vw = (v.astype(F32) * w_col).astype(BF16)
        c_ref[...] = a * cmat + lax.dot_general(vw, k, (((0,), (0,)), ((), ())),
                                                preferred_element_type=F32)
        n_ref[...] = a * nvec + jnp.sum(w_col * k.astype(F32), axis=0, keepdims=True)
        m_ref[...] = jnp.broadcast_to(m_new, m_ref.shape)
        return f_prev + b_last

    lax.fori_loop(0, t // chunk, body, jnp.zeros((1, 1), F32))


def mlstm_heads(qk_raw, conv_hist, w_conv, v, o_gate, gates, f_cum, gates_t, f_cum_t, c0, n0, m0,
                g_head, i_off, f_off):
    b, t, _ = qk_raw.shape
    _, n_h, dv, dk = c0.shape
    conv_w = w_conv.shape[0]
    chunk = _tile(t, MLSTM_CHUNK)
    nch = t // chunk
    hpad = 8
    hist = jnp.pad(conv_hist, ((0, 0), (hpad - (conv_w - 1), 0), (0, 0)))
    gt = gates_t.reshape(b, LANES, nch, chunk)
    ft = f_cum_t.reshape(b, LANES, nch, chunk)
    n0r = n0.reshape(b, n_h, 1, dk)
    m0r = jnp.broadcast_to(m0[:, :, None, None], (b, n_h, 1, LANES))
    ghr = g_head.reshape(n_h, 1, dv)

    qcol = lambda rows: pl.BlockSpec((None, rows, dk), lambda i, h: (i, 0, h))
    kcol = lambda rows: pl.BlockSpec((None, rows, dk), lambda i, h: (i, 0, n_h + h))
    in_specs = [
        qcol(t), kcol(t), qcol(hpad), kcol(hpad),
        pl.BlockSpec((conv_w, dk), lambda i, h: (0, h)),
        pl.BlockSpec((conv_w, dk), lambda i, h: (0, n_h + h)),
        pl.BlockSpec((None, t, dv), lambda i, h: (i, 0, h)),
        pl.BlockSpec((None, t, dv), lambda i, h: (i, 0, h)),
        pl.BlockSpec((None, t, LANES), lambda i, h: (i, 0, 0)),
        pl.BlockSpec((None, t, LANES), lambda i, h: (i, 0, 0)),
        pl.BlockSpec((None, None, nch, chunk), lambda i, h: (i, i_off + h, 0, 0)),
        pl.BlockSpec((None, None, nch, chunk), lambda i, h: (i, f_off + h, 0, 0)),
        pl.BlockSpec((None, None, dv, dk), lambda i, h: (i, h, 0, 0)),
        pl.BlockSpec((None, None, 1, dk), lambda i, h: (i, h, 0, 0)),
        pl.BlockSpec((None, None, 1, LANES), lambda i, h: (i, h, 0, 0)),
        pl.BlockSpec((None, 1, dv), lambda i, h: (h, 0, 0)),
    ]
    out_specs = [
        pl.BlockSpec((None, t, dv), lambda i, h: (i, 0, h)),
        pl.BlockSpec((None, None, dv, dk), lambda i, h: (i, h, 0, 0)),
        pl.BlockSpec((None, None, 1, dk), lambda i, h: (i, h, 0, 0)),
        pl.BlockSpec((None, None, 1, LANES), lambda i, h: (i, h, 0, 0)),
    ]
    out_shape = [
        jax.ShapeDtypeStruct((b, t, n_h * dv), BF16),
        jax.ShapeDtypeStruct((b, n_h, dv, dk), F32),
        jax.ShapeDtypeStruct((b, n_h, 1, dk), F32),
        jax.ShapeDtypeStruct((b, n_h, 1, LANES), F32),
    ]
    out, c_new, n_new, m_new = pl.pallas_call(
        functools.partial(_mlstm_kernel, t=t, chunk=chunk, conv_w=conv_w, i_off=i_off, f_off=f_off,
                          k_scale=float(dk) ** -0.5),
        grid=(b, n_h),
        in_specs=in_specs,
        out_specs=out_specs,
        out_shape=out_shape,
        scratch_shapes=[
            pltpu.VMEM((t + hpad, dk), F32), pltpu.VMEM((t + hpad, dk), F32),
            pltpu.VMEM((t, dk), BF16), pltpu.VMEM((t, dk), BF16),
        ],
        compiler_params=_params("parallel", "parallel"),
        name="mlstm_heads",
    )(qk_raw, qk_raw, hist, hist, w_conv, w_conv, v, o_gate, gates, f_cum, gt, ft, c0, n0r, m0r, ghr)
    return out, c_new, n_new[:, :, 0, :], m_new[:, :, 0, 0]


def _outproj_kernel(a1_ref, a2_ref, w1_ref, w2_ref, x_ref, o_ref):
    acc = jnp.dot(a1_ref[...], w1_ref[...], preferred_element_type=F32)
    acc = acc + jnp.dot(a2_ref[...], w2_ref[...], preferred_element_type=F32)
    o_ref[...] = x_ref[...] + acc


def out_projection(a1, a2, w_out, x):
    m, d = x.shape
    half = a1.shape[1]
    tm = _tile(m, 1024)
    tn = _tile(d, 512)
    return pl.pallas_call(
        _outproj_kernel,
        grid=(m // tm, d // tn),
        in_specs=[
            pl.BlockSpec((tm, half), lambda i, j: (i, 0)),
            pl.BlockSpec((tm, half), lambda i, j: (i, 0)),
            pl.BlockSpec((half, tn), lambda i, j: (0, j)),
            pl.BlockSpec((half, tn), lambda i, j: (1, j)),
            pl.BlockSpec((tm, tn), lambda i, j: (i, j)),
        ],
        out_specs=pl.BlockSpec((tm, tn), lambda i, j: (i, j)),
        out_shape=jax.ShapeDtypeStruct((m, d), F32),
        compiler_params=_params("parallel", "parallel"),
        name="out_projection",
    )(a1, a2, w_out, w_out, x)


def _router_kernel(x_ref, g_ref, w_ref, b_ref, xn_ref, eid_ref, gate_ref, *, n_groups, per_group):
    x = x_ref[...]
    xn = x * lax.rsqrt(jnp.mean(x * x, axis=-1, keepdims=True) + EPS) * g_ref[...]
    xn_ref[...] = xn
    logits = jnp.dot(xn, w_ref[...], precision=HIGHEST, preferred_element_type=F32) + b_ref[...]
    lane = lax.broadcasted_iota(jnp.int32, logits.shape, 1).astype(F32)
    big = float(LANES)
    is_g = lane < n_groups
    gl = jnp.where(is_g, logits, NEG_INF)
    gmax = jnp.max(gl, axis=1, keepdims=True)
    g_idx = jnp.min(jnp.where(gl == gmax, lane, big), axis=1, keepdims=True)
    g_prob = 1.0 / jnp.sum(jnp.where(is_g, jnp.exp(logits - gmax), 0.0), axis=1, keepdims=True)
    lo = n_groups + per_group * g_idx
    el = jnp.where((lane >= lo) & (lane < lo + per_group), logits, NEG_INF)
    e1 = jnp.max(el, axis=1, keepdims=True)
    i1 = jnp.min(jnp.where(el == e1, lane, big), axis=1, keepdims=True)
    el2 = jnp.where(lane == i1, NEG_INF, el)
    e2 = jnp.max(el2, axis=1, keepdims=True)
    i2 = jnp.min(jnp.where(el2 == e2, lane, big), axis=1, keepdims=True)
    r = jnp.exp(e2 - e1)
    w1 = g_prob / (1.0 + r)
    w2 = g_prob * r / (1.0 + r)
    eid = jnp.where(lane == 0.0, i1 - n_groups, jnp.where(lane == 1.0, i2 - n_groups, 0.0))
    eid_ref[...] = eid.astype(jnp.int32)
    gate_ref[...] = jnp.where(lane == 0.0, w1, jnp.where(lane == 1.0, w2, 0.0))


def router(x, g, w_router, b_router, n_groups, per_group):
    m, d = x.shape
    tm = _tile(m, 256)
    return pl.pallas_call(
        functools.partial(_router_kernel, n_groups=n_groups, per_group=per_group),
        grid=(m // tm,),
        in_specs=[
            pl.BlockSpec((tm, d), lambda i: (i, 0)),
            pl.BlockSpec((1, d), lambda i: (0, 0)),
            pl.BlockSpec((d, LANES), lambda i: (0, 0)),
            pl.BlockSpec((1, LANES), lambda i: (0, 0)),
        ],
        out_specs=[
            pl.BlockSpec((tm, d), lambda i: (i, 0)),
            pl.BlockSpec((tm, LANES), lambda i: (i, 0)),
            pl.BlockSpec((tm, LANES), lambda i: (i, 0)),
        ],
        out_shape=[
            jax.ShapeDtypeStruct((m, d), F32),
            jax.ShapeDtypeStruct((m, LANES), jnp.int32),
            jax.ShapeDtypeStruct((m, LANES), F32),
        ],
        compiler_params=_params("parallel"),
        name="router",
    )(x, g.reshape(1, d), w_router, b_router)


def _gather_kernel(idx_ref, src_ref, o_ref, buf_ref, sem_ref, *, tm):
    step = pl.program_id(0)
    nsteps = pl.num_programs(0)

    def issue(tile, slot):
        def body(r, carry):
            tok = idx_ref[tile * tm + r]
            pltpu.make_async_copy(src_ref.at[pl.ds(tok, 1), :], buf_ref.at[slot, pl.ds(r, 1), :],
                                  sem_ref.at[slot]).start()
            return carry
        lax.fori_loop(0, tm, body, 0)

    @pl.when(step == 0)
    def _():
        issue(0, 0)

    @pl.when(step + 1 < nsteps)
    def _():
        issue(step + 1, (step + 1) % 2)

    slot = step % 2
    pltpu.make_async_copy(src_ref.at[pl.ds(0, tm), :], buf_ref.at[slot], sem_ref.at[slot]).wait()
    o_ref[...] = buf_ref[slot].astype(o_ref.dtype)


def gather_rows(src, idx, tm, out_dtype):
    n, d = src.shape
    a = idx.shape[0]
    return pl.pallas_call(
        functools.partial(_gather_kernel, tm=tm),
        grid_spec=pltpu.PrefetchScalarGridSpec(
            num_scalar_prefetch=1,
            grid=(a // tm,),
            in_specs=[pl.BlockSpec(memory_space=pl.ANY)],
            out_specs=pl.BlockSpec((tm, d), lambda i, idx_ref: (i, 0)),
            scratch_shapes=[pltpu.VMEM((2, tm, d), src.dtype), pltpu.SemaphoreType.DMA((2,))],
        ),
        out_shape=jax.ShapeDtypeStruct((a, d), out_dtype),
        compiler_params=_params("arbitrary"),
        name="gather_rows",
    )(idx, src)


def _expert_up_kernel(te_ref, nv_ref, x_ref, wg_ref, wu_ref, h_ref):
    t = pl.program_id(1)

    @pl.when(t < nv_ref[0])
    def _():
        x = x_ref[...]
        g = jnp.dot(x, wg_ref[...], preferred_element_type=F32)
        u = jnp.dot(x, wu_ref[...], preferred_element_type=F32)
        h_ref[...] = (g * jax.nn.sigmoid(g) * u).astype(h_ref.dtype)

    @pl.when(t >= nv_ref[0])
    def _():
        h_ref[...] = jnp.zeros_like(h_ref)


def expert_up(xg, w_gate, w_up, tile_expert, n_valid, tm):
    a, d = xg.shape
    _, _, f = w_gate.shape
    tn = _tile(f, 512)
    return pl.pallas_call(
        _expert_up_kernel,
        grid_spec=pltpu.PrefetchScalarGridSpec(
            num_scalar_prefetch=2,
            grid=(f // tn, a // tm),
            in_specs=[
                pl.BlockSpec((tm, d), lambda c, t, te, nv: (t, 0)),
                pl.BlockSpec((None, d, tn), lambda c, t, te, nv: (te[t], 0, c)),
                pl.BlockSpec((None, d, tn), lambda c, t, te, nv: (te[t], 0, c)),
            ],
            out_specs=pl.BlockSpec((tm, tn), lambda c, t, te, nv: (t, c)),
        ),
        out_shape=jax.ShapeDtypeStruct((a, f), BF16),
        compiler_params=_params("arbitrary", "arbitrary"),
        name="expert_up",
    )(tile_expert, n_valid, xg, w_gate, w_up)


def _expert_down_kernel(te_ref, nv_ref, h_ref, wd_ref, y_ref):
    t = pl.program_id(1)

    @pl.when(t < nv_ref[0])
    def _():
        y_ref[...] = jnp.dot(h_ref[...], wd_ref[...], preferred_element_type=F32)

    @pl.when(t >= nv_ref[0])
    def _():
        y_ref[...] = jnp.zeros_like(y_ref)


def expert_down(hg, w_down, tile_expert, n_valid, tm):
    a, f = hg.shape
    _, _, d = w_down.shape
    tn = _tile(d, 2048)
    return pl.pallas_call(
        _expert_down_kernel,
        grid_spec=pltpu.PrefetchScalarGridSpec(
            num_scalar_prefetch=2,
            grid=(d // tn, a // tm),
            in_specs=[
                pl.BlockSpec((tm, f), lambda c, t, te, nv: (t, 0)),
                pl.BlockSpec((None, f, tn), lambda c, t, te, nv: (te[t], 0, c)),
            ],
            out_specs=pl.BlockSpec((tm, tn), lambda c, t, te, nv: (t, c)),
        ),
        out_shape=jax.ShapeDtypeStruct((a, d), F32),
        compiler_params=_params("arbitrary", "arbitrary"),
        name="expert_down",
    )(tile_expert, n_valid, hg, w_down)


def _combine_kernel(pos_ref, x_ref, gate_ref, g_ref, y_ref, o_ref, buf_ref, sem_ref, *, tm):
    step = pl.program_id(0)
    nsteps = pl.num_programs(0)

    def issue(tile, slot):
        def body(r, carry):
            for k in range(TOP_K):
                p = pos_ref[(tile * tm + r) * TOP_K + k]
                pltpu.make_async_copy(y_ref.at[pl.ds(p, 1), :], buf_ref.at[slot, k, pl.ds(r, 1), :],
                                      sem_ref.at[slot]).start()
            return carry
        lax.fori_loop(0, tm, body, 0)

    @pl.when(step == 0)
    def _():
        issue(0, 0)

    @pl.when(step + 1 < nsteps)
    def _():
        issue(step + 1, (step + 1) % 2)

    slot = step % 2
    for k in range(TOP_K):
        pltpu.make_async_copy(y_ref.at[pl.ds(0, tm), :], buf_ref.at[slot, k], sem_ref.at[slot]).wait()
    gates = gate_ref[...]
    x = x_ref[...] + gates[:, 0:1] * buf_ref[slot, 0] + gates[:, 1:2] * buf_ref[slot, 1]
    y = x * lax.rsqrt(jnp.mean(x * x, axis=-1, keepdims=True) + EPS)
    o_ref[...] = y * g_ref[...]


def combine_norm(x, gates, pos, yg, g_final, tm):
    m, d = x.shape
    return pl.pallas_call(
        functools.partial(_combine_kernel, tm=tm),
        grid_spec=pltpu.PrefetchScalarGridSpec(
            num_scalar_prefetch=1,
            grid=(m // tm,),
            in_specs=[
                pl.BlockSpec((tm, d), lambda i, p: (i, 0)),
                pl.BlockSpec((tm, LANES), lambda i, p: (i, 0)),
                pl.BlockSpec((1, d), lambda i, p: (0, 0)),
                pl.BlockSpec(memory_space=pl.ANY),
            ],
            out_specs=pl.BlockSpec((tm, d), lambda i, p: (i, 0)),
            scratch_shapes=[pltpu.VMEM((2, TOP_K, tm, d), F32), pltpu.SemaphoreType.DMA((2,))],
        ),
        out_shape=jax.ShapeDtypeStruct((m, d), F32),
        compiler_params=_params("arbitrary"),
        name="combine_norm",
    )(pos, x, gates, g_final.reshape(1, d), yg)


def _dispatch_plan(eid, n_experts, tm):
    n = eid.shape[0]
    a = n * TOP_K
    e_flat = eid.reshape(a)
    onehot = (e_flat[:, None] == jnp.arange(n_experts, dtype=jnp.int32)[None, :]).astype(jnp.int32)
    csum = jnp.cumsum(onehot, axis=0)
    counts = csum[-1]
    rank = jnp.take_along_axis(csum, e_flat[:, None], axis=1)[:, 0] - 1
    padded = ((counts + tm - 1) // tm) * tm
    pad_end = jnp.cumsum(padded)
    pos = (pad_end - padded)[e_flat] + rank
    n_tiles = a // tm + n_experts
    src = jnp.zeros((n_tiles * tm,), jnp.int32).at[pos].set(jnp.arange(a, dtype=jnp.int32) // TOP_K)
    tile_end = pad_end // tm
    tile_ids = jnp.arange(n_tiles, dtype=jnp.int32)
    tile_expert = jnp.sum((tile_ids[:, None] >= tile_end[None, :]).astype(jnp.int32), axis=1)
    tile_expert = jnp.minimum(tile_expert, n_experts - 1)
    n_valid = tile_end[-1:].astype(jnp.int32)
    return pos.astype(jnp.int32), src, tile_expert, n_valid


def moe_layer(x, g_ffn, w_router, b_router, w_gate, w_up, w_down, g_final, n_groups, per_group):
    m, d = x.shape
    n_experts = n_groups * per_group
    xn, eid, gates = router(x, g_ffn, w_router, b_router, n_groups, per_group)
    tm = _tile(m * TOP_K // n_experts, 512)
    tm = max(tm, 8)
    pos, src, tile_expert, n_valid = _dispatch_plan(eid[:, :TOP_K], n_experts, tm)
    xg = gather_rows(xn, src, _tile(tm, 256), BF16)
    hg = expert_up(xg, w_gate, w_up, tile_expert, n_valid, tm)
    yg = expert_down(hg, w_down, tile_expert, n_valid, tm)
    return combine_norm(x, gates, pos, yg, g_final, _tile(m, 128))


def _group_step(x, hist, wts, dims):
    b, t, d = x.shape
    n_fox, dh, n_ml, dv, dk = dims
    m = b * t
    half = d // 2
    x2 = x.reshape(m, d)
    xn = rmsnorm_cast(x2, wts["g_mix"], BF16)
    fq, fk, fv, qk_raw, mv, mo, gates = in_projection(
        xn, wts["w_main"], wts["w_gatecols"], wts["b_gatecols"], (BF16, F32, F32, F32, BF16, F32), n_fox, n_ml)
    gates3 = gates.reshape(b, t, LANES)
    i_off, f_off = n_fox, n_fox + n_ml
    if hist is None:
        logf_all = gates3
        past_k = past_v = None
        conv_hist = jnp.zeros((b, wts["w_conv"].shape[0] - 1, half), F32)
        c0 = jnp.zeros((b, n_ml, dv, dk), F32)
        n0 = jnp.zeros((b, n_ml, dk), F32)
        m0 = jnp.zeros((b, n_ml), F32)
        past = 0
    else:
        cache_k, cache_v, cache_logf, c0, n0, m0, conv_hist = hist
        past = cache_k.shape[1]
        past_k = cache_k.reshape(b, past, half)
        past_v = cache_v.reshape(b, past, half)
        cache_pad = jnp.pad(cache_logf.astype(F32), ((0, 0), (0, 0), (0, LANES - n_fox)))
        logf_all = jnp.concatenate([cache_pad, gates3], axis=1)
    f_all = cumsum_time(logf_all)
    f_row = jnp.swapaxes(f_all[:, :, :n_fox], 1, 2)[:, :, None, :]
    fox_out = fox_attention(fq.reshape(b, t, half), fk.reshape(b, t, half), fv.reshape(b, t, half),
                            f_all, f_row, n_fox, past_k, past_v)
    f_new = f_all if hist is None else cumsum_time(gates3)
    ml_out, c_new, n_new, m_new = mlstm_heads(
        qk_raw.reshape(b, t, half), conv_hist, wts["w_conv"], mv.reshape(b, t, half), mo.reshape(b, t, half),
        gates3, f_new, jnp.swapaxes(gates3, 1, 2), jnp.swapaxes(f_new, 1, 2), c0, n0, m0,
        wts["g_head"], i_off, f_off)
    x1 = out_projection(fox_out.reshape(m, half), ml_out.reshape(m, half), wts["w_out"], x2)
    y = moe_layer(x1, wts["g_ffn"], wts["w_router"], wts["b_router"], wts["w_gate"], wts["w_up"],
                  wts["w_down"], wts["g_final"], wts["n_groups"], wts["per_group"])
    conv_w = wts["w_conv"].shape[0]
    qk_hist = jnp.concatenate([conv_hist.astype(F32), qk_raw.reshape(b, t, half)[:, t - min(t, conv_w - 1):]],
                              axis=1)[:, -(conv_w - 1):]
    state = (fk.reshape(b, t, n_fox, dh), fv.reshape(b, t, n_fox, dh), gates3[:, :, :n_fox],
             c_new, n_new, m_new, qk_hist)
    return y.reshape(b, t, d), state


def kernel(x_prompt, x_sample, cache_fox_k, cache_fox_v, cache_fox_logf, state_mlstm_C, state_mlstm_n, state_mlstm_m, state_mlstm_conv, g_norm_mix, w_in, b_fox_f, b_mlstm_i, b_mlstm_f, w_conv, g_mlstm_head, w_out, g_norm_ffn, w_router_group, b_router_group, w_router_expert, b_router_expert, w_exp_gate, w_exp_up, w_exp_down, g_norm_final):
    depth = w_in.shape[0]
    assert depth == 1, "the final norm is fused into the last layer's MoE combine; one layer supported"
    d = x_prompt.shape[-1]
    n_fox, dh = cache_fox_k.shape[-2:]
    n_ml, dv, dk = state_mlstm_C.shape[-3:]
    n_groups = w_router_group.shape[-1]
    n_experts = w_router_expert.shape[-1]
    half = d // 2
    assert n_fox * dh == half and n_ml * dv == half and 2 * n_ml * dk == half
    assert n_fox + 2 * n_ml <= LANES and n_groups + n_experts <= LANES
    l = 0
    sizes = (half, half, half, n_fox, half, half, half, n_ml, n_ml)
    offs = [0]
    for s in sizes:
        offs.append(offs[-1] + s)
    col = lambda i: w_in[l][:, offs[i]:offs[i + 1]]
    w_main = jnp.concatenate([col(0), col(1), col(2), col(4), col(5), col(6)], axis=1).astype(BF16)
    n_gate = n_fox + 2 * n_ml
    w_gatecols = jnp.pad(jnp.concatenate([col(3), col(7), col(8)], axis=1), ((0, 0), (0, LANES - n_gate))).astype(BF16)
    b_gatecols = jnp.pad(jnp.concatenate([b_fox_f[l], b_mlstm_i[l], b_mlstm_f[l]]).astype(F32),
                         (0, LANES - n_gate)).reshape(1, LANES)
    n_r = n_groups + n_experts
    wts = {
        "g_mix": g_norm_mix[l], "w_main": w_main, "w_gatecols": w_gatecols, "b_gatecols": b_gatecols,
        "w_conv": w_conv[l], "g_head": g_mlstm_head[l], "w_out": w_out[l].astype(BF16),
        "g_ffn": g_norm_ffn[l],
        "w_router": jnp.pad(jnp.concatenate([w_router_group[l], w_router_expert[l]], axis=1),
                            ((0, 0), (0, LANES - n_r))),
        "b_router": jnp.pad(jnp.concatenate([b_router_group[l], b_router_expert[l]]).astype(F32),
                            (0, LANES - n_r)).reshape(1, LANES),
        "w_gate": w_exp_gate[l].astype(BF16), "w_up": w_exp_up[l].astype(BF16),
        "w_down": w_exp_down[l].astype(BF16), "g_final": g_norm_final,
        "n_groups": n_groups, "per_group": n_experts // n_groups,
    }
    dims = (n_fox, dh, n_ml, dv, dk)
    hist = (cache_fox_k[l], cache_fox_v[l], cache_fox_logf[l], state_mlstm_C[l], state_mlstm_n[l],
            state_mlstm_m[l], state_mlstm_conv[l])
    y_p, st_p = _group_step(x_prompt, None, wts, dims)
    y_s, st_s = _group_step(x_sample, hist, wts, dims)
    st_p = tuple(a[None] for a in st_p)
    st_s = tuple(a[None] for a in st_s)
    return (y_p, y_s) + st_p + st_s
```

```python
import functools

import jax
import jax.numpy as jnp
from jax import lax
from jax.experimental import pallas as pl
from jax.experimental.pallas import tpu as pltpu

F32 = jnp.float32
BF16 = jnp.bfloat16
EPS = 1e-6
LANES = 128
V7X_VMEM_LIMIT = 56 * 1024 * 1024
TOP_K = 2
MLSTM_CHUNK = 256
HIGHEST = lax.Precision.HIGHEST
NEG_INF = float("-inf")


def _params(*sem):
    return pltpu.CompilerParams(dimension_semantics=sem, vmem_limit_bytes=V7X_VMEM_LIMIT)


def _tile(n, pref):
    if n <= pref:
        return n
    for t in range(pref - pref % 8, 7, -8):
        if n % t == 0:
            return t
    return n


def _log_sigmoid(x):
    return jnp.minimum(x, 0.0) - jnp.log1p(jnp.exp(-jnp.abs(x)))


def _rmsnorm_kernel(x_ref, g_ref, o_ref):
    x = x_ref[...]
    y = x * lax.rsqrt(jnp.mean(x * x, axis=-1, keepdims=True) + EPS)
    o_ref[...] = (y * g_ref[...]).astype(o_ref.dtype)


def rmsnorm_cast(x, g, out_dtype):
    m, d = x.shape
    tm = _tile(m, 256)
    return pl.pallas_call(
        _rmsnorm_kernel,
        grid=(m // tm,),
        in_specs=[pl.BlockSpec((tm, d), lambda i: (i, 0)), pl.BlockSpec((1, d), lambda i: (0, 0))],
        out_specs=pl.BlockSpec((tm, d), lambda i: (i, 0)),
        out_shape=jax.ShapeDtypeStruct((m, d), out_dtype),
        compiler_params=_params("parallel"),
        name="rmsnorm_cast",
    )(x, g.reshape(1, d))


SEG_FQ, SEG_FK, SEG_FV, SEG_QK, SEG_MV, SEG_MO = range(6)
P32_SEGS = (SEG_FK, SEG_FV, SEG_QK, SEG_MO)
P16_SEGS = (SEG_FQ, SEG_MV)


def _inproj_kernel(x_ref, wa_ref, wb_ref, ws_ref, b_ref, p32_ref, p16_ref, gate_ref, *, nseg, n_fox, n_ml):
    j = pl.program_id(1)
    seg = j // nseg
    x = x_ref[...]

    def emit(w_ref):
        acc = jnp.dot(x, w_ref[...], preferred_element_type=F32)
        is16 = (seg == SEG_FQ) | (seg == SEG_MV)

        @pl.when(is16)
        def _():
            p16_ref[...] = acc.astype(p16_ref.dtype)

        @pl.when(jnp.logical_not(is16))
        def _():
            p32_ref[...] = acc

    @pl.when(seg < 3)
    def _():
        emit(wa_ref)

    @pl.when(seg >= 3)
    def _():
        emit(wb_ref)

    @pl.when(j == 0)
    def _():
        p = jnp.dot(x, ws_ref[...], preferred_element_type=F32) + b_ref[...]
        lane = lax.broadcasted_iota(jnp.int32, p.shape, 1)
        is_id = (lane >= n_fox) & (lane < n_fox + n_ml)
        gate_ref[...] = jnp.where(is_id, p, _log_sigmoid(p))


def _held_block(j, nseg, segs):
    seg = j // nseg
    blk = jnp.int32(0)
    for k, s in enumerate(segs):
        here = k * nseg + (j - s * nseg)
        done = (k + 1) * nseg - 1
        blk = jnp.where(seg == s, here, jnp.where(seg > s, done, blk))
    return blk


def in_projection(xn, w_a, w_b, w_gate, b_gate, n_fox, n_ml):
    m, d = xn.shape
    w = w_a.shape[1] // 3
    tm = _tile(m, 1024)
    tn = _tile(w, 512)
    nseg = w // tn
    return pl.pallas_call(
        functools.partial(_inproj_kernel, nseg=nseg, n_fox=n_fox, n_ml=n_ml),
        grid=(m // tm, 6 * nseg),
        in_specs=[
            pl.BlockSpec((tm, d), lambda i, j: (i, 0)),
            pl.BlockSpec((d, tn), lambda i, j: (0, jnp.minimum(j, 3 * nseg - 1))),
            pl.BlockSpec((d, tn), lambda i, j: (0, jnp.maximum(j - 3 * nseg, 0))),
            pl.BlockSpec((d, LANES), lambda i, j: (0, 0)),
            pl.BlockSpec((1, LANES), lambda i, j: (0, 0)),
        ],
        out_specs=[
            pl.BlockSpec((tm, tn), lambda i, j: (i, _held_block(j, nseg, P32_SEGS))),
            pl.BlockSpec((tm, tn), lambda i, j: (i, _held_block(j, nseg, P16_SEGS))),
            pl.BlockSpec((tm, LANES), lambda i, j: (i, 0)),
        ],
        out_shape=[
            jax.ShapeDtypeStruct((m, len(P32_SEGS) * w), F32),
            jax.ShapeDtypeStruct((m, len(P16_SEGS) * w), BF16),
            jax.ShapeDtypeStruct((m, LANES), F32),
        ],
        compiler_params=_params("arbitrary", "arbitrary"),
        name="in_projection",
    )(xn, w_a, w_b, w_gate, b_gate)


def _cumsum_kernel(g_ref, f_ref, *, chunk):
    s = g_ref.shape[0]
    r = lax.broadcasted_iota(jnp.int32, (chunk, chunk), 0)
    c = lax.broadcasted_iota(jnp.int32, (chunk, chunk), 1)
    tri = (c <= r).astype(F32)
    carry = jnp.zeros((1, g_ref.shape[1]), F32)
    for k in range(s // chunk):
        blk = g_ref[k * chunk:(k + 1) * chunk, :]
        loc = jnp.dot(tri, blk, precision=HIGHEST, preferred_element_type=F32)
        f_ref[k * chunk:(k + 1) * chunk, :] = loc + carry
        carry = carry + loc[chunk - 1:chunk, :]


def cumsum_time(g):
    b, s, n = g.shape
    chunk = next(c for c in (256, 128, 64, 32, 16, 8) if s % c == 0)
    return pl.pallas_call(
        functools.partial(_cumsum_kernel, chunk=chunk),
        grid=(b,),
        in_specs=[pl.BlockSpec((None, s, n), lambda i: (i, 0, 0))],
        out_specs=pl.BlockSpec((None, s, n), lambda i: (i, 0, 0)),
        out_shape=jax.ShapeDtypeStruct((b, s, n), F32),
        compiler_params=_params("parallel"),
        name="cumsum_time",
    )(g)


def _fox_kernel(*refs, past, t, tq, scale):
    if past:
        q_ref, k_ref, v_ref, pk_ref, pv_ref, fcol_ref, frow_ref, o_ref = refs
    else:
        q_ref, k_ref, v_ref, fcol_ref, frow_ref, o_ref = refs
    h = pl.program_id(1)
    fc = fcol_ref[past:past + t, :]
    lane = lax.broadcasted_iota(jnp.int32, fc.shape, 1)
    fq_all = jnp.sum(jnp.where(lane == h, fc, 0.0), axis=1, keepdims=True)
    fk_all = frow_ref[...]
    kb = k_ref[...].astype(BF16)
    vb = v_ref[...].astype(BF16)
    if past:
        pkb = pk_ref[...].astype(BF16)
        pvb = pv_ref[...].astype(BF16)
    dn_t = (((1,), (1,)), ((), ()))
    for qi in range(t // tq):
        q = q_ref[qi * tq:(qi + 1) * tq, :]
        fq = fq_all[qi * tq:(qi + 1) * tq, :]
        n_k = (qi + 1) * tq
        s = lax.dot_general(q, kb[0:n_k, :], dn_t, preferred_element_type=F32)
        logits = s * scale + (fq - fk_all[:, past:past + n_k])
        row = lax.broadcasted_iota(jnp.int32, (tq, n_k), 0) + qi * tq
        col = lax.broadcasted_iota(jnp.int32, (tq, n_k), 1)
        logits = jnp.where(col <= row, logits, NEG_INF)
        mx = jnp.max(logits, axis=1, keepdims=True)
        if past:
            sp = lax.dot_general(q, pkb, dn_t, preferred_element_type=F32)
            lp = sp * scale + (fq - fk_all[:, 0:past])
            mx = jnp.maximum(mx, jnp.max(lp, axis=1, keepdims=True))
        p = jnp.exp(logits - mx)
        den = jnp.sum(p, axis=1, keepdims=True)
        acc = jnp.dot(p.astype(BF16), vb[0:n_k, :], preferred_element_type=F32)
        if past:
            pp = jnp.exp(lp - mx)
            den = den + jnp.sum(pp, axis=1, keepdims=True)
            acc = acc + jnp.dot(pp.astype(BF16), pvb, preferred_element_type=F32)
        o_ref[qi * tq:(qi + 1) * tq, :] = (acc / den).astype(o_ref.dtype)


def fox_attention(q_arr, q_off, kv_arr, k_off, v_off, f_col, f_row, n_heads, dh, past_k=None, past_v=None):
    b, t, _ = q_arr.shape
    past = 0 if past_k is None else past_k.shape[1]
    tq = _tile(t, 256)

    def head_blk(rows, off):
        return pl.BlockSpec((None, rows, dh), lambda i, h: (i, 0, off + h))

    in_specs = [head_blk(t, q_off), head_blk(t, k_off), head_blk(t, v_off)]
    args = [q_arr, kv_arr, kv_arr]
    if past:
        in_specs += [head_blk(past, 0), head_blk(past, 0)]
        args += [past_k, past_v]
    in_specs += [
        pl.BlockSpec((None, past + t, LANES), lambda i, h: (i, 0, 0)),
        pl.BlockSpec((None, None, 1, past + t), lambda i, h: (i, h, 0, 0)),
    ]
    args += [f_col, f_row]
    return pl.pallas_call(
        functools.partial(_fox_kernel, past=past, t=t, tq=tq, scale=float(dh) ** -0.5),
        grid=(b, n_heads),
        in_specs=in_specs,
        out_specs=head_blk(t, 0),
        out_shape=jax.ShapeDtypeStruct((b, t, n_heads * dh), BF16),
        compiler_params=_params("parallel", "parallel"),
        name="fox_attention",
    )(*args)


def _mlstm_kernel(qraw_ref, kraw_ref, hq_ref, hk_ref, wq_ref, wk_ref, v_ref, og_ref, g_ref, f_ref,
                  irow_ref, frow_ref, c0_ref, n0_ref, m0_ref, gh_ref,
                  out_ref, c_ref, n_ref, m_ref,
                  histq_ref, histk_ref, qc_ref, kc_ref, *, t, chunk, conv_w, i_off, f_off, k_scale):
    h = pl.program_id(1)
    hpad = hq_ref.shape[0]

    def conv_silu(raw_ref, hist_in_ref, w_ref, hist_ref, scale, dst_ref):
        hist_ref[0:hpad, :] = hist_in_ref[...]
        hist_ref[hpad:hpad + t, :] = raw_ref[...]
        y = None
        for j in range(conv_w):
            start = hpad - (conv_w - 1) + j
            term = hist_ref[start:start + t, :] * w_ref[j:j + 1, :]
            y = term if y is None else y + term
        y = y * jax.nn.sigmoid(y)
        if scale != 1.0:
            y = y * scale
        dst_ref[...] = y.astype(dst_ref.dtype)

    conv_silu(qraw_ref, hq_ref, wq_ref, histq_ref, 1.0, qc_ref)
    conv_silu(kraw_ref, hk_ref, wk_ref, histk_ref, k_scale, kc_ref)

    c_ref[...] = c0_ref[...]
    n_ref[...] = n0_ref[...]
    m_ref[...] = m0_ref[...]
    gh = gh_ref[...]
    rr = lax.broadcasted_iota(jnp.int32, (chunk, chunk), 0)
    cc = lax.broadcasted_iota(jnp.int32, (chunk, chunk), 1)
    causal = cc <= rr
    lane = lax.broadcasted_iota(jnp.int32, (chunk, LANES), 1)

    def body(c, f_prev):
        r0 = pl.multiple_of(c * chunk, chunk)
        rows = pl.ds(r0, chunk)
        i_col = jnp.sum(jnp.where(lane == i_off + h, g_ref[rows, :], 0.0), axis=1, keepdims=True)
        b_col = jnp.sum(jnp.where(lane == f_off + h, f_ref[rows, :], 0.0), axis=1, keepdims=True) - f_prev
        i_row = irow_ref[pl.ds(c, 1), :]
        b_row = frow_ref[pl.ds(c, 1), :] - f_prev
        m_prev = m_ref[:, 0:1]
        q = qc_ref[rows, :]
        k = kc_ref[rows, :]
        v = v_ref[rows, :]
        cmat = c_ref[...]
        nvec = n_ref[...]

        d = jnp.where(causal, b_col - b_row + i_row, NEG_INF)
        inter = b_col + m_prev
        m_t = jnp.maximum(inter, jnp.max(d, axis=1, keepdims=True))
        w_inter = jnp.exp(inter - m_t)
        qk = lax.dot_general(q, k, (((1,), (1,)), ((), ())), preferred_element_type=F32)
        s = qk * jnp.exp(d - m_t)
        qc_state = lax.dot_general(q, cmat.astype(BF16), (((1,), (1,)), ((), ())),
                                   preferred_element_type=F32)
        num = w_inter * qc_state + jnp.dot(s.astype(BF16), v, preferred_element_type=F32)
        qn = jnp.sum(q.astype(F32) * nvec, axis=1, keepdims=True)
        den = w_inter * qn + jnp.sum(s, axis=1, keepdims=True)
        hval = num / jnp.maximum(jnp.abs(den), jnp.exp(-m_t))

        hn = hval * lax.rsqrt(jnp.mean(hval * hval, axis=1, keepdims=True) + EPS)
        out = jax.nn.sigmoid(og_ref[rows, :]) * hn * gh
        out_ref[rows, :] = out.astype(out_ref.dtype)

        b_last = b_col[chunk - 1:chunk, :]
        dec_row = b_last - b_row + i_row
        dec_col = b_last - b_col + i_col
        m_new = jnp.maximum(b_last + m_prev, jnp.max(dec_row, axis=1, keepdims=True))
        a = jnp.exp(b_last + m_prev - m_new)
        w_col = jnp.exp(dec_col - m_new)
        vw = (v.astype(F32) * w_col).astype(BF16)
        c_ref[...] = a * cmat + lax.dot_general(vw, k, (((0,), (0,)), ((), ())),
                                                preferred_element_type=F32)
        n_ref[...] = a * nvec + jnp.sum(w_col * k.astype(F32), axis=0, keepdims=True)
        m_ref[...] = jnp.broadcast_to(m_new, m_ref.shape)
        return f_prev + b_last

    lax.fori_loop(0, t // chunk, body, jnp.zeros((1, 1), F32))


def mlstm_heads(p32, qk_col, og_col, p16, v_col, conv_hist, w_conv, gates, f_cum, gates_t, f_cum_t,
                c0, n0, m0, g_head, i_off, f_off):
    b, t, _ = p32.shape
    _, n_h, dv, dk = c0.shape
    qb, ogb, vb = qk_col // dk, og_col // dv, v_col // dv
    conv_w = w_conv.shape[0]
    chunk = _tile(t, MLSTM_CHUNK)
    nch = t // chunk
    hpad = 8
    hist = jnp.pad(conv_hist, ((0, 0), (hpad - (conv_w - 1), 0), (0, 0)))
    gt = gates_t.reshape(b, LANES, nch, chunk)
    ft = f_cum_t.reshape(b, LANES, nch, chunk)
    n0r = n0.reshape(b, n_h, 1, dk)
    m0r = jnp.broadcast_to(m0[:, :, None, None], (b, n_h, 1, LANES))
    ghr = g_head.reshape(n_h, 1, dv)

    qcol = lambda rows, off: pl.BlockSpec((None, rows, dk), lambda i, h: (i, 0, off + h))
    in_specs = [
        qcol(t, qb), qcol(t, qb + n_h), qcol(hpad, 0), qcol(hpad, n_h),
        pl.BlockSpec((conv_w, dk), lambda i, h: (0, h)),
        pl.BlockSpec((conv_w, dk), lambda i, h: (0, n_h + h)),
        pl.BlockSpec((None, t, dv), lambda i, h: (i, 0, vb + h)),
        pl.BlockSpec((None, t, dv), lambda i, h: (i, 0, ogb + h)),
        pl.BlockSpec((None, t, LANES), lambda i, h: (i, 0, 0)),
        pl.BlockSpec((None, t, LANES), lambda i, h: (i, 0, 0)),
        pl.BlockSpec((None, None, nch, chunk), lambda i, h: (i, i_off + h, 0, 0)),
        pl.BlockSpec((None, None, nch, chunk), lambda i, h: (i, f_off + h, 0, 0)),
        pl.BlockSpec((None, None, dv, dk), lambda i, h: (i, h, 0, 0)),
        pl.BlockSpec((None, None, 1, dk), lambda i, h: (i, h, 0, 0)),
        pl.BlockSpec((None, None, 1, LANES), lambda i, h: (i, h, 0, 0)),
        pl.BlockSpec((None, 1, dv), lambda i, h: (h, 0, 0)),
    ]
    out_specs = [
        pl.BlockSpec((None, t, dv), lambda i, h: (i, 0, h)),
        pl.BlockSpec((None, None, dv, dk), lambda i, h: (i, h, 0, 0)),
        pl.BlockSpec((None, None, 1, dk), lambda i, h: (i, h, 0, 0)),
        pl.BlockSpec((None, None, 1, LANES), lambda i, h: (i, h, 0, 0)),
    ]
    out_shape = [
        jax.ShapeDtypeStruct((b, t, n_h * dv), BF16),
        jax.ShapeDtypeStruct((b, n_h, dv, dk), F32),
        jax.ShapeDtypeStruct((b, n_h, 1, dk), F32),
        jax.ShapeDtypeStruct((b, n_h, 1, LANES), F32),
    ]
    out, c_new, n_new, m_new = pl.pallas_call(
        functools.partial(_mlstm_kernel, t=t, chunk=chunk, conv_w=conv_w, i_off=i_off, f_off=f_off,
                          k_scale=float(dk) ** -0.5),
        grid=(b, n_h),
        in_specs=in_specs,
        out_specs=out_specs,
        out_shape=out_shape,
        scratch_shapes=[
            pltpu.VMEM((t + hpad, dk), F32), pltpu.VMEM((t + hpad, dk), F32),
            pltpu.VMEM((t, dk), BF16), pltpu.VMEM((t, dk), BF16),
        ],
        compiler_params=_params("parallel", "parallel"),
        name="mlstm_heads",
    )(p32, p32, hist, hist, w_conv, w_conv, p16, p32, gates, f_cum, gt, ft, c0, n0r, m0r, ghr)
    return out, c_new, n_new[:, :, 0, :], m_new[:, :, 0, 0]


def _outproj_kernel(a1_ref, a2_ref, w1_ref, w2_ref, x_ref, o_ref):
    acc = jnp.dot(a1_ref[...], w1_ref[...], preferred_element_type=F32)
    acc = acc + jnp.dot(a2_ref[...], w2_ref[...], preferred_element_type=F32)
    o_ref[...] = x_ref[...] + acc


def out_projection(a1, a2, w_out, x):
    m, d = x.shape
    half = a1.shape[1]
    tm = _tile(m, 1024)
    tn = _tile(d, 512)
    return pl.pallas_call(
        _outproj_kernel,
        grid=(m // tm, d // tn),
        in_specs=[
            pl.BlockSpec((tm, half), lambda i, j: (i, 0)),
            pl.BlockSpec((tm, half), lambda i, j: (i, 0)),
            pl.BlockSpec((half, tn), lambda i, j: (0, j)),
            pl.BlockSpec((half, tn), lambda i, j: (1, j)),
            pl.BlockSpec((tm, tn), lambda i, j: (i, j)),
        ],
        out_specs=pl.BlockSpec((tm, tn), lambda i, j: (i, j)),
        out_shape=jax.ShapeDtypeStruct((m, d), F32),
        compiler_params=_params("parallel", "parallel"),
        name="out_projection",
    )(a1, a2, w_out, w_out, x)


def _pack_bf16_pairs(x):
    half = x.shape[1] // 2
    bits = lax.bitcast_convert_type(x.astype(BF16).astype(F32), jnp.uint32)
    return (bits[:, :half] >> 16) | (bits[:, half:] & jnp.uint32(0xFFFF0000))


def _unpack_bf16_pairs(w):
    lo = lax.bitcast_convert_type(w << 16, F32).astype(BF16)
    hi = lax.bitcast_convert_type(w & jnp.uint32(0xFFFF0000), F32).astype(BF16)
    return lo, hi


def _router_kernel(x_ref, g_ref, w_ref, b_ref, xn_ref, eid_ref, gate_ref, *, n_groups, per_group):
    x = x_ref[...]
    xn = x * lax.rsqrt(jnp.mean(x * x, axis=-1, keepdims=True) + EPS) * g_ref[...]
    xn_ref[...] = _pack_bf16_pairs(xn)
    logits = jnp.dot(xn, w_ref[...], precision=HIGHEST, preferred_element_type=F32) + b_ref[...]
    lane = lax.broadcasted_iota(jnp.int32, logits.shape, 1).astype(F32)
    big = float(LANES)
    is_g = lane < n_groups
    gl = jnp.where(is_g, logits, NEG_INF)
    gmax = jnp.max(gl, axis=1, keepdims=True)
    g_idx = jnp.min(jnp.where(gl == gmax, lane, big), axis=1, keepdims=True)
    g_prob = 1.0 / jnp.sum(jnp.where(is_g, jnp.exp(logits - gmax), 0.0), axis=1, keepdims=True)
    lo = n_groups + per_group * g_idx
    el = jnp.where((lane >= lo) & (lane < lo + per_group), logits, NEG_INF)
    e1 = jnp.max(el, axis=1, keepdims=True)
    i1 = jnp.min(jnp.where(el == e1, lane, big), axis=1, keepdims=True)
    el2 = jnp.where(lane == i1, NEG_INF, el)
    e2 = jnp.max(el2, axis=1, keepdims=True)
    i2 = jnp.min(jnp.where(el2 == e2, lane, big), axis=1, keepdims=True)
    r = jnp.exp(e2 - e1)
    w1 = g_prob / (1.0 + r)
    w2 = g_prob * r / (1.0 + r)
    eid = jnp.where(lane == 0.0, i1 - n_groups, jnp.where(lane == 1.0, i2 - n_groups, 0.0))
    eid_ref[...] = eid.astype(jnp.int32)
    gate_ref[...] = jnp.where(lane == 0.0, w1, jnp.where(lane == 1.0, w2, 0.0))


def router(x, g, w_router, b_router, n_groups, per_group):
    m, d = x.shape
    tm = _tile(m, 256)
    return pl.pallas_call(
        functools.partial(_router_kernel, n_groups=n_groups, per_group=per_group),
        grid=(m // tm,),
        in_specs=[
            pl.BlockSpec((tm, d), lambda i: (i, 0)),
            pl.BlockSpec((1, d), lambda i: (0, 0)),
            pl.BlockSpec((d, LANES), lambda i: (0, 0)),
            pl.BlockSpec((1, LANES), lambda i: (0, 0)),
        ],
        out_specs=[
            pl.BlockSpec((tm, d // 2), lambda i: (i, 0)),
            pl.BlockSpec((tm, LANES), lambda i: (i, 0)),
            pl.BlockSpec((tm, LANES), lambda i: (i, 0)),
        ],
        out_shape=[
            jax.ShapeDtypeStruct((m, d // 2), jnp.uint32),
            jax.ShapeDtypeStruct((m, LANES), jnp.int32),
            jax.ShapeDtypeStruct((m, LANES), F32),
        ],
        compiler_params=_params("parallel"),
        name="router",
    )(x, g.reshape(1, d), w_router, b_router)


def _gather_kernel(idx_ref, src_ref, o_ref, buf_ref, sem_ref, *, tm):
    step = pl.program_id(0)
    nsteps = pl.num_programs(0)

    def issue(tile, slot):
        def body(r, carry):
            tok = idx_ref[tile * tm + r]
            pltpu.make_async_copy(src_ref.at[pl.ds(tok, 1), :], buf_ref.at[slot, pl.ds(r, 1), :],
                                  sem_ref.at[slot]).start()
            return carry
        lax.fori_loop(0, tm, body, 0)

    @pl.when(step == 0)
    def _():
        issue(0, 0)

    @pl.when(step + 1 < nsteps)
    def _():
        issue(step + 1, (step + 1) % 2)

    slot = step % 2
    pltpu.make_async_copy(src_ref.at[pl.ds(0, tm), :], buf_ref.at[slot], sem_ref.at[slot]).wait()
    half = buf_ref.shape[2]
    lo, hi = _unpack_bf16_pairs(buf_ref[slot])
    o_ref[:, :half] = lo
    o_ref[:, half:] = hi


def gather_rows(src, idx, tm):
    n, half = src.shape
    d = 2 * half
    a = idx.shape[0]
    return pl.pallas_call(
        functools.partial(_gather_kernel, tm=tm),
        grid_spec=pltpu.PrefetchScalarGridSpec(
            num_scalar_prefetch=1,
            grid=(a // tm,),
            in_specs=[pl.BlockSpec(memory_space=pl.ANY)],
            out_specs=pl.BlockSpec((tm, d), lambda i, idx_ref: (i, 0)),
            scratch_shapes=[pltpu.VMEM((2, tm, half), src.dtype), pltpu.SemaphoreType.DMA((2,))],
        ),
        out_shape=jax.ShapeDtypeStruct((a, d), BF16),
        compiler_params=_params("arbitrary"),
        name="gather_rows",
    )(idx, src)


def _expert_up_kernel(te_ref, nv_ref, x_ref, wg_ref, wu_ref, h_ref):
    t = pl.program_id(1)

    @pl.when(t < nv_ref[0])
    def _():
        x = x_ref[...]
        g = jnp.dot(x, wg_ref[...].astype(BF16), preferred_element_type=F32)
        u = jnp.dot(x, wu_ref[...].astype(BF16), preferred_element_type=F32)
        h_ref[...] = (g * jax.nn.sigmoid(g) * u).astype(h_ref.dtype)

    @pl.when(t >= nv_ref[0])
    def _():
        h_ref[...] = jnp.zeros_like(h_ref)


def expert_up(xg, w_gate, w_up, tile_expert, n_valid, tm):
    a, d = xg.shape
    _, _, f = w_gate.shape
    tn = _tile(f, 256)
    return pl.pallas_call(
        _expert_up_kernel,
        grid_spec=pltpu.PrefetchScalarGridSpec(
            num_scalar_prefetch=2,
            grid=(f // tn, a // tm),
            in_specs=[
                pl.BlockSpec((tm, d), lambda c, t, te, nv: (t, 0)),
                pl.BlockSpec((None, d, tn), lambda c, t, te, nv: (te[t], 0, c)),
                pl.BlockSpec((None, d, tn), lambda c, t, te, nv: (te[t], 0, c)),
            ],
            out_specs=pl.BlockSpec((tm, tn), lambda c, t, te, nv: (t, c)),
        ),
        out_shape=jax.ShapeDtypeStruct((a, f), BF16),
        compiler_params=_params("arbitrary", "arbitrary"),
        name="expert_up",
    )(tile_expert, n_valid, xg, w_gate, w_up)


def _expert_down_kernel(te_ref, nv_ref, h_ref, wd_ref, y_ref):
    t = pl.program_id(1)

    @pl.when(t < nv_ref[0])
    def _():
        y_ref[...] = jnp.dot(h_ref[...], wd_ref[...].astype(BF16), preferred_element_type=F32)

    @pl.when(t >= nv_ref[0])
    def _():
        y_ref[...] = jnp.zeros_like(y_ref)


def expert_down(hg, w_down, tile_expert, n_valid, tm):
    a, f = hg.shape
    _, _, d = w_down.shape
    tn = _tile(d, 2048)
    return pl.pallas_call(
        _expert_down_kernel,
        grid_spec=pltpu.PrefetchScalarGridSpec(
            num_scalar_prefetch=2,
            grid=(d // tn, a // tm),
            in_specs=[
                pl.BlockSpec((tm, f), lambda c, t, te, nv: (t, 0)),
                pl.BlockSpec((None, f, tn), lambda c, t, te, nv: (te[t], 0, c)),
            ],
            out_specs=pl.BlockSpec((tm, tn), lambda c, t, te, nv: (t, c)),
        ),
        out_shape=jax.ShapeDtypeStruct((a, d), F32),
        compiler_params=_params("arbitrary", "arbitrary"),
        name="expert_down",
    )(tile_expert, n_valid, hg, w_down)


def _combine_kernel(pos_ref, x_ref, gate_ref, g_ref, y_ref, o_ref, buf_ref, sem_ref, *, tm):
    step = pl.program_id(0)
    nsteps = pl.num_programs(0)

    def issue(tile, slot):
        def body(r, carry):
            for k in range(TOP_K):
                p = pos_ref[(tile * tm + r) * TOP_K + k]
                pltpu.make_async_copy(y_ref.at[pl.ds(p, 1), :], buf_ref.at[slot, k, pl.ds(r, 1), :],
                                      sem_ref.at[slot]).start()
            return carry
        lax.fori_loop(0, tm, body, 0)

    @pl.when(step == 0)
    def _():
        issue(0, 0)

    @pl.when(step + 1 < nsteps)
    def _():
        issue(step + 1, (step + 1) % 2)

    slot = step % 2
    for k in range(TOP_K):
        pltpu.make_async_copy(y_ref.at[pl.ds(0, tm), :], buf_ref.at[slot, k], sem_ref.at[slot]).wait()
    gates = gate_ref[...]
    x = x_ref[...] + gates[:, 0:1] * buf_ref[slot, 0] + gates[:, 1:2] * buf_ref[slot, 1]
    y = x * lax.rsqrt(jnp.mean(x * x, axis=-1, keepdims=True) + EPS)
    o_ref[...] = y * g_ref[...]


def combine_norm(x, gates, pos, yg, g_final, tm):
    m, d = x.shape
    return pl.pallas_call(
        functools.partial(_combine_kernel, tm=tm),
        grid_spec=pltpu.PrefetchScalarGridSpec(
            num_scalar_prefetch=1,
            grid=(m // tm,),
            in_specs=[
                pl.BlockSpec((tm, d), lambda i, p: (i, 0)),
                pl.BlockSpec((tm, LANES), lambda i, p: (i, 0)),
                pl.BlockSpec((1, d), lambda i, p: (0, 0)),
                pl.BlockSpec(memory_space=pl.ANY),
            ],
            out_specs=pl.BlockSpec((tm, d), lambda i, p: (i, 0)),
            scratch_shapes=[pltpu.VMEM((2, TOP_K, tm, d), F32), pltpu.SemaphoreType.DMA((2,))],
        ),
        out_shape=jax.ShapeDtypeStruct((m, d), F32),
        compiler_params=_params("arbitrary"),
        name="combine_norm",
    )(pos, x, gates, g_final.reshape(1, d), yg)


def _dispatch_plan(eid, n_experts, tm):
    n = eid.shape[0]
    a = n * TOP_K
    e_flat = eid.reshape(a)
    onehot = (e_flat[:, None] == jnp.arange(n_experts, dtype=jnp.int32)[None, :]).astype(jnp.int32)
    csum = jnp.cumsum(onehot, axis=0)
    counts = csum[-1]
    rank = jnp.take_along_axis(csum, e_flat[:, None], axis=1)[:, 0] - 1
    padded = ((counts + tm - 1) // tm) * tm
    pad_end = jnp.cumsum(padded)
    pos = (pad_end - padded)[e_flat] + rank
    n_tiles = a // tm + n_experts
    src = jnp.zeros((n_tiles * tm,), jnp.int32).at[pos].set(jnp.arange(a, dtype=jnp.int32) // TOP_K)
    tile_end = pad_end // tm
    tile_ids = jnp.arange(n_tiles, dtype=jnp.int32)
    tile_expert = jnp.sum((tile_ids[:, None] >= tile_end[None, :]).astype(jnp.int32), axis=1)
    tile_expert = jnp.minimum(tile_expert, n_experts - 1)
    n_valid = tile_end[-1:].astype(jnp.int32)
    return pos.astype(jnp.int32), src, tile_expert, n_valid


def moe_layer(x, g_ffn, w_router, b_router, w_gate, w_up, w_down, g_final, n_groups, per_group):
    m, d = x.shape
    n_experts = n_groups * per_group
    xn, eid, gates = router(x, g_ffn, w_router, b_router, n_groups, per_group)
    tm = _tile(m * TOP_K // n_experts, 512)
    tm = max(tm, 8)
    pos, src, tile_expert, n_valid = _dispatch_plan(eid[:, :TOP_K], n_experts, tm)
    xg = gather_rows(xn, src, _tile(tm, 256))
    hg = expert_up(xg, w_gate, w_up, tile_expert, n_valid, tm)
    yg = expert_down(hg, w_down, tile_expert, n_valid, tm)
    return combine_norm(x, gates, pos, yg, g_final, _tile(m, 128))


def _group_step(x, hist, wts, dims):
    b, t, d = x.shape
    n_fox, dh, n_ml, dv, dk = dims
    m = b * t
    half = d // 2
    x2 = x.reshape(m, d)
    xn = rmsnorm_cast(x2, wts["g_mix"], BF16)
    p32, p16, gates = in_projection(xn, wts["w_a"], wts["w_b"], wts["w_gatecols"], wts["b_gatecols"],
                                    n_fox, n_ml)
    p32 = p32.reshape(b, t, len(P32_SEGS) * half)
    p16 = p16.reshape(b, t, len(P16_SEGS) * half)
    col32 = {s: k * half for k, s in enumerate(P32_SEGS)}
    col16 = {s: k * half for k, s in enumerate(P16_SEGS)}
    gates3 = gates.reshape(b, t, LANES)
    i_off, f_off = n_fox, n_fox + n_ml
    if hist is None:
        logf_all = gates3
        past_k = past_v = None
        conv_hist = jnp.zeros((b, wts["w_conv"].shape[0] - 1, half), F32)
        c0 = jnp.zeros((b, n_ml, dv, dk), F32)
        n0 = jnp.zeros((b, n_ml, dk), F32)
        m0 = jnp.zeros((b, n_ml), F32)
        past = 0
    else:
        cache_k, cache_v, cache_logf, c0, n0, m0, conv_hist = hist
        past = cache_k.shape[1]
        past_k = cache_k.reshape(b, past, half)
        past_v = cache_v.reshape(b, past, half)
        cache_pad = jnp.pad(cache_logf.astype(F32), ((0, 0), (0, 0), (0, LANES - n_fox)))
        logf_all = jnp.concatenate([cache_pad, gates3], axis=1)
    f_all = cumsum_time(logf_all)
    f_row = jnp.swapaxes(f_all[:, :, :n_fox], 1, 2)[:, :, None, :]
    fox_out = fox_attention(p16, col16[SEG_FQ] // dh, p32, col32[SEG_FK] // dh, col32[SEG_FV] // dh,
                            f_all, f_row, n_fox, dh, past_k, past_v)
    f_new = f_all if hist is None else cumsum_time(gates3)
    ml_out, c_new, n_new, m_new = mlstm_heads(
        p32, col32[SEG_QK], col32[SEG_MO], p16, col16[SEG_MV], conv_hist, wts["w_conv"],
        gates3, f_new, jnp.swapaxes(gates3, 1, 2), jnp.swapaxes(f_new, 1, 2), c0, n0, m0,
        wts["g_head"], i_off, f_off)
    x1 = out_projection(fox_out.reshape(m, half), ml_out.reshape(m, half), wts["w_out"], x2)
    y = moe_layer(x1, wts["g_ffn"], wts["w_router"], wts["b_router"], wts["w_gate"], wts["w_up"],
                  wts["w_down"], wts["g_final"], wts["n_groups"], wts["per_group"])
    conv_w = wts["w_conv"].shape[0]
    qk_tail = p32[:, t - min(t, conv_w - 1):, col32[SEG_QK]:col32[SEG_QK] + half]
    qk_hist = jnp.concatenate([conv_hist.astype(F32), qk_tail], axis=1)[:, -(conv_w - 1):]
    fk = p32[:, :, col32[SEG_FK]:col32[SEG_FK] + half].reshape(b, t, n_fox, dh)
    fv = p32[:, :, col32[SEG_FV]:col32[SEG_FV] + half].reshape(b, t, n_fox, dh)
    state = (fk, fv, gates3[:, :, :n_fox], c_new, n_new, m_new, qk_hist)
    return y.reshape(b, t, d), state


def kernel(x_prompt, x_sample, cache_fox_k, cache_fox_v, cache_fox_logf, state_mlstm_C, state_mlstm_n, state_mlstm_m, state_mlstm_conv, g_norm_mix, w_in, b_fox_f, b_mlstm_i, b_mlstm_f, w_conv, g_mlstm_head, w_out, g_norm_ffn, w_router_group, b_router_group, w_router_expert, b_router_expert, w_exp_gate, w_exp_up, w_exp_down, g_norm_final):
    depth = w_in.shape[0]
    assert depth == 1, "the final norm is fused into the last layer's MoE combine; one layer supported"
    d = x_prompt.shape[-1]
    n_fox, dh = cache_fox_k.shape[-2:]
    n_ml, dv, dk = state_mlstm_C.shape[-3:]
    n_groups = w_router_group.shape[-1]
    n_experts = w_router_expert.shape[-1]
    half = d // 2
    assert n_fox * dh == half and n_ml * dv == half and 2 * n_ml * dk == half
    assert n_fox + 2 * n_ml <= LANES and n_groups + n_experts <= LANES
    l = 0
    sizes = (half, half, half, n_fox, half, half, half, n_ml, n_ml)
    offs = [0]
    for s in sizes:
        offs.append(offs[-1] + s)
    col = lambda i: w_in[l][:, offs[i]:offs[i + 1]]
    w_a = w_in[l][:, offs[0]:offs[3]].astype(BF16)
    w_b = w_in[l][:, offs[4]:offs[7]].astype(BF16)
    n_gate = n_fox + 2 * n_ml
    w_gatecols = jnp.pad(jnp.concatenate([col(3), col(7), col(8)], axis=1), ((0, 0), (0, LANES - n_gate))).astype(BF16)
    b_gatecols = jnp.pad(jnp.concatenate([b_fox_f[l], b_mlstm_i[l], b_mlstm_f[l]]).astype(F32),
                         (0, LANES - n_gate)).reshape(1, LANES)
    n_r = n_groups + n_experts
    wts = {
        "g_mix": g_norm_mix[l], "w_a": w_a, "w_b": w_b, "w_gatecols": w_gatecols, "b_gatecols": b_gatecols,
        "w_conv": w_conv[l], "g_head": g_mlstm_head[l], "w_out": w_out[l].astype(BF16),
        "g_ffn": g_norm_ffn[l],
        "w_router": jnp.pad(jnp.concatenate([w_router_group[l], w_router_expert[l]], axis=1),
                            ((0, 0), (0, LANES - n_r))),
        "b_router": jnp.pad(jnp.concatenate([b_router_group[l], b_router_expert[l]]).astype(F32),
                            (0, LANES - n_r)).reshape(1, LANES),
        "w_gate": w_exp_gate[l], "w_up": w_exp_up[l], "w_down": w_exp_down[l], "g_final": g_norm_final,
        "n_groups": n_groups, "per_group": n_experts // n_groups,
    }
    dims = (n_fox, dh, n_ml, dv, dk)
    hist = (cache_fox_k[l], cache_fox_v[l], cache_fox_logf[l], state_mlstm_C[l], state_mlstm_n[l],
            state_mlstm_m[l], state_mlstm_conv[l])
    y_p, st_p = _group_step(x_prompt, None, wts, dims)
    y_s, st_s = _group_step(x_sample, hist, wts, dims)
    st_p = tuple(a[None] for a in st_p)
    st_s = tuple(a[None] for a in st_s)
    return (y_p, y_s) + st_p + st_s
```

```python
import functools

import jax
import jax.numpy as jnp
from jax import lax
from jax.experimental import pallas as pl
from jax.experimental.pallas import tpu as pltpu

F32 = jnp.float32
BF16 = jnp.bfloat16
EPS = 1e-6
LANES = 128
V7X_VMEM_LIMIT = 56 * 1024 * 1024
TOP_K = 2
MLSTM_CHUNK = 256
DMA_ISSUE_UNROLL = 8
HIGHEST = lax.Precision.HIGHEST
NEG_INF = float("-inf")


def _params(*sem):
    return pltpu.CompilerParams(dimension_semantics=sem, vmem_limit_bytes=V7X_VMEM_LIMIT)


def _tile(n, pref):
    if n <= pref:
        return n
    for t in range(pref - pref % 8, 7, -8):
        if n % t == 0:
            return t
    return n


def _log_sigmoid(x):
    return jnp.minimum(x, 0.0) - jnp.log1p(jnp.exp(-jnp.abs(x)))


def _rmsnorm_kernel(x_ref, g_ref, o_ref):
    x = x_ref[...]
    y = x * lax.rsqrt(jnp.mean(x * x, axis=-1, keepdims=True) + EPS)
    o_ref[...] = (y * g_ref[...]).astype(o_ref.dtype)


def rmsnorm_cast(x, g, out_dtype):
    m, d = x.shape
    tm = _tile(m, 256)
    return pl.pallas_call(
        _rmsnorm_kernel,
        grid=(m // tm,),
        in_specs=[pl.BlockSpec((tm, d), lambda i: (i, 0)), pl.BlockSpec((1, d), lambda i: (0, 0))],
        out_specs=pl.BlockSpec((tm, d), lambda i: (i, 0)),
        out_shape=jax.ShapeDtypeStruct((m, d), out_dtype),
        compiler_params=_params("parallel"),
        name="rmsnorm_cast",
    )(x, g.reshape(1, d))


SEG_FQ, SEG_FK, SEG_FV, SEG_QK, SEG_MV, SEG_MO = range(6)
P32_SEGS = (SEG_FK, SEG_FV, SEG_QK, SEG_MO)
P16_SEGS = (SEG_FQ, SEG_MV)


def _inproj_kernel(x_ref, wa_ref, wb_ref, ws_ref, b_ref, p32_ref, p16_ref, gate_ref, *, nseg, n_fox, n_ml):
    j = pl.program_id(1)
    seg = j // nseg

    def emit(w_ref):
        acc = jnp.dot(x_ref[...], w_ref[...], preferred_element_type=F32)
        is16 = (seg == SEG_FQ) | (seg == SEG_MV)

        @pl.when(is16)
        def _():
            p16_ref[...] = acc.astype(p16_ref.dtype)

        @pl.when(jnp.logical_not(is16))
        def _():
            p32_ref[...] = acc

    @pl.when(seg < 3)
    def _():
        emit(wa_ref)

    @pl.when(seg >= 3)
    def _():
        emit(wb_ref)

    @pl.when(j == 0)
    def _():
        p = jnp.dot(x_ref[...], ws_ref[...], preferred_element_type=F32) + b_ref[...]
        lane = lax.broadcasted_iota(jnp.int32, p.shape, 1)
        is_id = (lane >= n_fox) & (lane < n_fox + n_ml)
        gate_ref[...] = jnp.where(is_id, p, _log_sigmoid(p))


def _held_block(j, nseg, segs):
    seg = j // nseg
    blk = jnp.int32(0)
    for k, s in enumerate(segs):
        here = k * nseg + (j - s * nseg)
        done = (k + 1) * nseg - 1
        blk = jnp.where(seg == s, here, jnp.where(seg > s, done, blk))
    return blk


def in_projection(xn, w_a, w_b, w_gate, b_gate, n_fox, n_ml):
    m, d = xn.shape
    w = w_a.shape[1] // 3
    tm = _tile(m, 1024)
    tn = _tile(w, 512)
    nseg = w // tn
    return pl.pallas_call(
        functools.partial(_inproj_kernel, nseg=nseg, n_fox=n_fox, n_ml=n_ml),
        grid=(m // tm, 6 * nseg),
        in_specs=[
            pl.BlockSpec((tm, d), lambda i, j: (i, 0)),
            pl.BlockSpec((d, tn), lambda i, j: (0, jnp.minimum(j, 3 * nseg - 1))),
            pl.BlockSpec((d, tn), lambda i, j: (0, jnp.maximum(j - 3 * nseg, 0))),
            pl.BlockSpec((d, LANES), lambda i, j: (0, 0)),
            pl.BlockSpec((1, LANES), lambda i, j: (0, 0)),
        ],
        out_specs=[
            pl.BlockSpec((tm, tn), lambda i, j: (i, _held_block(j, nseg, P32_SEGS))),
            pl.BlockSpec((tm, tn), lambda i, j: (i, _held_block(j, nseg, P16_SEGS))),
            pl.BlockSpec((tm, LANES), lambda i, j: (i, 0)),
        ],
        out_shape=[
            jax.ShapeDtypeStruct((m, len(P32_SEGS) * w), F32),
            jax.ShapeDtypeStruct((m, len(P16_SEGS) * w), BF16),
            jax.ShapeDtypeStruct((m, LANES), F32),
        ],
        compiler_params=_params("arbitrary", "arbitrary"),
        name="in_projection",
    )(xn, w_a, w_b, w_gate, b_gate)


def _cumsum_kernel(g_ref, f_ref, *, chunk):
    s = g_ref.shape[0]
    r = lax.broadcasted_iota(jnp.int32, (chunk, chunk), 0)
    c = lax.broadcasted_iota(jnp.int32, (chunk, chunk), 1)
    tri = (c <= r).astype(F32)
    carry = jnp.zeros((1, g_ref.shape[1]), F32)
    for k in range(s // chunk):
        blk = g_ref[k * chunk:(k + 1) * chunk, :]
        loc = jnp.dot(tri, blk, precision=HIGHEST, preferred_element_type=F32)
        f_ref[k * chunk:(k + 1) * chunk, :] = loc + carry
        carry = carry + loc[chunk - 1:chunk, :]


def cumsum_time(g):
    b, s, n = g.shape
    chunk = next(c for c in (256, 128, 64, 32, 16, 8) if s % c == 0)
    return pl.pallas_call(
        functools.partial(_cumsum_kernel, chunk=chunk),
        grid=(b,),
        in_specs=[pl.BlockSpec((None, s, n), lambda i: (i, 0, 0))],
        out_specs=pl.BlockSpec((None, s, n), lambda i: (i, 0, 0)),
        out_shape=jax.ShapeDtypeStruct((b, s, n), F32),
        compiler_params=_params("parallel"),
        name="cumsum_time",
    )(g)


def _fox_kernel(*refs, past, t, tq, scale):
    if past:
        q_ref, k_ref, v_ref, pk_ref, pv_ref, fcol_ref, frow_ref, o_ref = refs
    else:
        q_ref, k_ref, v_ref, fcol_ref, frow_ref, o_ref = refs
    h = pl.program_id(1)
    fc = fcol_ref[past:past + t, :]
    lane = lax.broadcasted_iota(jnp.int32, fc.shape, 1)
    fq_all = jnp.sum(jnp.where(lane == h, fc, 0.0), axis=1, keepdims=True)
    fk_all = frow_ref[...]
    kb = k_ref[...].astype(BF16)
    vb = v_ref[...].astype(BF16)
    if past:
        pkb = pk_ref[...].astype(BF16)
        pvb = pv_ref[...].astype(BF16)
    dn_t = (((1,), (1,)), ((), ()))
    for qi in range(t // tq):
        q = q_ref[qi * tq:(qi + 1) * tq, :]
        fq = fq_all[qi * tq:(qi + 1) * tq, :]
        n_k = (qi + 1) * tq
        s = lax.dot_general(q, kb[0:n_k, :], dn_t, preferred_element_type=F32)
        logits = s * scale + (fq - fk_all[:, past:past + n_k])
        row = lax.broadcasted_iota(jnp.int32, (tq, n_k), 0) + qi * tq
        col = lax.broadcasted_iota(jnp.int32, (tq, n_k), 1)
        logits = jnp.where(col <= row, logits, NEG_INF)
        mx = jnp.max(logits, axis=1, keepdims=True)
        if past:
            sp = lax.dot_general(q, pkb, dn_t, preferred_element_type=F32)
            lp = sp * scale + (fq - fk_all[:, 0:past])
            mx = jnp.maximum(mx, jnp.max(lp, axis=1, keepdims=True))
        p = jnp.exp(logits - mx)
        den = jnp.sum(p, axis=1, keepdims=True)
        acc = jnp.dot(p.astype(BF16), vb[0:n_k, :], preferred_element_type=F32)
        if past:
            pp = jnp.exp(lp - mx)
            den = den + jnp.sum(pp, axis=1, keepdims=True)
            acc = acc + jnp.dot(pp.astype(BF16), pvb, preferred_element_type=F32)
        o_ref[qi * tq:(qi + 1) * tq, :] = (acc / den).astype(o_ref.dtype)


def fox_attention(q_arr, q_off, kv_arr, k_off, v_off, f_col, f_row, n_heads, dh, past_k=None, past_v=None):
    b, t, _ = q_arr.shape
    past = 0 if past_k is None else past_k.shape[1]
    tq = _tile(t, 256)

    def head_blk(rows, off):
        return pl.BlockSpec((None, rows, dh), lambda i, h: (i, 0, off + h))

    in_specs = [head_blk(t, q_off), head_blk(t, k_off), head_blk(t, v_off)]
    args = [q_arr, kv_arr, kv_arr]
    if past:
        in_specs += [head_blk(past, 0), head_blk(past, 0)]
        args += [past_k, past_v]
    in_specs += [
        pl.BlockSpec((None, past + t, LANES), lambda i, h: (i, 0, 0)),
        pl.BlockSpec((None, None, 1, past + t), lambda i, h: (i, h, 0, 0)),
    ]
    args += [f_col, f_row]
    return pl.pallas_call(
        functools.partial(_fox_kernel, past=past, t=t, tq=tq, scale=float(dh) ** -0.5),
        grid=(b, n_heads),
        in_specs=in_specs,
        out_specs=head_blk(t, 0),
        out_shape=jax.ShapeDtypeStruct((b, t, n_heads * dh), BF16),
        compiler_params=_params("parallel", "parallel"),
        name="fox_attention",
    )(*args)


def _mlstm_kernel(qraw_ref, kraw_ref, hq_ref, hk_ref, wq_ref, wk_ref, v_ref, og_ref, g_ref, f_ref,
                  irow_ref, frow_ref, c0_ref, n0_ref, m0_ref, gh_ref,
                  out_ref, c_ref, n_ref, m_ref,
                  histq_ref, histk_ref, qc_ref, kc_ref, *, t, chunk, conv_w, i_off, f_off, k_scale):
    h = pl.program_id(1)
    hpad = hq_ref.shape[0]

    def conv_silu(raw_ref, hist_in_ref, w_ref, hist_ref, scale, dst_ref):
        hist_ref[0:hpad, :] = hist_in_ref[...]
        hist_ref[hpad:hpad + t, :] = raw_ref[...]
        y = None
        for j in range(conv_w):
            start = hpad - (conv_w - 1) + j
            term = hist_ref[start:start + t, :] * w_ref[j:j + 1, :]
            y = term if y is None else y + term
        y = y * jax.nn.sigmoid(y)
        if scale != 1.0:
            y = y * scale
        dst_ref[...] = y.astype(dst_ref.dtype)

    conv_silu(qraw_ref, hq_ref, wq_ref, histq_ref, 1.0, qc_ref)
    conv_silu(kraw_ref, hk_ref, wk_ref, histk_ref, k_scale, kc_ref)

    c_ref[...] = c0_ref[...]
    n_ref[...] = n0_ref[...]
    m_ref[...] = m0_ref[...]
    gh = gh_ref[...]
    rr = lax.broadcasted_iota(jnp.int32, (chunk, chunk), 0)
    cc = lax.broadcasted_iota(jnp.int32, (chunk, chunk), 1)
    causal = cc <= rr
    lane = lax.broadcasted_iota(jnp.int32, (chunk, LANES), 1)

    def body(c, f_prev):
        r0 = pl.multiple_of(c * chunk, chunk)
        rows = pl.ds(r0, chunk)
        i_col = jnp.sum(jnp.where(lane == i_off + h, g_ref[rows, :], 0.0), axis=1, keepdims=True)
        b_col = jnp.sum(jnp.where(lane == f_off + h, f_ref[rows, :], 0.0), axis=1, keepdims=True) - f_prev
        i_row = irow_ref[pl.ds(c, 1), :]
        b_row = frow_ref[pl.ds(c, 1), :] - f_prev
        m_prev = m_ref[:, 0:1]
        q = qc_ref[rows, :]
        k = kc_ref[rows, :]
        v = v_ref[rows, :]
        cmat = c_ref[...]
        nvec = n_ref[...]

        d = jnp.where(causal, b_col - b_row + i_row, NEG_INF)
        inter = b_col + m_prev
        m_t = jnp.maximum(inter, jnp.max(d, axis=1, keepdims=True))
        w_inter = jnp.exp(inter - m_t)
        qk = lax.dot_general(q, k, (((1,), (1,)), ((), ())), preferred_element_type=F32)
        s = qk * jnp.exp(d - m_t)
        qc_state = lax.dot_general(q, cmat.astype(BF16), (((1,), (1,)), ((), ())),
                                   preferred_element_type=F32)
        num = w_inter * qc_state + jnp.dot(s.astype(BF16), v, preferred_element_type=F32)
        qn = jnp.sum(q.astype(F32) * nvec, axis=1, keepdims=True)
        den = w_inter * qn + jnp.sum(s, axis=1, keepdims=True)
        hval = num / jnp.maximum(jnp.abs(den), jnp.exp(-m_t))

        hn = hval * lax.rsqrt(jnp.mean(hval * hval, axis=1, keepdims=True) + EPS)
        out = jax.nn.sigmoid(og_ref[rows, :]) * hn * gh
        out_ref[rows, :] = out.astype(out_ref.dtype)

        b_last = b_col[chunk - 1:chunk, :]
        dec_row = b_last - b_row + i_row
        dec_col = b_last - b_col + i_col
        m_new = jnp.maximum(b_last + m_prev, jnp.max(dec_row, axis=1, keepdims=True))
        a = jnp.exp(b_last + m_prev - m_new)
        w_col = jnp.exp(dec_col - m_new)
        vw = (v.astype(F32) * w_col).astype(BF16)
        c_ref[...] = a * cmat + lax.dot_general(vw, k, (((0,), (0,)), ((), ())),
                                                preferred_element_type=F32)
        n_ref[...] = a * nvec + jnp.sum(w_col * k.astype(F32), axis=0, keepdims=True)
        m_ref[...] = jnp.broadcast_to(m_new, m_ref.shape)
        return f_prev + b_last

    lax.fori_loop(0, t // chunk, body, jnp.zeros((1, 1), F32))


def mlstm_heads(p32, qk_col, og_col, p16, v_col, conv_hist, w_conv, gates, f_cum, gates_t, f_cum_t,
                c0, n0, m0, g_head, i_off, f_off):
    b, t, _ = p32.shape
    _, n_h, dv, dk = c0.shape
    qb, ogb, vb = qk_col // dk, og_col // dv, v_col // dv
    conv_w = w_conv.shape[0]
    chunk = _tile(t, MLSTM_CHUNK)
    nch = t // chunk
    hpad = 8
    hist = jnp.pad(conv_hist, ((0, 0), (hpad - (conv_w - 1), 0), (0, 0)))
    gt = gates_t.reshape(b, LANES, nch, chunk)
    ft = f_cum_t.reshape(b, LANES, nch, chunk)
    n0r = n0.reshape(b, n_h, 1, dk)
    m0r = jnp.broadcast_to(m0[:, :, None, None], (b, n_h, 1, LANES))
    ghr = g_head.reshape(n_h, 1, dv)

    qcol = lambda rows, off: pl.BlockSpec((None, rows, dk), lambda i, h: (i, 0, off + h))
    in_specs = [
        qcol(t, qb), qcol(t, qb + n_h), qcol(hpad, 0), qcol(hpad, n_h),
        pl.BlockSpec((conv_w, dk), lambda i, h: (0, h)),
        pl.BlockSpec((conv_w, dk), lambda i, h: (0, n_h + h)),
        pl.BlockSpec((None, t, dv), lambda i, h: (i, 0, vb + h)),
        pl.BlockSpec((None, t, dv), lambda i, h: (i, 0, ogb + h)),
        pl.BlockSpec((None, t, LANES), lambda i, h: (i, 0, 0)),
        pl.BlockSpec((None, t, LANES), lambda i, h: (i, 0, 0)),
        pl.BlockSpec((None, None, nch, chunk), lambda i, h: (i, i_off + h, 0, 0)),
        pl.BlockSpec((None, None, nch, chunk), lambda i, h: (i, f_off + h, 0, 0)),
        pl.BlockSpec((None, None, dv, dk), lambda i, h: (i, h, 0, 0)),
        pl.BlockSpec((None, None, 1, dk), lambda i, h: (i, h, 0, 0)),
        pl.BlockSpec((None, None, 1, LANES), lambda i, h: (i, h, 0, 0)),
        pl.BlockSpec((None, 1, dv), lambda i, h: (h, 0, 0)),
    ]
    out_specs = [
        pl.BlockSpec((None, t, dv), lambda i, h: (i, 0, h)),
        pl.BlockSpec((None, None, dv, dk), lambda i, h: (i, h, 0, 0)),
        pl.BlockSpec((None, None, 1, dk), lambda i, h: (i, h, 0, 0)),
        pl.BlockSpec((None, None, 1, LANES), lambda i, h: (i, h, 0, 0)),
    ]
    out_shape = [
        jax.ShapeDtypeStruct((b, t, n_h * dv), BF16),
        jax.ShapeDtypeStruct((b, n_h, dv, dk), F32),
        jax.ShapeDtypeStruct((b, n_h, 1, dk), F32),
        jax.ShapeDtypeStruct((b, n_h, 1, LANES), F32),
    ]
    out, c_new, n_new, m_new = pl.pallas_call(
        functools.partial(_mlstm_kernel, t=t, chunk=chunk, conv_w=conv_w, i_off=i_off, f_off=f_off,
                          k_scale=float(dk) ** -0.5),
        grid=(b, n_h),
        in_specs=in_specs,
        out_specs=out_specs,
        out_shape=out_shape,
        scratch_shapes=[
            pltpu.VMEM((t + hpad, dk), F32), pltpu.VMEM((t + hpad, dk), F32),
            pltpu.VMEM((t, dk), BF16), pltpu.VMEM((t, dk), BF16),
        ],
        compiler_params=_params("parallel", "parallel"),
        name="mlstm_heads",
    )(p32, p32, hist, hist, w_conv, w_conv, p16, p32, gates, f_cum, gt, ft, c0, n0r, m0r, ghr)
    return out, c_new, n_new[:, :, 0, :], m_new[:, :, 0, 0]


def _outproj_kernel(a1_ref, a2_ref, w1_ref, w2_ref, x_ref, o_ref):
    acc = jnp.dot(a1_ref[...], w1_ref[...], preferred_element_type=F32)
    acc = acc + jnp.dot(a2_ref[...], w2_ref[...], preferred_element_type=F32)
    o_ref[...] = x_ref[...] + acc


def out_projection(a1, a2, w_out, x):
    m, d = x.shape
    half = a1.shape[1]
    tm = _tile(m, 1024)
    tn = _tile(d, 512)
    return pl.pallas_call(
        _outproj_kernel,
        grid=(m // tm, d // tn),
        in_specs=[
            pl.BlockSpec((tm, half), lambda i, j: (i, 0)),
            pl.BlockSpec((tm, half), lambda i, j: (i, 0)),
            pl.BlockSpec((half, tn), lambda i, j: (0, j)),
            pl.BlockSpec((half, tn), lambda i, j: (1, j)),
            pl.BlockSpec((tm, tn), lambda i, j: (i, j)),
        ],
        out_specs=pl.BlockSpec((tm, tn), lambda i, j: (i, j)),
        out_shape=jax.ShapeDtypeStruct((m, d), F32),
        compiler_params=_params("parallel", "parallel"),
        name="out_projection",
    )(a1, a2, w_out, w_out, x)


def _pack_bf16_pairs(x):
    half = x.shape[1] // 2
    bits = lax.bitcast_convert_type(x.astype(BF16).astype(F32), jnp.uint32)
    return (bits[:, :half] >> 16) | (bits[:, half:] & jnp.uint32(0xFFFF0000))


def _unpack_bf16_pairs(w):
    lo = lax.bitcast_convert_type(w << 16, F32).astype(BF16)
    hi = lax.bitcast_convert_type(w & jnp.uint32(0xFFFF0000), F32).astype(BF16)
    return lo, hi


def _router_kernel(x_ref, g_ref, w_ref, b_ref, xn_ref, eid_ref, gate_ref, *, n_groups, per_group):
    x = x_ref[...]
    xn = x * lax.rsqrt(jnp.mean(x * x, axis=-1, keepdims=True) + EPS) * g_ref[...]
    xn_ref[...] = _pack_bf16_pairs(xn)
    logits = jnp.dot(xn, w_ref[...], precision=HIGHEST, preferred_element_type=F32) + b_ref[...]
    lane = lax.broadcasted_iota(jnp.int32, logits.shape, 1).astype(F32)
    big = float(LANES)
    is_g = lane < n_groups
    gl = jnp.where(is_g, logits, NEG_INF)
    gmax = jnp.max(gl, axis=1, keepdims=True)
    g_idx = jnp.min(jnp.where(gl == gmax, lane, big), axis=1, keepdims=True)
    g_prob = 1.0 / jnp.sum(jnp.where(is_g, jnp.exp(logits - gmax), 0.0), axis=1, keepdims=True)
    lo = n_groups + per_group * g_idx
    el = jnp.where((lane >= lo) & (lane < lo + per_group), logits, NEG_INF)
    e1 = jnp.max(el, axis=1, keepdims=True)
    i1 = jnp.min(jnp.where(el == e1, lane, big), axis=1, keepdims=True)
    el2 = jnp.where(lane == i1, NEG_INF, el)
    e2 = jnp.max(el2, axis=1, keepdims=True)
    i2 = jnp.min(jnp.where(el2 == e2, lane, big), axis=1, keepdims=True)
    r = jnp.exp(e2 - e1)
    w1 = g_prob / (1.0 + r)
    w2 = g_prob * r / (1.0 + r)
    eid = jnp.where(lane == 0.0, i1 - n_groups, jnp.where(lane == 1.0, i2 - n_groups, 0.0))
    eid_ref[...] = eid.astype(jnp.int32)
    gate_ref[...] = jnp.where(lane == 0.0, w1, jnp.where(lane == 1.0, w2, 0.0))


def router(x, g, w_router, b_router, n_groups, per_group):
    m, d = x.shape
    tm = _tile(m, 256)
    return pl.pallas_call(
        functools.partial(_router_kernel, n_groups=n_groups, per_group=per_group),
        grid=(m // tm,),
        in_specs=[
            pl.BlockSpec((tm, d), lambda i: (i, 0)),
            pl.BlockSpec((1, d), lambda i: (0, 0)),
            pl.BlockSpec((d, LANES), lambda i: (0, 0)),
            pl.BlockSpec((1, LANES), lambda i: (0, 0)),
        ],
        out_specs=[
            pl.BlockSpec((tm, d // 2), lambda i: (i, 0)),
            pl.BlockSpec((tm, LANES), lambda i: (i, 0)),
            pl.BlockSpec((tm, LANES), lambda i: (i, 0)),
        ],
        out_shape=[
            jax.ShapeDtypeStruct((m, d // 2), jnp.uint32),
            jax.ShapeDtypeStruct((m, LANES), jnp.int32),
            jax.ShapeDtypeStruct((m, LANES), F32),
        ],
        compiler_params=_params("parallel"),
        name="router",
    )(x, g.reshape(1, d), w_router, b_router)


def _gather_kernel(idx_ref, src_ref, o_ref, buf_ref, sem_ref, *, tm):
    step = pl.program_id(0)
    nsteps = pl.num_programs(0)

    def issue(tile, slot):
        def body(r, carry):
            tok = idx_ref[tile * tm + r]
            pltpu.make_async_copy(src_ref.at[pl.ds(tok, 1), :], buf_ref.at[slot, pl.ds(r, 1), :],
                                  sem_ref.at[slot]).start()
            return carry
        lax.fori_loop(0, tm, body, 0, unroll=DMA_ISSUE_UNROLL)

    @pl.when(step == 0)
    def _():
        issue(0, 0)

    @pl.when(step + 1 < nsteps)
    def _():
        issue(step + 1, (step + 1) % 2)

    slot = step % 2
    pltpu.make_async_copy(src_ref.at[pl.ds(0, tm), :], buf_ref.at[slot], sem_ref.at[slot]).wait()
    half = buf_ref.shape[2]
    lo, hi = _unpack_bf16_pairs(buf_ref[slot])
    o_ref[:, :half] = lo
    o_ref[:, half:] = hi


def gather_rows(src, idx, tm):
    n, half = src.shape
    d = 2 * half
    a = idx.shape[0]
    return pl.pallas_call(
        functools.partial(_gather_kernel, tm=tm),
        grid_spec=pltpu.PrefetchScalarGridSpec(
            num_scalar_prefetch=1,
            grid=(a // tm,),
            in_specs=[pl.BlockSpec(memory_space=pl.ANY)],
            out_specs=pl.BlockSpec((tm, d), lambda i, idx_ref: (i, 0)),
            scratch_shapes=[pltpu.VMEM((2, tm, half), src.dtype), pltpu.SemaphoreType.DMA((2,))],
        ),
        out_shape=jax.ShapeDtypeStruct((a, d), BF16),
        compiler_params=_params("arbitrary"),
        name="gather_rows",
    )(idx, src)


def _expert_up_kernel(te_ref, nv_ref, x_ref, wg_ref, wu_ref, h_ref):
    t = pl.program_id(1)

    @pl.when(t < nv_ref[0])
    def _():
        x = x_ref[...]
        g = jnp.dot(x, wg_ref[...].astype(BF16), preferred_element_type=F32)
        u = jnp.dot(x, wu_ref[...].astype(BF16), preferred_element_type=F32)
        h_ref[...] = (g * jax.nn.sigmoid(g) * u).astype(h_ref.dtype)

    @pl.when(t >= nv_ref[0])
    def _():
        h_ref[...] = jnp.zeros_like(h_ref)


def expert_up(xg, w_gate, w_up, tile_expert, n_valid, tm):
    a, d = xg.shape
    _, _, f = w_gate.shape
    tn = _tile(f, 512)
    return pl.pallas_call(
        _expert_up_kernel,
        grid_spec=pltpu.PrefetchScalarGridSpec(
            num_scalar_prefetch=2,
            grid=(f // tn, a // tm),
            in_specs=[
                pl.BlockSpec((tm, d), lambda c, t, te, nv: (t, 0)),
                pl.BlockSpec((None, d, tn), lambda c, t, te, nv: (te[t], 0, c)),
                pl.BlockSpec((None, d, tn), lambda c, t, te, nv: (te[t], 0, c)),
            ],
            out_specs=pl.BlockSpec((tm, tn), lambda c, t, te, nv: (t, c)),
        ),
        out_shape=jax.ShapeDtypeStruct((a, f), BF16),
        compiler_params=_params("arbitrary", "arbitrary"),
        name="expert_up",
    )(tile_expert, n_valid, xg, w_gate, w_up)


def _expert_down_kernel(te_ref, nv_ref, h_ref, wd_ref, y_ref):
    t = pl.program_id(1)

    @pl.when(t < nv_ref[0])
    def _():
        y_ref[...] = jnp.dot(h_ref[...], wd_ref[...].astype(BF16), preferred_element_type=F32)

    @pl.when(t >= nv_ref[0])
    def _():
        y_ref[...] = jnp.zeros_like(y_ref)


def expert_down(hg, w_down, tile_expert, n_valid, tm):
    a, f = hg.shape
    _, _, d = w_down.shape
    tn = _tile(d, 2048)
    return pl.pallas_call(
        _expert_down_kernel,
        grid_spec=pltpu.PrefetchScalarGridSpec(
            num_scalar_prefetch=2,
            grid=(d // tn, a // tm),
            in_specs=[
                pl.BlockSpec((tm, f), lambda c, t, te, nv: (t, 0)),
                pl.BlockSpec((None, f, tn), lambda c, t, te, nv: (te[t], 0, c)),
            ],
            out_specs=pl.BlockSpec((tm, tn), lambda c, t, te, nv: (t, c)),
        ),
        out_shape=jax.ShapeDtypeStruct((a, d), F32),
        compiler_params=_params("arbitrary", "arbitrary"),
        name="expert_down",
    )(tile_expert, n_valid, hg, w_down)


def _combine_kernel(pos_ref, x_ref, gate_ref, g_ref, y_ref, o_ref, buf_ref, sem_ref, *, tm):
    step = pl.program_id(0)
    nsteps = pl.num_programs(0)

    def issue(tile, slot):
        def body(r, carry):
            for k in range(TOP_K):
                p = pos_ref[(tile * tm + r) * TOP_K + k]
                pltpu.make_async_copy(y_ref.at[pl.ds(p, 1), :], buf_ref.at[slot, k, pl.ds(r, 1), :],
                                      sem_ref.at[slot]).start()
            return carry
        lax.fori_loop(0, tm, body, 0, unroll=DMA_ISSUE_UNROLL)

    @pl.when(step == 0)
    def _():
        issue(0, 0)

    @pl.when(step + 1 < nsteps)
    def _():
        issue(step + 1, (step + 1) % 2)

    slot = step % 2
    for k in range(TOP_K):
        pltpu.make_async_copy(y_ref.at[pl.ds(0, tm), :], buf_ref.at[slot, k], sem_ref.at[slot]).wait()
    gates = gate_ref[...]
    x = x_ref[...] + gates[:, 0:1] * buf_ref[slot, 0] + gates[:, 1:2] * buf_ref[slot, 1]
    y = x * lax.rsqrt(jnp.mean(x * x, axis=-1, keepdims=True) + EPS)
    o_ref[...] = y * g_ref[...]


def combine_norm(x, gates, pos, yg, g_final, tm):
    m, d = x.shape
    return pl.pallas_call(
        functools.partial(_combine_kernel, tm=tm),
        grid_spec=pltpu.PrefetchScalarGridSpec(
            num_scalar_prefetch=1,
            grid=(m // tm,),
            in_specs=[
                pl.BlockSpec((tm, d), lambda i, p: (i, 0)),
                pl.BlockSpec((tm, LANES), lambda i, p: (i, 0)),
                pl.BlockSpec((1, d), lambda i, p: (0, 0)),
                pl.BlockSpec(memory_space=pl.ANY),
            ],
            out_specs=pl.BlockSpec((tm, d), lambda i, p: (i, 0)),
            scratch_shapes=[pltpu.VMEM((2, TOP_K, tm, d), F32), pltpu.SemaphoreType.DMA((2,))],
        ),
        out_shape=jax.ShapeDtypeStruct((m, d), F32),
        compiler_params=_params("arbitrary"),
        name="combine_norm",
    )(pos, x, gates, g_final.reshape(1, d), yg)


def _dispatch_plan(eid, n_experts, tm):
    n = eid.shape[0]
    a = n * TOP_K
    e_flat = eid.reshape(a)
    onehot = (e_flat[:, None] == jnp.arange(n_experts, dtype=jnp.int32)[None, :]).astype(jnp.int32)
    csum = jnp.cumsum(onehot, axis=0)
    counts = csum[-1]
    rank = jnp.take_along_axis(csum, e_flat[:, None], axis=1)[:, 0] - 1
    padded = ((counts + tm - 1) // tm) * tm
    pad_end = jnp.cumsum(padded)
    pos = (pad_end - padded)[e_flat] + rank
    n_tiles = a // tm + n_experts
    src = jnp.zeros((n_tiles * tm,), jnp.int32).at[pos].set(jnp.arange(a, dtype=jnp.int32) // TOP_K)
    tile_end = pad_end // tm
    tile_ids = jnp.arange(n_tiles, dtype=jnp.int32)
    tile_expert = jnp.sum((tile_ids[:, None] >= tile_end[None, :]).astype(jnp.int32), axis=1)
    tile_expert = jnp.minimum(tile_expert, n_experts - 1)
    n_valid = tile_end[-1:].astype(jnp.int32)
    return pos.astype(jnp.int32), src, tile_expert, n_valid


def moe_layer(x, g_ffn, w_router, b_router, w_gate, w_up, w_down, g_final, n_groups, per_group):
    m, d = x.shape
    n_experts = n_groups * per_group
    xn, eid, gates = router(x, g_ffn, w_router, b_router, n_groups, per_group)
    tm = _tile(m * TOP_K // n_experts, 512)
    tm = max(tm, 8)
    pos, src, tile_expert, n_valid = _dispatch_plan(eid[:, :TOP_K], n_experts, tm)
    xg = gather_rows(xn, src, _tile(tm, 256))
    hg = expert_up(xg, w_gate, w_up, tile_expert, n_valid, tm)
    yg = expert_down(hg, w_down, tile_expert, n_valid, tm)
    return combine_norm(x, gates, pos, yg, g_final, _tile(m, 128))


def _group_step(x, hist, wts, dims):
    b, t, d = x.shape
    n_fox, dh, n_ml, dv, dk = dims
    m = b * t
    half = d // 2
    x2 = x.reshape(m, d)
    xn = rmsnorm_cast(x2, wts["g_mix"], BF16)
    p32, p16, gates = in_projection(xn, wts["w_a"], wts["w_b"], wts["w_gatecols"], wts["b_gatecols"],
                                    n_fox, n_ml)
    p32 = p32.reshape(b, t, len(P32_SEGS) * half)
    p16 = p16.reshape(b, t, len(P16_SEGS) * half)
    col32 = {s: k * half for k, s in enumerate(P32_SEGS)}
    col16 = {s: k * half for k, s in enumerate(P16_SEGS)}
    gates3 = gates.reshape(b, t, LANES)
    i_off, f_off = n_fox, n_fox + n_ml
    if hist is None:
        logf_all = gates3
        past_k = past_v = None
        conv_hist = jnp.zeros((b, wts["w_conv"].shape[0] - 1, half), F32)
        c0 = jnp.zeros((b, n_ml, dv, dk), F32)
        n0 = jnp.zeros((b, n_ml, dk), F32)
        m0 = jnp.zeros((b, n_ml), F32)
        past = 0
    else:
        cache_k, cache_v, cache_logf, c0, n0, m0, conv_hist = hist
        past = cache_k.shape[1]
        past_k = cache_k.reshape(b, past, half)
        past_v = cache_v.reshape(b, past, half)
        cache_pad = jnp.pad(cache_logf.astype(F32), ((0, 0), (0, 0), (0, LANES - n_fox)))
        logf_all = jnp.concatenate([cache_pad, gates3], axis=1)
    f_all = cumsum_time(logf_all)
    f_row = jnp.swapaxes(f_all[:, :, :n_fox], 1, 2)[:, :, None, :]
    fox_out = fox_attention(p16, col16[SEG_FQ] // dh, p32, col32[SEG_FK] // dh, col32[SEG_FV] // dh,
                            f_all, f_row, n_fox, dh, past_k, past_v)
    f_new = f_all if hist is None else cumsum_time(gates3)
    ml_out, c_new, n_new, m_new = mlstm_heads(
        p32, col32[SEG_QK], col32[SEG_MO], p16, col16[SEG_MV], conv_hist, wts["w_conv"],
        gates3, f_new, jnp.swapaxes(gates3, 1, 2), jnp.swapaxes(f_new, 1, 2), c0, n0, m0,
        wts["g_head"], i_off, f_off)
    x1 = out_projection(fox_out.reshape(m, half), ml_out.reshape(m, half), wts["w_out"], x2)
    y = moe_layer(x1, wts["g_ffn"], wts["w_router"], wts["b_router"], wts["w_gate"], wts["w_up"],
                  wts["w_down"], wts["g_final"], wts["n_groups"], wts["per_group"])
    conv_w = wts["w_conv"].shape[0]
    qk_tail = p32[:, t - min(t, conv_w - 1):, col32[SEG_QK]:col32[SEG_QK] + half]
    qk_hist = jnp.concatenate([conv_hist.astype(F32), qk_tail], axis=1)[:, -(conv_w - 1):]
    fk = p32[:, :, col32[SEG_FK]:col32[SEG_FK] + half].reshape(b, t, n_fox, dh)
    fv = p32[:, :, col32[SEG_FV]:col32[SEG_FV] + half].reshape(b, t, n_fox, dh)
    state = (fk, fv, gates3[:, :, :n_fox], c_new, n_new, m_new, qk_hist)
    return y.reshape(b, t, d), state


def kernel(x_prompt, x_sample, cache_fox_k, cache_fox_v, cache_fox_logf, state_mlstm_C, state_mlstm_n, state_mlstm_m, state_mlstm_conv, g_norm_mix, w_in, b_fox_f, b_mlstm_i, b_mlstm_f, w_conv, g_mlstm_head, w_out, g_norm_ffn, w_router_group, b_router_group, w_router_expert, b_router_expert, w_exp_gate, w_exp_up, w_exp_down, g_norm_final):
    depth = w_in.shape[0]
    assert depth == 1, "the final norm is fused into the last layer's MoE combine; one layer supported"
    d = x_prompt.shape[-1]
    n_fox, dh = cache_fox_k.shape[-2:]
    n_ml, dv, dk = state_mlstm_C.shape[-3:]
    n_groups = w_router_group.shape[-1]
    n_experts = w_router_expert.shape[-1]
    half = d // 2
    assert n_fox * dh == half and n_ml * dv == half and 2 * n_ml * dk == half
    assert n_fox + 2 * n_ml <= LANES and n_groups + n_experts <= LANES
    l = 0
    sizes = (half, half, half, n_fox, half, half, half, n_ml, n_ml)
    offs = [0]
    for s in sizes:
        offs.append(offs[-1] + s)
    col = lambda i: w_in[l][:, offs[i]:offs[i + 1]]
    w_a = w_in[l][:, offs[0]:offs[3]].astype(BF16)
    w_b = w_in[l][:, offs[4]:offs[7]].astype(BF16)
    n_gate = n_fox + 2 * n_ml
    w_gatecols = jnp.pad(jnp.concatenate([col(3), col(7), col(8)], axis=1), ((0, 0), (0, LANES - n_gate))).astype(BF16)
    b_gatecols = jnp.pad(jnp.concatenate([b_fox_f[l], b_mlstm_i[l], b_mlstm_f[l]]).astype(F32),
                         (0, LANES - n_gate)).reshape(1, LANES)
    n_r = n_groups + n_experts
    wts = {
        "g_mix": g_norm_mix[l], "w_a": w_a, "w_b": w_b, "w_gatecols": w_gatecols, "b_gatecols": b_gatecols,
        "w_conv": w_conv[l], "g_head": g_mlstm_head[l], "w_out": w_out[l].astype(BF16),
        "g_ffn": g_norm_ffn[l],
        "w_router": jnp.pad(jnp.concatenate([w_router_group[l], w_router_expert[l]], axis=1),
                            ((0, 0), (0, LANES - n_r))),
        "b_router": jnp.pad(jnp.concatenate([b_router_group[l], b_router_expert[l]]).astype(F32),
                            (0, LANES - n_r)).reshape(1, LANES),
        "w_gate": w_exp_gate[l], "w_up": w_exp_up[l], "w_down": w_exp_down[l], "g_final": g_norm_final,
        "n_groups": n_groups, "per_group": n_experts // n_groups,
    }
    dims = (n_fox, dh, n_ml, dv, dk)
    hist = (cache_fox_k[l], cache_fox_v[l], cache_fox_logf[l], state_mlstm_C[l], state_mlstm_n[l],
            state_mlstm_m[l], state_mlstm_conv[l])
    y_p, st_p = _group_step(x_prompt, None, wts, dims)
    y_s, st_s = _group_step(x_sample, hist, wts, dims)
    st_p = tuple(a[None] for a in st_p)
    st_s = tuple(a[None] for a in st_s)
    return (y_p, y_s) + st_p + st_s
```

```python
import functools
import math

import jax
import jax.numpy as jnp
from jax import lax
from jax.experimental import pallas as pl
from jax.experimental.pallas import tpu as pltpu

F32 = jnp.float32
BF16 = jnp.bfloat16
EPS = 1e-6
LANES = 128
V7X_VMEM_LIMIT = 56 * 1024 * 1024
TOP_K = 2
MLSTM_CHUNK = 256
DMA_ISSUE_UNROLL = 8
HIGHEST = lax.Precision.HIGHEST
NEG_INF = float("-inf")
LOG2E = math.log2(math.e)


def _params(*sem):
    return pltpu.CompilerParams(dimension_semantics=sem, vmem_limit_bytes=V7X_VMEM_LIMIT)


def _tile(n, pref):
    if n <= pref:
        return n
    for t in range(pref - pref % 8, 7, -8):
        if n % t == 0:
            return t
    return n


def _log_sigmoid(x):
    return jnp.minimum(x, 0.0) - jnp.log1p(jnp.exp(-jnp.abs(x)))


def _split_maps(n_first):
    first = lambda i, *_: (jnp.minimum(i, n_first - 1), 0)
    second = lambda i, *_: (jnp.maximum(i - n_first, 0), 0)
    return first, second


def _rmsnorm_kernel(xa_ref, xb_ref, g_ref, o_ref, *, n_first):
    def emit(x_ref):
        x = x_ref[...]
        y = x * lax.rsqrt(jnp.mean(x * x, axis=-1, keepdims=True) + EPS)
        o_ref[...] = (y * g_ref[...]).astype(o_ref.dtype)

    @pl.when(pl.program_id(0) < n_first)
    def _():
        emit(xa_ref)

    @pl.when(pl.program_id(0) >= n_first)
    def _():
        emit(xb_ref)


def rmsnorm_cast(xa, xb, g, out_dtype):
    (ma, d), mb = xa.shape, xb.shape[0]
    tm = _tile(math.gcd(ma, mb), 256)
    first, second = _split_maps(ma // tm)
    return pl.pallas_call(
        functools.partial(_rmsnorm_kernel, n_first=ma // tm),
        grid=((ma + mb) // tm,),
        in_specs=[pl.BlockSpec((tm, d), first), pl.BlockSpec((tm, d), second),
                  pl.BlockSpec((1, d), lambda i: (0, 0))],
        out_specs=pl.BlockSpec((tm, d), lambda i: (i, 0)),
        out_shape=jax.ShapeDtypeStruct((ma + mb, d), out_dtype),
        compiler_params=_params("arbitrary"),
        name="rmsnorm_cast",
    )(xa, xb, g.reshape(1, d))


SEG_FQ, SEG_FK, SEG_FV, SEG_QK, SEG_MV, SEG_MO = range(6)
P32_SEGS = (SEG_FK, SEG_FV, SEG_QK, SEG_MO)
P16_SEGS = (SEG_FQ, SEG_MV)


def _inproj_kernel(x_ref, wa_ref, wb_ref, ws_ref, b_ref, p32_ref, p16_ref, gate_ref, *, nseg, n_fox, n_ml):
    j = pl.program_id(1)
    seg = j // nseg

    def emit(w_ref):
        acc = jnp.dot(x_ref[...], w_ref[...], preferred_element_type=F32)
        is16 = (seg == SEG_FQ) | (seg == SEG_MV)

        @pl.when(is16)
        def _():
            p16_ref[...] = acc.astype(p16_ref.dtype)

        @pl.when(jnp.logical_not(is16))
        def _():
            p32_ref[...] = acc

    @pl.when(seg < 3)
    def _():
        emit(wa_ref)

    @pl.when(seg >= 3)
    def _():
        emit(wb_ref)

    @pl.when(j == 0)
    def _():
        p = jnp.dot(x_ref[...], ws_ref[...], preferred_element_type=F32) + b_ref[...]
        lane = lax.broadcasted_iota(jnp.int32, p.shape, 1)
        is_id = (lane >= n_fox) & (lane < n_fox + n_ml)
        gate_ref[...] = jnp.where(is_id, p, _log_sigmoid(p))


def _held_block(j, nseg, segs):
    seg = j // nseg
    blk = jnp.int32(0)
    for k, s in enumerate(segs):
        here = k * nseg + (j - s * nseg)
        done = (k + 1) * nseg - 1
        blk = jnp.where(seg == s, here, jnp.where(seg > s, done, blk))
    return blk


def in_projection(xn, w_a, w_b, w_gate, b_gate, n_fox, n_ml, tm):
    m, d = xn.shape
    w = w_a.shape[1] // 3
    tn = _tile(w, 512)
    nseg = w // tn
    return pl.pallas_call(
        functools.partial(_inproj_kernel, nseg=nseg, n_fox=n_fox, n_ml=n_ml),
        grid=(m // tm, 6 * nseg),
        in_specs=[
            pl.BlockSpec((tm, d), lambda i, j: (i, 0)),
            pl.BlockSpec((d, tn), lambda i, j: (0, jnp.minimum(j, 3 * nseg - 1))),
            pl.BlockSpec((d, tn), lambda i, j: (0, jnp.maximum(j - 3 * nseg, 0))),
            pl.BlockSpec((d, LANES), lambda i, j: (0, 0)),
            pl.BlockSpec((1, LANES), lambda i, j: (0, 0)),
        ],
        out_specs=[
            pl.BlockSpec((tm, tn), lambda i, j: (i, _held_block(j, nseg, P32_SEGS))),
            pl.BlockSpec((tm, tn), lambda i, j: (i, _held_block(j, nseg, P16_SEGS))),
            pl.BlockSpec((tm, LANES), lambda i, j: (i, 0)),
        ],
        out_shape=[
            jax.ShapeDtypeStruct((m, len(P32_SEGS) * w), F32),
            jax.ShapeDtypeStruct((m, len(P16_SEGS) * w), BF16),
            jax.ShapeDtypeStruct((m, LANES), F32),
        ],
        compiler_params=_params("arbitrary", "arbitrary"),
        name="in_projection",
    )(xn, w_a, w_b, w_gate, b_gate)


def _cumsum_kernel(g_ref, f_ref, *, chunk):
    s = g_ref.shape[0]
    r = lax.broadcasted_iota(jnp.int32, (chunk, chunk), 0)
    c = lax.broadcasted_iota(jnp.int32, (chunk, chunk), 1)
    tri = (c <= r).astype(F32)
    carry = jnp.zeros((1, g_ref.shape[1]), F32)
    for k in range(s // chunk):
        blk = g_ref[k * chunk:(k + 1) * chunk, :]
        loc = jnp.dot(tri, blk, precision=HIGHEST, preferred_element_type=F32)
        f_ref[k * chunk:(k + 1) * chunk, :] = loc + carry
        carry = carry + loc[chunk - 1:chunk, :]


def cumsum_time(g, n_seq, s):
    n = g.shape[1]
    chunk = next(c for c in (256, 128, 64, 32, 16, 8) if s % c == 0)
    return pl.pallas_call(
        functools.partial(_cumsum_kernel, chunk=chunk),
        grid=(n_seq,),
        in_specs=[pl.BlockSpec((s, n), lambda i: (i, 0))],
        out_specs=pl.BlockSpec((s, n), lambda i: (i, 0)),
        out_shape=jax.ShapeDtypeStruct((n_seq * s, n), F32),
        compiler_params=_params("parallel"),
        name="cumsum_time",
    )(g)


def _fox_kernel(*refs, past, t, tq, scale):
    if past:
        q_ref, k_ref, v_ref, pk_ref, pv_ref, fcol_ref, frow_ref, o_ref = refs
    else:
        q_ref, k_ref, v_ref, fcol_ref, frow_ref, o_ref = refs
    h = pl.program_id(1)
    fc = fcol_ref[past:past + t, :]
    lane = lax.broadcasted_iota(jnp.int32, fc.shape, 1)
    fq_all = jnp.sum(jnp.where(lane == h, fc, 0.0), axis=1, keepdims=True) * LOG2E
    fk_all = frow_ref[...] * LOG2E
    kb = k_ref[...].astype(BF16)
    vb = v_ref[...].astype(BF16)
    dn_t = (((1,), (1,)), ((), ()))
    row = lax.broadcasted_iota(jnp.int32, (tq, tq), 0)
    col = lax.broadcasted_iota(jnp.int32, (tq, tq), 1)
    diag_mask = col <= row
    for qi in range(t // tq):
        q = q_ref[qi * tq:(qi + 1) * tq, :]
        fq = fq_all[qi * tq:(qi + 1) * tq, :]
        lo = qi * tq
        parts = []
        if past:
            parts.append((pk_ref[...].astype(BF16), pv_ref[...].astype(BF16), fk_all[:, 0:past], False))
        if qi:
            parts.append((kb[0:lo, :], vb[0:lo, :], fk_all[:, past:past + lo], False))
        parts.append((kb[lo:lo + tq, :], vb[lo:lo + tq, :], fk_all[:, past + lo:past + lo + tq], True))
        logits = []
        for kp, _, fk, masked in parts:
            s = lax.dot_general(q, kp, dn_t, preferred_element_type=F32)
            lg = s * (scale * LOG2E) + (fq - fk)
            logits.append(jnp.where(diag_mask, lg, NEG_INF) if masked else lg)
        mx = functools.reduce(jnp.maximum, [jnp.max(lg, axis=1, keepdims=True) for lg in logits])
        den = 0.0
        acc = 0.0
        for (_, vp, _, _), lg in zip(parts, logits):
            p = jnp.exp2(lg - mx)
            den = den + jnp.sum(p, axis=1, keepdims=True)
            acc = acc + jnp.dot(p.astype(BF16), vp, preferred_element_type=F32)
        o_ref[qi * tq:(qi + 1) * tq, :] = (acc / den).astype(o_ref.dtype)


def fox_attention(p16, q_col, p32, k_col, v_col, row_blk, n_seq, t, f_col, f_row, n_heads, dh,
                  past_k=None, past_v=None):
    past = 0 if past_k is None else past_k.shape[1]
    tq = _tile(t, 256)

    def head_blk(off):
        return pl.BlockSpec((t, dh), lambda i, h: (row_blk + i, off // dh + h))

    in_specs = [head_blk(q_col), head_blk(k_col), head_blk(v_col)]
    args = [p16, p32, p32]
    if past:
        past_blk = pl.BlockSpec((None, past, dh), lambda i, h: (i, 0, h))
        in_specs += [past_blk, past_blk]
        args += [past_k, past_v]
    in_specs += [
        pl.BlockSpec((None, past + t, LANES), lambda i, h: (i, 0, 0)),
        pl.BlockSpec((None, None, 1, past + t), lambda i, h: (i, h, 0, 0)),
    ]
    args += [f_col, f_row]
    return pl.pallas_call(
        functools.partial(_fox_kernel, past=past, t=t, tq=tq, scale=float(dh) ** -0.5),
        grid=(n_seq, n_heads),
        in_specs=in_specs,
        out_specs=pl.BlockSpec((t, dh), lambda i, h: (i, h)),
        out_shape=jax.ShapeDtypeStruct((n_seq * t, n_heads * dh), BF16),
        compiler_params=_params("parallel", "parallel"),
        name="fox_attention",
    )(*args)


def _mlstm_kernel(qraw_ref, kraw_ref, hq_ref, hk_ref, wq_ref, wk_ref, v_ref, og_ref, g_ref, f_ref,
                  irow_ref, frow_ref, c0_ref, n0_ref, m0_ref, gh_ref,
                  out_ref, c_ref, n_ref, m_ref,
                  histq_ref, histk_ref, qc_ref, kc_ref, *, t, chunk, conv_w, i_off, f_off, k_scale):
    h = pl.program_id(1)
    hpad = hq_ref.shape[0]

    def conv_silu(raw_ref, hist_in_ref, w_ref, hist_ref, scale, dst_ref):
        hist_ref[0:hpad, :] = hist_in_ref[...]
        hist_ref[hpad:hpad + t, :] = raw_ref[...]
        y = None
        for j in range(conv_w):
            start = hpad - (conv_w - 1) + j
            term = hist_ref[start:start + t, :] * w_ref[j:j + 1, :]
            y = term if y is None else y + term
        y = y * jax.nn.sigmoid(y)
        if scale != 1.0:
            y = y * scale
        dst_ref[...] = y.astype(dst_ref.dtype)

    conv_silu(qraw_ref, hq_ref, wq_ref, histq_ref, 1.0, qc_ref)
    conv_silu(kraw_ref, hk_ref, wk_ref, histk_ref, k_scale, kc_ref)

    c_ref[...] = c0_ref[...]
    n_ref[...] = n0_ref[...]
    m_ref[...] = m0_ref[...]
    gh = gh_ref[...]
    rr = lax.broadcasted_iota(jnp.int32, (chunk, chunk), 0)
    cc = lax.broadcasted_iota(jnp.int32, (chunk, chunk), 1)
    causal = cc <= rr
    lane = lax.broadcasted_iota(jnp.int32, (chunk, LANES), 1)

    def body(c, f_prev):
        r0 = pl.multiple_of(c * chunk, chunk)
        rows = pl.ds(r0, chunk)
        i_col = jnp.sum(jnp.where(lane == i_off + h, g_ref[rows, :], 0.0), axis=1, keepdims=True)
        b_col = jnp.sum(jnp.where(lane == f_off + h, f_ref[rows, :], 0.0), axis=1, keepdims=True) - f_prev
        i_row = irow_ref[pl.ds(c, 1), :]
        b_row = frow_ref[pl.ds(c, 1), :] - f_prev
        m_prev = m_ref[:, 0:1]
        q = qc_ref[rows, :]
        k = kc_ref[rows, :]
        v = v_ref[rows, :]
        cmat = c_ref[...]
        nvec = n_ref[...]

        d = jnp.where(causal, b_col - b_row + i_row, NEG_INF)
        inter = b_col + m_prev
        m_t = jnp.maximum(inter, jnp.max(d, axis=1, keepdims=True))
        w_inter = jnp.exp(inter - m_t)
        qk = lax.dot_general(q, k, (((1,), (1,)), ((), ())), preferred_element_type=F32)
        s = qk * jnp.exp(d - m_t)
        qc_state = lax.dot_general(q, cmat.astype(BF16), (((1,), (1,)), ((), ())),
                                   preferred_element_type=F32)
        num = w_inter * qc_state + jnp.dot(s.astype(BF16), v, preferred_element_type=F32)
        qn = jnp.sum(q.astype(F32) * nvec, axis=1, keepdims=True)
        den = w_inter * qn + jnp.sum(s, axis=1, keepdims=True)
        hval = num / jnp.maximum(jnp.abs(den), jnp.exp(-m_t))

        hn = hval * lax.rsqrt(jnp.mean(hval * hval, axis=1, keepdims=True) + EPS)
        out = jax.nn.sigmoid(og_ref[rows, :]) * hn * gh
        out_ref[rows, :] = out.astype(out_ref.dtype)

        b_last = b_col[chunk - 1:chunk, :]
        dec_row = b_last - b_row + i_row
        dec_col = b_last - b_col + i_col
        m_new = jnp.maximum(b_last + m_prev, jnp.max(dec_row, axis=1, keepdims=True))
        a = jnp.exp(b_last + m_prev - m_new)
        w_col = jnp.exp(dec_col - m_new)
        vw = (v.astype(F32) * w_col).astype(BF16)
        c_ref[...] = a * cmat + lax.dot_general(vw, k, (((0,), (0,)), ((), ())),
                                                preferred_element_type=F32)
        n_ref[...] = a * nvec + jnp.sum(w_col * k.astype(F32), axis=0, keepdims=True)
        m_ref[...] = jnp.broadcast_to(m_new, m_ref.shape)
        return f_prev + b_last

    lax.fori_loop(0, t // chunk, body, jnp.zeros((1, 1), F32))


def mlstm_heads(p32, qk_col, og_col, p16, v_col, gates, row_blk, n_seq, t, f_cum, gates_t, f_cum_t,
                conv_hist, w_conv, c0, n0, m0, g_head, i_off, f_off):
    b = n_seq
    _, n_h, dv, dk = c0.shape
    qb, ogb, vb = qk_col // dk, og_col // dv, v_col // dv
    conv_w = w_conv.shape[0]
    chunk = _tile(t, MLSTM_CHUNK)
    nch = t // chunk
    hpad = 8
    hist = jnp.pad(conv_hist, ((0, 0), (hpad - (conv_w - 1), 0), (0, 0)))
    gt = gates_t.reshape(b, LANES, nch, chunk)
    ft = f_cum_t.reshape(b, LANES, nch, chunk)
    n0r = n0.reshape(b, n_h, 1, dk)
    m0r = jnp.broadcast_to(m0[:, :, None, None], (b, n_h, 1, LANES))
    ghr = g_head.reshape(n_h, 1, dv)

    rowcol = lambda width, off: pl.BlockSpec((t, width), lambda i, h: (row_blk + i, off + h))
    hcol = lambda off: pl.BlockSpec((None, hpad, dk), lambda i, h: (i, 0, off + h))
    in_specs = [
        rowcol(dk, qb), rowcol(dk, qb + n_h), hcol(0), hcol(n_h),
        pl.BlockSpec((conv_w, dk), lambda i, h: (0, h)),
        pl.BlockSpec((conv_w, dk), lambda i, h: (0, n_h + h)),
        rowcol(dv, vb), rowcol(dv, ogb),
        pl.BlockSpec((t, LANES), lambda i, h: (row_blk + i, 0)),
        pl.BlockSpec((t, LANES), lambda i, h: (i, 0)),
        pl.BlockSpec((None, None, nch, chunk), lambda i, h: (i, i_off + h, 0, 0)),
        pl.BlockSpec((None, None, nch, chunk), lambda i, h: (i, f_off + h, 0, 0)),
        pl.BlockSpec((None, None, dv, dk), lambda i, h: (i, h, 0, 0)),
        pl.BlockSpec((None, None, 1, dk), lambda i, h: (i, h, 0, 0)),
        pl.BlockSpec((None, None, 1, LANES), lambda i, h: (i, h, 0, 0)),
        pl.BlockSpec((None, 1, dv), lambda i, h: (h, 0, 0)),
    ]
    out_specs = [
        pl.BlockSpec((t, dv), lambda i, h: (i, h)),
        pl.BlockSpec((None, None, dv, dk), lambda i, h: (i, h, 0, 0)),
        pl.BlockSpec((None, None, 1, dk), lambda i, h: (i, h, 0, 0)),
        pl.BlockSpec((None, None, 1, LANES), lambda i, h: (i, h, 0, 0)),
    ]
    out_shape = [
        jax.ShapeDtypeStruct((b * t, n_h * dv), BF16),
        jax.ShapeDtypeStruct((b, n_h, dv, dk), F32),
        jax.ShapeDtypeStruct((b, n_h, 1, dk), F32),
        jax.ShapeDtypeStruct((b, n_h, 1, LANES), F32),
    ]
    out, c_new, n_new, m_new = pl.pallas_call(
        functools.partial(_mlstm_kernel, t=t, chunk=chunk, conv_w=conv_w, i_off=i_off, f_off=f_off,
                          k_scale=float(dk) ** -0.5),
        grid=(b, n_h),
        in_specs=in_specs,
        out_specs=out_specs,
        out_shape=out_shape,
        scratch_shapes=[
            pltpu.VMEM((t + hpad, dk), F32), pltpu.VMEM((t + hpad, dk), F32),
            pltpu.VMEM((t, dk), BF16), pltpu.VMEM((t, dk), BF16),
        ],
        compiler_params=_params("parallel", "parallel"),
        name="mlstm_heads",
    )(p32, p32, hist, hist, w_conv, w_conv, p16, p32, gates, f_cum, gt, ft, c0, n0r, m0r, ghr)
    return out, c_new, n_new[:, :, 0, :], m_new[:, :, 0, 0]


def _outproj_kernel(fa_ref, ma_ref, fb_ref, mb_ref, w1_ref, w2_ref, xa_ref, xb_ref, o_ref, *, n_first):
    def emit(f_ref, m_ref, x_ref):
        acc = jnp.dot(f_ref[...], w1_ref[...], preferred_element_type=F32)
        acc = acc + jnp.dot(m_ref[...], w2_ref[...], preferred_element_type=F32)
        o_ref[...] = x_ref[...] + acc

    @pl.when(pl.program_id(0) < n_first)
    def _():
        emit(fa_ref, ma_ref, xa_ref)

    @pl.when(pl.program_id(0) >= n_first)
    def _():
        emit(fb_ref, mb_ref, xb_ref)


def out_projection(fox_a, ml_a, fox_b, ml_b, w_out, xa, xb, tm):
    (ma, d), mb = xa.shape, xb.shape[0]
    half = d // 2
    tn = _tile(d, 1024)
    n_first = ma // tm
    first = lambda i, j: (jnp.minimum(i, n_first - 1), 0)
    second = lambda i, j: (jnp.maximum(i - n_first, 0), 0)
    first_j = lambda i, j: (jnp.minimum(i, n_first - 1), j)
    second_j = lambda i, j: (jnp.maximum(i - n_first, 0), j)
    return pl.pallas_call(
        functools.partial(_outproj_kernel, n_first=n_first),
        grid=((ma + mb) // tm, d // tn),
        in_specs=[
            pl.BlockSpec((tm, half), first), pl.BlockSpec((tm, half), first),
            pl.BlockSpec((tm, half), second), pl.BlockSpec((tm, half), second),
            pl.BlockSpec((half, tn), lambda i, j: (0, j)),
            pl.BlockSpec((half, tn), lambda i, j: (1, j)),
            pl.BlockSpec((tm, tn), first_j), pl.BlockSpec((tm, tn), second_j),
        ],
        out_specs=pl.BlockSpec((tm, tn), lambda i, j: (i, j)),
        out_shape=jax.ShapeDtypeStruct((ma + mb, d), F32),
        compiler_params=_params("arbitrary", "arbitrary"),
        name="out_projection",
    )(fox_a, ml_a, fox_b, ml_b, w_out, w_out, xa, xb)


def _pack_bf16_pairs(x):
    half = x.shape[1] // 2
    bits = lax.bitcast_convert_type(x.astype(BF16).astype(F32), jnp.uint32)
    return (bits[:, :half] >> 16) | (bits[:, half:] & jnp.uint32(0xFFFF0000))


def _unpack_bf16_pairs(w):
    lo = lax.bitcast_convert_type(w << 16, F32).astype(BF16)
    hi = lax.bitcast_convert_type(w & jnp.uint32(0xFFFF0000), F32).astype(BF16)
    return lo, hi


def _router_kernel(x_ref, g_ref, w_ref, b_ref, xn_ref, eid_ref, gate_ref, *, n_groups, per_group):
    x = x_ref[...]
    xn = x * lax.rsqrt(jnp.mean(x * x, axis=-1, keepdims=True) + EPS) * g_ref[...]
    xn_ref[...] = _pack_bf16_pairs(xn)
    logits = jnp.dot(xn, w_ref[...], precision=HIGHEST, preferred_element_type=F32) + b_ref[...]
    lane = lax.broadcasted_iota(jnp.int32, logits.shape, 1).astype(F32)
    big = float(LANES)
    is_g = lane < n_groups
    gl = jnp.where(is_g, logits, NEG_INF)
    gmax = jnp.max(gl, axis=1, keepdims=True)
    g_idx = jnp.min(jnp.where(gl == gmax, lane, big), axis=1, keepdims=True)
    g_prob = 1.0 / jnp.sum(jnp.where(is_g, jnp.exp(logits - gmax), 0.0), axis=1, keepdims=True)
    lo = n_groups + per_group * g_idx
    el = jnp.where((lane >= lo) & (lane < lo + per_group), logits, NEG_INF)
    e1 = jnp.max(el, axis=1, keepdims=True)
    i1 = jnp.min(jnp.where(el == e1, lane, big), axis=1, keepdims=True)
    el2 = jnp.where(lane == i1, NEG_INF, el)
    e2 = jnp.max(el2, axis=1, keepdims=True)
    i2 = jnp.min(jnp.where(el2 == e2, lane, big), axis=1, keepdims=True)
    r = jnp.exp(e2 - e1)
    w1 = g_prob / (1.0 + r)
    w2 = g_prob * r / (1.0 + r)
    eid = jnp.where(lane == 0.0, i1 - n_groups, jnp.where(lane == 1.0, i2 - n_groups, 0.0))
    eid_ref[...] = eid.astype(jnp.int32)
    gate_ref[...] = jnp.where(lane == 0.0, w1, jnp.where(lane == 1.0, w2, 0.0))


def router(x, g, w_router, b_router, n_groups, per_group):
    m, d = x.shape
    tm = _tile(m, 256)
    return pl.pallas_call(
        functools.partial(_router_kernel, n_groups=n_groups, per_group=per_group),
        grid=(m // tm,),
        in_specs=[
            pl.BlockSpec((tm, d), lambda i: (i, 0)),
            pl.BlockSpec((1, d), lambda i: (0, 0)),
            pl.BlockSpec((d, LANES), lambda i: (0, 0)),
            pl.BlockSpec((1, LANES), lambda i: (0, 0)),
        ],
        out_specs=[
            pl.BlockSpec((tm, d // 2), lambda i: (i, 0)),
            pl.BlockSpec((tm, LANES), lambda i: (i, 0)),
            pl.BlockSpec((tm, LANES), lambda i: (i, 0)),
        ],
        out_shape=[
            jax.ShapeDtypeStruct((m, d // 2), jnp.uint32),
            jax.ShapeDtypeStruct((m, LANES), jnp.int32),
            jax.ShapeDtypeStruct((m, LANES), F32),
        ],
        compiler_params=_params("parallel"),
        name="router",
    )(x, g.reshape(1, d), w_router, b_router)


def _gather_kernel(idx0_ref, idx1_ref, src_ref, o_ref, buf_ref, sem_ref, *, tm):
    step = pl.program_id(0)
    nsteps = pl.num_programs(0)

    def issue(idx_ref, slot):
        def body(r, carry):
            tok = idx_ref[0, r]
            pltpu.make_async_copy(src_ref.at[pl.ds(tok, 1), :], buf_ref.at[slot, pl.ds(r, 1), :],
                                  sem_ref.at[slot]).start()
            return carry
        lax.fori_loop(0, tm, body, 0, unroll=DMA_ISSUE_UNROLL)

    @pl.when(step == 0)
    def _():
        issue(idx0_ref, 0)

    @pl.when(step + 1 < nsteps)
    def _():
        issue(idx1_ref, (step + 1) % 2)

    slot = step % 2
    pltpu.make_async_copy(src_ref.at[pl.ds(0, tm), :], buf_ref.at[slot], sem_ref.at[slot]).wait()
    half = buf_ref.shape[2]
    lo, hi = _unpack_bf16_pairs(buf_ref[slot])
    o_ref[:, :half] = lo
    o_ref[:, half:] = hi


def _smem_tiles(n_steps, width):
    cur = pl.BlockSpec((None, 1, width), lambda i: (i, 0, 0), memory_space=pltpu.SMEM)
    nxt = pl.BlockSpec((None, 1, width), lambda i: (jnp.minimum(i + 1, n_steps - 1), 0, 0),
                       memory_space=pltpu.SMEM)
    return cur, nxt


def gather_rows(src, idx, tm):
    n, half = src.shape
    d = 2 * half
    a = idx.shape[0]
    n_steps = a // tm
    cur, nxt = _smem_tiles(n_steps, tm)
    idx3 = idx.reshape(n_steps, 1, tm)
    return pl.pallas_call(
        functools.partial(_gather_kernel, tm=tm),
        grid=(n_steps,),
        in_specs=[cur, nxt, pl.BlockSpec(memory_space=pl.ANY)],
        out_specs=pl.BlockSpec((tm, d), lambda i: (i, 0)),
        scratch_shapes=[pltpu.VMEM((2, tm, half), src.dtype), pltpu.SemaphoreType.DMA((2,))],
        out_shape=jax.ShapeDtypeStruct((a, d), BF16),
        compiler_params=_params("arbitrary"),
        name="gather_rows",
    )(idx3, idx3, src)


def _expert_up_kernel(te_ref, nv_ref, x_ref, wg_ref, wu_ref, h_ref):
    t = pl.program_id(1)

    @pl.when(t < nv_ref[0])
    def _():
        x = x_ref[...]
        g = jnp.dot(x, wg_ref[...].astype(BF16), preferred_element_type=F32)
        u = jnp.dot(x, wu_ref[...].astype(BF16), preferred_element_type=F32)
        h_ref[...] = (g * jax.nn.sigmoid(g) * u).astype(h_ref.dtype)

    @pl.when(t >= nv_ref[0])
    def _():
        h_ref[...] = jnp.zeros_like(h_ref)


def expert_up(xg, w_gate, w_up, tile_expert, n_valid, tm):
    a, d = xg.shape
    _, _, f = w_gate.shape
    tn = _tile(f, 512)
    return pl.pallas_call(
        _expert_up_kernel,
        grid_spec=pltpu.PrefetchScalarGridSpec(
            num_scalar_prefetch=2,
            grid=(f // tn, a // tm),
            in_specs=[
                pl.BlockSpec((tm, d), lambda c, t, te, nv: (t, 0)),
                pl.BlockSpec((None, d, tn), lambda c, t, te, nv: (te[t], 0, c)),
                pl.BlockSpec((None, d, tn), lambda c, t, te, nv: (te[t], 0, c)),
            ],
            out_specs=pl.BlockSpec((tm, tn), lambda c, t, te, nv: (t, c)),
        ),
        out_shape=jax.ShapeDtypeStruct((a, f), BF16),
        compiler_params=_params("arbitrary", "arbitrary"),
        name="expert_up",
    )(tile_expert, n_valid, xg, w_gate, w_up)


def _expert_down_kernel(te_ref, nv_ref, h_ref, wd_ref, y_ref):
    t = pl.program_id(1)

    @pl.when(t < nv_ref[0])
    def _():
        y_ref[...] = jnp.dot(h_ref[...], wd_ref[...].astype(BF16), preferred_element_type=F32)

    @pl.when(t >= nv_ref[0])
    def _():
        y_ref[...] = jnp.zeros_like(y_ref)


def expert_down(hg, w_down, tile_expert, n_valid, tm):
    a, f = hg.shape
    _, _, d = w_down.shape
    tn = _tile(d, 2048)
    return pl.pallas_call(
        _expert_down_kernel,
        grid_spec=pltpu.PrefetchScalarGridSpec(
            num_scalar_prefetch=2,
            grid=(d // tn, a // tm),
            in_specs=[
                pl.BlockSpec((tm, f), lambda c, t, te, nv: (t, 0)),
                pl.BlockSpec((None, f, tn), lambda c, t, te, nv: (te[t], 0, c)),
            ],
            out_specs=pl.BlockSpec((tm, tn), lambda c, t, te, nv: (t, c)),
        ),
        out_shape=jax.ShapeDtypeStruct((a, d), F32),
        compiler_params=_params("arbitrary", "arbitrary"),
        name="expert_down",
    )(tile_expert, n_valid, hg, w_down)


def _combine_kernel(pos0_ref, pos1_ref, x_ref, gate_ref, g_ref, y_ref, oa_ref, ob_ref, buf_ref, sem_ref,
                    *, tm, n_first):
    step = pl.program_id(0)
    nsteps = pl.num_programs(0)

    def issue(pos_ref, slot):
        def body(r, carry):
            for k in range(TOP_K):
                p = pos_ref[0, r * TOP_K + k]
                pltpu.make_async_copy(y_ref.at[pl.ds(p, 1), :], buf_ref.at[slot, k, pl.ds(r, 1), :],
                                      sem_ref.at[slot]).start()
            return carry
        lax.fori_loop(0, tm, body, 0, unroll=DMA_ISSUE_UNROLL)

    @pl.when(step == 0)
    def _():
        issue(pos0_ref, 0)

    @pl.when(step + 1 < nsteps)
    def _():
        issue(pos1_ref, (step + 1) % 2)

    slot = step % 2
    for k in range(TOP_K):
        pltpu.make_async_copy(y_ref.at[pl.ds(0, tm), :], buf_ref.at[slot, k], sem_ref.at[slot]).wait()
    gates = gate_ref[...]
    x = x_ref[...] + gates[:, 0:1] * buf_ref[slot, 0] + gates[:, 1:2] * buf_ref[slot, 1]
    y = x * lax.rsqrt(jnp.mean(x * x, axis=-1, keepdims=True) + EPS) * g_ref[...]

    @pl.when(step < n_first)
    def _():
        oa_ref[...] = y

    @pl.when(step >= n_first)
    def _():
        ob_ref[...] = y


def combine_norm(x, gates, pos, yg, g_final, m_first, tm):
    m, d = x.shape
    n_steps = m // tm
    n_first = m_first // tm
    cur, nxt = _smem_tiles(n_steps, tm * TOP_K)
    pos3 = pos.reshape(n_steps, 1, tm * TOP_K)
    first, second = _split_maps(n_first)
    return pl.pallas_call(
        functools.partial(_combine_kernel, tm=tm, n_first=n_first),
        grid=(n_steps,),
        in_specs=[
            cur, nxt,
            pl.BlockSpec((tm, d), lambda i: (i, 0)),
            pl.BlockSpec((tm, LANES), lambda i: (i, 0)),
            pl.BlockSpec((1, d), lambda i: (0, 0)),
            pl.BlockSpec(memory_space=pl.ANY),
        ],
        out_specs=[pl.BlockSpec((tm, d), first), pl.BlockSpec((tm, d), second)],
        scratch_shapes=[pltpu.VMEM((2, TOP_K, tm, d), F32), pltpu.SemaphoreType.DMA((2,))],
        out_shape=[jax.ShapeDtypeStruct((m_first, d), F32), jax.ShapeDtypeStruct((m - m_first, d), F32)],
        compiler_params=_params("arbitrary"),
        name="combine_norm",
    )(pos3, pos3, x, gates, g_final.reshape(1, d), yg)


def _dispatch_plan(eid, n_experts, tm):
    n = eid.shape[0]
    a = n * TOP_K
    e_flat = eid.reshape(a)
    onehot = (e_flat[:, None] == jnp.arange(n_experts, dtype=jnp.int32)[None, :]).astype(jnp.int32)
    csum = jnp.cumsum(onehot, axis=0)
    counts = csum[-1]
    rank = jnp.take_along_axis(csum, e_flat[:, None], axis=1)[:, 0] - 1
    padded = ((counts + tm - 1) // tm) * tm
    pad_end = jnp.cumsum(padded)
    pos = (pad_end - padded)[e_flat] + rank
    n_tiles = a // tm + n_experts
    src = jnp.zeros((n_tiles * tm,), jnp.int32).at[pos].set(jnp.arange(a, dtype=jnp.int32) // TOP_K)
    tile_end = pad_end // tm
    tile_ids = jnp.arange(n_tiles, dtype=jnp.int32)
    tile_expert = jnp.sum((tile_ids[:, None] >= tile_end[None, :]).astype(jnp.int32), axis=1)
    tile_expert = jnp.minimum(tile_expert, n_experts - 1)
    n_valid = tile_end[-1:].astype(jnp.int32)
    return pos.astype(jnp.int32), src, tile_expert, n_valid


def moe_layer(x, m_first, g_ffn, w_router, b_router, w_gate, w_up, w_down, g_final, n_groups, per_group):
    m, d = x.shape
    n_experts = n_groups * per_group
    xn, eid, gates = router(x, g_ffn, w_router, b_router, n_groups, per_group)
    a = m * TOP_K
    tm = _tile(a, min(512, max(64, a // n_experts)))
    pos, src, tile_expert, n_valid = _dispatch_plan(eid[:, :TOP_K], n_experts, tm)
    xg = gather_rows(xn, src, _tile(tm, 256))
    hg = expert_up(xg, w_gate, w_up, tile_expert, n_valid, tm)
    yg = expert_down(hg, w_down, tile_expert, n_valid, tm)
    return combine_norm(x, gates, pos, yg, g_final, m_first, _tile(math.gcd(m_first, m - m_first), 128))


def kernel(x_prompt, x_sample, cache_fox_k, cache_fox_v, cache_fox_logf, state_mlstm_C, state_mlstm_n, state_mlstm_m, state_mlstm_conv, g_norm_mix, w_in, b_fox_f, b_mlstm_i, b_mlstm_f, w_conv, g_mlstm_head, w_out, g_norm_ffn, w_router_group, b_router_group, w_router_expert, b_router_expert, w_exp_gate, w_exp_up, w_exp_down, g_norm_final):
    depth = w_in.shape[0]
    assert depth == 1, "the final norm is fused into the last layer's MoE combine; one layer supported"
    bp, tp, d = x_prompt.shape
    bs, ts, _ = x_sample.shape
    n_fox, dh = cache_fox_k.shape[-2:]
    n_ml, dv, dk = state_mlstm_C.shape[-3:]
    past = cache_fox_k.shape[2]
    n_groups = w_router_group.shape[-1]
    n_experts = w_router_expert.shape[-1]
    conv_w = w_conv.shape[1]
    half = d // 2
    mp, ms = bp * tp, bs * ts
    assert n_fox * dh == half and n_ml * dv == half and 2 * n_ml * dk == half
    assert n_fox + 2 * n_ml <= LANES and n_groups + n_experts <= LANES
    assert mp % ts == 0 and ts >= conv_w - 1, "sample sequences are addressed as row blocks after the prompt rows"
    l = 0
    sizes = (half, half, half, n_fox, half, half, half, n_ml, n_ml)
    offs = [0]
    for s in sizes:
        offs.append(offs[-1] + s)
    col = lambda i: w_in[l][:, offs[i]:offs[i + 1]]
    w_a = w_in[l][:, offs[0]:offs[3]].astype(BF16)
    w_b = w_in[l][:, offs[4]:offs[7]].astype(BF16)
    n_gate = n_fox + 2 * n_ml
    w_gatecols = jnp.pad(jnp.concatenate([col(3), col(7), col(8)], axis=1), ((0, 0), (0, LANES - n_gate))).astype(BF16)
    b_gatecols = jnp.pad(jnp.concatenate([b_fox_f[l], b_mlstm_i[l], b_mlstm_f[l]]).astype(F32),
                         (0, LANES - n_gate)).reshape(1, LANES)
    n_r = n_groups + n_experts
    w_router = jnp.pad(jnp.concatenate([w_router_group[l], w_router_expert[l]], axis=1), ((0, 0), (0, LANES - n_r)))
    b_router = jnp.pad(jnp.concatenate([b_router_group[l], b_router_expert[l]]).astype(F32),
                       (0, LANES - n_r)).reshape(1, LANES)
    i_off, f_off = n_fox, n_fox + n_ml
    col32 = {s: k * half for k, s in enumerate(P32_SEGS)}
    col16 = {s: k * half for k, s in enumerate(P16_SEGS)}

    xp2, xs2 = x_prompt.reshape(mp, d), x_sample.reshape(ms, d)
    tm_rows = _tile(math.gcd(mp, ms), 512)
    xn = rmsnorm_cast(xp2, xs2, g_norm_mix[l], BF16)
    p32, p16, gates = in_projection(xn, w_a, w_b, w_gatecols, b_gatecols, n_fox, n_ml, tm_rows)

    f_p = cumsum_time(gates, bp, tp)
    f_p3 = f_p.reshape(bp, tp, LANES)
    g_p3 = gates[:mp].reshape(bp, tp, LANES)
    fox_p = fox_attention(p16, col16[SEG_FQ], p32, col32[SEG_FK], col32[SEG_FV], 0, bp, tp,
                          f_p3, jnp.swapaxes(f_p3[:, :, :n_fox], 1, 2)[:, :, None, :], n_fox, dh)
    hist_p = jnp.zeros((bp, conv_w - 1, half), F32)
    ml_p, c_p, n_p, m_p = mlstm_heads(
        p32, col32[SEG_QK], col32[SEG_MO], p16, col16[SEG_MV], gates, 0, bp, tp, f_p,
        jnp.swapaxes(g_p3, 1, 2), jnp.swapaxes(f_p3, 1, 2), hist_p, w_conv[l],
        jnp.zeros((bp, n_ml, dv, dk), F32), jnp.zeros((bp, n_ml, dk), F32), jnp.zeros((bp, n_ml), F32),
        g_mlstm_head[l], i_off, f_off)

    g_s3 = gates[mp:].reshape(bs, ts, LANES)
    cache_pad = jnp.pad(cache_fox_logf[l].astype(F32), ((0, 0), (0, 0), (0, LANES - n_fox)))
    logf_s = jnp.concatenate([cache_pad, g_s3], axis=1)
    f_s3 = cumsum_time(logf_s.reshape(bs * (past + ts), LANES), bs, past + ts).reshape(bs, past + ts, LANES)
    fox_s = fox_attention(p16, col16[SEG_FQ], p32, col32[SEG_FK], col32[SEG_FV], mp // ts, bs, ts,
                          f_s3, jnp.swapaxes(f_s3[:, :, :n_fox], 1, 2)[:, :, None, :], n_fox, dh,
                          cache_fox_k[l].reshape(bs, past, half), cache_fox_v[l].reshape(bs, past, half))
    fn_s = cumsum_time(g_s3.reshape(ms, LANES), bs, ts)
    ml_s, c_s, n_s, m_s = mlstm_heads(
        p32, col32[SEG_QK], col32[SEG_MO], p16, col16[SEG_MV], gates, mp // ts, bs, ts, fn_s,
        jnp.swapaxes(g_s3, 1, 2), jnp.swapaxes(fn_s.reshape(bs, ts, LANES), 1, 2), state_mlstm_conv[l], w_conv[l],
        state_mlstm_C[l], state_mlstm_n[l], state_mlstm_m[l], g_mlstm_head[l], i_off, f_off)

    x1 = out_projection(fox_p, ml_p, fox_s, ml_s, w_out[l].astype(BF16), xp2, xs2, tm_rows)
    y_p, y_s = moe_layer(x1, mp, g_norm_ffn[l], w_router, b_router, w_exp_gate[l], w_exp_up[l], w_exp_down[l],
                         g_norm_final, n_groups, n_experts // n_groups)

    def new_state(rows, b, t, g3, c_new, n_new, m_new, conv_hist):
        blk = p32[rows]
        fk = blk[:, col32[SEG_FK]:col32[SEG_FK] + half].reshape(b, t, n_fox, dh)
        fv = blk[:, col32[SEG_FV]:col32[SEG_FV] + half].reshape(b, t, n_fox, dh)
        qk_tail = blk[:, col32[SEG_QK]:col32[SEG_QK] + half].reshape(b, t, half)[:, t - min(t, conv_w - 1):]
        qk_hist = jnp.concatenate([conv_hist.astype(F32), qk_tail], axis=1)[:, -(conv_w - 1):]
        return tuple(a[None] for a in (fk, fv, g3[:, :, :n_fox], c_new, n_new, m_new, qk_hist))

    st_p = new_state(slice(0, mp), bp, tp, g_p3, c_p, n_p, m_p, hist_p)
    st_s = new_state(slice(mp, mp + ms), bs, ts, g_s3, c_s, n_s, m_s, state_mlstm_conv[l])
    return (y_p.reshape(bp, tp, d), y_s.reshape(bs, ts, d)) + st_p + st_s
```

```python
import functools
import math

import jax
import jax.numpy as jnp
from jax import lax
from jax.experimental import pallas as pl
from jax.experimental.pallas import tpu as pltpu

F32 = jnp.float32
BF16 = jnp.bfloat16
EPS = 1e-6
LANES = 128
V7X_VMEM_LIMIT = 56 * 1024 * 1024
TOP_K = 2
MLSTM_CHUNK = 256
DMA_ISSUE_UNROLL = 8
DOWN_CHUNK = 2048
HIGHEST = lax.Precision.HIGHEST
NEG_INF = float("-inf")
LOG2E = math.log2(math.e)


def _params(*sem):
    return pltpu.CompilerParams(dimension_semantics=sem, vmem_limit_bytes=V7X_VMEM_LIMIT)


def _tile(n, pref):
    if n <= pref:
        return n
    for t in range(pref - pref % 8, 7, -8):
        if n % t == 0:
            return t
    return n


def _log_sigmoid(x):
    return jnp.minimum(x, 0.0) - jnp.log1p(jnp.exp(-jnp.abs(x)))


def _split_maps(n_first):
    first = lambda i, *_: (jnp.minimum(i, n_first - 1), 0)
    second = lambda i, *_: (jnp.maximum(i - n_first, 0), 0)
    return first, second


def _rmsnorm_kernel(xa_ref, xb_ref, g_ref, o_ref, *, n_first):
    def emit(x_ref):
        x = x_ref[...]
        y = x * lax.rsqrt(jnp.mean(x * x, axis=-1, keepdims=True) + EPS)
        o_ref[...] = (y * g_ref[...]).astype(o_ref.dtype)

    @pl.when(pl.program_id(0) < n_first)
    def _():
        emit(xa_ref)

    @pl.when(pl.program_id(0) >= n_first)
    def _():
        emit(xb_ref)


def rmsnorm_cast(xa, xb, g, out_dtype):
    (ma, d), mb = xa.shape, xb.shape[0]
    tm = _tile(math.gcd(ma, mb), 256)
    first, second = _split_maps(ma // tm)
    return pl.pallas_call(
        functools.partial(_rmsnorm_kernel, n_first=ma // tm),
        grid=((ma + mb) // tm,),
        in_specs=[pl.BlockSpec((tm, d), first), pl.BlockSpec((tm, d), second),
                  pl.BlockSpec((1, d), lambda i: (0, 0))],
        out_specs=pl.BlockSpec((tm, d), lambda i: (i, 0)),
        out_shape=jax.ShapeDtypeStruct((ma + mb, d), out_dtype),
        compiler_params=_params("arbitrary"),
        name="rmsnorm_cast",
    )(xa, xb, g.reshape(1, d))


SEG_FQ, SEG_FK, SEG_FV, SEG_QK, SEG_MV, SEG_MO = range(6)
P32_SEGS = (SEG_FK, SEG_FV, SEG_QK, SEG_MO)
P16_SEGS = (SEG_FQ, SEG_MV)


def _inproj_kernel(x_ref, wa_ref, wb_ref, ws_ref, b_ref, p32_ref, p16_ref, gate_ref, *, nseg, n_fox, n_ml):
    j = pl.program_id(1)
    seg = j // nseg

    def emit(w_ref):
        acc = jnp.dot(x_ref[...], w_ref[...], preferred_element_type=F32)
        is16 = (seg == SEG_FQ) | (seg == SEG_MV)

        @pl.when(is16)
        def _():
            p16_ref[...] = acc.astype(p16_ref.dtype)

        @pl.when(jnp.logical_not(is16))
        def _():
            p32_ref[...] = acc

    @pl.when(seg < 3)
    def _():
        emit(wa_ref)

    @pl.when(seg >= 3)
    def _():
        emit(wb_ref)

    @pl.when(j == 0)
    def _():
        p = jnp.dot(x_ref[...], ws_ref[...], preferred_element_type=F32) + b_ref[...]
        lane = lax.broadcasted_iota(jnp.int32, p.shape, 1)
        is_id = (lane >= n_fox) & (lane < n_fox + n_ml)
        gate_ref[...] = jnp.where(is_id, p, _log_sigmoid(p))


def _held_block(j, nseg, segs):
    seg = j // nseg
    blk = jnp.int32(0)
    for k, s in enumerate(segs):
        here = k * nseg + (j - s * nseg)
        done = (k + 1) * nseg - 1
        blk = jnp.where(seg == s, here, jnp.where(seg > s, done, blk))
    return blk


def in_projection(xn, w_a, w_b, w_gate, b_gate, n_fox, n_ml, tm):
    m, d = xn.shape
    w = w_a.shape[1] // 3
    tn = _tile(w, 512)
    nseg = w // tn
    return pl.pallas_call(
        functools.partial(_inproj_kernel, nseg=nseg, n_fox=n_fox, n_ml=n_ml),
        grid=(pl.cdiv(m, tm), 6 * nseg),
        in_specs=[
            pl.BlockSpec((tm, d), lambda i, j: (i, 0)),
            pl.BlockSpec((d, tn), lambda i, j: (0, jnp.minimum(j, 3 * nseg - 1))),
            pl.BlockSpec((d, tn), lambda i, j: (0, jnp.maximum(j - 3 * nseg, 0))),
            pl.BlockSpec((d, LANES), lambda i, j: (0, 0)),
            pl.BlockSpec((1, LANES), lambda i, j: (0, 0)),
        ],
        out_specs=[
            pl.BlockSpec((tm, tn), lambda i, j: (i, _held_block(j, nseg, P32_SEGS))),
            pl.BlockSpec((tm, tn), lambda i, j: (i, _held_block(j, nseg, P16_SEGS))),
            pl.BlockSpec((tm, LANES), lambda i, j: (i, 0)),
        ],
        out_shape=[
            jax.ShapeDtypeStruct((m, len(P32_SEGS) * w), F32),
            jax.ShapeDtypeStruct((m, len(P16_SEGS) * w), BF16),
            jax.ShapeDtypeStruct((m, LANES), F32),
        ],
        compiler_params=_params("arbitrary", "arbitrary"),
        name="in_projection",
    )(xn, w_a, w_b, w_gate, b_gate)


def _cumsum_kernel(g_ref, f_ref, *, chunk):
    s = g_ref.shape[0]
    r = lax.broadcasted_iota(jnp.int32, (chunk, chunk), 0)
    c = lax.broadcasted_iota(jnp.int32, (chunk, chunk), 1)
    tri = (c <= r).astype(F32)
    carry = jnp.zeros((1, g_ref.shape[1]), F32)
    for k in range(s // chunk):
        blk = g_ref[k * chunk:(k + 1) * chunk, :]
        loc = jnp.dot(tri, blk, precision=HIGHEST, preferred_element_type=F32)
        f_ref[k * chunk:(k + 1) * chunk, :] = loc + carry
        carry = carry + loc[chunk - 1:chunk, :]


def cumsum_time(g, n_seq, s):
    n = g.shape[1]
    chunk = next(c for c in (256, 128, 64, 32, 16, 8) if s % c == 0)
    return pl.pallas_call(
        functools.partial(_cumsum_kernel, chunk=chunk),
        grid=(n_seq,),
        in_specs=[pl.BlockSpec((s, n), lambda i: (i, 0))],
        out_specs=pl.BlockSpec((s, n), lambda i: (i, 0)),
        out_shape=jax.ShapeDtypeStruct((n_seq * s, n), F32),
        compiler_params=_params("parallel"),
        name="cumsum_time",
    )(g)


def _fox_kernel(*refs, past, t, tq, scale, aliased):
    if aliased:
        refs = refs[:-2] + refs[-1:]
    if past:
        q_ref, k_ref, v_ref, pk_ref, pv_ref, fcol_ref, frow_ref, o_ref = refs
    else:
        q_ref, k_ref, v_ref, fcol_ref, frow_ref, o_ref = refs
    h = pl.program_id(1)
    fc = fcol_ref[past:past + t, :]
    lane = lax.broadcasted_iota(jnp.int32, fc.shape, 1)
    fq_all = jnp.sum(jnp.where(lane == h, fc, 0.0), axis=1, keepdims=True) * LOG2E
    fk_all = frow_ref[...] * LOG2E
    kb = k_ref[...].astype(BF16)
    vb = v_ref[...].astype(BF16)
    dn_t = (((1,), (1,)), ((), ()))
    row = lax.broadcasted_iota(jnp.int32, (tq, tq), 0)
    col = lax.broadcasted_iota(jnp.int32, (tq, tq), 1)
    diag_mask = col <= row
    for qi in range(t // tq):
        q = q_ref[qi * tq:(qi + 1) * tq, :]
        fq = fq_all[qi * tq:(qi + 1) * tq, :]
        lo = qi * tq
        parts = []
        if past:
            parts.append((pk_ref[...].astype(BF16), pv_ref[...].astype(BF16), fk_all[:, 0:past], False))
        if qi:
            parts.append((kb[0:lo, :], vb[0:lo, :], fk_all[:, past:past + lo], False))
        parts.append((kb[lo:lo + tq, :], vb[lo:lo + tq, :], fk_all[:, past + lo:past + lo + tq], True))
        logits = []
        for kp, _, fk, masked in parts:
            s = lax.dot_general(q, kp, dn_t, preferred_element_type=F32)
            lg = s * (scale * LOG2E) + (fq - fk)
            logits.append(jnp.where(diag_mask, lg, NEG_INF) if masked else lg)
        mx = functools.reduce(jnp.maximum, [jnp.max(lg, axis=1, keepdims=True) for lg in logits])
        den = 0.0
        acc = 0.0
        for (_, vp, _, _), lg in zip(parts, logits):
            p = jnp.exp2(lg - mx)
            den = den + jnp.sum(p, axis=1, keepdims=True)
            acc = acc + jnp.dot(p.astype(BF16), vp, preferred_element_type=F32)
        o_ref[qi * tq:(qi + 1) * tq, :] = (acc / den).astype(o_ref.dtype)


def fox_attention(p16, q_col, p32, k_col, v_col, row_blk, n_seq, t, f_col, f_row, n_heads, dh,
                  past_k=None, past_v=None, into=None):
    past = 0 if past_k is None else past_k.shape[1]
    tq = _tile(t, 256)

    def head_blk(off):
        return pl.BlockSpec((t, dh), lambda i, h: (row_blk + i, off // dh + h))

    in_specs = [head_blk(q_col), head_blk(k_col), head_blk(v_col)]
    args = [p16, p32, p32]
    if past:
        past_blk = pl.BlockSpec((None, past, dh), lambda i, h: (i, 0, h))
        in_specs += [past_blk, past_blk]
        args += [past_k, past_v]
    in_specs += [
        pl.BlockSpec((None, past + t, LANES), lambda i, h: (i, 0, 0)),
        pl.BlockSpec((None, None, 1, past + t), lambda i, h: (i, h, 0, 0)),
    ]
    args += [f_col, f_row]
    aliases = {}
    if into is not None:
        in_specs.append(pl.BlockSpec(memory_space=pl.ANY))
        args.append(into)
        aliases = {len(args) - 1: 0}
    return pl.pallas_call(
        functools.partial(_fox_kernel, past=past, t=t, tq=tq, scale=float(dh) ** -0.5,
                          aliased=into is not None),
        grid=(n_seq, n_heads),
        in_specs=in_specs,
        out_specs=pl.BlockSpec((t, dh), lambda i, h: (row_blk + i, h)),
        out_shape=jax.ShapeDtypeStruct((p16.shape[0], n_heads * dh), BF16),
        input_output_aliases=aliases,
        compiler_params=_params("parallel", "parallel"),
        name="fox_attention",
    )(*args)


def _mlstm_kernel(*refs, t, chunk, conv_w, i_off, f_off, k_scale, aliased):
    if aliased:
        refs = refs[:16] + refs[17:]
    (qraw_ref, kraw_ref, hq_ref, hk_ref, wq_ref, wk_ref, v_ref, og_ref, g_ref, f_ref,
     irow_ref, frow_ref, c0_ref, n0_ref, m0_ref, gh_ref,
     out_ref, c_ref, n_ref, m_ref,
     histq_ref, histk_ref, qc_ref, kc_ref) = refs
    h = pl.program_id(1)
    hpad = hq_ref.shape[0]

    def conv_silu(raw_ref, hist_in_ref, w_ref, hist_ref, scale, dst_ref):
        hist_ref[0:hpad, :] = hist_in_ref[...]
        hist_ref[hpad:hpad + t, :] = raw_ref[...]
        y = None
        for j in range(conv_w):
            start = hpad - (conv_w - 1) + j
            term = hist_ref[start:start + t, :] * w_ref[j:j + 1, :]
            y = term if y is None else y + term
        y = y * jax.nn.sigmoid(y)
        if scale != 1.0:
            y = y * scale
        dst_ref[...] = y.astype(dst_ref.dtype)

    conv_silu(qraw_ref, hq_ref, wq_ref, histq_ref, 1.0, qc_ref)
    conv_silu(kraw_ref, hk_ref, wk_ref, histk_ref, k_scale, kc_ref)

    c_ref[...] = c0_ref[...]
    n_ref[...] = n0_ref[...]
    m_ref[...] = m0_ref[...]
    gh = gh_ref[...]
    rr = lax.broadcasted_iota(jnp.int32, (chunk, chunk), 0)
    cc = lax.broadcasted_iota(jnp.int32, (chunk, chunk), 1)
    causal = cc <= rr
    lane = lax.broadcasted_iota(jnp.int32, (chunk, LANES), 1)

    def body(c, f_prev):
        r0 = pl.multiple_of(c * chunk, chunk)
        rows = pl.ds(r0, chunk)
        i_col = jnp.sum(jnp.where(lane == i_off + h, g_ref[rows, :], 0.0), axis=1, keepdims=True)
        b_col = jnp.sum(jnp.where(lane == f_off + h, f_ref[rows, :], 0.0), axis=1, keepdims=True) - f_prev
        i_row = irow_ref[pl.ds(c, 1), :]
        b_row = frow_ref[pl.ds(c, 1), :] - f_prev
        m_prev = m_ref[:, 0:1]
        q = qc_ref[rows, :]
        k = kc_ref[rows, :]
        v = v_ref[rows, :]
        cmat = c_ref[...]
        nvec = n_ref[...]

        d = jnp.where(causal, b_col - b_row + i_row, NEG_INF)
        inter = b_col + m_prev
        m_t = jnp.maximum(inter, jnp.max(d, axis=1, keepdims=True))
        w_inter = jnp.exp(inter - m_t)
        qk = lax.dot_general(q, k, (((1,), (1,)), ((), ())), preferred_element_type=F32)
        s = qk * jnp.exp(d - m_t)
        qc_state = lax.dot_general(q, cmat.astype(BF16), (((1,), (1,)), ((), ())),
                                   preferred_element_type=F32)
        num = w_inter * qc_state + jnp.dot(s.astype(BF16), v, preferred_element_type=F32)
        qn = jnp.sum(q.astype(F32) * nvec, axis=1, keepdims=True)
        den = w_inter * qn + jnp.sum(s, axis=1, keepdims=True)
        hval = num / jnp.maximum(jnp.abs(den), jnp.exp(-m_t))

        hn = hval * lax.rsqrt(jnp.mean(hval * hval, axis=1, keepdims=True) + EPS)
        out = jax.nn.sigmoid(og_ref[rows, :]) * hn * gh
        out_ref[rows, :] = out.astype(out_ref.dtype)

        b_last = b_col[chunk - 1:chunk, :]
        dec_row = b_last - b_row + i_row
        dec_col = b_last - b_col + i_col
        m_new = jnp.maximum(b_last + m_prev, jnp.max(dec_row, axis=1, keepdims=True))
        a = jnp.exp(b_last + m_prev - m_new)
        w_col = jnp.exp(dec_col - m_new)
        vw = (v.astype(F32) * w_col).astype(BF16)
        c_ref[...] = a * cmat + lax.dot_general(vw, k, (((0,), (0,)), ((), ())),
                                                preferred_element_type=F32)
        n_ref[...] = a * nvec + jnp.sum(w_col * k.astype(F32), axis=0, keepdims=True)
        m_ref[...] = jnp.broadcast_to(m_new, m_ref.shape)
        return f_prev + b_last

    lax.fori_loop(0, t // chunk, body, jnp.zeros((1, 1), F32))


def mlstm_heads(p32, qk_col, og_col, p16, v_col, gates, row_blk, n_seq, t, f_cum, gates_t, f_cum_t,
                conv_hist, w_conv, c0, n0, m0, g_head, i_off, f_off, into=None):
    b = n_seq
    _, n_h, dv, dk = c0.shape
    qb, ogb, vb = qk_col // dk, og_col // dv, v_col // dv
    conv_w = w_conv.shape[0]
    chunk = _tile(t, MLSTM_CHUNK)
    nch = t // chunk
    hpad = 8
    hist = jnp.pad(conv_hist, ((0, 0), (hpad - (conv_w - 1), 0), (0, 0)))
    gt = gates_t.reshape(b, LANES, nch, chunk)
    ft = f_cum_t.reshape(b, LANES, nch, chunk)
    n0r = n0.reshape(b, n_h, 1, dk)
    m0r = jnp.broadcast_to(m0[:, :, None, None], (b, n_h, 1, LANES))
    ghr = g_head.reshape(n_h, 1, dv)

    rowcol = lambda width, off: pl.BlockSpec((t, width), lambda i, h: (row_blk + i, off + h))
    hcol = lambda off: pl.BlockSpec((None, hpad, dk), lambda i, h: (i, 0, off + h))
    in_specs = [
        rowcol(dk, qb), rowcol(dk, qb + n_h), hcol(0), hcol(n_h),
        pl.BlockSpec((conv_w, dk), lambda i, h: (0, h)),
        pl.BlockSpec((conv_w, dk), lambda i, h: (0, n_h + h)),
        rowcol(dv, vb), rowcol(dv, ogb),
        pl.BlockSpec((t, LANES), lambda i, h: (row_blk + i, 0)),
        pl.BlockSpec((t, LANES), lambda i, h: (i, 0)),
        pl.BlockSpec((None, None, nch, chunk), lambda i, h: (i, i_off + h, 0, 0)),
        pl.BlockSpec((None, None, nch, chunk), lambda i, h: (i, f_off + h, 0, 0)),
        pl.BlockSpec((None, None, dv, dk), lambda i, h: (i, h, 0, 0)),
        pl.BlockSpec((None, None, 1, dk), lambda i, h: (i, h, 0, 0)),
        pl.BlockSpec((None, None, 1, LANES), lambda i, h: (i, h, 0, 0)),
        pl.BlockSpec((None, 1, dv), lambda i, h: (h, 0, 0)),
    ]
    args = [p32, p32, hist, hist, w_conv, w_conv, p16, p32, gates, f_cum, gt, ft, c0, n0r, m0r, ghr]
    aliases = {}
    if into is not None:
        in_specs.append(pl.BlockSpec(memory_space=pl.ANY))
        args.append(into)
        aliases = {len(args) - 1: 0}
    out_specs = [
        pl.BlockSpec((t, dv), lambda i, h: (row_blk + i, h)),
        pl.BlockSpec((None, None, dv, dk), lambda i, h: (i, h, 0, 0)),
        pl.BlockSpec((None, None, 1, dk), lambda i, h: (i, h, 0, 0)),
        pl.BlockSpec((None, None, 1, LANES), lambda i, h: (i, h, 0, 0)),
    ]
    out_shape = [
        jax.ShapeDtypeStruct((p16.shape[0], n_h * dv), BF16),
        jax.ShapeDtypeStruct((b, n_h, dv, dk), F32),
        jax.ShapeDtypeStruct((b, n_h, 1, dk), F32),
        jax.ShapeDtypeStruct((b, n_h, 1, LANES), F32),
    ]
    out, c_new, n_new, m_new = pl.pallas_call(
        functools.partial(_mlstm_kernel, t=t, chunk=chunk, conv_w=conv_w, i_off=i_off, f_off=f_off,
                          k_scale=float(dk) ** -0.5, aliased=into is not None),
        grid=(b, n_h),
        in_specs=in_specs,
        out_specs=out_specs,
        out_shape=out_shape,
        input_output_aliases=aliases,
        scratch_shapes=[
            pltpu.VMEM((t + hpad, dk), F32), pltpu.VMEM((t + hpad, dk), F32),
            pltpu.VMEM((t, dk), BF16), pltpu.VMEM((t, dk), BF16),
        ],
        compiler_params=_params("parallel", "parallel"),
        name="mlstm_heads",
    )(*args)
    return out, c_new, n_new[:, :, 0, :], m_new[:, :, 0, 0]


def _outproj_kernel(f_ref, m_ref, w1_ref, w2_ref, xa_ref, xb_ref, o_ref, *, n_first):
    acc = jnp.dot(f_ref[...], w1_ref[...], preferred_element_type=F32)
    acc = acc + jnp.dot(m_ref[...], w2_ref[...], preferred_element_type=F32)

    @pl.when(pl.program_id(0) < n_first)
    def _():
        o_ref[...] = xa_ref[...] + acc

    @pl.when(pl.program_id(0) >= n_first)
    def _():
        o_ref[...] = xb_ref[...] + acc


def out_projection(fox, ml, w_out, xa, xb, tm):
    (ma, d), mb = xa.shape, xb.shape[0]
    half = d // 2
    tn = _tile(d, 512)
    assert ma % tm == 0
    n_first = ma // tm
    first_j = lambda i, j: (jnp.minimum(i, n_first - 1), j)
    second_j = lambda i, j: (jnp.maximum(i - n_first, 0), j)
    return pl.pallas_call(
        functools.partial(_outproj_kernel, n_first=n_first),
        grid=(pl.cdiv(ma + mb, tm), d // tn),
        in_specs=[
            pl.BlockSpec((tm, half), lambda i, j: (i, 0)), pl.BlockSpec((tm, half), lambda i, j: (i, 0)),
            pl.BlockSpec((half, tn), lambda i, j: (0, j)),
            pl.BlockSpec((half, tn), lambda i, j: (1, j)),
            pl.BlockSpec((tm, tn), first_j), pl.BlockSpec((tm, tn), second_j),
        ],
        out_specs=pl.BlockSpec((tm, tn), lambda i, j: (i, j)),
        out_shape=jax.ShapeDtypeStruct((ma + mb, d), F32),
        compiler_params=_params("arbitrary", "arbitrary"),
        name="out_projection",
    )(fox, ml, w_out, w_out, xa, xb)


def _pack_bf16_pairs(x):
    half = x.shape[1] // 2
    bits = lax.bitcast_convert_type(x.astype(BF16).astype(F32), jnp.uint32)
    return (bits[:, :half] >> 16) | (bits[:, half:] & jnp.uint32(0xFFFF0000))


def _unpack_pairs_f32(w):
    lo = lax.bitcast_convert_type(w << 16, F32)
    hi = lax.bitcast_convert_type(w & jnp.uint32(0xFFFF0000), F32)
    return lo, hi


def _unpack_bf16_pairs(w):
    lo, hi = _unpack_pairs_f32(w)
    return lo.astype(BF16), hi.astype(BF16)


def _router_kernel(x_ref, g_ref, w_ref, b_ref, xn_ref, eid_ref, gate_ref, *, n_groups, per_group):
    x = x_ref[...]
    xn = x * lax.rsqrt(jnp.mean(x * x, axis=-1, keepdims=True) + EPS) * g_ref[...]
    xn_ref[...] = _pack_bf16_pairs(xn)
    logits = jnp.dot(xn, w_ref[...], precision=HIGHEST, preferred_element_type=F32) + b_ref[...]
    lane = lax.broadcasted_iota(jnp.int32, logits.shape, 1).astype(F32)
    big = float(LANES)
    is_g = lane < n_groups
    gl = jnp.where(is_g, logits, NEG_INF)
    gmax = jnp.max(gl, axis=1, keepdims=True)
    g_idx = jnp.min(jnp.where(gl == gmax, lane, big), axis=1, keepdims=True)
    g_prob = 1.0 / jnp.sum(jnp.where(is_g, jnp.exp(logits - gmax), 0.0), axis=1, keepdims=True)
    lo = n_groups + per_group * g_idx
    el = jnp.where((lane >= lo) & (lane < lo + per_group), logits, NEG_INF)
    e1 = jnp.max(el, axis=1, keepdims=True)
    i1 = jnp.min(jnp.where(el == e1, lane, big), axis=1, keepdims=True)
    el2 = jnp.where(lane == i1, NEG_INF, el)
    e2 = jnp.max(el2, axis=1, keepdims=True)
    i2 = jnp.min(jnp.where(el2 == e2, lane, big), axis=1, keepdims=True)
    r = jnp.exp(e2 - e1)
    w1 = g_prob / (1.0 + r)
    w2 = g_prob * r / (1.0 + r)
    eid = jnp.where(lane == 0.0, i1 - n_groups, jnp.where(lane == 1.0, i2 - n_groups, 0.0))
    eid_ref[...] = eid.astype(jnp.int32)
    gate_ref[...] = jnp.where(lane == 0.0, w1, jnp.where(lane == 1.0, w2, 0.0))


def router(x, g, w_router, b_router, n_groups, per_group):
    m, d = x.shape
    tm = _tile(m, 256)
    return pl.pallas_call(
        functools.partial(_router_kernel, n_groups=n_groups, per_group=per_group),
        grid=(m // tm,),
        in_specs=[
            pl.BlockSpec((tm, d), lambda i: (i, 0)),
            pl.BlockSpec((1, d), lambda i: (0, 0)),
            pl.BlockSpec((d, LANES), lambda i: (0, 0)),
            pl.BlockSpec((1, LANES), lambda i: (0, 0)),
        ],
        out_specs=[
            pl.BlockSpec((tm, d // 2), lambda i: (i, 0)),
            pl.BlockSpec((tm, LANES), lambda i: (i, 0)),
            pl.BlockSpec((tm, LANES), lambda i: (i, 0)),
        ],
        out_shape=[
            jax.ShapeDtypeStruct((m, d // 2), jnp.uint32),
            jax.ShapeDtypeStruct((m, LANES), jnp.int32),
            jax.ShapeDtypeStruct((m, LANES), F32),
        ],
        compiler_params=_params("parallel"),
        name="router",
    )(x, g.reshape(1, d), w_router, b_router)


def _gather_kernel(idx0_ref, idx1_ref, src_ref, o_ref, buf_ref, sem_ref, *, tm):
    step = pl.program_id(0)
    nsteps = pl.num_programs(0)

    def issue(idx_ref, slot):
        def body(r, carry):
            for q, row in enumerate((r, r + tm // 2)):
                tok = idx_ref[0, row]
                pltpu.make_async_copy(src_ref.at[pl.ds(tok, 1), :], buf_ref.at[slot, pl.ds(row, 1), :],
                                      sem_ref.at[slot]).start(priority=q)
            return carry
        lax.fori_loop(0, tm // 2, body, 0, unroll=DMA_ISSUE_UNROLL // 2)

    @pl.when(step == 0)
    def _():
        issue(idx0_ref, 0)

    @pl.when(step + 1 < nsteps)
    def _():
        issue(idx1_ref, (step + 1) % 2)

    slot = step % 2
    pltpu.make_async_copy(src_ref.at[pl.ds(0, tm), :], buf_ref.at[slot], sem_ref.at[slot]).wait()
    half = buf_ref.shape[2]
    lo, hi = _unpack_bf16_pairs(buf_ref[slot])
    o_ref[:, :half] = lo
    o_ref[:, half:] = hi


def _smem_tiles(n_steps, width):
    cur = pl.BlockSpec((None, 1, width), lambda i: (i, 0, 0), memory_space=pltpu.SMEM)
    nxt = pl.BlockSpec((None, 1, width), lambda i: (jnp.minimum(i + 1, n_steps - 1), 0, 0),
                       memory_space=pltpu.SMEM)
    return cur, nxt


def gather_rows(src, idx, tm):
    n, half = src.shape
    d = 2 * half
    a = idx.shape[0]
    n_steps = a // tm
    cur, nxt = _smem_tiles(n_steps, tm)
    idx3 = idx.reshape(n_steps, 1, tm)
    return pl.pallas_call(
        functools.partial(_gather_kernel, tm=tm),
        grid=(n_steps,),
        in_specs=[cur, nxt, pl.BlockSpec(memory_space=pl.ANY)],
        out_specs=pl.BlockSpec((tm, d), lambda i: (i, 0)),
        scratch_shapes=[pltpu.VMEM((2, tm, half), src.dtype), pltpu.SemaphoreType.DMA((2,))],
        out_shape=jax.ShapeDtypeStruct((a, d), BF16),
        compiler_params=_params("arbitrary"),
        name="gather_rows",
    )(idx3, idx3, src)


def _expert_up_kernel(te_ref, nv_ref, x_ref, wg_ref, wu_ref, h_ref):
    t = pl.program_id(1)

    @pl.when(t < nv_ref[0])
    def _():
        x = x_ref[...]
        g = jnp.dot(x, wg_ref[...].astype(BF16), preferred_element_type=F32)
        u = jnp.dot(x, wu_ref[...].astype(BF16), preferred_element_type=F32)
        h_ref[...] = (g * jax.nn.sigmoid(g) * u).astype(h_ref.dtype)

    @pl.when(t >= nv_ref[0])
    def _():
        h_ref[...] = jnp.zeros_like(h_ref)


def expert_up(xg, w_gate, w_up, tile_expert, n_valid, tm):
    a, d = xg.shape
    _, _, f = w_gate.shape
    tn = _tile(f, 512)
    return pl.pallas_call(
        _expert_up_kernel,
        grid_spec=pltpu.PrefetchScalarGridSpec(
            num_scalar_prefetch=2,
            grid=(f // tn, a // tm),
            in_specs=[
                pl.BlockSpec((tm, d), lambda c, t, te, nv: (t, 0)),
                pl.BlockSpec((None, d, tn), lambda c, t, te, nv: (te[t], 0, c)),
                pl.BlockSpec((None, d, tn), lambda c, t, te, nv: (te[t], 0, c)),
            ],
            out_specs=pl.BlockSpec((tm, tn), lambda c, t, te, nv: (t, c)),
        ),
        out_shape=jax.ShapeDtypeStruct((a, f), BF16),
        compiler_params=_params("arbitrary", "arbitrary"),
        name="expert_up",
    )(tile_expert, n_valid, xg, w_gate, w_up)


def _expert_down_kernel(te_ref, nv_ref, h_ref, wd_ref, y_ref):
    t = pl.program_id(1)

    @pl.when(t < nv_ref[0])
    def _():
        y = jnp.dot(h_ref[...], wd_ref[...].astype(BF16), preferred_element_type=F32)
        y_ref[...] = _pack_bf16_pairs(y)

    @pl.when(t >= nv_ref[0])
    def _():
        y_ref[...] = jnp.zeros_like(y_ref)


def expert_down(hg, w_down, tile_expert, n_valid, tm):
    a, f = hg.shape
    _, _, d = w_down.shape
    tn = _tile(d, DOWN_CHUNK)
    return pl.pallas_call(
        _expert_down_kernel,
        grid_spec=pltpu.PrefetchScalarGridSpec(
            num_scalar_prefetch=2,
            grid=(d // tn, a // tm),
            in_specs=[
                pl.BlockSpec((tm, f), lambda c, t, te, nv: (t, 0)),
                pl.BlockSpec((None, f, tn), lambda c, t, te, nv: (te[t], 0, c)),
            ],
            out_specs=pl.BlockSpec((tm, tn // 2), lambda c, t, te, nv: (t, c)),
        ),
        out_shape=jax.ShapeDtypeStruct((a, d // 2), jnp.uint32),
        compiler_params=_params("arbitrary", "arbitrary"),
        name="expert_down",
    )(tile_expert, n_valid, hg, w_down)


def _combine_kernel(pos0_ref, pos1_ref, x_ref, gate_ref, g_ref, y_ref, oa_ref, ob_ref, buf_ref, sem_ref,
                    *, tm, n_first, chunk):
    step = pl.program_id(0)
    nsteps = pl.num_programs(0)

    def issue(pos_ref, slot):
        def body(r, carry):
            for k in range(TOP_K):
                p = pos_ref[0, r * TOP_K + k]
                pltpu.make_async_copy(y_ref.at[pl.ds(p, 1), :], buf_ref.at[slot, k, pl.ds(r, 1), :],
                                      sem_ref.at[slot]).start(priority=k)
            return carry
        lax.fori_loop(0, tm, body, 0, unroll=DMA_ISSUE_UNROLL)

    @pl.when(step == 0)
    def _():
        issue(pos0_ref, 0)

    @pl.when(step + 1 < nsteps)
    def _():
        issue(pos1_ref, (step + 1) % 2)

    slot = step % 2
    for k in range(TOP_K):
        pltpu.make_async_copy(y_ref.at[pl.ds(0, tm), :], buf_ref.at[slot, k], sem_ref.at[slot]).wait()
    gates = gate_ref[...]
    x = x_ref[...]
    for k in range(TOP_K):
        lo, hi = _unpack_pairs_f32(buf_ref[slot, k])
        pieces = []
        for c in range(0, lo.shape[1], chunk // 2):
            pieces += [lo[:, c:c + chunk // 2], hi[:, c:c + chunk // 2]]
        x = x + gates[:, k:k + 1] * jnp.concatenate(pieces, axis=1)
    y = x * lax.rsqrt(jnp.mean(x * x, axis=-1, keepdims=True) + EPS) * g_ref[...]

    @pl.when(step < n_first)
    def _():
        oa_ref[...] = y

    @pl.when(step >= n_first)
    def _():
        ob_ref[...] = y


def combine_norm(x, gates, pos, yg, g_final, m_first, tm):
    m, d = x.shape
    n_steps = m // tm
    n_first = m_first // tm
    cur, nxt = _smem_tiles(n_steps, tm * TOP_K)
    pos3 = pos.reshape(n_steps, 1, tm * TOP_K)
    first, second = _split_maps(n_first)
    return pl.pallas_call(
        functools.partial(_combine_kernel, tm=tm, n_first=n_first, chunk=_tile(d, DOWN_CHUNK)),
        grid=(n_steps,),
        in_specs=[
            cur, nxt,
            pl.BlockSpec((tm, d), lambda i: (i, 0)),
            pl.BlockSpec((tm, LANES), lambda i: (i, 0)),
            pl.BlockSpec((1, d), lambda i: (0, 0)),
            pl.BlockSpec(memory_space=pl.ANY),
        ],
        out_specs=[pl.BlockSpec((tm, d), first), pl.BlockSpec((tm, d), second)],
        scratch_shapes=[pltpu.VMEM((2, TOP_K, tm, d // 2), jnp.uint32), pltpu.SemaphoreType.DMA((2,))],
        out_shape=[jax.ShapeDtypeStruct((m_first, d), F32), jax.ShapeDtypeStruct((m - m_first, d), F32)],
        compiler_params=_params("arbitrary"),
        name="combine_norm",
    )(pos3, pos3, x, gates, g_final.reshape(1, d), yg)


def _dispatch_plan(eid, n_experts, tm):
    n = eid.shape[0]
    a = n * TOP_K
    e_flat = eid.reshape(a)
    onehot = (e_flat[:, None] == jnp.arange(n_experts, dtype=jnp.int32)[None, :]).astype(jnp.int32)
    csum = jnp.cumsum(onehot, axis=0)
    counts = csum[-1]
    rank = jnp.take_along_axis(csum, e_flat[:, None], axis=1)[:, 0] - 1
    padded = ((counts + tm - 1) // tm) * tm
    pad_end = jnp.cumsum(padded)
    pos = (pad_end - padded)[e_flat] + rank
    n_tiles = a // tm + n_experts
    src = jnp.zeros((n_tiles * tm,), jnp.int32).at[pos].set(jnp.arange(a, dtype=jnp.int32) // TOP_K)
    tile_end = pad_end // tm
    tile_ids = jnp.arange(n_tiles, dtype=jnp.int32)
    tile_expert = jnp.sum((tile_ids[:, None] >= tile_end[None, :]).astype(jnp.int32), axis=1)
    tile_expert = jnp.minimum(tile_expert, n_experts - 1)
    n_valid = tile_end[-1:].astype(jnp.int32)
    return pos.astype(jnp.int32), src, tile_expert, n_valid


def moe_layer(x, m_first, g_ffn, w_router, b_router, w_gate, w_up, w_down, g_final, n_groups, per_group):
    m, d = x.shape
    n_experts = n_groups * per_group
    xn, eid, gates = router(x, g_ffn, w_router, b_router, n_groups, per_group)
    a = m * TOP_K
    tm = _tile(a, min(512, max(64, a // n_experts)))
    pos, src, tile_expert, n_valid = _dispatch_plan(eid[:, :TOP_K], n_experts, tm)
    xg = gather_rows(xn, src, _tile(tm, 256))
    hg = expert_up(xg, w_gate, w_up, tile_expert, n_valid, tm)
    yg = expert_down(hg, w_down, tile_expert, n_valid, tm)
    return combine_norm(x, gates, pos, yg, g_final, m_first, _tile(math.gcd(m_first, m - m_first), 128))


def kernel(x_prompt, x_sample, cache_fox_k, cache_fox_v, cache_fox_logf, state_mlstm_C, state_mlstm_n, state_mlstm_m, state_mlstm_conv, g_norm_mix, w_in, b_fox_f, b_mlstm_i, b_mlstm_f, w_conv, g_mlstm_head, w_out, g_norm_ffn, w_router_group, b_router_group, w_router_expert, b_router_expert, w_exp_gate, w_exp_up, w_exp_down, g_norm_final):
    depth = w_in.shape[0]
    assert depth == 1, "the final norm is fused into the last layer's MoE combine; one layer supported"
    bp, tp, d = x_prompt.shape
    bs, ts, _ = x_sample.shape
    n_fox, dh = cache_fox_k.shape[-2:]
    n_ml, dv, dk = state_mlstm_C.shape[-3:]
    past = cache_fox_k.shape[2]
    n_groups = w_router_group.shape[-1]
    n_experts = w_router_expert.shape[-1]
    conv_w = w_conv.shape[1]
    half = d // 2
    mp, ms = bp * tp, bs * ts
    assert n_fox * dh == half and n_ml * dv == half and 2 * n_ml * dk == half
    assert n_fox + 2 * n_ml <= LANES and n_groups + n_experts <= LANES
    assert mp % ts == 0 and ts >= conv_w - 1, "sample sequences are addressed as row blocks after the prompt rows"
    l = 0
    sizes = (half, half, half, n_fox, half, half, half, n_ml, n_ml)
    offs = [0]
    for s in sizes:
        offs.append(offs[-1] + s)
    col = lambda i: w_in[l][:, offs[i]:offs[i + 1]]
    w_a = w_in[l][:, offs[0]:offs[3]].astype(BF16)
    w_b = w_in[l][:, offs[4]:offs[7]].astype(BF16)
    n_gate = n_fox + 2 * n_ml
    w_gatecols = jnp.pad(jnp.concatenate([col(3), col(7), col(8)], axis=1), ((0, 0), (0, LANES - n_gate))).astype(BF16)
    b_gatecols = jnp.pad(jnp.concatenate([b_fox_f[l], b_mlstm_i[l], b_mlstm_f[l]]).astype(F32),
                         (0, LANES - n_gate)).reshape(1, LANES)
    n_r = n_groups + n_experts
    w_router = jnp.pad(jnp.concatenate([w_router_group[l], w_router_expert[l]], axis=1), ((0, 0), (0, LANES - n_r)))
    b_router = jnp.pad(jnp.concatenate([b_router_group[l], b_router_expert[l]]).astype(F32),
                       (0, LANES - n_r)).reshape(1, LANES)
    i_off, f_off = n_fox, n_fox + n_ml
    col32 = {s: k * half for k, s in enumerate(P32_SEGS)}
    col16 = {s: k * half for k, s in enumerate(P16_SEGS)}

    xp2, xs2 = x_prompt.reshape(mp, d), x_sample.reshape(ms, d)
    tm_rows = _tile(mp, 1024)
    xn = rmsnorm_cast(xp2, xs2, g_norm_mix[l], BF16)
    p32, p16, gates = in_projection(xn, w_a, w_b, w_gatecols, b_gatecols, n_fox, n_ml, tm_rows)

    f_p = cumsum_time(gates, bp, tp)
    f_p3 = f_p.reshape(bp, tp, LANES)
    g_p3 = gates[:mp].reshape(bp, tp, LANES)
    fox_p = fox_attention(p16, col16[SEG_FQ], p32, col32[SEG_FK], col32[SEG_FV], 0, bp, tp,
                          f_p3, jnp.swapaxes(f_p3[:, :, :n_fox], 1, 2)[:, :, None, :], n_fox, dh,
                          into=jnp.zeros((mp + ms, half), BF16))
    hist_p = jnp.zeros((bp, conv_w - 1, half), F32)
    ml_p, c_p, n_p, m_p = mlstm_heads(
        p32, col32[SEG_QK], col32[SEG_MO], p16, col16[SEG_MV], gates, 0, bp, tp, f_p,
        jnp.swapaxes(g_p3, 1, 2), jnp.swapaxes(f_p3, 1, 2), hist_p, w_conv[l],
        jnp.zeros((bp, n_ml, dv, dk), F32), jnp.zeros((bp, n_ml, dk), F32), jnp.zeros((bp, n_ml), F32),
        g_mlstm_head[l], i_off, f_off, into=jnp.zeros((mp + ms, half), BF16))

    g_s3 = gates[mp:].reshape(bs, ts, LANES)
    cache_pad = jnp.pad(cache_fox_logf[l].astype(F32), ((0, 0), (0, 0), (0, LANES - n_fox)))
    logf_s = jnp.concatenate([cache_pad, g_s3], axis=1)
    f_s3 = cumsum_time(logf_s.reshape(bs * (past + ts), LANES), bs, past + ts).reshape(bs, past + ts, LANES)
    fox = fox_attention(p16, col16[SEG_FQ], p32, col32[SEG_FK], col32[SEG_FV], mp // ts, bs, ts,
                        f_s3, jnp.swapaxes(f_s3[:, :, :n_fox], 1, 2)[:, :, None, :], n_fox, dh,
                        cache_fox_k[l].reshape(bs, past, half), cache_fox_v[l].reshape(bs, past, half),
                        into=fox_p)
    fn_s = cumsum_time(g_s3.reshape(ms, LANES), bs, ts)
    ml, c_s, n_s, m_s = mlstm_heads(
        p32, col32[SEG_QK], col32[SEG_MO], p16, col16[SEG_MV], gates, mp // ts, bs, ts, fn_s,
        jnp.swapaxes(g_s3, 1, 2), jnp.swapaxes(fn_s.reshape(bs, ts, LANES), 1, 2), state_mlstm_conv[l], w_conv[l],
        state_mlstm_C[l], state_mlstm_n[l], state_mlstm_m[l], g_mlstm_head[l], i_off, f_off, into=ml_p)

    x1 = out_projection(fox, ml, w_out[l].astype(BF16), xp2, xs2, tm_rows)
    y_p, y_s = moe_layer(x1, mp, g_norm_ffn[l], w_router, b_router, w_exp_gate[l], w_exp_up[l], w_exp_down[l],
                         g_norm_final, n_groups, n_experts // n_groups)

    def new_state(rows, b, t, g3, c_new, n_new, m_new, conv_hist):
        blk = p32[rows]
        fk = blk[:, col32[SEG_FK]:col32[SEG_FK] + half].reshape(b, t, n_fox, dh)
        fv = blk[:, col32[SEG_FV]:col32[SEG_FV] + half].reshape(b, t, n_fox, dh)
        qk_tail = blk[:, col32[SEG_QK]:col32[SEG_QK] + half].reshape(b, t, half)[:, t - min(t, conv_w - 1):]
        qk_hist = jnp.concatenate([conv_hist.astype(F32), qk_tail], axis=1)[:, -(conv_w - 1):]
        return tuple(a[None] for a in (fk, fv, g3[:, :, :n_fox], c_new, n_new, m_new, qk_hist))

    st_p = new_state(slice(0, mp), bp, tp, g_p3, c_p, n_p, m_p, hist_p)
    st_s = new_state(slice(mp, mp + ms), bs, ts, g_s3, c_s, n_s, m_s, state_mlstm_conv[l])
    return (y_p.reshape(bp, tp, d), y_s.reshape(bs, ts, d)) + st_p + st_s
```

```python
import functools
import math

import jax
import jax.numpy as jnp
from jax import lax
from jax.experimental import pallas as pl
from jax.experimental.pallas import tpu as pltpu

F32 = jnp.float32
BF16 = jnp.bfloat16
EPS = 1e-6
LANES = 128
V7X_VMEM_LIMIT = 56 * 1024 * 1024
TOP_K = 2
MLSTM_CHUNK = 256
DMA_ISSUE_UNROLL = 8
FOX_KEY_BLOCK = 256
DOWN_CHUNK = 2048
HIGHEST = lax.Precision.HIGHEST
NEG_INF = float("-inf")
LOG2E = math.log2(math.e)


def _params(*sem):
    return pltpu.CompilerParams(dimension_semantics=sem, vmem_limit_bytes=V7X_VMEM_LIMIT)


def _tile(n, pref):
    if n <= pref:
        return n
    for t in range(pref - pref % 8, 7, -8):
        if n % t == 0:
            return t
    return n


def _log_sigmoid(x):
    return jnp.minimum(x, 0.0) - jnp.log1p(jnp.exp(-jnp.abs(x)))


def _split_maps(n_first):
    first = lambda i, *_: (jnp.minimum(i, n_first - 1), 0)
    second = lambda i, *_: (jnp.maximum(i - n_first, 0), 0)
    return first, second


def _rmsnorm_kernel(xa_ref, xb_ref, g_ref, o_ref, *, n_first):
    def emit(x_ref):
        x = x_ref[...]
        y = x * lax.rsqrt(jnp.mean(x * x, axis=-1, keepdims=True) + EPS)
        o_ref[...] = (y * g_ref[...]).astype(o_ref.dtype)

    @pl.when(pl.program_id(0) < n_first)
    def _():
        emit(xa_ref)

    @pl.when(pl.program_id(0) >= n_first)
    def _():
        emit(xb_ref)


def rmsnorm_cast(xa, xb, g, out_dtype):
    (ma, d), mb = xa.shape, xb.shape[0]
    tm = _tile(math.gcd(ma, mb), 256)
    first, second = _split_maps(ma // tm)
    return pl.pallas_call(
        functools.partial(_rmsnorm_kernel, n_first=ma // tm),
        grid=((ma + mb) // tm,),
        in_specs=[pl.BlockSpec((tm, d), first), pl.BlockSpec((tm, d), second),
                  pl.BlockSpec((1, d), lambda i: (0, 0))],
        out_specs=pl.BlockSpec((tm, d), lambda i: (i, 0)),
        out_shape=jax.ShapeDtypeStruct((ma + mb, d), out_dtype),
        compiler_params=_params("arbitrary"),
        name="rmsnorm_cast",
    )(xa, xb, g.reshape(1, d))


SEG_FQ, SEG_FK, SEG_FV, SEG_QK, SEG_MV, SEG_MO = range(6)
P32_SEGS = (SEG_FK, SEG_FV, SEG_QK, SEG_MO)
P16_SEGS = (SEG_FQ, SEG_MV)


def _inproj_kernel(x_ref, wa_ref, wb_ref, ws_ref, b_ref, p32_ref, p16_ref, gate_ref, *, nseg, n_fox, n_ml):
    j = pl.program_id(1)
    seg = j // nseg

    def emit(w_ref):
        acc = jnp.dot(x_ref[...], w_ref[...], preferred_element_type=F32)
        is16 = (seg == SEG_FQ) | (seg == SEG_MV)

        @pl.when(is16)
        def _():
            p16_ref[...] = acc.astype(p16_ref.dtype)

        @pl.when(jnp.logical_not(is16))
        def _():
            p32_ref[...] = acc

    @pl.when(seg < 3)
    def _():
        emit(wa_ref)

    @pl.when(seg >= 3)
    def _():
        emit(wb_ref)

    @pl.when(j == 0)
    def _():
        p = jnp.dot(x_ref[...], ws_ref[...], preferred_element_type=F32) + b_ref[...]
        lane = lax.broadcasted_iota(jnp.int32, p.shape, 1)
        is_id = (lane >= n_fox) & (lane < n_fox + n_ml)
        gate_ref[...] = jnp.where(is_id, p, _log_sigmoid(p))


def _held_block(j, nseg, segs):
    seg = j // nseg
    blk = jnp.int32(0)
    for k, s in enumerate(segs):
        here = k * nseg + (j - s * nseg)
        done = (k + 1) * nseg - 1
        blk = jnp.where(seg == s, here, jnp.where(seg > s, done, blk))
    return blk


def in_projection(xn, w_a, w_b, w_gate, b_gate, n_fox, n_ml, tm):
    m, d = xn.shape
    w = w_a.shape[1] // 3
    tn = _tile(w, 512)
    nseg = w // tn
    return pl.pallas_call(
        functools.partial(_inproj_kernel, nseg=nseg, n_fox=n_fox, n_ml=n_ml),
        grid=(pl.cdiv(m, tm), 6 * nseg),
        in_specs=[
            pl.BlockSpec((tm, d), lambda i, j: (i, 0)),
            pl.BlockSpec((d, tn), lambda i, j: (0, jnp.minimum(j, 3 * nseg - 1))),
            pl.BlockSpec((d, tn), lambda i, j: (0, jnp.maximum(j - 3 * nseg, 0))),
            pl.BlockSpec((d, LANES), lambda i, j: (0, 0)),
            pl.BlockSpec((1, LANES), lambda i, j: (0, 0)),
        ],
        out_specs=[
            pl.BlockSpec((tm, tn), lambda i, j: (i, _held_block(j, nseg, P32_SEGS))),
            pl.BlockSpec((tm, tn), lambda i, j: (i, _held_block(j, nseg, P16_SEGS))),
            pl.BlockSpec((tm, LANES), lambda i, j: (i, 0)),
        ],
        out_shape=[
            jax.ShapeDtypeStruct((m, len(P32_SEGS) * w), F32),
            jax.ShapeDtypeStruct((m, len(P16_SEGS) * w), BF16),
            jax.ShapeDtypeStruct((m, LANES), F32),
        ],
        compiler_params=_params("arbitrary", "arbitrary"),
        name="in_projection",
    )(xn, w_a, w_b, w_gate, b_gate)


def _cumsum_kernel(g_ref, f_ref, *, chunk):
    s = g_ref.shape[0]
    r = lax.broadcasted_iota(jnp.int32, (chunk, chunk), 0)
    c = lax.broadcasted_iota(jnp.int32, (chunk, chunk), 1)
    tri = (c <= r).astype(F32)
    carry = jnp.zeros((1, g_ref.shape[1]), F32)
    for k in range(s // chunk):
        blk = g_ref[k * chunk:(k + 1) * chunk, :]
        loc = jnp.dot(tri, blk, precision=HIGHEST, preferred_element_type=F32)
        f_ref[k * chunk:(k + 1) * chunk, :] = loc + carry
        carry = carry + loc[chunk - 1:chunk, :]


def cumsum_time(g, n_seq, s):
    n = g.shape[1]
    chunk = next(c for c in (256, 128, 64, 32, 16, 8) if s % c == 0)
    return pl.pallas_call(
        functools.partial(_cumsum_kernel, chunk=chunk),
        grid=(n_seq,),
        in_specs=[pl.BlockSpec((s, n), lambda i: (i, 0))],
        out_specs=pl.BlockSpec((s, n), lambda i: (i, 0)),
        out_shape=jax.ShapeDtypeStruct((n_seq * s, n), F32),
        compiler_params=_params("parallel"),
        name="cumsum_time",
    )(g)


def _fox_kernel(*refs, past, t, tq, scale, aliased):
    o_ref, kall_ref, vall_ref, lg_ref, p_ref = refs[-5:]
    ins = refs[:-6] if aliased else refs[:-5]
    if past:
        q_ref, k_ref, v_ref, pk_ref, pv_ref, fcol_ref = ins
    else:
        q_ref, k_ref, v_ref, fcol_ref = ins
    h = pl.program_id(1)
    dh = q_ref.shape[1]
    fc = fcol_ref[...]
    lane = lax.broadcasted_iota(jnp.int32, fc.shape, 1)
    f_sc = jnp.sum(jnp.where(lane == h, fc, 0.0), axis=1, keepdims=True) * (1.0 / scale)
    pieces = []
    rest = f_sc
    for _ in range(3):
        piece = rest.astype(BF16).astype(F32)
        pieces.append(piece)
        rest = rest - piece

    def bias_cols(first, sign):
        cols = jnp.where((lane >= 3 - first) & (lane < 6 - first), 1.0, 0.0)
        for j, piece in enumerate(pieces):
            cols = jnp.where(lane == first + j, sign * piece, cols)
        return cols.astype(BF16)

    q_bias = bias_cols(0, 1.0)
    kall_ref[:, dh:] = bias_cols(3, -1.0)
    kall_ref[past:, :dh] = k_ref[...].astype(BF16)
    vall_ref[:, dh:] = jnp.where(lane == 0, 1.0, 0.0).astype(BF16)
    vall_ref[past:, :dh] = v_ref[...].astype(BF16)
    if past:
        kall_ref[:past, :dh] = pk_ref[...].astype(BF16)
        vall_ref[:past, :dh] = pv_ref[...].astype(BF16)
    dn_t = (((1,), (1,)), ((), ()))
    row = lax.broadcasted_iota(jnp.int32, (tq, tq), 0)
    col = lax.broadcasted_iota(jnp.int32, (tq, tq), 1)
    diag_mask = col <= row
    for qi in range(t // tq):
        lo = qi * tq
        slot = qi % 2
        n_vis = past + lo
        q = jnp.concatenate([q_ref[lo:lo + tq, :], q_bias[n_vis:n_vis + tq, :]], axis=1)
        blocks = [(c0, min(FOX_KEY_BLOCK, n_vis - c0), False) for c0 in range(0, n_vis, FOX_KEY_BLOCK)]
        blocks.append((n_vis, tq, True))
        mx = None
        for c0, w, masked in blocks:
            lg = lax.dot_general(q, kall_ref[c0:c0 + w, :], dn_t, preferred_element_type=F32) * (scale * LOG2E)
            if masked:
                lg = jnp.where(diag_mask, lg, NEG_INF)
            lg_ref[slot, :, c0:c0 + w] = lg
            bmax = jnp.max(lg, axis=1, keepdims=True)
            mx = bmax if mx is None else jnp.maximum(mx, bmax)
        for c0, w, _ in blocks:
            p_ref[slot, :, c0:c0 + w] = jnp.exp2(lg_ref[slot, :, c0:c0 + w] - mx).astype(BF16)
        n_keys = n_vis + tq
        acc = jnp.dot(p_ref[slot, :, 0:n_keys], vall_ref[0:n_keys, :], preferred_element_type=F32)
        o_ref[lo:lo + tq, :] = (acc[:, :dh] / acc[:, dh:dh + 1]).astype(o_ref.dtype)


def fox_attention(p16, q_col, p32, k_col, v_col, row_blk, n_seq, t, f_col, n_heads, dh,
                  past_k=None, past_v=None, into=None):
    assert dh == LANES
    past = 0 if past_k is None else past_k.shape[1]
    tq = _tile(t, 256)

    def head_blk(off):
        return pl.BlockSpec((t, dh), lambda i, h: (row_blk + i, off // dh + h))

    in_specs = [head_blk(q_col), head_blk(k_col), head_blk(v_col)]
    args = [p16, p32, p32]
    if past:
        past_blk = pl.BlockSpec((None, past, dh), lambda i, h: (i, 0, h))
        in_specs += [past_blk, past_blk]
        args += [past_k, past_v]
    in_specs.append(pl.BlockSpec((None, past + t, LANES), lambda i, h: (i, 0, 0)))
    args.append(f_col)
    aliases = {}
    if into is not None:
        in_specs.append(pl.BlockSpec(memory_space=pl.ANY))
        args.append(into)
        aliases = {len(args) - 1: 0}
    return pl.pallas_call(
        functools.partial(_fox_kernel, past=past, t=t, tq=tq, scale=float(dh) ** -0.5,
                          aliased=into is not None),
        grid=(n_seq, n_heads),
        in_specs=in_specs,
        out_specs=pl.BlockSpec((t, dh), lambda i, h: (row_blk + i, h)),
        out_shape=jax.ShapeDtypeStruct((p16.shape[0], n_heads * dh), BF16),
        scratch_shapes=[
            pltpu.VMEM((past + t, 2 * dh), BF16), pltpu.VMEM((past + t, 2 * dh), BF16),
            pltpu.VMEM((2, tq, past + t), F32), pltpu.VMEM((2, tq, past + t), BF16),
        ],
        input_output_aliases=aliases,
        compiler_params=_params("parallel", "parallel"),
        name="fox_attention",
    )(*args)


def _mlstm_kernel(*refs, t, chunk, conv_w, i_off, f_off, k_scale, aliased):
    if aliased:
        refs = refs[:16] + refs[17:]
    (qraw_ref, kraw_ref, hq_ref, hk_ref, wq_ref, wk_ref, v_ref, og_ref, g_ref, f_ref,
     irow_ref, frow_ref, c0_ref, n0_ref, m0_ref, gh_ref,
     out_ref, c_ref, n_ref, m_ref,
     histq_ref, histk_ref, qc_ref, kc_ref) = refs
    h = pl.program_id(1)
    hpad = hq_ref.shape[0]

    def conv_silu(raw_ref, hist_in_ref, w_ref, hist_ref, scale, dst_ref):
        hist_ref[0:hpad, :] = hist_in_ref[...]
        hist_ref[hpad:hpad + t, :] = raw_ref[...]
        y = None
        for j in range(conv_w):
            start = hpad - (conv_w - 1) + j
            term = hist_ref[start:start + t, :] * w_ref[j:j + 1, :]
            y = term if y is None else y + term
        y = y * jax.nn.sigmoid(y)
        if scale != 1.0:
            y = y * scale
        dst_ref[...] = y.astype(dst_ref.dtype)

    conv_silu(qraw_ref, hq_ref, wq_ref, histq_ref, 1.0, qc_ref)
    conv_silu(kraw_ref, hk_ref, wk_ref, histk_ref, k_scale, kc_ref)

    c_ref[...] = c0_ref[...]
    n_ref[...] = n0_ref[...]
    m_ref[...] = m0_ref[...]
    gh = gh_ref[...]
    rr = lax.broadcasted_iota(jnp.int32, (chunk, chunk), 0)
    cc = lax.broadcasted_iota(jnp.int32, (chunk, chunk), 1)
    causal = cc <= rr
    lane = lax.broadcasted_iota(jnp.int32, (chunk, LANES), 1)

    def body(c, f_prev):
        r0 = pl.multiple_of(c * chunk, chunk)
        rows = pl.ds(r0, chunk)
        i_col = jnp.sum(jnp.where(lane == i_off + h, g_ref[rows, :], 0.0), axis=1, keepdims=True)
        b_col = jnp.sum(jnp.where(lane == f_off + h, f_ref[rows, :], 0.0), axis=1, keepdims=True) - f_prev
        i_row = irow_ref[pl.ds(c, 1), :]
        b_row = frow_ref[pl.ds(c, 1), :] - f_prev
        m_prev = m_ref[:, 0:1]
        q = qc_ref[rows, :]
        k = kc_ref[rows, :]
        v = v_ref[rows, :]
        cmat = c_ref[...]
        nvec = n_ref[...]

        d = jnp.where(causal, b_col - b_row + i_row, NEG_INF)
        inter = b_col + m_prev
        m_t = jnp.maximum(inter, jnp.max(d, axis=1, keepdims=True))
        w_inter = jnp.exp(inter - m_t)
        qk = lax.dot_general(q, k, (((1,), (1,)), ((), ())), preferred_element_type=F32)
        s = qk * jnp.exp(d - m_t)
        qc_state = lax.dot_general(q, cmat.astype(BF16), (((1,), (1,)), ((), ())),
                                   preferred_element_type=F32)
        num = w_inter * qc_state + jnp.dot(s.astype(BF16), v, preferred_element_type=F32)
        qn = jnp.sum(q.astype(F32) * nvec, axis=1, keepdims=True)
        den = w_inter * qn + jnp.sum(s, axis=1, keepdims=True)
        hval = num / jnp.maximum(jnp.abs(den), jnp.exp(-m_t))

        hn = hval * lax.rsqrt(jnp.mean(hval * hval, axis=1, keepdims=True) + EPS)
        out = jax.nn.sigmoid(og_ref[rows, :]) * hn * gh
        out_ref[rows, :] = out.astype(out_ref.dtype)

        b_last = b_col[chunk - 1:chunk, :]
        dec_row = b_last - b_row + i_row
        dec_col = b_last - b_col + i_col
        m_new = jnp.maximum(b_last + m_prev, jnp.max(dec_row, axis=1, keepdims=True))
        a = jnp.exp(b_last + m_prev - m_new)
        w_col = jnp.exp(dec_col - m_new)
        vw = (v.astype(F32) * w_col).astype(BF16)
        c_ref[...] = a * cmat + lax.dot_general(vw, k, (((0,), (0,)), ((), ())),
                                                preferred_element_type=F32)
        n_ref[...] = a * nvec + jnp.sum(w_col * k.astype(F32), axis=0, keepdims=True)
        m_ref[...] = jnp.broadcast_to(m_new, m_ref.shape)
        return f_prev + b_last

    lax.fori_loop(0, t // chunk, body, jnp.zeros((1, 1), F32))


def mlstm_heads(p32, qk_col, og_col, p16, v_col, gates, row_blk, n_seq, t, f_cum, gates_t, f_cum_t,
                conv_hist, w_conv, c0, n0, m0, g_head, i_off, f_off, into=None):
    b = n_seq
    _, n_h, dv, dk = c0.shape
    qb, ogb, vb = qk_col // dk, og_col // dv, v_col // dv
    conv_w = w_conv.shape[0]
    chunk = _tile(t, MLSTM_CHUNK)
    nch = t // chunk
    hpad = 8
    hist = jnp.pad(conv_hist, ((0, 0), (hpad - (conv_w - 1), 0), (0, 0)))
    gt = gates_t.reshape(b, LANES, nch, chunk)
    ft = f_cum_t.reshape(b, LANES, nch, chunk)
    n0r = n0.reshape(b, n_h, 1, dk)
    m0r = jnp.broadcast_to(m0[:, :, None, None], (b, n_h, 1, LANES))
    ghr = g_head.reshape(n_h, 1, dv)

    rowcol = lambda width, off: pl.BlockSpec((t, width), lambda i, h: (row_blk + i, off + h))
    hcol = lambda off: pl.BlockSpec((None, hpad, dk), lambda i, h: (i, 0, off + h))
    in_specs = [
        rowcol(dk, qb), rowcol(dk, qb + n_h), hcol(0), hcol(n_h),
        pl.BlockSpec((conv_w, dk), lambda i, h: (0, h)),
        pl.BlockSpec((conv_w, dk), lambda i, h: (0, n_h + h)),
        rowcol(dv, vb), rowcol(dv, ogb),
        pl.BlockSpec((t, LANES), lambda i, h: (row_blk + i, 0)),
        pl.BlockSpec((t, LANES), lambda i, h: (i, 0)),
        pl.BlockSpec((None, None, nch, chunk), lambda i, h: (i, i_off + h, 0, 0)),
        pl.BlockSpec((None, None, nch, chunk), lambda i, h: (i, f_off + h, 0, 0)),
        pl.BlockSpec((None, None, dv, dk), lambda i, h: (i, h, 0, 0)),
        pl.BlockSpec((None, None, 1, dk), lambda i, h: (i, h, 0, 0)),
        pl.BlockSpec((None, None, 1, LANES), lambda i, h: (i, h, 0, 0)),
        pl.BlockSpec((None, 1, dv), lambda i, h: (h, 0, 0)),
    ]
    args = [p32, p32, hist, hist, w_conv, w_conv, p16, p32, gates, f_cum, gt, ft, c0, n0r, m0r, ghr]
    aliases = {}
    if into is not None:
        in_specs.append(pl.BlockSpec(memory_space=pl.ANY))
        args.append(into)
        aliases = {len(args) - 1: 0}
    out_specs = [
        pl.BlockSpec((t, dv), lambda i, h: (row_blk + i, h)),
        pl.BlockSpec((None, None, dv, dk), lambda i, h: (i, h, 0, 0)),
        pl.BlockSpec((None, None, 1, dk), lambda i, h: (i, h, 0, 0)),
        pl.BlockSpec((None, None, 1, LANES), lambda i, h: (i, h, 0, 0)),
    ]
    out_shape = [
        jax.ShapeDtypeStruct((p16.shape[0], n_h * dv), BF16),
        jax.ShapeDtypeStruct((b, n_h, dv, dk), F32),
        jax.ShapeDtypeStruct((b, n_h, 1, dk), F32),
        jax.ShapeDtypeStruct((b, n_h, 1, LANES), F32),
    ]
    out, c_new, n_new, m_new = pl.pallas_call(
        functools.partial(_mlstm_kernel, t=t, chunk=chunk, conv_w=conv_w, i_off=i_off, f_off=f_off,
                          k_scale=float(dk) ** -0.5, aliased=into is not None),
        grid=(b, n_h),
        in_specs=in_specs,
        out_specs=out_specs,
        out_shape=out_shape,
        input_output_aliases=aliases,
        scratch_shapes=[
            pltpu.VMEM((t + hpad, dk), F32), pltpu.VMEM((t + hpad, dk), F32),
            pltpu.VMEM((t, dk), BF16), pltpu.VMEM((t, dk), BF16),
        ],
        compiler_params=_params("parallel", "parallel"),
        name="mlstm_heads",
    )(*args)
    return out, c_new, n_new[:, :, 0, :], m_new[:, :, 0, 0]


def _outproj_kernel(f_ref, m_ref, w1_ref, w2_ref, xa_ref, xb_ref, o_ref, *, n_first):
    acc = jnp.dot(f_ref[...], w1_ref[...], preferred_element_type=F32)
    acc = acc + jnp.dot(m_ref[...], w2_ref[...], preferred_element_type=F32)

    @pl.when(pl.program_id(0) < n_first)
    def _():
        o_ref[...] = xa_ref[...] + acc

    @pl.when(pl.program_id(0) >= n_first)
    def _():
        o_ref[...] = xb_ref[...] + acc


def out_projection(fox, ml, w_out, xa, xb, tm):
    (ma, d), mb = xa.shape, xb.shape[0]
    half = d // 2
    tn = _tile(d, 512)
    assert ma % tm == 0
    n_first = ma // tm
    first_j = lambda i, j: (jnp.minimum(i, n_first - 1), j)
    second_j = lambda i, j: (jnp.maximum(i - n_first, 0), j)
    return pl.pallas_call(
        functools.partial(_outproj_kernel, n_first=n_first),
        grid=(pl.cdiv(ma + mb, tm), d // tn),
        in_specs=[
            pl.BlockSpec((tm, half), lambda i, j: (i, 0)), pl.BlockSpec((tm, half), lambda i, j: (i, 0)),
            pl.BlockSpec((half, tn), lambda i, j: (0, j)),
            pl.BlockSpec((half, tn), lambda i, j: (1, j)),
            pl.BlockSpec((tm, tn), first_j), pl.BlockSpec((tm, tn), second_j),
        ],
        out_specs=pl.BlockSpec((tm, tn), lambda i, j: (i, j)),
        out_shape=jax.ShapeDtypeStruct((ma + mb, d), F32),
        compiler_params=_params("arbitrary", "arbitrary"),
        name="out_projection",
    )(fox, ml, w_out, w_out, xa, xb)


def _pack_bf16_pairs(x):
    half = x.shape[1] // 2
    bits = lax.bitcast_convert_type(x.astype(BF16).astype(F32), jnp.uint32)
    return (bits[:, :half] >> 16) | (bits[:, half:] & jnp.uint32(0xFFFF0000))


def _unpack_pairs_f32(w):
    lo = lax.bitcast_convert_type(w << 16, F32)
    hi = lax.bitcast_convert_type(w & jnp.uint32(0xFFFF0000), F32)
    return lo, hi


def _unpack_bf16_pairs(w):
    lo, hi = _unpack_pairs_f32(w)
    return lo.astype(BF16), hi.astype(BF16)


def _router_kernel(x_ref, g_ref, w_ref, b_ref, xn_ref, eid_ref, gate_ref, *, n_groups, per_group):
    x = x_ref[...]
    xn = x * lax.rsqrt(jnp.mean(x * x, axis=-1, keepdims=True) + EPS) * g_ref[...]
    xn_ref[...] = _pack_bf16_pairs(xn)
    logits = jnp.dot(xn, w_ref[...], precision=HIGHEST, preferred_element_type=F32) + b_ref[...]
    lane = lax.broadcasted_iota(jnp.int32, logits.shape, 1).astype(F32)
    big = float(LANES)
    is_g = lane < n_groups
    gl = jnp.where(is_g, logits, NEG_INF)
    gmax = jnp.max(gl, axis=1, keepdims=True)
    g_idx = jnp.min(jnp.where(gl == gmax, lane, big), axis=1, keepdims=True)
    g_prob = 1.0 / jnp.sum(jnp.where(is_g, jnp.exp(logits - gmax), 0.0), axis=1, keepdims=True)
    lo = n_groups + per_group * g_idx
    el = jnp.where((lane >= lo) & (lane < lo + per_group), logits, NEG_INF)
    e1 = jnp.max(el, axis=1, keepdims=True)
    i1 = jnp.min(jnp.where(el == e1, lane, big), axis=1, keepdims=True)
    el2 = jnp.where(lane == i1, NEG_INF, el)
    e2 = jnp.max(el2, axis=1, keepdims=True)
    i2 = jnp.min(jnp.where(el2 == e2, lane, big), axis=1, keepdims=True)
    r = jnp.exp(e2 - e1)
    w1 = g_prob / (1.0 + r)
    w2 = g_prob * r / (1.0 + r)
    eid = jnp.where(lane == 0.0, i1 - n_groups, jnp.where(lane == 1.0, i2 - n_groups, 0.0))
    eid_ref[...] = eid.astype(jnp.int32)
    gate_ref[...] = jnp.where(lane == 0.0, w1, jnp.where(lane == 1.0, w2, 0.0))


def router(x, g, w_router, b_router, n_groups, per_group):
    m, d = x.shape
    tm = _tile(m, 256)
    return pl.pallas_call(
        functools.partial(_router_kernel, n_groups=n_groups, per_group=per_group),
        grid=(m // tm,),
        in_specs=[
            pl.BlockSpec((tm, d), lambda i: (i, 0)),
            pl.BlockSpec((1, d), lambda i: (0, 0)),
            pl.BlockSpec((d, LANES), lambda i: (0, 0)),
            pl.BlockSpec((1, LANES), lambda i: (0, 0)),
        ],
        out_specs=[
            pl.BlockSpec((tm, d // 2), lambda i: (i, 0)),
            pl.BlockSpec((tm, LANES), lambda i: (i, 0)),
            pl.BlockSpec((tm, LANES), lambda i: (i, 0)),
        ],
        out_shape=[
            jax.ShapeDtypeStruct((m, d // 2), jnp.uint32),
            jax.ShapeDtypeStruct((m, LANES), jnp.int32),
            jax.ShapeDtypeStruct((m, LANES), F32),
        ],
        compiler_params=_params("parallel"),
        name="router",
    )(x, g.reshape(1, d), w_router, b_router)


def _gather_kernel(idx0_ref, idx1_ref, src_ref, o_ref, buf_ref, sem_ref, *, tm):
    step = pl.program_id(0)
    nsteps = pl.num_programs(0)

    def issue(idx_ref, slot):
        def body(r, carry):
            for q, row in enumerate((r, r + tm // 2)):
                tok = idx_ref[0, row]
                pltpu.make_async_copy(src_ref.at[pl.ds(tok, 1), :], buf_ref.at[slot, pl.ds(row, 1), :],
                                      sem_ref.at[slot]).start(priority=q)
            return carry
        lax.fori_loop(0, tm // 2, body, 0, unroll=DMA_ISSUE_UNROLL // 2)

    @pl.when(step == 0)
    def _():
        issue(idx0_ref, 0)

    @pl.when(step + 1 < nsteps)
    def _():
        issue(idx1_ref, (step + 1) % 2)

    slot = step % 2
    pltpu.make_async_copy(src_ref.at[pl.ds(0, tm), :], buf_ref.at[slot], sem_ref.at[slot]).wait()
    half = buf_ref.shape[2]
    lo, hi = _unpack_bf16_pairs(buf_ref[slot])
    o_ref[:, :half] = lo
    o_ref[:, half:] = hi


def _smem_tiles(n_steps, width):
    cur = pl.BlockSpec((None, 1, width), lambda i: (i, 0, 0), memory_space=pltpu.SMEM)
    nxt = pl.BlockSpec((None, 1, width), lambda i: (jnp.minimum(i + 1, n_steps - 1), 0, 0),
                       memory_space=pltpu.SMEM)
    return cur, nxt


def gather_rows(src, idx, tm):
    n, half = src.shape
    d = 2 * half
    a = idx.shape[0]
    n_steps = a // tm
    cur, nxt = _smem_tiles(n_steps, tm)
    idx3 = idx.reshape(n_steps, 1, tm)
    return pl.pallas_call(
        functools.partial(_gather_kernel, tm=tm),
        grid=(n_steps,),
        in_specs=[cur, nxt, pl.BlockSpec(memory_space=pl.ANY)],
        out_specs=pl.BlockSpec((tm, d), lambda i: (i, 0)),
        scratch_shapes=[pltpu.VMEM((2, tm, half), src.dtype), pltpu.SemaphoreType.DMA((2,))],
        out_shape=jax.ShapeDtypeStruct((a, d), BF16),
        compiler_params=_params("arbitrary"),
        name="gather_rows",
    )(idx3, idx3, src)


def _expert_up_kernel(te_ref, nv_ref, x_ref, wg_ref, wu_ref, h_ref):
    t = pl.program_id(1)

    @pl.when(t < nv_ref[0])
    def _():
        x = x_ref[...]
        g = jnp.dot(x, wg_ref[...].astype(BF16), preferred_element_type=F32)
        u = jnp.dot(x, wu_ref[...].astype(BF16), preferred_element_type=F32)
        h_ref[...] = (g * jax.nn.sigmoid(g) * u).astype(h_ref.dtype)

    @pl.when(t >= nv_ref[0])
    def _():
        h_ref[...] = jnp.zeros_like(h_ref)


def expert_up(xg, w_gate, w_up, tile_expert, n_valid, tm):
    a, d = xg.shape
    _, _, f = w_gate.shape
    tn = _tile(f, 512)
    return pl.pallas_call(
        _expert_up_kernel,
        grid_spec=pltpu.PrefetchScalarGridSpec(
            num_scalar_prefetch=2,
            grid=(f // tn, a // tm),
            in_specs=[
                pl.BlockSpec((tm, d), lambda c, t, te, nv: (t, 0)),
                pl.BlockSpec((None, d, tn), lambda c, t, te, nv: (te[t], 0, c)),
                pl.BlockSpec((None, d, tn), lambda c, t, te, nv: (te[t], 0, c)),
            ],
            out_specs=pl.BlockSpec((tm, tn), lambda c, t, te, nv: (t, c)),
        ),
        out_shape=jax.ShapeDtypeStruct((a, f), BF16),
        compiler_params=_params("arbitrary", "arbitrary"),
        name="expert_up",
    )(tile_expert, n_valid, xg, w_gate, w_up)


def _expert_down_kernel(te_ref, nv_ref, h_ref, wd_ref, y_ref):
    t = pl.program_id(1)

    @pl.when(t < nv_ref[0])
    def _():
        y = jnp.dot(h_ref[...], wd_ref[...].astype(BF16), preferred_element_type=F32)
        y_ref[...] = _pack_bf16_pairs(y)

    @pl.when(t >= nv_ref[0])
    def _():
        y_ref[...] = jnp.zeros_like(y_ref)


def expert_down(hg, w_down, tile_expert, n_valid, tm):
    a, f = hg.shape
    _, _, d = w_down.shape
    tn = _tile(d, DOWN_CHUNK)
    return pl.pallas_call(
        _expert_down_kernel,
        grid_spec=pltpu.PrefetchScalarGridSpec(
            num_scalar_prefetch=2,
            grid=(d // tn, a // tm),
            in_specs=[
                pl.BlockSpec((tm, f), lambda c, t, te, nv: (t, 0)),
                pl.BlockSpec((None, f, tn), lambda c, t, te, nv: (te[t], 0, c)),
            ],
            out_specs=pl.BlockSpec((tm, tn // 2), lambda c, t, te, nv: (t, c)),
        ),
        out_shape=jax.ShapeDtypeStruct((a, d // 2), jnp.uint32),
        compiler_params=_params("arbitrary", "arbitrary"),
        name="expert_down",
    )(tile_expert, n_valid, hg, w_down)


def _combine_kernel(pos0_ref, pos1_ref, x_ref, gate_ref, g_ref, y_ref, oa_ref, ob_ref, buf_ref, sem_ref,
                    *, tm, n_first, chunk):
    step = pl.program_id(0)
    nsteps = pl.num_programs(0)

    def issue(pos_ref, slot):
        def body(r, carry):
            for k in range(TOP_K):
                p = pos_ref[0, r * TOP_K + k]
                pltpu.make_async_copy(y_ref.at[pl.ds(p, 1), :], buf_ref.at[slot, k, pl.ds(r, 1), :],
                                      sem_ref.at[slot]).start(priority=k)
            return carry
        lax.fori_loop(0, tm, body, 0, unroll=DMA_ISSUE_UNROLL)

    @pl.when(step == 0)
    def _():
        issue(pos0_ref, 0)

    @pl.when(step + 1 < nsteps)
    def _():
        issue(pos1_ref, (step + 1) % 2)

    slot = step % 2
    for k in range(TOP_K):
        pltpu.make_async_copy(y_ref.at[pl.ds(0, tm), :], buf_ref.at[slot, k], sem_ref.at[slot]).wait()
    gates = gate_ref[...]
    x = x_ref[...]
    for k in range(TOP_K):
        lo, hi = _unpack_pairs_f32(buf_ref[slot, k])
        pieces = []
        for c in range(0, lo.shape[1], chunk // 2):
            pieces += [lo[:, c:c + chunk // 2], hi[:, c:c + chunk // 2]]
        x = x + gates[:, k:k + 1] * jnp.concatenate(pieces, axis=1)
    y = x * lax.rsqrt(jnp.mean(x * x, axis=-1, keepdims=True) + EPS) * g_ref[...]

    @pl.when(step < n_first)
    def _():
        oa_ref[...] = y

    @pl.when(step >= n_first)
    def _():
        ob_ref[...] = y


def combine_norm(x, gates, pos, yg, g_final, m_first, tm):
    m, d = x.shape
    n_steps = m // tm
    n_first = m_first // tm
    cur, nxt = _smem_tiles(n_steps, tm * TOP_K)
    pos3 = pos.reshape(n_steps, 1, tm * TOP_K)
    first, second = _split_maps(n_first)
    return pl.pallas_call(
        functools.partial(_combine_kernel, tm=tm, n_first=n_first, chunk=_tile(d, DOWN_CHUNK)),
        grid=(n_steps,),
        in_specs=[
            cur, nxt,
            pl.BlockSpec((tm, d), lambda i: (i, 0)),
            pl.BlockSpec((tm, LANES), lambda i: (i, 0)),
            pl.BlockSpec((1, d), lambda i: (0, 0)),
            pl.BlockSpec(memory_space=pl.ANY),
        ],
        out_specs=[pl.BlockSpec((tm, d), first), pl.BlockSpec((tm, d), second)],
        scratch_shapes=[pltpu.VMEM((2, TOP_K, tm, d // 2), jnp.uint32), pltpu.SemaphoreType.DMA((2,))],
        out_shape=[jax.ShapeDtypeStruct((m_first, d), F32), jax.ShapeDtypeStruct((m - m_first, d), F32)],
        compiler_params=_params("arbitrary"),
        name="combine_norm",
    )(pos3, pos3, x, gates, g_final.reshape(1, d), yg)


def _dispatch_plan(eid, n_experts, tm):
    n = eid.shape[0]
    a = n * TOP_K
    e_flat = eid.reshape(a)
    onehot = (e_flat[:, None] == jnp.arange(n_experts, dtype=jnp.int32)[None, :]).astype(jnp.int32)
    csum = jnp.cumsum(onehot, axis=0)
    counts = csum[-1]
    rank = jnp.take_along_axis(csum, e_flat[:, None], axis=1)[:, 0] - 1
    padded = ((counts + tm - 1) // tm) * tm
    pad_end = jnp.cumsum(padded)
    pos = (pad_end - padded)[e_flat] + rank
    n_tiles = a // tm + n_experts
    src = (jnp.arange(n_tiles * tm, dtype=jnp.int32) % n).at[pos].set(jnp.arange(a, dtype=jnp.int32) // TOP_K)
    tile_end = pad_end // tm
    tile_ids = jnp.arange(n_tiles, dtype=jnp.int32)
    tile_expert = jnp.sum((tile_ids[:, None] >= tile_end[None, :]).astype(jnp.int32), axis=1)
    tile_expert = jnp.minimum(tile_expert, n_experts - 1)
    n_valid = tile_end[-1:].astype(jnp.int32)
    return pos.astype(jnp.int32), src, tile_expert, n_valid


def moe_layer(x, m_first, g_ffn, w_router, b_router, w_gate, w_up, w_down, g_final, n_groups, per_group):
    m, d = x.shape
    n_experts = n_groups * per_group
    xn, eid, gates = router(x, g_ffn, w_router, b_router, n_groups, per_group)
    a = m * TOP_K
    tm = _tile(a, min(512, max(64, a // n_experts)))
    pos, src, tile_expert, n_valid = _dispatch_plan(eid[:, :TOP_K], n_experts, tm)
    xg = gather_rows(xn, src, _tile(tm, 256))
    hg = expert_up(xg, w_gate, w_up, tile_expert, n_valid, tm)
    yg = expert_down(hg, w_down, tile_expert, n_valid, tm)
    return combine_norm(x, gates, pos, yg, g_final, m_first, _tile(math.gcd(m_first, m - m_first), 128))


def kernel(x_prompt, x_sample, cache_fox_k, cache_fox_v, cache_fox_logf, state_mlstm_C, state_mlstm_n, state_mlstm_m, state_mlstm_conv, g_norm_mix, w_in, b_fox_f, b_mlstm_i, b_mlstm_f, w_conv, g_mlstm_head, w_out, g_norm_ffn, w_router_group, b_router_group, w_router_expert, b_router_expert, w_exp_gate, w_exp_up, w_exp_down, g_norm_final):
    depth = w_in.shape[0]
    assert depth == 1, "the final norm is fused into the last layer's MoE combine; one layer supported"
    bp, tp, d = x_prompt.shape
    bs, ts, _ = x_sample.shape
    n_fox, dh = cache_fox_k.shape[-2:]
    n_ml, dv, dk = state_mlstm_C.shape[-3:]
    past = cache_fox_k.shape[2]
    n_groups = w_router_group.shape[-1]
    n_experts = w_router_expert.shape[-1]
    conv_w = w_conv.shape[1]
    half = d // 2
    mp, ms = bp * tp, bs * ts
    assert n_fox * dh == half and n_ml * dv == half and 2 * n_ml * dk == half
    assert n_fox + 2 * n_ml <= LANES and n_groups + n_experts <= LANES
    assert mp % ts == 0 and ts >= conv_w - 1, "sample sequences are addressed as row blocks after the prompt rows"
    l = 0
    sizes = (half, half, half, n_fox, half, half, half, n_ml, n_ml)
    offs = [0]
    for s in sizes:
        offs.append(offs[-1] + s)
    col = lambda i: w_in[l][:, offs[i]:offs[i + 1]]
    w_a = w_in[l][:, offs[0]:offs[3]].astype(BF16)
    w_b = w_in[l][:, offs[4]:offs[7]].astype(BF16)
    n_gate = n_fox + 2 * n_ml
    w_gatecols = jnp.pad(jnp.concatenate([col(3), col(7), col(8)], axis=1), ((0, 0), (0, LANES - n_gate))).astype(BF16)
    b_gatecols = jnp.pad(jnp.concatenate([b_fox_f[l], b_mlstm_i[l], b_mlstm_f[l]]).astype(F32),
                         (0, LANES - n_gate)).reshape(1, LANES)
    n_r = n_groups + n_experts
    w_router = jnp.pad(jnp.concatenate([w_router_group[l], w_router_expert[l]], axis=1), ((0, 0), (0, LANES - n_r)))
    b_router = jnp.pad(jnp.concatenate([b_router_group[l], b_router_expert[l]]).astype(F32),
                       (0, LANES - n_r)).reshape(1, LANES)
    i_off, f_off = n_fox, n_fox + n_ml
    col32 = {s: k * half for k, s in enumerate(P32_SEGS)}
    col16 = {s: k * half for k, s in enumerate(P16_SEGS)}

    xp2, xs2 = x_prompt.reshape(mp, d), x_sample.reshape(ms, d)
    tm_rows = _tile(mp, 1024)
    xn = rmsnorm_cast(xp2, xs2, g_norm_mix[l], BF16)
    p32, p16, gates = in_projection(xn, w_a, w_b, w_gatecols, b_gatecols, n_fox, n_ml, tm_rows)

    f_p = cumsum_time(gates, bp, tp)
    f_p3 = f_p.reshape(bp, tp, LANES)
    g_p3 = gates[:mp].reshape(bp, tp, LANES)
    fox_p = fox_attention(p16, col16[SEG_FQ], p32, col32[SEG_FK], col32[SEG_FV], 0, bp, tp,
                          f_p3, n_fox, dh, into=jnp.zeros((mp + ms, half), BF16))
    hist_p = jnp.zeros((bp, conv_w - 1, half), F32)
    ml_p, c_p, n_p, m_p = mlstm_heads(
        p32, col32[SEG_QK], col32[SEG_MO], p16, col16[SEG_MV], gates, 0, bp, tp, f_p,
        jnp.swapaxes(g_p3, 1, 2), jnp.swapaxes(f_p3, 1, 2), hist_p, w_conv[l],
        jnp.zeros((bp, n_ml, dv, dk), F32), jnp.zeros((bp, n_ml, dk), F32), jnp.zeros((bp, n_ml), F32),
        g_mlstm_head[l], i_off, f_off, into=jnp.zeros((mp + ms, half), BF16))

    g_s3 = gates[mp:].reshape(bs, ts, LANES)
    cache_pad = jnp.pad(cache_fox_logf[l].astype(F32), ((0, 0), (0, 0), (0, LANES - n_fox)))
    logf_s = jnp.concatenate([cache_pad, g_s3], axis=1)
    f_s3 = cumsum_time(logf_s.reshape(bs * (past + ts), LANES), bs, past + ts).reshape(bs, past + ts, LANES)
    fox = fox_attention(p16, col16[SEG_FQ], p32, col32[SEG_FK], col32[SEG_FV], mp // ts, bs, ts,
                        f_s3, n_fox, dh,
                        cache_fox_k[l].reshape(bs, past, half), cache_fox_v[l].reshape(bs, past, half),
                        into=fox_p)
    fn_s = cumsum_time(g_s3.reshape(ms, LANES), bs, ts)
    ml, c_s, n_s, m_s = mlstm_heads(
        p32, col32[SEG_QK], col32[SEG_MO], p16, col16[SEG_MV], gates, mp // ts, bs, ts, fn_s,
        jnp.swapaxes(g_s3, 1, 2), jnp.swapaxes(fn_s.reshape(bs, ts, LANES), 1, 2), state_mlstm_conv[l], w_conv[l],
        state_mlstm_C[l], state_mlstm_n[l], state_mlstm_m[l], g_mlstm_head[l], i_off, f_off, into=ml_p)

    x1 = out_projection(fox, ml, w_out[l].astype(BF16), xp2, xs2, tm_rows)
    y_p, y_s = moe_layer(x1, mp, g_norm_ffn[l], w_router, b_router, w_exp_gate[l], w_exp_up[l], w_exp_down[l],
                         g_norm_final, n_groups, n_experts // n_groups)

    def new_state(rows, b, t, g3, c_new, n_new, m_new, conv_hist):
        blk = p32[rows]
        fk = blk[:, col32[SEG_FK]:col32[SEG_FK] + half].reshape(b, t, n_fox, dh)
        fv = blk[:, col32[SEG_FV]:col32[SEG_FV] + half].reshape(b, t, n_fox, dh)
        qk_tail = blk[:, col32[SEG_QK]:col32[SEG_QK] + half].reshape(b, t, half)[:, t - min(t, conv_w - 1):]
        qk_hist = jnp.concatenate([conv_hist.astype(F32), qk_tail], axis=1)[:, -(conv_w - 1):]
        return tuple(a[None] for a in (fk, fv, g3[:, :, :n_fox], c_new, n_new, m_new, qk_hist))

    st_p = new_state(slice(0, mp), bp, tp, g_p3, c_p, n_p, m_p, hist_p)
    st_s = new_state(slice(mp, mp + ms), bs, ts, g_s3, c_s, n_s, m_s, state_mlstm_conv[l])
    return (y_p.reshape(bp, tp, d), y_s.reshape(bs, ts, d)) + st_p + st_s
```

```python
import functools
import math

import jax
import jax.numpy as jnp
from jax import lax
from jax.experimental import pallas as pl
from jax.experimental.pallas import tpu as pltpu

F32 = jnp.float32
BF16 = jnp.bfloat16
EPS = 1e-6
LANES = 128
V7X_VMEM_LIMIT = 56 * 1024 * 1024
TOP_K = 2
MLSTM_CHUNK = 256
DMA_ISSUE_UNROLL = 8
DOWN_CHUNK = 2048
HIGHEST = lax.Precision.HIGHEST
NEG_INF = float("-inf")
LOG2E = math.log2(math.e)


def _params(*sem):
    return pltpu.CompilerParams(dimension_semantics=sem, vmem_limit_bytes=V7X_VMEM_LIMIT)


def _tile(n, pref):
    if n <= pref:
        return n
    for t in range(pref - pref % 8, 7, -8):
        if n % t == 0:
            return t
    return n


def _log_sigmoid(x):
    return jnp.minimum(x, 0.0) - jnp.log1p(jnp.exp(-jnp.abs(x)))


def _split_maps(n_first):
    first = lambda i, *_: (jnp.minimum(i, n_first - 1), 0)
    second = lambda i, *_: (jnp.maximum(i - n_first, 0), 0)
    return first, second


def _rmsnorm_kernel(xa_ref, xb_ref, g_ref, o_ref, *, n_first):
    def emit(x_ref):
        x = x_ref[...]
        y = x * lax.rsqrt(jnp.mean(x * x, axis=-1, keepdims=True) + EPS)
        o_ref[...] = (y * g_ref[...]).astype(o_ref.dtype)

    @pl.when(pl.program_id(0) < n_first)
    def _():
        emit(xa_ref)

    @pl.when(pl.program_id(0) >= n_first)
    def _():
        emit(xb_ref)


def rmsnorm_cast(xa, xb, g, out_dtype):
    (ma, d), mb = xa.shape, xb.shape[0]
    tm = _tile(math.gcd(ma, mb), 256)
    first, second = _split_maps(ma // tm)
    return pl.pallas_call(
        functools.partial(_rmsnorm_kernel, n_first=ma // tm),
        grid=((ma + mb) // tm,),
        in_specs=[pl.BlockSpec((tm, d), first), pl.BlockSpec((tm, d), second),
                  pl.BlockSpec((1, d), lambda i: (0, 0))],
        out_specs=pl.BlockSpec((tm, d), lambda i: (i, 0)),
        out_shape=jax.ShapeDtypeStruct((ma + mb, d), out_dtype),
        compiler_params=_params("arbitrary"),
        name="rmsnorm_cast",
    )(xa, xb, g.reshape(1, d))


SEG_FQ, SEG_FK, SEG_FV, SEG_QK, SEG_MV, SEG_MO = range(6)
P32_SEGS = (SEG_FK, SEG_FV, SEG_QK, SEG_MO)
P16_SEGS = (SEG_FQ, SEG_MV)


def _inproj_kernel(x_ref, wa_ref, wb_ref, ws_ref, b_ref, p32_ref, p16_ref, gate_ref, *, nseg, n_fox, n_ml):
    j = pl.program_id(1)
    seg = j // nseg

    def emit(w_ref):
        acc = jnp.dot(x_ref[...], w_ref[...], preferred_element_type=F32)
        is16 = (seg == SEG_FQ) | (seg == SEG_MV)

        @pl.when(is16)
        def _():
            p16_ref[...] = acc.astype(p16_ref.dtype)

        @pl.when(jnp.logical_not(is16))
        def _():
            p32_ref[...] = acc

    @pl.when(seg < 3)
    def _():
        emit(wa_ref)

    @pl.when(seg >= 3)
    def _():
        emit(wb_ref)

    @pl.when(j == 0)
    def _():
        p = jnp.dot(x_ref[...], ws_ref[...], preferred_element_type=F32) + b_ref[...]
        lane = lax.broadcasted_iota(jnp.int32, p.shape, 1)
        is_id = (lane >= n_fox) & (lane < n_fox + n_ml)
        gate_ref[...] = jnp.where(is_id, p, _log_sigmoid(p))


def _held_block(j, nseg, segs):
    seg = j // nseg
    blk = jnp.int32(0)
    for k, s in enumerate(segs):
        here = k * nseg + (j - s * nseg)
        done = (k + 1) * nseg - 1
        blk = jnp.where(seg == s, here, jnp.where(seg > s, done, blk))
    return blk


def in_projection(xn, w_a, w_b, w_gate, b_gate, n_fox, n_ml, tm):
    m, d = xn.shape
    w = w_a.shape[1] // 3
    tn = _tile(w, 512)
    nseg = w // tn
    return pl.pallas_call(
        functools.partial(_inproj_kernel, nseg=nseg, n_fox=n_fox, n_ml=n_ml),
        grid=(pl.cdiv(m, tm), 6 * nseg),
        in_specs=[
            pl.BlockSpec((tm, d), lambda i, j: (i, 0)),
            pl.BlockSpec((d, tn), lambda i, j: (0, jnp.minimum(j, 3 * nseg - 1))),
            pl.BlockSpec((d, tn), lambda i, j: (0, jnp.maximum(j - 3 * nseg, 0))),
            pl.BlockSpec((d, LANES), lambda i, j: (0, 0)),
            pl.BlockSpec((1, LANES), lambda i, j: (0, 0)),
        ],
        out_specs=[
            pl.BlockSpec((tm, tn), lambda i, j: (i, _held_block(j, nseg, P32_SEGS))),
            pl.BlockSpec((tm, tn), lambda i, j: (i, _held_block(j, nseg, P16_SEGS))),
            pl.BlockSpec((tm, LANES), lambda i, j: (i, 0)),
        ],
        out_shape=[
            jax.ShapeDtypeStruct((m, len(P32_SEGS) * w), F32),
            jax.ShapeDtypeStruct((m, len(P16_SEGS) * w), BF16),
            jax.ShapeDtypeStruct((m, LANES), F32),
        ],
        compiler_params=_params("arbitrary", "arbitrary"),
        name="in_projection",
    )(xn, w_a, w_b, w_gate, b_gate)


def _cumsum_kernel(g_ref, f_ref, *, chunk):
    s = g_ref.shape[0]
    r = lax.broadcasted_iota(jnp.int32, (chunk, chunk), 0)
    c = lax.broadcasted_iota(jnp.int32, (chunk, chunk), 1)
    tri = (c <= r).astype(F32)
    carry = jnp.zeros((1, g_ref.shape[1]), F32)
    for k in range(s // chunk):
        blk = g_ref[k * chunk:(k + 1) * chunk, :]
        loc = jnp.dot(tri, blk, precision=HIGHEST, preferred_element_type=F32)
        f_ref[k * chunk:(k + 1) * chunk, :] = loc + carry
        carry = carry + loc[chunk - 1:chunk, :]


def cumsum_time(g, n_seq, s):
    n = g.shape[1]
    chunk = next(c for c in (256, 128, 64, 32, 16, 8) if s % c == 0)
    return pl.pallas_call(
        functools.partial(_cumsum_kernel, chunk=chunk),
        grid=(n_seq,),
        in_specs=[pl.BlockSpec((s, n), lambda i: (i, 0))],
        out_specs=pl.BlockSpec((s, n), lambda i: (i, 0)),
        out_shape=jax.ShapeDtypeStruct((n_seq * s, n), F32),
        compiler_params=_params("parallel"),
        name="cumsum_time",
    )(g)


def _fox_kernel(*refs, past, t, tq, scale):
    o_ref = refs[-1]
    if past:
        q_ref, k_ref, v_ref, pk_ref, pv_ref, fcol_ref, frow_ref = refs[:-2]
    else:
        q_ref, k_ref, v_ref, fcol_ref, frow_ref = refs[:-2]
    h = pl.program_id(1)
    fc = fcol_ref[past:past + t, :]
    lane = lax.broadcasted_iota(jnp.int32, fc.shape, 1)
    fq_all = jnp.sum(jnp.where(lane == h, fc, 0.0), axis=1, keepdims=True) * LOG2E
    fk_all = frow_ref[...] * LOG2E
    kb = k_ref[...].astype(BF16)
    vb = v_ref[...].astype(BF16)
    dn_t = (((1,), (1,)), ((), ()))
    row = lax.broadcasted_iota(jnp.int32, (tq, tq), 0)
    col = lax.broadcasted_iota(jnp.int32, (tq, tq), 1)
    diag_mask = col <= row
    for qi in range(t // tq):
        q = q_ref[qi * tq:(qi + 1) * tq, :]
        fq = fq_all[qi * tq:(qi + 1) * tq, :]
        lo = qi * tq
        parts = []
        if past:
            parts.append((pk_ref[...].astype(BF16), pv_ref[...].astype(BF16), fk_all[:, 0:past], False))
        if qi:
            parts.append((kb[0:lo, :], vb[0:lo, :], fk_all[:, past:past + lo], False))
        parts.append((kb[lo:lo + tq, :], vb[lo:lo + tq, :], fk_all[:, past + lo:past + lo + tq], True))
        logits = []
        for kp, _, fk, masked in parts:
            s = lax.dot_general(q, kp, dn_t, preferred_element_type=F32)
            lg = s * (scale * LOG2E) + (fq - fk)
            logits.append(jnp.where(diag_mask, lg, NEG_INF) if masked else lg)
        mx = functools.reduce(jnp.maximum, [jnp.max(lg, axis=1, keepdims=True) for lg in logits])
        den = 0.0
        acc = 0.0
        for (_, vp, _, _), lg in zip(parts, logits):
            p = jnp.exp2(lg - mx)
            den = den + jnp.sum(p, axis=1, keepdims=True)
            acc = acc + jnp.dot(p.astype(BF16), vp, preferred_element_type=F32)
        o_ref[qi * tq:(qi + 1) * tq, :] = (acc / den).astype(o_ref.dtype)


def fox_attention(p16, q_col, p32, k_col, v_col, row_blk, n_seq, t, f_col, f_row, n_heads, dh, heads,
                  past_k=None, past_v=None):
    past = 0 if past_k is None else past_k.shape[1]
    tq = _tile(t, 256)

    def head_blk(off):
        return pl.BlockSpec((t, dh), lambda i, h: (row_blk + i, off // dh + h))

    in_specs = [head_blk(q_col), head_blk(k_col), head_blk(v_col)]
    args = [p16, p32, p32]
    if past:
        past_blk = pl.BlockSpec((None, past, dh), lambda i, h: (i, 0, h))
        in_specs += [past_blk, past_blk]
        args += [past_k, past_v]
    in_specs += [
        pl.BlockSpec((None, past + t, LANES), lambda i, h: (i, 0, 0)),
        pl.BlockSpec((None, None, 1, past + t), lambda i, h: (i, h, 0, 0)),
        pl.BlockSpec(memory_space=pl.ANY),
    ]
    args += [f_col, f_row, heads]
    return pl.pallas_call(
        functools.partial(_fox_kernel, past=past, t=t, tq=tq, scale=float(dh) ** -0.5),
        grid=(n_seq, n_heads),
        in_specs=in_specs,
        out_specs=head_blk(0),
        out_shape=jax.ShapeDtypeStruct(heads.shape, heads.dtype),
        input_output_aliases={len(args) - 1: 0},
        compiler_params=_params("parallel", "parallel"),
        name="fox_attention",
    )(*args)


def _mlstm_kernel(*refs, t, chunk, conv_w, i_off, f_off, k_scale):
    refs = refs[:16] + refs[17:]
    (qraw_ref, kraw_ref, hq_ref, hk_ref, wq_ref, wk_ref, v_ref, og_ref, g_ref, f_ref,
     irow_ref, frow_ref, c0_ref, n0_ref, m0_ref, gh_ref,
     out_ref, c_ref, n_ref, m_ref,
     histq_ref, histk_ref, qc_ref, kc_ref) = refs
    h = pl.program_id(1)
    hpad = hq_ref.shape[0]

    def conv_silu(raw_ref, hist_in_ref, w_ref, hist_ref, scale, dst_ref):
        hist_ref[0:hpad, :] = hist_in_ref[...]
        hist_ref[hpad:hpad + t, :] = raw_ref[...]
        y = None
        for j in range(conv_w):
            start = hpad - (conv_w - 1) + j
            term = hist_ref[start:start + t, :] * w_ref[j:j + 1, :]
            y = term if y is None else y + term
        y = y * jax.nn.sigmoid(y)
        if scale != 1.0:
            y = y * scale
        dst_ref[...] = y.astype(dst_ref.dtype)

    conv_silu(qraw_ref, hq_ref, wq_ref, histq_ref, 1.0, qc_ref)
    conv_silu(kraw_ref, hk_ref, wk_ref, histk_ref, k_scale, kc_ref)

    c_ref[...] = c0_ref[...]
    n_ref[...] = n0_ref[...]
    m_ref[...] = m0_ref[...]
    gh = gh_ref[...]
    rr = lax.broadcasted_iota(jnp.int32, (chunk, chunk), 0)
    cc = lax.broadcasted_iota(jnp.int32, (chunk, chunk), 1)
    causal = cc <= rr
    lane = lax.broadcasted_iota(jnp.int32, (chunk, LANES), 1)

    def body(c, f_prev):
        r0 = pl.multiple_of(c * chunk, chunk)
        rows = pl.ds(r0, chunk)
        i_col = jnp.sum(jnp.where(lane == i_off + h, g_ref[rows, :], 0.0), axis=1, keepdims=True)
        b_col = jnp.sum(jnp.where(lane == f_off + h, f_ref[rows, :], 0.0), axis=1, keepdims=True) - f_prev
        i_row = irow_ref[pl.ds(c, 1), :]
        b_row = frow_ref[pl.ds(c, 1), :] - f_prev
        m_prev = m_ref[:, 0:1]
        q = qc_ref[rows, :]
        k = kc_ref[rows, :]
        v = v_ref[rows, :]
        cmat = c_ref[...]
        nvec = n_ref[...]

        d = jnp.where(causal, b_col - b_row + i_row, NEG_INF)
        inter = b_col + m_prev
        m_t = jnp.maximum(inter, jnp.max(d, axis=1, keepdims=True))
        w_inter = jnp.exp(inter - m_t)
        qk = lax.dot_general(q, k, (((1,), (1,)), ((), ())), preferred_element_type=F32)
        s = qk * jnp.exp(d - m_t)
        qc_state = lax.dot_general(q, cmat.astype(BF16), (((1,), (1,)), ((), ())),
                                   preferred_element_type=F32)
        num = w_inter * qc_state + jnp.dot(s.astype(BF16), v, preferred_element_type=F32)
        qn = jnp.sum(q.astype(F32) * nvec, axis=1, keepdims=True)
        den = w_inter * qn + jnp.sum(s, axis=1, keepdims=True)
        hval = num / jnp.maximum(jnp.abs(den), jnp.exp(-m_t))

        hn = hval * lax.rsqrt(jnp.mean(hval * hval, axis=1, keepdims=True) + EPS)
        out = jax.nn.sigmoid(og_ref[rows, :]) * hn * gh
        out_ref[rows, :] = out.astype(out_ref.dtype)

        b_last = b_col[chunk - 1:chunk, :]
        dec_row = b_last - b_row + i_row
        dec_col = b_last - b_col + i_col
        m_new = jnp.maximum(b_last + m_prev, jnp.max(dec_row, axis=1, keepdims=True))
        a = jnp.exp(b_last + m_prev - m_new)
        w_col = jnp.exp(dec_col - m_new)
        vw = (v.astype(F32) * w_col).astype(BF16)
        c_ref[...] = a * cmat + lax.dot_general(vw, k, (((0,), (0,)), ((), ())),
                                                preferred_element_type=F32)
        n_ref[...] = a * nvec + jnp.sum(w_col * k.astype(F32), axis=0, keepdims=True)
        m_ref[...] = jnp.broadcast_to(m_new, m_ref.shape)
        return f_prev + b_last

    lax.fori_loop(0, t // chunk, body, jnp.zeros((1, 1), F32), unroll=True)


def mlstm_heads(p32, qk_col, og_col, p16, v_col, gates, row_blk, n_seq, t, f_cum, gates_t, f_cum_t,
                conv_hist, w_conv, c0, n0, m0, g_head, i_off, f_off, heads, out_col):
    b = n_seq
    _, n_h, dv, dk = c0.shape
    qb, ogb, vb = qk_col // dk, og_col // dv, v_col // dv
    conv_w = w_conv.shape[0]
    chunk = _tile(t, MLSTM_CHUNK)
    nch = t // chunk
    hpad = 8
    hist = jnp.pad(conv_hist, ((0, 0), (hpad - (conv_w - 1), 0), (0, 0)))
    gt = gates_t.reshape(b, LANES, nch, chunk)
    ft = f_cum_t.reshape(b, LANES, nch, chunk)
    n0r = n0.reshape(b, n_h, 1, dk)
    m0r = jnp.broadcast_to(m0[:, :, None, None], (b, n_h, 1, LANES))
    ghr = g_head.reshape(n_h, 1, dv)

    rowcol = lambda width, off: pl.BlockSpec((t, width), lambda i, h: (row_blk + i, off + h))
    hcol = lambda off: pl.BlockSpec((None, hpad, dk), lambda i, h: (i, 0, off + h))
    in_specs = [
        rowcol(dk, qb), rowcol(dk, qb + n_h), hcol(0), hcol(n_h),
        pl.BlockSpec((conv_w, dk), lambda i, h: (0, h)),
        pl.BlockSpec((conv_w, dk), lambda i, h: (0, n_h + h)),
        rowcol(dv, vb), rowcol(dv, ogb),
        pl.BlockSpec((t, LANES), lambda i, h: (row_blk + i, 0)),
        pl.BlockSpec((t, LANES), lambda i, h: (i, 0)),
        pl.BlockSpec((None, None, nch, chunk), lambda i, h: (i, i_off + h, 0, 0)),
        pl.BlockSpec((None, None, nch, chunk), lambda i, h: (i, f_off + h, 0, 0)),
        pl.BlockSpec((None, None, dv, dk), lambda i, h: (i, h, 0, 0)),
        pl.BlockSpec((None, None, 1, dk), lambda i, h: (i, h, 0, 0)),
        pl.BlockSpec((None, None, 1, LANES), lambda i, h: (i, h, 0, 0)),
        pl.BlockSpec((None, 1, dv), lambda i, h: (h, 0, 0)),
    ]
    in_specs.append(pl.BlockSpec(memory_space=pl.ANY))
    args = [p32, p32, hist, hist, w_conv, w_conv, p16, p32, gates, f_cum, gt, ft, c0, n0r, m0r, ghr, heads]
    out_col = out_col // dv
    out_specs = [
        pl.BlockSpec((t, dv), lambda i, h: (row_blk + i, out_col + h)),
        pl.BlockSpec((None, None, dv, dk), lambda i, h: (i, h, 0, 0)),
        pl.BlockSpec((None, None, 1, dk), lambda i, h: (i, h, 0, 0)),
        pl.BlockSpec((None, None, 1, LANES), lambda i, h: (i, h, 0, 0)),
    ]
    out_shape = [
        jax.ShapeDtypeStruct(heads.shape, heads.dtype),
        jax.ShapeDtypeStruct((b, n_h, dv, dk), F32),
        jax.ShapeDtypeStruct((b, n_h, 1, dk), F32),
        jax.ShapeDtypeStruct((b, n_h, 1, LANES), F32),
    ]
    out, c_new, n_new, m_new = pl.pallas_call(
        functools.partial(_mlstm_kernel, t=t, chunk=chunk, conv_w=conv_w, i_off=i_off, f_off=f_off,
                          k_scale=float(dk) ** -0.5),
        grid=(b, n_h),
        in_specs=in_specs,
        out_specs=out_specs,
        out_shape=out_shape,
        input_output_aliases={len(args) - 1: 0},
        scratch_shapes=[
            pltpu.VMEM((t + hpad, dk), F32), pltpu.VMEM((t + hpad, dk), F32),
            pltpu.VMEM((t, dk), BF16), pltpu.VMEM((t, dk), BF16),
        ],
        compiler_params=_params("parallel", "parallel"),
        name="mlstm_heads",
    )(*args)
    return out, c_new, n_new[:, :, 0, :], m_new[:, :, 0, 0]


def _outproj_kernel(a_ref, w_ref, xa_ref, xb_ref, o_ref, *, n_first):
    acc = jnp.dot(a_ref[...], w_ref[...], preferred_element_type=F32)

    @pl.when(pl.program_id(0) < n_first)
    def _():
        o_ref[...] = xa_ref[...] + acc

    @pl.when(pl.program_id(0) >= n_first)
    def _():
        o_ref[...] = xb_ref[...] + acc


def out_projection(heads, w_out, xa, xb, tm):
    (ma, d), mb = xa.shape, xb.shape[0]
    tn = _tile(d, 512)
    assert ma % tm == 0
    n_first = ma // tm
    first_j = lambda i, j: (jnp.minimum(i, n_first - 1), j)
    second_j = lambda i, j: (jnp.maximum(i - n_first, 0), j)
    return pl.pallas_call(
        functools.partial(_outproj_kernel, n_first=n_first),
        grid=(pl.cdiv(ma + mb, tm), d // tn),
        in_specs=[
            pl.BlockSpec((tm, d), lambda i, j: (i, 0)),
            pl.BlockSpec((d, tn), lambda i, j: (0, j)),
            pl.BlockSpec((tm, tn), first_j), pl.BlockSpec((tm, tn), second_j),
        ],
        out_specs=pl.BlockSpec((tm, tn), lambda i, j: (i, j)),
        out_shape=jax.ShapeDtypeStruct((ma + mb, d), F32),
        compiler_params=_params("arbitrary", "arbitrary"),
        name="out_projection",
    )(heads, w_out, xa, xb)


def _pack_bf16_pairs(x):
    half = x.shape[1] // 2
    bits = lax.bitcast_convert_type(x.astype(BF16).astype(F32), jnp.uint32)
    return (bits[:, :half] >> 16) | (bits[:, half:] & jnp.uint32(0xFFFF0000))


def _unpack_pairs_f32(w):
    lo = lax.bitcast_convert_type(w << 16, F32)
    hi = lax.bitcast_convert_type(w & jnp.uint32(0xFFFF0000), F32)
    return lo, hi


def _unpack_bf16_pairs(w):
    lo, hi = _unpack_pairs_f32(w)
    return lo.astype(BF16), hi.astype(BF16)


def _router_kernel(x_ref, g_ref, whi_ref, wlo_ref, b_ref, xn_ref, eid_ref, gate_ref, *, n_groups, per_group):
    x = x_ref[...]
    xn = x * lax.rsqrt(jnp.mean(x * x, axis=-1, keepdims=True) + EPS) * g_ref[...]
    xn_ref[...] = _pack_bf16_pairs(xn)
    x_hi = xn.astype(BF16)
    x_lo = (xn - x_hi.astype(F32)).astype(BF16)
    logits = (jnp.dot(x_hi, whi_ref[...], preferred_element_type=F32)
              + jnp.dot(x_lo, whi_ref[...], preferred_element_type=F32)
              + jnp.dot(x_hi, wlo_ref[...], preferred_element_type=F32)) + b_ref[...]
    lane = lax.broadcasted_iota(jnp.int32, logits.shape, 1).astype(F32)
    big = float(LANES)
    is_g = lane < n_groups
    gl = jnp.where(is_g, logits, NEG_INF)
    gmax = jnp.max(gl, axis=1, keepdims=True)
    g_idx = jnp.min(jnp.where(gl == gmax, lane, big), axis=1, keepdims=True)
    g_prob = 1.0 / jnp.sum(jnp.where(is_g, jnp.exp(logits - gmax), 0.0), axis=1, keepdims=True)
    lo = n_groups + per_group * g_idx
    el = jnp.where((lane >= lo) & (lane < lo + per_group), logits, NEG_INF)
    e1 = jnp.max(el, axis=1, keepdims=True)
    i1 = jnp.min(jnp.where(el == e1, lane, big), axis=1, keepdims=True)
    el2 = jnp.where(lane == i1, NEG_INF, el)
    e2 = jnp.max(el2, axis=1, keepdims=True)
    i2 = jnp.min(jnp.where(el2 == e2, lane, big), axis=1, keepdims=True)
    r = jnp.exp(e2 - e1)
    w1 = g_prob / (1.0 + r)
    w2 = g_prob * r / (1.0 + r)
    eid = jnp.where(lane == 0.0, i1 - n_groups, jnp.where(lane == 1.0, i2 - n_groups, 0.0))
    eid_ref[...] = eid.astype(jnp.int32)
    gate_ref[...] = jnp.where(lane == 0.0, w1, jnp.where(lane == 1.0, w2, 0.0))


def router(x, g, w_router, b_router, n_groups, per_group):
    m, d = x.shape
    tm = _tile(m, 256)
    w_hi = w_router.astype(BF16)
    w_lo = (w_router - w_hi.astype(F32)).astype(BF16)
    return pl.pallas_call(
        functools.partial(_router_kernel, n_groups=n_groups, per_group=per_group),
        grid=(m // tm,),
        in_specs=[
            pl.BlockSpec((tm, d), lambda i: (i, 0)),
            pl.BlockSpec((1, d), lambda i: (0, 0)),
            pl.BlockSpec((d, LANES), lambda i: (0, 0)),
            pl.BlockSpec((d, LANES), lambda i: (0, 0)),
            pl.BlockSpec((1, LANES), lambda i: (0, 0)),
        ],
        out_specs=[
            pl.BlockSpec((tm, d // 2), lambda i: (i, 0)),
            pl.BlockSpec((tm, LANES), lambda i: (i, 0)),
            pl.BlockSpec((tm, LANES), lambda i: (i, 0)),
        ],
        out_shape=[
            jax.ShapeDtypeStruct((m, d // 2), jnp.uint32),
            jax.ShapeDtypeStruct((m, LANES), jnp.int32),
            jax.ShapeDtypeStruct((m, LANES), F32),
        ],
        compiler_params=_params("parallel"),
        name="router",
    )(x, g.reshape(1, d), w_hi, w_lo, b_router)


def _gather_kernel(idx0_ref, idx1_ref, src_ref, o_ref, buf_ref, sem_ref, *, tm):
    step = pl.program_id(0)
    nsteps = pl.num_programs(0)

    def issue(idx_ref, slot):
        def body(r, carry):
            for q, row in enumerate((r, r + tm // 2)):
                tok = idx_ref[0, row]
                pltpu.make_async_copy(src_ref.at[pl.ds(tok, 1), :], buf_ref.at[slot, pl.ds(row, 1), :],
                                      sem_ref.at[slot]).start(priority=q)
            return carry
        lax.fori_loop(0, tm // 2, body, 0, unroll=DMA_ISSUE_UNROLL // 2)

    @pl.when(step == 0)
    def _():
        issue(idx0_ref, 0)

    @pl.when(step + 1 < nsteps)
    def _():
        issue(idx1_ref, (step + 1) % 2)

    slot = step % 2
    pltpu.make_async_copy(src_ref.at[pl.ds(0, tm), :], buf_ref.at[slot], sem_ref.at[slot]).wait()
    half = buf_ref.shape[2]
    lo, hi = _unpack_bf16_pairs(buf_ref[slot])
    o_ref[:, :half] = lo
    o_ref[:, half:] = hi


def _smem_tiles(n_steps, width):
    cur = pl.BlockSpec((None, 1, width), lambda i: (i, 0, 0), memory_space=pltpu.SMEM)
    nxt = pl.BlockSpec((None, 1, width), lambda i: (jnp.minimum(i + 1, n_steps - 1), 0, 0),
                       memory_space=pltpu.SMEM)
    return cur, nxt


def gather_rows(src, idx, tm):
    n, half = src.shape
    d = 2 * half
    a = idx.shape[0]
    n_steps = a // tm
    cur, nxt = _smem_tiles(n_steps, tm)
    idx3 = idx.reshape(n_steps, 1, tm)
    return pl.pallas_call(
        functools.partial(_gather_kernel, tm=tm),
        grid=(n_steps,),
        in_specs=[cur, nxt, pl.BlockSpec(memory_space=pl.ANY)],
        out_specs=pl.BlockSpec((tm, d), lambda i: (i, 0)),
        scratch_shapes=[pltpu.VMEM((2, tm, half), src.dtype), pltpu.SemaphoreType.DMA((2,))],
        out_shape=jax.ShapeDtypeStruct((a, d), BF16),
        compiler_params=_params("arbitrary"),
        name="gather_rows",
    )(idx3, idx3, src)


def _expert_up_kernel(te_ref, nv_ref, x_ref, wg_ref, wu_ref, h_ref):
    t = pl.program_id(1)

    @pl.when(t < nv_ref[0])
    def _():
        x = x_ref[...]
        g = jnp.dot(x, wg_ref[...].astype(BF16), preferred_element_type=F32)
        u = jnp.dot(x, wu_ref[...].astype(BF16), preferred_element_type=F32)
        h_ref[...] = (g * jax.nn.sigmoid(g) * u).astype(h_ref.dtype)

    @pl.when(t >= nv_ref[0])
    def _():
        h_ref[...] = jnp.zeros_like(h_ref)


def expert_up(xg, w_gate, w_up, tile_expert, n_valid, tm):
    a, d = xg.shape
    _, _, f = w_gate.shape
    tn = _tile(f, 512)
    return pl.pallas_call(
        _expert_up_kernel,
        grid_spec=pltpu.PrefetchScalarGridSpec(
            num_scalar_prefetch=2,
            grid=(f // tn, a // tm),
            in_specs=[
                pl.BlockSpec((tm, d), lambda c, t, te, nv: (t, 0)),
                pl.BlockSpec((None, d, tn), lambda c, t, te, nv: (te[t], 0, c)),
                pl.BlockSpec((None, d, tn), lambda c, t, te, nv: (te[t], 0, c)),
            ],
            out_specs=pl.BlockSpec((tm, tn), lambda c, t, te, nv: (t, c)),
        ),
        out_shape=jax.ShapeDtypeStruct((a, f), BF16),
        compiler_params=_params("arbitrary", "arbitrary"),
        name="expert_up",
    )(tile_expert, n_valid, xg, w_gate, w_up)


def _expert_down_kernel(te_ref, nv_ref, h_ref, wd_ref, y_ref):
    t = pl.program_id(1)

    @pl.when(t < nv_ref[0])
    def _():
        y = jnp.dot(h_ref[...], wd_ref[...].astype(BF16), preferred_element_type=F32)
        y_ref[...] = _pack_bf16_pairs(y)

    @pl.when(t >= nv_ref[0])
    def _():
        y_ref[...] = jnp.zeros_like(y_ref)


def expert_down(hg, w_down, tile_expert, n_valid, tm):
    a, f = hg.shape
    _, _, d = w_down.shape
    tn = _tile(d, DOWN_CHUNK)
    return pl.pallas_call(
        _expert_down_kernel,
        grid_spec=pltpu.PrefetchScalarGridSpec(
            num_scalar_prefetch=2,
            grid=(d // tn, a // tm),
            in_specs=[
                pl.BlockSpec((tm, f), lambda c, t, te, nv: (t, 0)),
                pl.BlockSpec((None, f, tn), lambda c, t, te, nv: (te[t], 0, c)),
            ],
            out_specs=pl.BlockSpec((tm, tn // 2), lambda c, t, te, nv: (t, c)),
        ),
        out_shape=jax.ShapeDtypeStruct((a, d // 2), jnp.uint32),
        compiler_params=_params("arbitrary", "arbitrary"),
        name="expert_down",
    )(tile_expert, n_valid, hg, w_down)


def _combine_kernel(pos0_ref, pos1_ref, x_ref, gate_ref, g_ref, y_ref, oa_ref, ob_ref, buf_ref, sem_ref,
                    *, tm, n_first, chunk):
    step = pl.program_id(0)
    nsteps = pl.num_programs(0)

    def issue(pos_ref, slot):
        def body(r, carry):
            for k in range(TOP_K):
                p = pos_ref[0, r * TOP_K + k]
                pltpu.make_async_copy(y_ref.at[pl.ds(p, 1), :], buf_ref.at[slot, k, pl.ds(r, 1), :],
                                      sem_ref.at[slot]).start(priority=k)
            return carry
        lax.fori_loop(0, tm, body, 0, unroll=DMA_ISSUE_UNROLL)

    @pl.when(step == 0)
    def _():
        issue(pos0_ref, 0)

    @pl.when(step + 1 < nsteps)
    def _():
        issue(pos1_ref, (step + 1) % 2)

    slot = step % 2
    for k in range(TOP_K):
        pltpu.make_async_copy(y_ref.at[pl.ds(0, tm), :], buf_ref.at[slot, k], sem_ref.at[slot]).wait()
    gates = gate_ref[...]
    x = x_ref[...]
    for k in range(TOP_K):
        lo, hi = _unpack_pairs_f32(buf_ref[slot, k])
        pieces = []
        for c in range(0, lo.shape[1], chunk // 2):
            pieces += [lo[:, c:c + chunk // 2], hi[:, c:c + chunk // 2]]
        x = x + gates[:, k:k + 1] * jnp.concatenate(pieces, axis=1)
    y = x * lax.rsqrt(jnp.mean(x * x, axis=-1, keepdims=True) + EPS) * g_ref[...]

    @pl.when(step < n_first)
    def _():
        oa_ref[...] = y

    @pl.when(step >= n_first)
    def _():
        ob_ref[...] = y


def combine_norm(x, gates, pos, yg, g_final, m_first, tm):
    m, d = x.shape
    n_steps = m // tm
    n_first = m_first // tm
    cur, nxt = _smem_tiles(n_steps, tm * TOP_K)
    pos3 = pos.reshape(n_steps, 1, tm * TOP_K)
    first, second = _split_maps(n_first)
    return pl.pallas_call(
        functools.partial(_combine_kernel, tm=tm, n_first=n_first, chunk=_tile(d, DOWN_CHUNK)),
        grid=(n_steps,),
        in_specs=[
            cur, nxt,
            pl.BlockSpec((tm, d), lambda i: (i, 0)),
            pl.BlockSpec((tm, LANES), lambda i: (i, 0)),
            pl.BlockSpec((1, d), lambda i: (0, 0)),
            pl.BlockSpec(memory_space=pl.ANY),
        ],
        out_specs=[pl.BlockSpec((tm, d), first), pl.BlockSpec((tm, d), second)],
        scratch_shapes=[pltpu.VMEM((2, TOP_K, tm, d // 2), jnp.uint32), pltpu.SemaphoreType.DMA((2,))],
        out_shape=[jax.ShapeDtypeStruct((m_first, d), F32), jax.ShapeDtypeStruct((m - m_first, d), F32)],
        compiler_params=_params("arbitrary"),
        name="combine_norm",
    )(pos3, pos3, x, gates, g_final.reshape(1, d), yg)


def _dispatch_plan(eid, n_experts, tm):
    n = eid.shape[0]
    a = n * TOP_K
    e_flat = eid.reshape(a)
    onehot = (e_flat[:, None] == jnp.arange(n_experts, dtype=jnp.int32)[None, :]).astype(jnp.int32)
    csum = jnp.cumsum(onehot, axis=0)
    counts = csum[-1]
    rank = jnp.take_along_axis(csum, e_flat[:, None], axis=1)[:, 0] - 1
    padded = ((counts + tm - 1) // tm) * tm
    pad_end = jnp.cumsum(padded)
    pos = (pad_end - padded)[e_flat] + rank
    n_tiles = a // tm + n_experts
    src = (jnp.arange(n_tiles * tm, dtype=jnp.int32) % n).at[pos].set(jnp.arange(a, dtype=jnp.int32) // TOP_K)
    tile_end = pad_end // tm
    tile_ids = jnp.arange(n_tiles, dtype=jnp.int32)
    tile_expert = jnp.sum((tile_ids[:, None] >= tile_end[None, :]).astype(jnp.int32), axis=1)
    tile_expert = jnp.minimum(tile_expert, n_experts - 1)
    n_valid = tile_end[-1:].astype(jnp.int32)
    return pos.astype(jnp.int32), src, tile_expert, n_valid


def moe_layer(x, m_first, g_ffn, w_router, b_router, w_gate, w_up, w_down, g_final, n_groups, per_group):
    m, d = x.shape
    n_experts = n_groups * per_group
    xn, eid, gates = router(x, g_ffn, w_router, b_router, n_groups, per_group)
    a = m * TOP_K
    tm = _tile(a, min(512, max(64, a // n_experts)))
    pos, src, tile_expert, n_valid = _dispatch_plan(eid[:, :TOP_K], n_experts, tm)
    xg = gather_rows(xn, src, _tile(tm, 256))
    hg = expert_up(xg, w_gate, w_up, tile_expert, n_valid, tm)
    yg = expert_down(hg, w_down, tile_expert, n_valid, tm)
    return combine_norm(x, gates, pos, yg, g_final, m_first, _tile(math.gcd(m_first, m - m_first), 128))


def kernel(x_prompt, x_sample, cache_fox_k, cache_fox_v, cache_fox_logf, state_mlstm_C, state_mlstm_n, state_mlstm_m, state_mlstm_conv, g_norm_mix, w_in, b_fox_f, b_mlstm_i, b_mlstm_f, w_conv, g_mlstm_head, w_out, g_norm_ffn, w_router_group, b_router_group, w_router_expert, b_router_expert, w_exp_gate, w_exp_up, w_exp_down, g_norm_final):
    depth = w_in.shape[0]
    assert depth == 1, "the final norm is fused into the last layer's MoE combine; one layer supported"
    bp, tp, d = x_prompt.shape
    bs, ts, _ = x_sample.shape
    n_fox, dh = cache_fox_k.shape[-2:]
    n_ml, dv, dk = state_mlstm_C.shape[-3:]
    past = cache_fox_k.shape[2]
    n_groups = w_router_group.shape[-1]
    n_experts = w_router_expert.shape[-1]
    conv_w = w_conv.shape[1]
    half = d // 2
    mp, ms = bp * tp, bs * ts
    assert n_fox * dh == half and n_ml * dv == half and 2 * n_ml * dk == half
    assert n_fox + 2 * n_ml <= LANES and n_groups + n_experts <= LANES
    assert mp % ts == 0 and ts >= conv_w - 1, "sample sequences are addressed as row blocks after the prompt rows"
    l = 0
    sizes = (half, half, half, n_fox, half, half, half, n_ml, n_ml)
    offs = [0]
    for s in sizes:
        offs.append(offs[-1] + s)
    col = lambda i: w_in[l][:, offs[i]:offs[i + 1]]
    w_a = w_in[l][:, offs[0]:offs[3]].astype(BF16)
    w_b = w_in[l][:, offs[4]:offs[7]].astype(BF16)
    n_gate = n_fox + 2 * n_ml
    w_gatecols = jnp.pad(jnp.concatenate([col(3), col(7), col(8)], axis=1), ((0, 0), (0, LANES - n_gate))).astype(BF16)
    b_gatecols = jnp.pad(jnp.concatenate([b_fox_f[l], b_mlstm_i[l], b_mlstm_f[l]]).astype(F32),
                         (0, LANES - n_gate)).reshape(1, LANES)
    n_r = n_groups + n_experts
    w_router = jnp.pad(jnp.concatenate([w_router_group[l], w_router_expert[l]], axis=1), ((0, 0), (0, LANES - n_r)))
    b_router = jnp.pad(jnp.concatenate([b_router_group[l], b_router_expert[l]]).astype(F32),
                       (0, LANES - n_r)).reshape(1, LANES)
    i_off, f_off = n_fox, n_fox + n_ml
    col32 = {s: k * half for k, s in enumerate(P32_SEGS)}
    col16 = {s: k * half for k, s in enumerate(P16_SEGS)}

    xp2, xs2 = x_prompt.reshape(mp, d), x_sample.reshape(ms, d)
    tm_rows = _tile(mp, 1024)
    xn = rmsnorm_cast(xp2, xs2, g_norm_mix[l], BF16)
    p32, p16, gates = in_projection(xn, w_a, w_b, w_gatecols, b_gatecols, n_fox, n_ml, tm_rows)

    f_p = cumsum_time(gates, bp, tp)
    f_p3 = f_p.reshape(bp, tp, LANES)
    g_p3 = gates[:mp].reshape(bp, tp, LANES)
    f_row = lambda f3: jnp.swapaxes(f3[:, :, :n_fox], 1, 2)[:, :, None, :]
    heads = fox_attention(p16, col16[SEG_FQ], p32, col32[SEG_FK], col32[SEG_FV], 0, bp, tp,
                          f_p3, f_row(f_p3), n_fox, dh, xn)
    hist_p = jnp.zeros((bp, conv_w - 1, half), F32)
    heads, c_p, n_p, m_p = mlstm_heads(
        p32, col32[SEG_QK], col32[SEG_MO], p16, col16[SEG_MV], gates, 0, bp, tp, f_p,
        jnp.swapaxes(g_p3, 1, 2), jnp.swapaxes(f_p3, 1, 2), hist_p, w_conv[l],
        jnp.zeros((bp, n_ml, dv, dk), F32), jnp.zeros((bp, n_ml, dk), F32), jnp.zeros((bp, n_ml), F32),
        g_mlstm_head[l], i_off, f_off, heads, half)

    g_s3 = gates[mp:].reshape(bs, ts, LANES)
    cache_pad = jnp.pad(cache_fox_logf[l].astype(F32), ((0, 0), (0, 0), (0, LANES - n_fox)))
    logf_s = jnp.concatenate([cache_pad, g_s3], axis=1)
    f_s3 = cumsum_time(logf_s.reshape(bs * (past + ts), LANES), bs, past + ts).reshape(bs, past + ts, LANES)
    heads = fox_attention(p16, col16[SEG_FQ], p32, col32[SEG_FK], col32[SEG_FV], mp // ts, bs, ts,
                          f_s3, f_row(f_s3), n_fox, dh, heads,
                          cache_fox_k[l].reshape(bs, past, half), cache_fox_v[l].reshape(bs, past, half))
    fn_s = cumsum_time(g_s3.reshape(ms, LANES), bs, ts)
    heads, c_s, n_s, m_s = mlstm_heads(
        p32, col32[SEG_QK], col32[SEG_MO], p16, col16[SEG_MV], gates, mp // ts, bs, ts, fn_s,
        jnp.swapaxes(g_s3, 1, 2), jnp.swapaxes(fn_s.reshape(bs, ts, LANES), 1, 2), state_mlstm_conv[l], w_conv[l],
        state_mlstm_C[l], state_mlstm_n[l], state_mlstm_m[l], g_mlstm_head[l], i_off, f_off, heads, half)

    x1 = out_projection(heads, w_out[l].astype(BF16), xp2, xs2, tm_rows)
    y_p, y_s = moe_layer(x1, mp, g_norm_ffn[l], w_router, b_router, w_exp_gate[l], w_exp_up[l], w_exp_down[l],
                         g_norm_final, n_groups, n_experts // n_groups)

    def new_state(rows, b, t, g3, c_new, n_new, m_new, conv_hist):
        blk = p32[rows]
        fk = blk[:, col32[SEG_FK]:col32[SEG_FK] + half].reshape(b, t, n_fox, dh)
        fv = blk[:, col32[SEG_FV]:col32[SEG_FV] + half].reshape(b, t, n_fox, dh)
        qk_tail = blk[:, col32[SEG_QK]:col32[SEG_QK] + half].reshape(b, t, half)[:, t - min(t, conv_w - 1):]
        qk_hist = jnp.concatenate([conv_hist.astype(F32), qk_tail], axis=1)[:, -(conv_w - 1):]
        return tuple(a[None] for a in (fk, fv, g3[:, :, :n_fox], c_new, n_new, m_new, qk_hist))

    st_p = new_state(slice(0, mp), bp, tp, g_p3, c_p, n_p, m_p, hist_p)
    st_s = new_state(slice(mp, mp + ms), bs, ts, g_s3, c_s, n_s, m_s, state_mlstm_conv[l])
    return (y_p.reshape(bp, tp, d), y_s.reshape(bs, ts, d)) + st_p + st_s
```

```python
import functools
import math

import jax
import jax.numpy as jnp
from jax import lax
from jax.experimental import pallas as pl
from jax.experimental.pallas import tpu as pltpu

F32 = jnp.float32
BF16 = jnp.bfloat16
EPS = 1e-6
LANES = 128
V7X_VMEM_LIMIT = 56 * 1024 * 1024
TOP_K = 2
MLSTM_CHUNK = 256
DMA_ISSUE_UNROLL = 8
DOWN_CHUNK = 2048
HIGHEST = lax.Precision.HIGHEST
NEG_INF = float("-inf")
LOG2E = math.log2(math.e)


def _params(*sem):
    return pltpu.CompilerParams(dimension_semantics=sem, vmem_limit_bytes=V7X_VMEM_LIMIT)


def _tile(n, pref):
    if n <= pref:
        return n
    for t in range(pref - pref % 8, 7, -8):
        if n % t == 0:
            return t
    return n


def _log_sigmoid(x):
    return jnp.minimum(x, 0.0) - jnp.log1p(jnp.exp(-jnp.abs(x)))


def _split_maps(n_first):
    first = lambda i, *_: (jnp.minimum(i, n_first - 1), 0)
    second = lambda i, *_: (jnp.maximum(i - n_first, 0), 0)
    return first, second


def _rmsnorm_kernel(xa_ref, xb_ref, g_ref, o_ref, *, n_first):
    def emit(x_ref):
        x = x_ref[...]
        y = x * lax.rsqrt(jnp.mean(x * x, axis=-1, keepdims=True) + EPS)
        o_ref[...] = (y * g_ref[...]).astype(o_ref.dtype)

    @pl.when(pl.program_id(0) < n_first)
    def _():
        emit(xa_ref)

    @pl.when(pl.program_id(0) >= n_first)
    def _():
        emit(xb_ref)


def rmsnorm_cast(xa, xb, g, out_dtype):
    (ma, d), mb = xa.shape, xb.shape[0]
    tm = _tile(math.gcd(ma, mb), 256)
    first, second = _split_maps(ma // tm)
    return pl.pallas_call(
        functools.partial(_rmsnorm_kernel, n_first=ma // tm),
        grid=((ma + mb) // tm,),
        in_specs=[pl.BlockSpec((tm, d), first), pl.BlockSpec((tm, d), second),
                  pl.BlockSpec((1, d), lambda i: (0, 0))],
        out_specs=pl.BlockSpec((tm, d), lambda i: (i, 0)),
        out_shape=jax.ShapeDtypeStruct((ma + mb, d), out_dtype),
        compiler_params=_params("arbitrary"),
        name="rmsnorm_cast",
    )(xa, xb, g.reshape(1, d))


SEG_FQ, SEG_FK, SEG_FV, SEG_QK, SEG_MV, SEG_MO = range(6)
P32_SEGS = (SEG_FK, SEG_FV, SEG_QK, SEG_MO)
P16_SEGS = (SEG_FQ, SEG_MV)


def _inproj_kernel(x_ref, wa_ref, wb_ref, ws_ref, b_ref, p32_ref, p16_ref, gate_ref, *, nseg, n_fox, n_ml,
                   q_scale):
    j = pl.program_id(1)
    seg = j // nseg

    def emit(w_ref):
        acc = jnp.dot(x_ref[...], w_ref[...], preferred_element_type=F32)
        is16 = (seg == SEG_FQ) | (seg == SEG_MV)

        @pl.when(is16)
        def _():
            p16_ref[...] = (acc * jnp.where(seg == SEG_FQ, q_scale, 1.0)).astype(p16_ref.dtype)

        @pl.when(jnp.logical_not(is16))
        def _():
            p32_ref[...] = acc

    @pl.when(seg < 3)
    def _():
        emit(wa_ref)

    @pl.when(seg >= 3)
    def _():
        emit(wb_ref)

    @pl.when(j == 0)
    def _():
        p = jnp.dot(x_ref[...], ws_ref[...], preferred_element_type=F32) + b_ref[...]
        lane = lax.broadcasted_iota(jnp.int32, p.shape, 1)
        is_id = (lane >= n_fox) & (lane < n_fox + n_ml)
        gate_ref[...] = jnp.where(is_id, p, _log_sigmoid(p))


def _held_block(j, nseg, segs):
    seg = j // nseg
    blk = jnp.int32(0)
    for k, s in enumerate(segs):
        here = k * nseg + (j - s * nseg)
        done = (k + 1) * nseg - 1
        blk = jnp.where(seg == s, here, jnp.where(seg > s, done, blk))
    return blk


def in_projection(xn, w_a, w_b, w_gate, b_gate, n_fox, n_ml, tm, q_scale):
    m, d = xn.shape
    w = w_a.shape[1] // 3
    tn = _tile(w, 512)
    nseg = w // tn
    return pl.pallas_call(
        functools.partial(_inproj_kernel, nseg=nseg, n_fox=n_fox, n_ml=n_ml, q_scale=q_scale),
        grid=(pl.cdiv(m, tm), 6 * nseg),
        in_specs=[
            pl.BlockSpec((tm, d), lambda i, j: (i, 0)),
            pl.BlockSpec((d, tn), lambda i, j: (0, jnp.minimum(j, 3 * nseg - 1))),
            pl.BlockSpec((d, tn), lambda i, j: (0, jnp.maximum(j - 3 * nseg, 0))),
            pl.BlockSpec((d, LANES), lambda i, j: (0, 0)),
            pl.BlockSpec((1, LANES), lambda i, j: (0, 0)),
        ],
        out_specs=[
            pl.BlockSpec((tm, tn), lambda i, j: (i, _held_block(j, nseg, P32_SEGS))),
            pl.BlockSpec((tm, tn), lambda i, j: (i, _held_block(j, nseg, P16_SEGS))),
            pl.BlockSpec((tm, LANES), lambda i, j: (i, 0)),
        ],
        out_shape=[
            jax.ShapeDtypeStruct((m, len(P32_SEGS) * w), F32),
            jax.ShapeDtypeStruct((m, len(P16_SEGS) * w), BF16),
            jax.ShapeDtypeStruct((m, LANES), F32),
        ],
        compiler_params=_params("arbitrary", "arbitrary"),
        name="in_projection",
    )(xn, w_a, w_b, w_gate, b_gate)


def _cumsum_kernel(g_ref, f_ref, *, chunk):
    s = g_ref.shape[0]
    r = lax.broadcasted_iota(jnp.int32, (chunk, chunk), 0)
    c = lax.broadcasted_iota(jnp.int32, (chunk, chunk), 1)
    tri = (c <= r).astype(F32)
    carry = jnp.zeros((1, g_ref.shape[1]), F32)
    for k in range(s // chunk):
        blk = g_ref[k * chunk:(k + 1) * chunk, :]
        loc = jnp.dot(tri, blk, precision=HIGHEST, preferred_element_type=F32)
        f_ref[k * chunk:(k + 1) * chunk, :] = loc + carry
        carry = carry + loc[chunk - 1:chunk, :]


def cumsum_time(g, n_seq, s):
    n = g.shape[1]
    chunk = next(c for c in (256, 128, 64, 32, 16, 8) if s % c == 0)
    return pl.pallas_call(
        functools.partial(_cumsum_kernel, chunk=chunk),
        grid=(n_seq,),
        in_specs=[pl.BlockSpec((s, n), lambda i: (i, 0))],
        out_specs=pl.BlockSpec((s, n), lambda i: (i, 0)),
        out_shape=jax.ShapeDtypeStruct((n_seq * s, n), F32),
        compiler_params=_params("parallel"),
        name="cumsum_time",
    )(g)


def _fox_kernel(*refs, past, t, tq):
    o_ref = refs[-1]
    if past:
        q_ref, k_ref, v_ref, pk_ref, pv_ref, frow_ref = refs[:-2]
    else:
        q_ref, k_ref, v_ref, frow_ref = refs[:-2]
    dh = q_ref.shape[1]
    fk_all = frow_ref[...] * LOG2E
    lane = lax.broadcasted_iota(jnp.int32, (1, dh), 1)
    ones_col = jnp.where(lane == 0, 1.0, 0.0).astype(BF16)

    den_from_matmul = not past

    def with_ones(v):
        if not den_from_matmul:
            return v.astype(BF16)
        return jnp.concatenate([v.astype(BF16), jnp.broadcast_to(ones_col, v.shape)], axis=1)

    kb = k_ref[...].astype(BF16)
    vb = with_ones(v_ref[...])
    dn_t = (((1,), (1,)), ((), ()))
    row = lax.broadcasted_iota(jnp.int32, (tq, tq), 0)
    col = lax.broadcasted_iota(jnp.int32, (tq, tq), 1)
    diag_mask = col <= row

    def scores(qi):
        q = q_ref[qi * tq:(qi + 1) * tq, :]
        lo = qi * tq
        parts = []
        if past:
            parts.append((pk_ref[...].astype(BF16), with_ones(pv_ref[...]), fk_all[:, 0:past], False))
        if qi:
            parts.append((kb[0:lo, :], vb[0:lo, :], fk_all[:, past:past + lo], False))
        parts.append((kb[lo:lo + tq, :], vb[lo:lo + tq, :], fk_all[:, past + lo:past + lo + tq], True))
        logits = []
        for kp, _, fk, masked in parts:
            lg = lax.dot_general(q, kp, dn_t, preferred_element_type=F32) - fk
            logits.append(jnp.where(diag_mask, lg, NEG_INF) if masked else lg)
        mx = functools.reduce(jnp.maximum, [jnp.max(lg, axis=1, keepdims=True) for lg in logits])
        return [vp for _, vp, _, _ in parts], logits, mx

    def finish(qi, values, logits, mx):
        acc = 0.0
        den = 0.0
        for vp, lg in zip(values, logits):
            p = jnp.exp2(lg - mx)
            if not den_from_matmul:
                den = den + jnp.sum(p, axis=1, keepdims=True)
            acc = acc + jnp.dot(p.astype(BF16), vp, preferred_element_type=F32)
        if den_from_matmul:
            acc, den = acc[:, :dh], acc[:, dh:dh + 1]
        o_ref[qi * tq:(qi + 1) * tq, :] = (acc / den).astype(o_ref.dtype)

    n_tiles = t // tq
    pending = scores(0)
    for qi in range(n_tiles):
        upcoming = scores(qi + 1) if qi + 1 < n_tiles else None
        finish(qi, *pending)
        pending = upcoming


def fox_attention(p16, q_col, p32, k_col, v_col, row_blk, n_seq, t, f_row, n_heads, dh, heads,
                  past_k=None, past_v=None):
    past = 0 if past_k is None else past_k.shape[1]
    tq = _tile(t, 256)

    def head_blk(off):
        return pl.BlockSpec((t, dh), lambda i, h: (row_blk + i, off // dh + h))

    in_specs = [head_blk(q_col), head_blk(k_col), head_blk(v_col)]
    args = [p16, p32, p32]
    if past:
        past_blk = pl.BlockSpec((None, past, dh), lambda i, h: (i, 0, h))
        in_specs += [past_blk, past_blk]
        args += [past_k, past_v]
    in_specs += [
        pl.BlockSpec((None, None, 1, past + t), lambda i, h: (i, h, 0, 0)),
        pl.BlockSpec(memory_space=pl.ANY),
    ]
    args += [f_row, heads]
    return pl.pallas_call(
        functools.partial(_fox_kernel, past=past, t=t, tq=tq),
        grid=(n_seq, n_heads),
        in_specs=in_specs,
        out_specs=head_blk(0),
        out_shape=jax.ShapeDtypeStruct(heads.shape, heads.dtype),
        input_output_aliases={len(args) - 1: 0},
        compiler_params=_params("parallel", "parallel"),
        name="fox_attention",
    )(*args)


def _mlstm_kernel(*refs, t, chunk, conv_w, i_off, f_off, k_scale):
    refs = refs[:16] + refs[17:]
    (qraw_ref, kraw_ref, hq_ref, hk_ref, wq_ref, wk_ref, v_ref, og_ref, g_ref, f_ref,
     irow_ref, frow_ref, c0_ref, n0_ref, m0_ref, gh_ref,
     out_ref, c_ref, n_ref, m_ref,
     histq_ref, histk_ref, qc_ref, kc_ref) = refs
    h = pl.program_id(1)
    hpad = hq_ref.shape[0]

    def conv_silu(raw_ref, hist_in_ref, w_ref, hist_ref, scale, dst_ref):
        hist_ref[0:hpad, :] = hist_in_ref[...]
        hist_ref[hpad:hpad + t, :] = raw_ref[...]
        y = None
        for j in range(conv_w):
            start = hpad - (conv_w - 1) + j
            term = hist_ref[start:start + t, :] * w_ref[j:j + 1, :]
            y = term if y is None else y + term
        y = y * jax.nn.sigmoid(y)
        if scale != 1.0:
            y = y * scale
        dst_ref[...] = y.astype(dst_ref.dtype)

    conv_silu(qraw_ref, hq_ref, wq_ref, histq_ref, 1.0, qc_ref)
    conv_silu(kraw_ref, hk_ref, wk_ref, histk_ref, k_scale, kc_ref)

    gh = gh_ref[...]
    rr = lax.broadcasted_iota(jnp.int32, (chunk, chunk), 0)
    cc = lax.broadcasted_iota(jnp.int32, (chunk, chunk), 1)
    causal = cc <= rr
    lane = lax.broadcasted_iota(jnp.int32, (chunk, LANES), 1)

    def body(c, f_prev):
        r0 = pl.multiple_of(c * chunk, chunk)
        rows = pl.ds(r0, chunk)
        i_col = jnp.sum(jnp.where(lane == i_off + h, g_ref[rows, :], 0.0), axis=1, keepdims=True)
        b_col = jnp.sum(jnp.where(lane == f_off + h, f_ref[rows, :], 0.0), axis=1, keepdims=True) - f_prev
        i_row = irow_ref[pl.ds(c, 1), :]
        b_row = frow_ref[pl.ds(c, 1), :] - f_prev
        m_prev = m_ref[:, 0:1]
        q = qc_ref[rows, :]
        k = kc_ref[rows, :]
        v = v_ref[rows, :]
        cmat = c_ref[...]
        nvec = n_ref[...]

        d = jnp.where(causal, b_col - b_row + i_row, NEG_INF)
        inter = b_col + m_prev
        m_t = jnp.maximum(inter, jnp.max(d, axis=1, keepdims=True))
        w_inter = jnp.exp(inter - m_t)
        qk = lax.dot_general(q, k, (((1,), (1,)), ((), ())), preferred_element_type=F32)
        s = qk * jnp.exp(d - m_t)
        qc_state = lax.dot_general(q, cmat.astype(BF16), (((1,), (1,)), ((), ())),
                                   preferred_element_type=F32)
        num = w_inter * qc_state + jnp.dot(s.astype(BF16), v, preferred_element_type=F32)
        qn = jnp.sum(q.astype(F32) * nvec, axis=1, keepdims=True)
        den = w_inter * qn + jnp.sum(s, axis=1, keepdims=True)
        hval = num / jnp.maximum(jnp.abs(den), jnp.exp(-m_t))

        hn = hval * lax.rsqrt(jnp.mean(hval * hval, axis=1, keepdims=True) + EPS)
        out = jax.nn.sigmoid(og_ref[rows, :]) * hn * gh
        out_ref[rows, :] = out.astype(out_ref.dtype)

        b_last = b_col[chunk - 1:chunk, :]
        dec_row = b_last - b_row + i_row
        dec_col = b_last - b_col + i_col
        m_new = jnp.maximum(b_last + m_prev, jnp.max(dec_row, axis=1, keepdims=True))
        a = jnp.exp(b_last + m_prev - m_new)
        w_col = jnp.exp(dec_col - m_new)
        vw = (v.astype(F32) * w_col).astype(BF16)
        c_ref[...] = a * cmat + lax.dot_general(vw, k, (((0,), (0,)), ((), ())),
                                                preferred_element_type=F32)
        n_ref[...] = a * nvec + jnp.sum(w_col * k.astype(F32), axis=0, keepdims=True)
        m_ref[...] = jnp.broadcast_to(m_new, m_ref.shape)
        return f_prev + b_last

    c_ref[...] = c0_ref[...]
    n_ref[...] = n0_ref[...]
    m_ref[...] = m0_ref[...]
    lax.fori_loop(0, t // chunk, body, jnp.zeros((1, 1), F32), unroll=True)


def mlstm_heads(p32, qk_col, og_col, p16, v_col, gates, row_blk, n_seq, t, f_cum, gates_t, f_cum_t,
                conv_hist, w_conv, c0, n0, m0, g_head, i_off, f_off, heads, out_col):
    b = n_seq
    _, n_h, dv, dk = c0.shape
    qb, ogb, vb = qk_col // dk, og_col // dv, v_col // dv
    conv_w = w_conv.shape[0]
    chunk = _tile(t, MLSTM_CHUNK)
    nch = t // chunk
    hpad = 8
    hist = jnp.pad(conv_hist, ((0, 0), (hpad - (conv_w - 1), 0), (0, 0)))
    gt = gates_t.reshape(b, LANES, nch, chunk)
    ft = f_cum_t.reshape(b, LANES, nch, chunk)
    n0r = n0.reshape(b, n_h, 1, dk)
    m0r = jnp.broadcast_to(m0[:, :, None, None], (b, n_h, 1, LANES))
    ghr = g_head.reshape(n_h, 1, dv)

    rowcol = lambda width, off: pl.BlockSpec((t, width), lambda i, h: (row_blk + i, off + h))
    hcol = lambda off: pl.BlockSpec((None, hpad, dk), lambda i, h: (i, 0, off + h))
    in_specs = [
        rowcol(dk, qb), rowcol(dk, qb + n_h), hcol(0), hcol(n_h),
        pl.BlockSpec((conv_w, dk), lambda i, h: (0, h)),
        pl.BlockSpec((conv_w, dk), lambda i, h: (0, n_h + h)),
        rowcol(dv, vb), rowcol(dv, ogb),
        pl.BlockSpec((t, LANES), lambda i, h: (row_blk + i, 0)),
        pl.BlockSpec((t, LANES), lambda i, h: (i, 0)),
        pl.BlockSpec((None, None, nch, chunk), lambda i, h: (i, i_off + h, 0, 0)),
        pl.BlockSpec((None, None, nch, chunk), lambda i, h: (i, f_off + h, 0, 0)),
        pl.BlockSpec((None, None, dv, dk), lambda i, h: (i, h, 0, 0)),
        pl.BlockSpec((None, None, 1, dk), lambda i, h: (i, h, 0, 0)),
        pl.BlockSpec((None, None, 1, LANES), lambda i, h: (i, h, 0, 0)),
        pl.BlockSpec((None, 1, dv), lambda i, h: (h, 0, 0)),
    ]
    in_specs.append(pl.BlockSpec(memory_space=pl.ANY))
    args = [p32, p32, hist, hist, w_conv, w_conv, p16, p32, gates, f_cum, gt, ft, c0, n0r, m0r, ghr, heads]
    out_col = out_col // dv
    out_specs = [
        pl.BlockSpec((t, dv), lambda i, h: (row_blk + i, out_col + h)),
        pl.BlockSpec((None, None, dv, dk), lambda i, h: (i, h, 0, 0)),
        pl.BlockSpec((None, None, 1, dk), lambda i, h: (i, h, 0, 0)),
        pl.BlockSpec((None, None, 1, LANES), lambda i, h: (i, h, 0, 0)),
    ]
    out_shape = [
        jax.ShapeDtypeStruct(heads.shape, heads.dtype),
        jax.ShapeDtypeStruct((b, n_h, dv, dk), F32),
        jax.ShapeDtypeStruct((b, n_h, 1, dk), F32),
        jax.ShapeDtypeStruct((b, n_h, 1, LANES), F32),
    ]
    out, c_new, n_new, m_new = pl.pallas_call(
        functools.partial(_mlstm_kernel, t=t, chunk=chunk, conv_w=conv_w, i_off=i_off, f_off=f_off,
                          k_scale=float(dk) ** -0.5),
        grid=(b, n_h),
        in_specs=in_specs,
        out_specs=out_specs,
        out_shape=out_shape,
        input_output_aliases={len(args) - 1: 0},
        scratch_shapes=[
            pltpu.VMEM((t + hpad, dk), F32), pltpu.VMEM((t + hpad, dk), F32),
            pltpu.VMEM((t, dk), BF16), pltpu.VMEM((t, dk), BF16),
        ],
        compiler_params=_params("parallel", "parallel"),
        name="mlstm_heads",
    )(*args)
    return out, c_new, n_new[:, :, 0, :], m_new[:, :, 0, 0]


def _outproj_kernel(a_ref, w_ref, xa_ref, xb_ref, o_ref, *, n_first):
    acc = jnp.dot(a_ref[...], w_ref[...], preferred_element_type=F32)

    @pl.when(pl.program_id(0) < n_first)
    def _():
        o_ref[...] = xa_ref[...] + acc

    @pl.when(pl.program_id(0) >= n_first)
    def _():
        o_ref[...] = xb_ref[...] + acc


def out_projection(heads, w_out, xa, xb, tm):
    (ma, d), mb = xa.shape, xb.shape[0]
    tn = _tile(d, 512)
    assert ma % tm == 0
    n_first = ma // tm
    first_j = lambda i, j: (jnp.minimum(i, n_first - 1), j)
    second_j = lambda i, j: (jnp.maximum(i - n_first, 0), j)
    return pl.pallas_call(
        functools.partial(_outproj_kernel, n_first=n_first),
        grid=(pl.cdiv(ma + mb, tm), d // tn),
        in_specs=[
            pl.BlockSpec((tm, d), lambda i, j: (i, 0)),
            pl.BlockSpec((d, tn), lambda i, j: (0, j)),
            pl.BlockSpec((tm, tn), first_j), pl.BlockSpec((tm, tn), second_j),
        ],
        out_specs=pl.BlockSpec((tm, tn), lambda i, j: (i, j)),
        out_shape=jax.ShapeDtypeStruct((ma + mb, d), F32),
        compiler_params=_params("arbitrary", "arbitrary"),
        name="out_projection",
    )(heads, w_out, xa, xb)


def _pack_bf16_pairs(x):
    half = x.shape[1] // 2
    bits = lax.bitcast_convert_type(x.astype(BF16).astype(F32), jnp.uint32)
    return (bits[:, :half] >> 16) | (bits[:, half:] & jnp.uint32(0xFFFF0000))


def _unpack_pairs_f32(w):
    lo = lax.bitcast_convert_type(w << 16, F32)
    hi = lax.bitcast_convert_type(w & jnp.uint32(0xFFFF0000), F32)
    return lo, hi


def _unpack_bf16_pairs(w):
    lo, hi = _unpack_pairs_f32(w)
    return lo.astype(BF16), hi.astype(BF16)


def _router_kernel(x_ref, g_ref, whi_ref, wlo_ref, b_ref, xn_ref, eid_ref, gate_ref, *, n_groups, per_group):
    x = x_ref[...]
    xn = x * lax.rsqrt(jnp.mean(x * x, axis=-1, keepdims=True) + EPS) * g_ref[...]
    xn_ref[...] = _pack_bf16_pairs(xn)
    x_hi = xn.astype(BF16)
    x_lo = (xn - x_hi.astype(F32)).astype(BF16)
    logits = (jnp.dot(x_hi, whi_ref[...], preferred_element_type=F32)
              + jnp.dot(x_lo, whi_ref[...], preferred_element_type=F32)
              + jnp.dot(x_hi, wlo_ref[...], preferred_element_type=F32)) + b_ref[...]
    lane = lax.broadcasted_iota(jnp.int32, logits.shape, 1).astype(F32)
    big = float(LANES)
    is_g = lane < n_groups
    gl = jnp.where(is_g, logits, NEG_INF)
    gmax = jnp.max(gl, axis=1, keepdims=True)
    g_idx = jnp.min(jnp.where(gl == gmax, lane, big), axis=1, keepdims=True)
    g_prob = 1.0 / jnp.sum(jnp.where(is_g, jnp.exp(logits - gmax), 0.0), axis=1, keepdims=True)
    lo = n_groups + per_group * g_idx
    el = jnp.where((lane >= lo) & (lane < lo + per_group), logits, NEG_INF)
    e1 = jnp.max(el, axis=1, keepdims=True)
    i1 = jnp.min(jnp.where(el == e1, lane, big), axis=1, keepdims=True)
    el2 = jnp.where(lane == i1, NEG_INF, el)
    e2 = jnp.max(el2, axis=1, keepdims=True)
    i2 = jnp.min(jnp.where(el2 == e2, lane, big), axis=1, keepdims=True)
    r = jnp.exp(e2 - e1)
    w1 = g_prob / (1.0 + r)
    w2 = g_prob * r / (1.0 + r)
    eid = jnp.where(lane == 0.0, i1 - n_groups, jnp.where(lane == 1.0, i2 - n_groups, 0.0))
    eid_ref[...] = eid.astype(jnp.int32)
    gate_ref[...] = jnp.where(lane == 0.0, w1, jnp.where(lane == 1.0, w2, 0.0))


def router(x, g, w_router, b_router, n_groups, per_group):
    m, d = x.shape
    tm = _tile(m, 256)
    w_hi = w_router.astype(BF16)
    w_lo = (w_router - w_hi.astype(F32)).astype(BF16)
    return pl.pallas_call(
        functools.partial(_router_kernel, n_groups=n_groups, per_group=per_group),
        grid=(m // tm,),
        in_specs=[
            pl.BlockSpec((tm, d), lambda i: (i, 0)),
            pl.BlockSpec((1, d), lambda i: (0, 0)),
            pl.BlockSpec((d, LANES), lambda i: (0, 0)),
            pl.BlockSpec((d, LANES), lambda i: (0, 0)),
            pl.BlockSpec((1, LANES), lambda i: (0, 0)),
        ],
        out_specs=[
            pl.BlockSpec((tm, d // 2), lambda i: (i, 0)),
            pl.BlockSpec((tm, LANES), lambda i: (i, 0)),
            pl.BlockSpec((tm, LANES), lambda i: (i, 0)),
        ],
        out_shape=[
            jax.ShapeDtypeStruct((m, d // 2), jnp.uint32),
            jax.ShapeDtypeStruct((m, LANES), jnp.int32),
            jax.ShapeDtypeStruct((m, LANES), F32),
        ],
        compiler_params=_params("parallel"),
        name="router",
    )(x, g.reshape(1, d), w_hi, w_lo, b_router)


def _gather_kernel(idx0_ref, idx1_ref, src_ref, o_ref, buf_ref, sem_ref, *, tm):
    step = pl.program_id(0)
    nsteps = pl.num_programs(0)

    def issue(idx_ref, slot):
        def body(r, carry):
            for q, row in enumerate((r, r + tm // 2)):
                tok = idx_ref[0, row]
                pltpu.make_async_copy(src_ref.at[pl.ds(tok, 1), :], buf_ref.at[slot, pl.ds(row, 1), :],
                                      sem_ref.at[slot]).start(priority=q)
            return carry
        lax.fori_loop(0, tm // 2, body, 0, unroll=DMA_ISSUE_UNROLL // 2)

    @pl.when(step == 0)
    def _():
        issue(idx0_ref, 0)

    @pl.when(step + 1 < nsteps)
    def _():
        issue(idx1_ref, (step + 1) % 2)

    slot = step % 2
    pltpu.make_async_copy(src_ref.at[pl.ds(0, tm), :], buf_ref.at[slot], sem_ref.at[slot]).wait()
    half = buf_ref.shape[2]
    lo, hi = _unpack_bf16_pairs(buf_ref[slot])
    o_ref[:, :half] = lo
    o_ref[:, half:] = hi


def _smem_tiles(n_steps, width):
    cur = pl.BlockSpec((None, 1, width), lambda i: (i, 0, 0), memory_space=pltpu.SMEM)
    nxt = pl.BlockSpec((None, 1, width), lambda i: (jnp.minimum(i + 1, n_steps - 1), 0, 0),
                       memory_space=pltpu.SMEM)
    return cur, nxt


def gather_rows(src, idx, tm):
    n, half = src.shape
    d = 2 * half
    a = idx.shape[0]
    n_steps = a // tm
    cur, nxt = _smem_tiles(n_steps, tm)
    idx3 = idx.reshape(n_steps, 1, tm)
    return pl.pallas_call(
        functools.partial(_gather_kernel, tm=tm),
        grid=(n_steps,),
        in_specs=[cur, nxt, pl.BlockSpec(memory_space=pl.ANY)],
        out_specs=pl.BlockSpec((tm, d), lambda i: (i, 0)),
        scratch_shapes=[pltpu.VMEM((2, tm, half), src.dtype), pltpu.SemaphoreType.DMA((2,))],
        out_shape=jax.ShapeDtypeStruct((a, d), BF16),
        compiler_params=_params("arbitrary"),
        name="gather_rows",
    )(idx3, idx3, src)


def _expert_up_kernel(te_ref, nv_ref, x_ref, wg_ref, wu_ref, h_ref):
    t = pl.program_id(1)

    @pl.when(t < nv_ref[0])
    def _():
        x = x_ref[...]
        g = jnp.dot(x, wg_ref[...].astype(BF16), preferred_element_type=F32)
        u = jnp.dot(x, wu_ref[...].astype(BF16), preferred_element_type=F32)
        h_ref[...] = (g * jax.nn.sigmoid(g) * u).astype(h_ref.dtype)

    @pl.when(t >= nv_ref[0])
    def _():
        h_ref[...] = jnp.zeros_like(h_ref)


def expert_up(xg, w_gate, w_up, tile_expert, n_valid, tm):
    a, d = xg.shape
    _, _, f = w_gate.shape
    tn = _tile(f, 512)
    return pl.pallas_call(
        _expert_up_kernel,
        grid_spec=pltpu.PrefetchScalarGridSpec(
            num_scalar_prefetch=2,
            grid=(f // tn, a // tm),
            in_specs=[
                pl.BlockSpec((tm, d), lambda c, t, te, nv: (t, 0)),
                pl.BlockSpec((None, d, tn), lambda c, t, te, nv: (te[t], 0, c)),
                pl.BlockSpec((None, d, tn), lambda c, t, te, nv: (te[t], 0, c)),
            ],
            out_specs=pl.BlockSpec((tm, tn), lambda c, t, te, nv: (t, c)),
        ),
        out_shape=jax.ShapeDtypeStruct((a, f), BF16),
        compiler_params=_params("arbitrary", "arbitrary"),
        name="expert_up",
    )(tile_expert, n_valid, xg, w_gate, w_up)


def _expert_down_kernel(te_ref, nv_ref, h_ref, wd_ref, y_ref):
    t = pl.program_id(1)

    @pl.when(t < nv_ref[0])
    def _():
        y = jnp.dot(h_ref[...], wd_ref[...].astype(BF16), preferred_element_type=F32)
        y_ref[...] = _pack_bf16_pairs(y)

    @pl.when(t >= nv_ref[0])
    def _():
        y_ref[...] = jnp.zeros_like(y_ref)


def expert_down(hg, w_down, tile_expert, n_valid, tm):
    a, f = hg.shape
    _, _, d = w_down.shape
    tn = _tile(d, DOWN_CHUNK)
    return pl.pallas_call(
        _expert_down_kernel,
        grid_spec=pltpu.PrefetchScalarGridSpec(
            num_scalar_prefetch=2,
            grid=(d // tn, a // tm),
            in_specs=[
                pl.BlockSpec((tm, f), lambda c, t, te, nv: (t, 0)),
                pl.BlockSpec((None, f, tn), lambda c, t, te, nv: (te[t], 0, c)),
            ],
            out_specs=pl.BlockSpec((tm, tn // 2), lambda c, t, te, nv: (t, c)),
        ),
        out_shape=jax.ShapeDtypeStruct((a, d // 2), jnp.uint32),
        compiler_params=_params("arbitrary", "arbitrary"),
        name="expert_down",
    )(tile_expert, n_valid, hg, w_down)


def _combine_kernel(pos0_ref, pos1_ref, x_ref, gate_ref, g_ref, y_ref, oa_ref, ob_ref, buf_ref, sem_ref,
                    *, tm, n_first, chunk):
    step = pl.program_id(0)
    nsteps = pl.num_programs(0)

    def issue(pos_ref, slot):
        def body(r, carry):
            for k in range(TOP_K):
                p = pos_ref[0, r * TOP_K + k]
                pltpu.make_async_copy(y_ref.at[pl.ds(p, 1), :], buf_ref.at[slot, k, pl.ds(r, 1), :],
                                      sem_ref.at[slot]).start(priority=k)
            return carry
        lax.fori_loop(0, tm, body, 0, unroll=DMA_ISSUE_UNROLL)

    @pl.when(step == 0)
    def _():
        issue(pos0_ref, 0)

    @pl.when(step + 1 < nsteps)
    def _():
        issue(pos1_ref, (step + 1) % 2)

    slot = step % 2
    for k in range(TOP_K):
        pltpu.make_async_copy(y_ref.at[pl.ds(0, tm), :], buf_ref.at[slot, k], sem_ref.at[slot]).wait()
    gates = gate_ref[...]
    x = x_ref[...]
    for k in range(TOP_K):
        lo, hi = _unpack_pairs_f32(buf_ref[slot, k])
        pieces = []
        for c in range(0, lo.shape[1], chunk // 2):
            pieces += [lo[:, c:c + chunk // 2], hi[:, c:c + chunk // 2]]
        x = x + gates[:, k:k + 1] * jnp.concatenate(pieces, axis=1)
    y = x * lax.rsqrt(jnp.mean(x * x, axis=-1, keepdims=True) + EPS) * g_ref[...]

    @pl.when(step < n_first)
    def _():
        oa_ref[...] = y

    @pl.when(step >= n_first)
    def _():
        ob_ref[...] = y


def combine_norm(x, gates, pos, yg, g_final, m_first, tm):
    m, d = x.shape
    n_steps = m // tm
    n_first = m_first // tm
    cur, nxt = _smem_tiles(n_steps, tm * TOP_K)
    pos3 = pos.reshape(n_steps, 1, tm * TOP_K)
    first, second = _split_maps(n_first)
    return pl.pallas_call(
        functools.partial(_combine_kernel, tm=tm, n_first=n_first, chunk=_tile(d, DOWN_CHUNK)),
        grid=(n_steps,),
        in_specs=[
            cur, nxt,
            pl.BlockSpec((tm, d), lambda i: (i, 0)),
            pl.BlockSpec((tm, LANES), lambda i: (i, 0)),
            pl.BlockSpec((1, d), lambda i: (0, 0)),
            pl.BlockSpec(memory_space=pl.ANY),
        ],
        out_specs=[pl.BlockSpec((tm, d), first), pl.BlockSpec((tm, d), second)],
        scratch_shapes=[pltpu.VMEM((2, TOP_K, tm, d // 2), jnp.uint32), pltpu.SemaphoreType.DMA((2,))],
        out_shape=[jax.ShapeDtypeStruct((m_first, d), F32), jax.ShapeDtypeStruct((m - m_first, d), F32)],
        compiler_params=_params("arbitrary"),
        name="combine_norm",
    )(pos3, pos3, x, gates, g_final.reshape(1, d), yg)


def _dispatch_plan(eid, n_experts, tm):
    n = eid.shape[0]
    a = n * TOP_K
    e_flat = eid.reshape(a)
    onehot = (e_flat[:, None] == jnp.arange(n_experts, dtype=jnp.int32)[None, :]).astype(jnp.int32)
    csum = jnp.cumsum(onehot, axis=0)
    counts = csum[-1]
    rank = jnp.take_along_axis(csum, e_flat[:, None], axis=1)[:, 0] - 1
    padded = ((counts + tm - 1) // tm) * tm
    pad_end = jnp.cumsum(padded)
    pos = (pad_end - padded)[e_flat] + rank
    n_tiles = a // tm + n_experts
    src = (jnp.arange(n_tiles * tm, dtype=jnp.int32) % n).at[pos].set(jnp.arange(a, dtype=jnp.int32) // TOP_K)
    tile_end = pad_end // tm
    tile_ids = jnp.arange(n_tiles, dtype=jnp.int32)
    tile_expert = jnp.sum((tile_ids[:, None] >= tile_end[None, :]).astype(jnp.int32), axis=1)
    tile_expert = jnp.minimum(tile_expert, n_experts - 1)
    n_valid = tile_end[-1:].astype(jnp.int32)
    return pos.astype(jnp.int32), src, tile_expert, n_valid


def moe_layer(x, m_first, g_ffn, w_router, b_router, w_gate, w_up, w_down, g_final, n_groups, per_group):
    m, d = x.shape
    n_experts = n_groups * per_group
    xn, eid, gates = router(x, g_ffn, w_router, b_router, n_groups, per_group)
    a = m * TOP_K
    tm = _tile(a, min(512, max(64, a // n_experts)))
    pos, src, tile_expert, n_valid = _dispatch_plan(eid[:, :TOP_K], n_experts, tm)
    xg = gather_rows(xn, src, _tile(tm, 256))
    hg = expert_up(xg, w_gate, w_up, tile_expert, n_valid, tm)
    yg = expert_down(hg, w_down, tile_expert, n_valid, tm)
    return combine_norm(x, gates, pos, yg, g_final, m_first, _tile(math.gcd(m_first, m - m_first), 128))


def kernel(x_prompt, x_sample, cache_fox_k, cache_fox_v, cache_fox_logf, state_mlstm_C, state_mlstm_n, state_mlstm_m, state_mlstm_conv, g_norm_mix, w_in, b_fox_f, b_mlstm_i, b_mlstm_f, w_conv, g_mlstm_head, w_out, g_norm_ffn, w_router_group, b_router_group, w_router_expert, b_router_expert, w_exp_gate, w_exp_up, w_exp_down, g_norm_final):
    depth = w_in.shape[0]
    assert depth == 1, "the final norm is fused into the last layer's MoE combine; one layer supported"
    bp, tp, d = x_prompt.shape
    bs, ts, _ = x_sample.shape
    n_fox, dh = cache_fox_k.shape[-2:]
    n_ml, dv, dk = state_mlstm_C.shape[-3:]
    past = cache_fox_k.shape[2]
    n_groups = w_router_group.shape[-1]
    n_experts = w_router_expert.shape[-1]
    conv_w = w_conv.shape[1]
    half = d // 2
    mp, ms = bp * tp, bs * ts
    assert n_fox * dh == half and n_ml * dv == half and 2 * n_ml * dk == half
    assert n_fox + 2 * n_ml <= LANES and n_groups + n_experts <= LANES
    assert mp % ts == 0 and ts >= conv_w - 1, "sample sequences are addressed as row blocks after the prompt rows"
    l = 0
    sizes = (half, half, half, n_fox, half, half, half, n_ml, n_ml)
    offs = [0]
    for s in sizes:
        offs.append(offs[-1] + s)
    col = lambda i: w_in[l][:, offs[i]:offs[i + 1]]
    w_a = w_in[l][:, offs[0]:offs[3]].astype(BF16)
    w_b = w_in[l][:, offs[4]:offs[7]].astype(BF16)
    n_gate = n_fox + 2 * n_ml
    w_gatecols = jnp.pad(jnp.concatenate([col(3), col(7), col(8)], axis=1), ((0, 0), (0, LANES - n_gate))).astype(BF16)
    b_gatecols = jnp.pad(jnp.concatenate([b_fox_f[l], b_mlstm_i[l], b_mlstm_f[l]]).astype(F32),
                         (0, LANES - n_gate)).reshape(1, LANES)
    n_r = n_groups + n_experts
    w_router = jnp.pad(jnp.concatenate([w_router_group[l], w_router_expert[l]], axis=1), ((0, 0), (0, LANES - n_r)))
    b_router = jnp.pad(jnp.concatenate([b_router_group[l], b_router_expert[l]]).astype(F32),
                       (0, LANES - n_r)).reshape(1, LANES)
    i_off, f_off = n_fox, n_fox + n_ml
    col32 = {s: k * half for k, s in enumerate(P32_SEGS)}
    col16 = {s: k * half for k, s in enumerate(P16_SEGS)}

    xp2, xs2 = x_prompt.reshape(mp, d), x_sample.reshape(ms, d)
    tm_rows = _tile(mp, 1024)
    xn = rmsnorm_cast(xp2, xs2, g_norm_mix[l], BF16)
    p32, p16, gates = in_projection(xn, w_a, w_b, w_gatecols, b_gatecols, n_fox, n_ml, tm_rows,
                                    float(dh) ** -0.5 * LOG2E)

    f_p = cumsum_time(gates, bp, tp)
    f_p3 = f_p.reshape(bp, tp, LANES)
    g_p3 = gates[:mp].reshape(bp, tp, LANES)
    f_row = lambda f3: jnp.swapaxes(f3[:, :, :n_fox], 1, 2)[:, :, None, :]
    heads = fox_attention(p16, col16[SEG_FQ], p32, col32[SEG_FK], col32[SEG_FV], 0, bp, tp,
                          f_row(f_p3), n_fox, dh, xn)
    hist_p = jnp.zeros((bp, conv_w - 1, half), F32)
    heads, c_p, n_p, m_p = mlstm_heads(
        p32, col32[SEG_QK], col32[SEG_MO], p16, col16[SEG_MV], gates, 0, bp, tp, f_p,
        jnp.swapaxes(g_p3, 1, 2), jnp.swapaxes(f_p3, 1, 2), hist_p, w_conv[l],
        jnp.zeros((bp, n_ml, dv, dk), F32), jnp.zeros((bp, n_ml, dk), F32), jnp.zeros((bp, n_ml), F32),
        g_mlstm_head[l], i_off, f_off, heads, half)

    g_s3 = gates[mp:].reshape(bs, ts, LANES)
    cache_pad = jnp.pad(cache_fox_logf[l].astype(F32), ((0, 0), (0, 0), (0, LANES - n_fox)))
    logf_s = jnp.concatenate([cache_pad, g_s3], axis=1)
    f_s3 = cumsum_time(logf_s.reshape(bs * (past + ts), LANES), bs, past + ts).reshape(bs, past + ts, LANES)
    heads = fox_attention(p16, col16[SEG_FQ], p32, col32[SEG_FK], col32[SEG_FV], mp // ts, bs, ts,
                          f_row(f_s3), n_fox, dh, heads,
                          cache_fox_k[l].reshape(bs, past, half), cache_fox_v[l].reshape(bs, past, half))
    fn_s = cumsum_time(g_s3.reshape(ms, LANES), bs, ts)
    heads, c_s, n_s, m_s = mlstm_heads(
        p32, col32[SEG_QK], col32[SEG_MO], p16, col16[SEG_MV], gates, mp // ts, bs, ts, fn_s,
        jnp.swapaxes(g_s3, 1, 2), jnp.swapaxes(fn_s.reshape(bs, ts, LANES), 1, 2), state_mlstm_conv[l], w_conv[l],
        state_mlstm_C[l], state_mlstm_n[l], state_mlstm_m[l], g_mlstm_head[l], i_off, f_off, heads, half)

    x1 = out_projection(heads, w_out[l].astype(BF16), xp2, xs2, tm_rows)
    y_p, y_s = moe_layer(x1, mp, g_norm_ffn[l], w_router, b_router, w_exp_gate[l], w_exp_up[l], w_exp_down[l],
                         g_norm_final, n_groups, n_experts // n_groups)

    def new_state(rows, b, t, g3, c_new, n_new, m_new, conv_hist):
        blk = p32[rows]
        fk = blk[:, col32[SEG_FK]:col32[SEG_FK] + half].reshape(b, t, n_fox, dh)
        fv = blk[:, col32[SEG_FV]:col32[SEG_FV] + half].reshape(b, t, n_fox, dh)
        qk_tail = blk[:, col32[SEG_QK]:col32[SEG_QK] + half].reshape(b, t, half)[:, t - min(t, conv_w - 1):]
        qk_hist = jnp.concatenate([conv_hist.astype(F32), qk_tail], axis=1)[:, -(conv_w - 1):]
        return tuple(a[None] for a in (fk, fv, g3[:, :, :n_fox], c_new, n_new, m_new, qk_hist))

    st_p = new_state(slice(0, mp), bp, tp, g_p3, c_p, n_p, m_p, hist_p)
    st_s = new_state(slice(mp, mp + ms), bs, ts, g_s3, c_s, n_s, m_s, state_mlstm_conv[l])
    return (y_p.reshape(bp, tp, d), y_s.reshape(bs, ts, d)) + st_p + st_s
```

```python
import functools
import math

import jax
import jax.numpy as jnp
from jax import lax
from jax.experimental import pallas as pl
from jax.experimental.pallas import tpu as pltpu

F32 = jnp.float32
BF16 = jnp.bfloat16
EPS = 1e-6
LANES = 128
V7X_VMEM_LIMIT = 56 * 1024 * 1024
TOP_K = 2
MLSTM_CHUNK = 256
DMA_ISSUE_UNROLL = 8
COMBINE_COLS = 256
DOWN_CHUNK = 2048
HIGHEST = lax.Precision.HIGHEST
NEG_INF = float("-inf")
LOG2E = math.log2(math.e)


def _params(*sem):
    return pltpu.CompilerParams(dimension_semantics=sem, vmem_limit_bytes=V7X_VMEM_LIMIT)


def _tile(n, pref):
    if n <= pref:
        return n
    for t in range(pref - pref % 8, 7, -8):
        if n % t == 0:
            return t
    return n


def _log_sigmoid(x):
    return jnp.minimum(x, 0.0) - jnp.log1p(jnp.exp(-jnp.abs(x)))


def _split_maps(n_first):
    first = lambda i, *_: (jnp.minimum(i, n_first - 1), 0)
    second = lambda i, *_: (jnp.maximum(i - n_first, 0), 0)
    return first, second


def _rmsnorm_kernel(xa_ref, xb_ref, g_ref, o_ref, *, n_first):
    def emit(x_ref):
        x = x_ref[...]
        y = x * lax.rsqrt(jnp.mean(x * x, axis=-1, keepdims=True) + EPS)
        o_ref[...] = (y * g_ref[...]).astype(o_ref.dtype)

    @pl.when(pl.program_id(0) < n_first)
    def _():
        emit(xa_ref)

    @pl.when(pl.program_id(0) >= n_first)
    def _():
        emit(xb_ref)


def rmsnorm_cast(xa, xb, g, out_dtype):
    (ma, d), mb = xa.shape, xb.shape[0]
    tm = _tile(math.gcd(ma, mb), 256)
    first, second = _split_maps(ma // tm)
    return pl.pallas_call(
        functools.partial(_rmsnorm_kernel, n_first=ma // tm),
        grid=((ma + mb) // tm,),
        in_specs=[pl.BlockSpec((tm, d), first), pl.BlockSpec((tm, d), second),
                  pl.BlockSpec((1, d), lambda i: (0, 0))],
        out_specs=pl.BlockSpec((tm, d), lambda i: (i, 0)),
        out_shape=jax.ShapeDtypeStruct((ma + mb, d), out_dtype),
        compiler_params=_params("arbitrary"),
        name="rmsnorm_cast",
    )(xa, xb, g.reshape(1, d))


def _wsplit_kernel(w_ref, wa_ref, wb_ref, *, b_start):
    w = w_ref[...]
    wa_ref[...] = w[:, :wa_ref.shape[1]].astype(wa_ref.dtype)
    wb_ref[...] = w[:, b_start:b_start + wb_ref.shape[1]].astype(wb_ref.dtype)


def split_cast_weights(w, a_cols, b_start, b_cols):
    d, n = w.shape
    tk = _tile(d, 256)
    return pl.pallas_call(
        functools.partial(_wsplit_kernel, b_start=b_start),
        grid=(d // tk,),
        in_specs=[pl.BlockSpec((tk, n), lambda i: (i, 0))],
        out_specs=[pl.BlockSpec((tk, a_cols), lambda i: (i, 0)), pl.BlockSpec((tk, b_cols), lambda i: (i, 0))],
        out_shape=[jax.ShapeDtypeStruct((d, a_cols), BF16), jax.ShapeDtypeStruct((d, b_cols), BF16)],
        compiler_params=_params("parallel"),
        name="split_cast_weights",
    )(w)


SEG_FQ, SEG_FK, SEG_FV, SEG_QK, SEG_MV, SEG_MO = range(6)
P32_SEGS = (SEG_FK, SEG_FV, SEG_QK, SEG_MO)
P16_SEGS = (SEG_FQ, SEG_MV)


def _inproj_kernel(x_ref, wa_ref, wb_ref, ws_ref, b_ref, p32_ref, p16_ref, gate_ref, *, nseg, n_fox, n_ml,
                   q_scale):
    j = pl.program_id(1)
    seg = j // nseg

    def emit(w_ref):
        acc = jnp.dot(x_ref[...], w_ref[...], preferred_element_type=F32)
        is16 = (seg == SEG_FQ) | (seg == SEG_MV)

        @pl.when(is16)
        def _():
            p16_ref[...] = (acc * jnp.where(seg == SEG_FQ, q_scale, 1.0)).astype(p16_ref.dtype)

        @pl.when(jnp.logical_not(is16))
        def _():
            p32_ref[...] = acc

    @pl.when(seg < 3)
    def _():
        emit(wa_ref)

    @pl.when(seg >= 3)
    def _():
        emit(wb_ref)

    @pl.when(j == 0)
    def _():
        p = jnp.dot(x_ref[...], ws_ref[...], preferred_element_type=F32) + b_ref[...]
        lane = lax.broadcasted_iota(jnp.int32, p.shape, 1)
        is_id = (lane >= n_fox) & (lane < n_fox + n_ml)
        gate_ref[...] = jnp.where(is_id, p, _log_sigmoid(p))


def _held_block(j, nseg, segs):
    seg = j // nseg
    blk = jnp.int32(0)
    for k, s in enumerate(segs):
        here = k * nseg + (j - s * nseg)
        done = (k + 1) * nseg - 1
        blk = jnp.where(seg == s, here, jnp.where(seg > s, done, blk))
    return blk


def in_projection(xn, w_a, w_b, w_gate, b_gate, n_fox, n_ml, tm, q_scale):
    m, d = xn.shape
    w = w_a.shape[1] // 3
    tn = _tile(w, 512)
    nseg = w // tn
    return pl.pallas_call(
        functools.partial(_inproj_kernel, nseg=nseg, n_fox=n_fox, n_ml=n_ml, q_scale=q_scale),
        grid=(pl.cdiv(m, tm), 6 * nseg),
        in_specs=[
            pl.BlockSpec((tm, d), lambda i, j: (i, 0)),
            pl.BlockSpec((d, tn), lambda i, j: (0, jnp.minimum(j, 3 * nseg - 1))),
            pl.BlockSpec((d, tn), lambda i, j: (0, jnp.maximum(j - 3 * nseg, 0))),
            pl.BlockSpec((d, LANES), lambda i, j: (0, 0)),
            pl.BlockSpec((1, LANES), lambda i, j: (0, 0)),
        ],
        out_specs=[
            pl.BlockSpec((tm, tn), lambda i, j: (i, _held_block(j, nseg, P32_SEGS))),
            pl.BlockSpec((tm, tn), lambda i, j: (i, _held_block(j, nseg, P16_SEGS))),
            pl.BlockSpec((tm, LANES), lambda i, j: (i, 0)),
        ],
        out_shape=[
            jax.ShapeDtypeStruct((m, len(P32_SEGS) * w), F32),
            jax.ShapeDtypeStruct((m, len(P16_SEGS) * w), BF16),
            jax.ShapeDtypeStruct((m, LANES), F32),
        ],
        compiler_params=_params("arbitrary", "arbitrary"),
        name="in_projection",
    )(xn, w_a, w_b, w_gate, b_gate)


def _cumsum_kernel(g_ref, f_ref, *, chunk):
    s = g_ref.shape[0]
    r = lax.broadcasted_iota(jnp.int32, (chunk, chunk), 0)
    c = lax.broadcasted_iota(jnp.int32, (chunk, chunk), 1)
    tri = (c <= r).astype(F32)
    carry = jnp.zeros((1, g_ref.shape[1]), F32)
    for k in range(s // chunk):
        blk = g_ref[k * chunk:(k + 1) * chunk, :]
        loc = jnp.dot(tri, blk, precision=HIGHEST, preferred_element_type=F32)
        f_ref[k * chunk:(k + 1) * chunk, :] = loc + carry
        carry = carry + loc[chunk - 1:chunk, :]


def cumsum_time(g, n_seq, s):
    n = g.shape[1]
    chunk = next(c for c in (256, 128, 64, 32, 16, 8) if s % c == 0)
    return pl.pallas_call(
        functools.partial(_cumsum_kernel, chunk=chunk),
        grid=(n_seq,),
        in_specs=[pl.BlockSpec((s, n), lambda i: (i, 0))],
        out_specs=pl.BlockSpec((s, n), lambda i: (i, 0)),
        out_shape=jax.ShapeDtypeStruct((n_seq * s, n), F32),
        compiler_params=_params("parallel"),
        name="cumsum_time",
    )(g)


def _fox_kernel(*refs, past, t, tq):
    o_ref = refs[-1]
    if past:
        q_ref, k_ref, v_ref, pk_ref, pv_ref, frow_ref = refs[:-2]
    else:
        q_ref, k_ref, v_ref, frow_ref = refs[:-2]
    dh = q_ref.shape[1]
    fk_all = frow_ref[...] * LOG2E
    lane = lax.broadcasted_iota(jnp.int32, (1, dh), 1)
    ones_col = jnp.where(lane == 0, 1.0, 0.0).astype(BF16)

    den_from_matmul = not past

    def with_ones(v):
        if not den_from_matmul:
            return v.astype(BF16)
        return jnp.concatenate([v.astype(BF16), jnp.broadcast_to(ones_col, v.shape)], axis=1)

    kb = k_ref[...].astype(BF16)
    vb = with_ones(v_ref[...])
    dn_t = (((1,), (1,)), ((), ()))
    row = lax.broadcasted_iota(jnp.int32, (tq, tq), 0)
    col = lax.broadcasted_iota(jnp.int32, (tq, tq), 1)
    diag_mask = col <= row

    def scores(qi):
        q = q_ref[qi * tq:(qi + 1) * tq, :]
        lo = qi * tq
        parts = []
        if past:
            parts.append((pk_ref[...].astype(BF16), with_ones(pv_ref[...]), fk_all[:, 0:past], False))
        if qi:
            parts.append((kb[0:lo, :], vb[0:lo, :], fk_all[:, past:past + lo], False))
        parts.append((kb[lo:lo + tq, :], vb[lo:lo + tq, :], fk_all[:, past + lo:past + lo + tq], True))
        logits = []
        for kp, _, fk, masked in parts:
            lg = lax.dot_general(q, kp, dn_t, preferred_element_type=F32) - fk
            logits.append(jnp.where(diag_mask, lg, NEG_INF) if masked else lg)
        mx = functools.reduce(jnp.maximum, [jnp.max(lg, axis=1, keepdims=True) for lg in logits])
        return [vp for _, vp, _, _ in parts], logits, mx

    def finish(qi, values, logits, mx):
        acc = 0.0
        den = 0.0
        for vp, lg in zip(values, logits):
            p = jnp.exp2(lg - mx)
            if not den_from_matmul:
                den = den + jnp.sum(p, axis=1, keepdims=True)
            acc = acc + jnp.dot(p.astype(BF16), vp, preferred_element_type=F32)
        if den_from_matmul:
            acc, den = acc[:, :dh], acc[:, dh:dh + 1]
        o_ref[qi * tq:(qi + 1) * tq, :] = (acc / den).astype(o_ref.dtype)

    n_tiles = t // tq
    pending = scores(0)
    for qi in range(n_tiles):
        upcoming = scores(qi + 1) if qi + 1 < n_tiles else None
        finish(qi, *pending)
        pending = upcoming


def fox_attention(p16, q_col, p32, k_col, v_col, row_blk, n_seq, t, f_row, n_heads, dh, heads,
                  past_k=None, past_v=None):
    past = 0 if past_k is None else past_k.shape[1]
    tq = _tile(t, 256)

    def head_blk(off):
        return pl.BlockSpec((t, dh), lambda i, h: (row_blk + i, off // dh + h))

    in_specs = [head_blk(q_col), head_blk(k_col), head_blk(v_col)]
    args = [p16, p32, p32]
    if past:
        past_blk = pl.BlockSpec((None, past, dh), lambda i, h: (i, 0, h))
        in_specs += [past_blk, past_blk]
        args += [past_k, past_v]
    in_specs += [
        pl.BlockSpec((None, None, 1, past + t), lambda i, h: (i, h, 0, 0)),
        pl.BlockSpec(memory_space=pl.ANY),
    ]
    args += [f_row, heads]
    return pl.pallas_call(
        functools.partial(_fox_kernel, past=past, t=t, tq=tq),
        grid=(n_seq, n_heads),
        in_specs=in_specs,
        out_specs=head_blk(0),
        out_shape=jax.ShapeDtypeStruct(heads.shape, heads.dtype),
        input_output_aliases={len(args) - 1: 0},
        compiler_params=_params("parallel", "parallel"),
        name="fox_attention",
    )(*args)


def _mlstm_kernel(*refs, t, chunk, conv_w, i_off, f_off, k_scale):
    refs = refs[:16] + refs[17:]
    (qraw_ref, kraw_ref, hq_ref, hk_ref, wq_ref, wk_ref, v_ref, og_ref, g_ref, f_ref,
     irow_ref, frow_ref, c0_ref, n0_ref, m0_ref, gh_ref,
     out_ref, c_ref, n_ref, m_ref,
     histq_ref, histk_ref, qc_ref, kc_ref) = refs
    h = pl.program_id(1)
    hpad = hq_ref.shape[0]

    def conv_silu(raw_ref, hist_in_ref, w_ref, hist_ref, scale, dst_ref):
        hist_ref[0:hpad, :] = hist_in_ref[...]
        hist_ref[hpad:hpad + t, :] = raw_ref[...]
        y = None
        for j in range(conv_w):
            start = hpad - (conv_w - 1) + j
            term = hist_ref[start:start + t, :] * w_ref[j:j + 1, :]
            y = term if y is None else y + term
        y = y * jax.nn.sigmoid(y)
        if scale != 1.0:
            y = y * scale
        dst_ref[...] = y.astype(dst_ref.dtype)

    conv_silu(qraw_ref, hq_ref, wq_ref, histq_ref, 1.0, qc_ref)
    conv_silu(kraw_ref, hk_ref, wk_ref, histk_ref, k_scale, kc_ref)

    gh = gh_ref[...]
    rr = lax.broadcasted_iota(jnp.int32, (chunk, chunk), 0)
    cc = lax.broadcasted_iota(jnp.int32, (chunk, chunk), 1)
    causal = cc <= rr
    lane = lax.broadcasted_iota(jnp.int32, (chunk, LANES), 1)

    def body(c, f_prev):
        r0 = pl.multiple_of(c * chunk, chunk)
        rows = pl.ds(r0, chunk)
        i_col = jnp.sum(jnp.where(lane == i_off + h, g_ref[rows, :], 0.0), axis=1, keepdims=True)
        b_col = jnp.sum(jnp.where(lane == f_off + h, f_ref[rows, :], 0.0), axis=1, keepdims=True) - f_prev
        i_row = irow_ref[pl.ds(c, 1), :]
        b_row = frow_ref[pl.ds(c, 1), :] - f_prev
        m_prev = m_ref[:, 0:1]
        q = qc_ref[rows, :]
        k = kc_ref[rows, :]
        v = v_ref[rows, :]
        cmat = c_ref[...]
        nvec = n_ref[...]

        d = jnp.where(causal, b_col - b_row + i_row, NEG_INF)
        inter = b_col + m_prev
        m_t = jnp.maximum(inter, jnp.max(d, axis=1, keepdims=True))
        w_inter = jnp.exp(inter - m_t)
        qk = lax.dot_general(q, k, (((1,), (1,)), ((), ())), preferred_element_type=F32)
        s = qk * jnp.exp(d - m_t)
        qc_state = lax.dot_general(q, cmat.astype(BF16), (((1,), (1,)), ((), ())),
                                   preferred_element_type=F32)
        num = w_inter * qc_state + jnp.dot(s.astype(BF16), v, preferred_element_type=F32)
        qn = jnp.sum(q.astype(F32) * nvec, axis=1, keepdims=True)
        den = w_inter * qn + jnp.sum(s, axis=1, keepdims=True)
        hval = num / jnp.maximum(jnp.abs(den), jnp.exp(-m_t))

        hn = hval * lax.rsqrt(jnp.mean(hval * hval, axis=1, keepdims=True) + EPS)
        out = jax.nn.sigmoid(og_ref[rows, :]) * hn * gh
        out_ref[rows, :] = out.astype(out_ref.dtype)

        b_last = b_col[chunk - 1:chunk, :]
        dec_row = b_last - b_row + i_row
        dec_col = b_last - b_col + i_col
        m_new = jnp.maximum(b_last + m_prev, jnp.max(dec_row, axis=1, keepdims=True))
        a = jnp.exp(b_last + m_prev - m_new)
        w_col = jnp.exp(dec_col - m_new)
        vw = (v.astype(F32) * w_col).astype(BF16)
        c_ref[...] = a * cmat + lax.dot_general(vw, k, (((0,), (0,)), ((), ())),
                                                preferred_element_type=F32)
        n_ref[...] = a * nvec + jnp.sum(w_col * k.astype(F32), axis=0, keepdims=True)
        m_ref[...] = jnp.broadcast_to(m_new, m_ref.shape)
        return f_prev + b_last

    c_ref[...] = c0_ref[...]
    n_ref[...] = n0_ref[...]
    m_ref[...] = m0_ref[...]
    lax.fori_loop(0, t // chunk, body, jnp.zeros((1, 1), F32), unroll=True)


def mlstm_heads(p32, qk_col, og_col, p16, v_col, gates, row_blk, n_seq, t, f_cum, gates_t, f_cum_t,
                conv_hist, w_conv, c0, n0, m0, g_head, i_off, f_off, heads, out_col):
    b = n_seq
    _, n_h, dv, dk = c0.shape
    qb, ogb, vb = qk_col // dk, og_col // dv, v_col // dv
    conv_w = w_conv.shape[0]
    chunk = _tile(t, MLSTM_CHUNK)
    nch = t // chunk
    hpad = 8
    hist = jnp.pad(conv_hist, ((0, 0), (hpad - (conv_w - 1), 0), (0, 0)))
    gt = gates_t.reshape(b, LANES, nch, chunk)
    ft = f_cum_t.reshape(b, LANES, nch, chunk)
    n0r = n0.reshape(b, n_h, 1, dk)
    m0r = jnp.broadcast_to(m0[:, :, None, None], (b, n_h, 1, LANES))
    ghr = g_head.reshape(n_h, 1, dv)

    rowcol = lambda width, off: pl.BlockSpec((t, width), lambda i, h: (row_blk + i, off + h))
    hcol = lambda off: pl.BlockSpec((None, hpad, dk), lambda i, h: (i, 0, off + h))
    in_specs = [
        rowcol(dk, qb), rowcol(dk, qb + n_h), hcol(0), hcol(n_h),
        pl.BlockSpec((conv_w, dk), lambda i, h: (0, h)),
        pl.BlockSpec((conv_w, dk), lambda i, h: (0, n_h + h)),
        rowcol(dv, vb), rowcol(dv, ogb),
        pl.BlockSpec((t, LANES), lambda i, h: (row_blk + i, 0)),
        pl.BlockSpec((t, LANES), lambda i, h: (i, 0)),
        pl.BlockSpec((None, None, nch, chunk), lambda i, h: (i, i_off + h, 0, 0)),
        pl.BlockSpec((None, None, nch, chunk), lambda i, h: (i, f_off + h, 0, 0)),
        pl.BlockSpec((None, None, dv, dk), lambda i, h: (i, h, 0, 0)),
        pl.BlockSpec((None, None, 1, dk), lambda i, h: (i, h, 0, 0)),
        pl.BlockSpec((None, None, 1, LANES), lambda i, h: (i, h, 0, 0)),
        pl.BlockSpec((None, 1, dv), lambda i, h: (h, 0, 0)),
    ]
    in_specs.append(pl.BlockSpec(memory_space=pl.ANY))
    args = [p32, p32, hist, hist, w_conv, w_conv, p16, p32, gates, f_cum, gt, ft, c0, n0r, m0r, ghr, heads]
    out_col = out_col // dv
    out_specs = [
        pl.BlockSpec((t, dv), lambda i, h: (row_blk + i, out_col + h)),
        pl.BlockSpec((None, None, dv, dk), lambda i, h: (i, h, 0, 0)),
        pl.BlockSpec((None, None, 1, dk), lambda i, h: (i, h, 0, 0)),
        pl.BlockSpec((None, None, 1, LANES), lambda i, h: (i, h, 0, 0)),
    ]
    out_shape = [
        jax.ShapeDtypeStruct(heads.shape, heads.dtype),
        jax.ShapeDtypeStruct((b, n_h, dv, dk), F32),
        jax.ShapeDtypeStruct((b, n_h, 1, dk), F32),
        jax.ShapeDtypeStruct((b, n_h, 1, LANES), F32),
    ]
    out, c_new, n_new, m_new = pl.pallas_call(
        functools.partial(_mlstm_kernel, t=t, chunk=chunk, conv_w=conv_w, i_off=i_off, f_off=f_off,
                          k_scale=float(dk) ** -0.5),
        grid=(b, n_h),
        in_specs=in_specs,
        out_specs=out_specs,
        out_shape=out_shape,
        input_output_aliases={len(args) - 1: 0},
        scratch_shapes=[
            pltpu.VMEM((t + hpad, dk), F32), pltpu.VMEM((t + hpad, dk), F32),
            pltpu.VMEM((t, dk), BF16), pltpu.VMEM((t, dk), BF16),
        ],
        compiler_params=_params("parallel", "parallel"),
        name="mlstm_heads",
    )(*args)
    return out, c_new, n_new[:, :, 0, :], m_new[:, :, 0, 0]


def _outproj_kernel(a_ref, w_ref, xa_ref, xb_ref, o_ref, *, n_first):
    acc = jnp.dot(a_ref[...], w_ref[...], preferred_element_type=F32)

    @pl.when(pl.program_id(0) < n_first)
    def _():
        o_ref[...] = xa_ref[...] + acc

    @pl.when(pl.program_id(0) >= n_first)
    def _():
        o_ref[...] = xb_ref[...] + acc


def out_projection(heads, w_out, xa, xb, tm):
    (ma, d), mb = xa.shape, xb.shape[0]
    tn = _tile(d, 512)
    assert ma % tm == 0
    n_first = ma // tm
    first_j = lambda i, j: (jnp.minimum(i, n_first - 1), j)
    second_j = lambda i, j: (jnp.maximum(i - n_first, 0), j)
    return pl.pallas_call(
        functools.partial(_outproj_kernel, n_first=n_first),
        grid=(pl.cdiv(ma + mb, tm), d // tn),
        in_specs=[
            pl.BlockSpec((tm, d), lambda i, j: (i, 0)),
            pl.BlockSpec((d, tn), lambda i, j: (0, j)),
            pl.BlockSpec((tm, tn), first_j), pl.BlockSpec((tm, tn), second_j),
        ],
        out_specs=pl.BlockSpec((tm, tn), lambda i, j: (i, j)),
        out_shape=jax.ShapeDtypeStruct((ma + mb, d), F32),
        compiler_params=_params("arbitrary", "arbitrary"),
        name="out_projection",
    )(heads, w_out, xa, xb)


def _pack_bf16_pairs(x):
    half = x.shape[1] // 2
    bits = lax.bitcast_convert_type(x.astype(BF16).astype(F32), jnp.uint32)
    return (bits[:, :half] >> 16) | (bits[:, half:] & jnp.uint32(0xFFFF0000))


def _unpack_pairs_f32(w):
    lo = lax.bitcast_convert_type(w << 16, F32)
    hi = lax.bitcast_convert_type(w & jnp.uint32(0xFFFF0000), F32)
    return lo, hi


def _unpack_bf16_pairs(w):
    lo, hi = _unpack_pairs_f32(w)
    return lo.astype(BF16), hi.astype(BF16)


def _router_kernel(x_ref, g_ref, whi_ref, wlo_ref, b_ref, xn_ref, eid_ref, gate_ref, *, n_groups, per_group):
    x = x_ref[...]
    xn = x * lax.rsqrt(jnp.mean(x * x, axis=-1, keepdims=True) + EPS) * g_ref[...]
    xn_ref[...] = _pack_bf16_pairs(xn)
    x_hi = xn.astype(BF16)
    x_lo = (xn - x_hi.astype(F32)).astype(BF16)
    logits = (jnp.dot(x_hi, whi_ref[...], preferred_element_type=F32)
              + jnp.dot(x_lo, whi_ref[...], preferred_element_type=F32)
              + jnp.dot(x_hi, wlo_ref[...], preferred_element_type=F32)) + b_ref[...]
    lane = lax.broadcasted_iota(jnp.int32, logits.shape, 1).astype(F32)
    big = float(LANES)
    is_g = lane < n_groups
    gl = jnp.where(is_g, logits, NEG_INF)
    gmax = jnp.max(gl, axis=1, keepdims=True)
    g_idx = jnp.min(jnp.where(gl == gmax, lane, big), axis=1, keepdims=True)
    g_prob = 1.0 / jnp.sum(jnp.where(is_g, jnp.exp(logits - gmax), 0.0), axis=1, keepdims=True)
    lo = n_groups + per_group * g_idx
    el = jnp.where((lane >= lo) & (lane < lo + per_group), logits, NEG_INF)
    e1 = jnp.max(el, axis=1, keepdims=True)
    i1 = jnp.min(jnp.where(el == e1, lane, big), axis=1, keepdims=True)
    el2 = jnp.where(lane == i1, NEG_INF, el)
    e2 = jnp.max(el2, axis=1, keepdims=True)
    i2 = jnp.min(jnp.where(el2 == e2, lane, big), axis=1, keepdims=True)
    r = jnp.exp(e2 - e1)
    w1 = g_prob / (1.0 + r)
    w2 = g_prob * r / (1.0 + r)
    eid = jnp.where(lane == 0.0, i1 - n_groups, jnp.where(lane == 1.0, i2 - n_groups, 0.0))
    eid_ref[...] = eid.astype(jnp.int32)
    gate_ref[...] = jnp.where(lane == 0.0, w1, jnp.where(lane == 1.0, w2, 0.0))


def router(x, g, w_router, b_router, n_groups, per_group):
    m, d = x.shape
    tm = _tile(m, 256)
    w_hi = w_router.astype(BF16)
    w_lo = (w_router - w_hi.astype(F32)).astype(BF16)
    return pl.pallas_call(
        functools.partial(_router_kernel, n_groups=n_groups, per_group=per_group),
        grid=(m // tm,),
        in_specs=[
            pl.BlockSpec((tm, d), lambda i: (i, 0)),
            pl.BlockSpec((1, d), lambda i: (0, 0)),
            pl.BlockSpec((d, LANES), lambda i: (0, 0)),
            pl.BlockSpec((d, LANES), lambda i: (0, 0)),
            pl.BlockSpec((1, LANES), lambda i: (0, 0)),
        ],
        out_specs=[
            pl.BlockSpec((tm, d // 2), lambda i: (i, 0)),
            pl.BlockSpec((tm, LANES), lambda i: (i, 0)),
            pl.BlockSpec((tm, LANES), lambda i: (i, 0)),
        ],
        out_shape=[
            jax.ShapeDtypeStruct((m, d // 2), jnp.uint32),
            jax.ShapeDtypeStruct((m, LANES), jnp.int32),
            jax.ShapeDtypeStruct((m, LANES), F32),
        ],
        compiler_params=_params("parallel"),
        name="router",
    )(x, g.reshape(1, d), w_hi, w_lo, b_router)


def _gather_kernel(nrows_ref, idx0_ref, idx1_ref, src_ref, o_ref, buf_ref, sem_ref, *, tm):
    step = pl.program_id(0)
    nsteps = pl.num_programs(0)
    used = lambda s: s * tm < nrows_ref[0]

    def issue(idx_ref, slot):
        def body(r, carry):
            for q, row in enumerate((r, r + tm // 2)):
                tok = idx_ref[0, row]
                pltpu.make_async_copy(src_ref.at[pl.ds(tok, 1), :], buf_ref.at[slot, pl.ds(row, 1), :],
                                      sem_ref.at[slot]).start(priority=q)
            return carry
        lax.fori_loop(0, tm // 2, body, 0, unroll=DMA_ISSUE_UNROLL // 2)

    @pl.when(step == 0)
    def _():
        issue(idx0_ref, 0)

    @pl.when((step + 1 < nsteps) & used(step + 1))
    def _():
        issue(idx1_ref, (step + 1) % 2)

    @pl.when(used(step))
    def _():
        slot = step % 2
        pltpu.make_async_copy(src_ref.at[pl.ds(0, tm), :], buf_ref.at[slot], sem_ref.at[slot]).wait()
        half = buf_ref.shape[2]
        lo, hi = _unpack_bf16_pairs(buf_ref[slot])
        o_ref[:, :half] = lo
        o_ref[:, half:] = hi

    @pl.when(jnp.logical_not(used(step)))
    def _():
        o_ref[...] = jnp.zeros_like(o_ref)


def _smem_tiles(n_steps, width):
    cur = pl.BlockSpec((None, 1, width), lambda i: (i, 0, 0), memory_space=pltpu.SMEM)
    nxt = pl.BlockSpec((None, 1, width), lambda i: (jnp.minimum(i + 1, n_steps - 1), 0, 0),
                       memory_space=pltpu.SMEM)
    return cur, nxt


def gather_rows(src, idx, n_rows, tm):
    n, half = src.shape
    d = 2 * half
    a = idx.shape[0]
    n_steps = a // tm
    cur, nxt = _smem_tiles(n_steps, tm)
    idx3 = idx.reshape(n_steps, 1, tm)
    return pl.pallas_call(
        functools.partial(_gather_kernel, tm=tm),
        grid=(n_steps,),
        in_specs=[pl.BlockSpec(memory_space=pltpu.SMEM), cur, nxt, pl.BlockSpec(memory_space=pl.ANY)],
        out_specs=pl.BlockSpec((tm, d), lambda i: (i, 0)),
        scratch_shapes=[pltpu.VMEM((2, tm, half), src.dtype), pltpu.SemaphoreType.DMA((2,))],
        out_shape=jax.ShapeDtypeStruct((a, d), BF16),
        compiler_params=_params("arbitrary"),
        name="gather_rows",
    )(n_rows, idx3, idx3, src)


def _expert_up_kernel(te_ref, nv_ref, x_ref, wg_ref, wu_ref, h_ref):
    t = pl.program_id(1)

    @pl.when(t < nv_ref[0])
    def _():
        x = x_ref[...]
        g = jnp.dot(x, wg_ref[...].astype(BF16), preferred_element_type=F32)
        u = jnp.dot(x, wu_ref[...].astype(BF16), preferred_element_type=F32)
        h_ref[...] = (g * jax.nn.sigmoid(g) * u).astype(h_ref.dtype)

    @pl.when(t >= nv_ref[0])
    def _():
        h_ref[...] = jnp.zeros_like(h_ref)


def expert_up(xg, w_gate, w_up, tile_expert, n_valid, tm):
    a, d = xg.shape
    _, _, f = w_gate.shape
    tn = _tile(f, 512)
    return pl.pallas_call(
        _expert_up_kernel,
        grid_spec=pltpu.PrefetchScalarGridSpec(
            num_scalar_prefetch=2,
            grid=(f // tn, a // tm),
            in_specs=[
                pl.BlockSpec((tm, d), lambda c, t, te, nv: (t, 0)),
                pl.BlockSpec((None, d, tn), lambda c, t, te, nv: (te[t], 0, c)),
                pl.BlockSpec((None, d, tn), lambda c, t, te, nv: (te[t], 0, c)),
            ],
            out_specs=pl.BlockSpec((tm, tn), lambda c, t, te, nv: (t, c)),
        ),
        out_shape=jax.ShapeDtypeStruct((a, f), BF16),
        compiler_params=_params("arbitrary", "arbitrary"),
        name="expert_up",
    )(tile_expert, n_valid, xg, w_gate, w_up)


def _expert_down_kernel(te_ref, nv_ref, h_ref, wd_ref, y_ref):
    t = pl.program_id(1)

    @pl.when(t < nv_ref[0])
    def _():
        y = jnp.dot(h_ref[...], wd_ref[...].astype(BF16), preferred_element_type=F32)
        y_ref[...] = _pack_bf16_pairs(y)

    @pl.when(t >= nv_ref[0])
    def _():
        y_ref[...] = jnp.zeros_like(y_ref)


def expert_down(hg, w_down, tile_expert, n_valid, tm):
    a, f = hg.shape
    _, _, d = w_down.shape
    tn = _tile(d, DOWN_CHUNK)
    return pl.pallas_call(
        _expert_down_kernel,
        grid_spec=pltpu.PrefetchScalarGridSpec(
            num_scalar_prefetch=2,
            grid=(d // tn, a // tm),
            in_specs=[
                pl.BlockSpec((tm, f), lambda c, t, te, nv: (t, 0)),
                pl.BlockSpec((None, f, tn), lambda c, t, te, nv: (te[t], 0, c)),
            ],
            out_specs=pl.BlockSpec((tm, tn // 2), lambda c, t, te, nv: (t, c)),
        ),
        out_shape=jax.ShapeDtypeStruct((a, d // 2), jnp.uint32),
        compiler_params=_params("arbitrary", "arbitrary"),
        name="expert_down",
    )(tile_expert, n_valid, hg, w_down)


def _combine_kernel(pos0_ref, pos1_ref, x_ref, gate_ref, g_ref, y_ref, oa_ref, ob_ref, buf_ref, sem_ref,
                    xs_ref, *, tm, n_first, chunk):
    step = pl.program_id(0)
    nsteps = pl.num_programs(0)

    def issue(pos_ref, slot):
        def body(r, carry):
            for k in range(TOP_K):
                p = pos_ref[0, r * TOP_K + k]
                pltpu.make_async_copy(y_ref.at[pl.ds(p, 1), :], buf_ref.at[slot, k, pl.ds(r, 1), :],
                                      sem_ref.at[slot]).start(priority=k)
            return carry
        lax.fori_loop(0, tm, body, 0, unroll=DMA_ISSUE_UNROLL)

    @pl.when(step == 0)
    def _():
        issue(pos0_ref, 0)

    @pl.when(step + 1 < nsteps)
    def _():
        issue(pos1_ref, (step + 1) % 2)

    slot = step % 2
    for k in range(TOP_K):
        pltpu.make_async_copy(y_ref.at[pl.ds(0, tm), :], buf_ref.at[slot, k], sem_ref.at[slot]).wait()
    gates = gate_ref[...]
    d = x_ref.shape[1]
    half_chunk = chunk // 2
    cw = min(COMBINE_COLS, half_chunk)
    ssq = jnp.zeros((tm, 1), F32)
    for pc in range(0, d // 2, cw):
        real_lo = (pc // half_chunk) * chunk + pc % half_chunk
        real_hi = real_lo + half_chunk
        x_lo = x_ref[:, real_lo:real_lo + cw]
        x_hi = x_ref[:, real_hi:real_hi + cw]
        for k in range(TOP_K):
            lo, hi = _unpack_pairs_f32(buf_ref[slot, k, :, pc:pc + cw])
            x_lo = x_lo + gates[:, k:k + 1] * lo
            x_hi = x_hi + gates[:, k:k + 1] * hi
        xs_ref[:, real_lo:real_lo + cw] = x_lo
        xs_ref[:, real_hi:real_hi + cw] = x_hi
        ssq = ssq + jnp.sum(x_lo * x_lo, axis=1, keepdims=True) + jnp.sum(x_hi * x_hi, axis=1, keepdims=True)
    inv = lax.rsqrt(ssq * (1.0 / d) + EPS)

    def emit(o_ref):
        for c in range(0, d, 2 * cw):
            o_ref[:, c:c + 2 * cw] = xs_ref[:, c:c + 2 * cw] * inv * g_ref[:, c:c + 2 * cw]

    @pl.when(step < n_first)
    def _():
        emit(oa_ref)

    @pl.when(step >= n_first)
    def _():
        emit(ob_ref)


def combine_norm(x, gates, pos, yg, g_final, m_first, tm):
    m, d = x.shape
    n_steps = m // tm
    n_first = m_first // tm
    cur, nxt = _smem_tiles(n_steps, tm * TOP_K)
    pos3 = pos.reshape(n_steps, 1, tm * TOP_K)
    first, second = _split_maps(n_first)
    return pl.pallas_call(
        functools.partial(_combine_kernel, tm=tm, n_first=n_first, chunk=_tile(d, DOWN_CHUNK)),
        grid=(n_steps,),
        in_specs=[
            cur, nxt,
            pl.BlockSpec((tm, d), lambda i: (i, 0)),
            pl.BlockSpec((tm, LANES), lambda i: (i, 0)),
            pl.BlockSpec((1, d), lambda i: (0, 0)),
            pl.BlockSpec(memory_space=pl.ANY),
        ],
        out_specs=[pl.BlockSpec((tm, d), first), pl.BlockSpec((tm, d), second)],
        scratch_shapes=[pltpu.VMEM((2, TOP_K, tm, d // 2), jnp.uint32), pltpu.SemaphoreType.DMA((2,)),
                        pltpu.VMEM((tm, d), F32)],
        out_shape=[jax.ShapeDtypeStruct((m_first, d), F32), jax.ShapeDtypeStruct((m - m_first, d), F32)],
        compiler_params=_params("arbitrary"),
        name="combine_norm",
    )(pos3, pos3, x, gates, g_final.reshape(1, d), yg)


def _dispatch_plan(eid, n_experts, tm):
    n = eid.shape[0]
    a = n * TOP_K
    e_flat = eid.reshape(a)
    onehot = (e_flat[:, None] == jnp.arange(n_experts, dtype=jnp.int32)[None, :]).astype(jnp.int32)
    csum = jnp.cumsum(onehot, axis=0)
    counts = csum[-1]
    rank = jnp.take_along_axis(csum, e_flat[:, None], axis=1)[:, 0] - 1
    padded = ((counts + tm - 1) // tm) * tm
    pad_end = jnp.cumsum(padded)
    pos = (pad_end - padded)[e_flat] + rank
    n_tiles = a // tm + n_experts
    src = (jnp.arange(n_tiles * tm, dtype=jnp.int32) % n).at[pos].set(jnp.arange(a, dtype=jnp.int32) // TOP_K)
    tile_end = pad_end // tm
    tile_ids = jnp.arange(n_tiles, dtype=jnp.int32)
    tile_expert = jnp.sum((tile_ids[:, None] >= tile_end[None, :]).astype(jnp.int32), axis=1)
    tile_expert = jnp.minimum(tile_expert, n_experts - 1)
    n_valid = tile_end[-1:].astype(jnp.int32)
    return pos.astype(jnp.int32), src, tile_expert, n_valid


def moe_layer(x, m_first, g_ffn, w_router, b_router, w_gate, w_up, w_down, g_final, n_groups, per_group):
    m, d = x.shape
    n_experts = n_groups * per_group
    xn, eid, gates = router(x, g_ffn, w_router, b_router, n_groups, per_group)
    a = m * TOP_K
    tm = _tile(a, min(512, max(64, a // n_experts)))
    pos, src, tile_expert, n_valid = _dispatch_plan(eid[:, :TOP_K], n_experts, tm)
    xg = gather_rows(xn, src, n_valid * tm, _tile(tm, 256))
    hg = expert_up(xg, w_gate, w_up, tile_expert, n_valid, tm)
    yg = expert_down(hg, w_down, tile_expert, n_valid, tm)
    return combine_norm(x, gates, pos, yg, g_final, m_first, _tile(math.gcd(m_first, m - m_first), 128))


def kernel(x_prompt, x_sample, cache_fox_k, cache_fox_v, cache_fox_logf, state_mlstm_C, state_mlstm_n, state_mlstm_m, state_mlstm_conv, g_norm_mix, w_in, b_fox_f, b_mlstm_i, b_mlstm_f, w_conv, g_mlstm_head, w_out, g_norm_ffn, w_router_group, b_router_group, w_router_expert, b_router_expert, w_exp_gate, w_exp_up, w_exp_down, g_norm_final):
    depth = w_in.shape[0]
    assert depth == 1, "the final norm is fused into the last layer's MoE combine; one layer supported"
    bp, tp, d = x_prompt.shape
    bs, ts, _ = x_sample.shape
    n_fox, dh = cache_fox_k.shape[-2:]
    n_ml, dv, dk = state_mlstm_C.shape[-3:]
    past = cache_fox_k.shape[2]
    n_groups = w_router_group.shape[-1]
    n_experts = w_router_expert.shape[-1]
    conv_w = w_conv.shape[1]
    half = d // 2
    mp, ms = bp * tp, bs * ts
    assert n_fox * dh == half and n_ml * dv == half and 2 * n_ml * dk == half
    assert n_fox + 2 * n_ml <= LANES and n_groups + n_experts <= LANES
    assert mp % ts == 0 and ts >= conv_w - 1, "sample sequences are addressed as row blocks after the prompt rows"
    l = 0
    sizes = (half, half, half, n_fox, half, half, half, n_ml, n_ml)
    offs = [0]
    for s in sizes:
        offs.append(offs[-1] + s)
    col = lambda i: w_in[l][:, offs[i]:offs[i + 1]]
    w_a, w_b = split_cast_weights(w_in[l], offs[3], offs[4], offs[7] - offs[4])
    n_gate = n_fox + 2 * n_ml
    w_gatecols = jnp.pad(jnp.concatenate([col(3), col(7), col(8)], axis=1), ((0, 0), (0, LANES - n_gate))).astype(BF16)
    b_gatecols = jnp.pad(jnp.concatenate([b_fox_f[l], b_mlstm_i[l], b_mlstm_f[l]]).astype(F32),
                         (0, LANES - n_gate)).reshape(1, LANES)
    n_r = n_groups + n_experts
    w_router = jnp.pad(jnp.concatenate([w_router_group[l], w_router_expert[l]], axis=1), ((0, 0), (0, LANES - n_r)))
    b_router = jnp.pad(jnp.concatenate([b_router_group[l], b_router_expert[l]]).astype(F32),
                       (0, LANES - n_r)).reshape(1, LANES)
    i_off, f_off = n_fox, n_fox + n_ml
    col32 = {s: k * half for k, s in enumerate(P32_SEGS)}
    col16 = {s: k * half for k, s in enumerate(P16_SEGS)}

    xp2, xs2 = x_prompt.reshape(mp, d), x_sample.reshape(ms, d)
    tm_rows = _tile(mp, 1024)
    xn = rmsnorm_cast(xp2, xs2, g_norm_mix[l], BF16)
    p32, p16, gates = in_projection(xn, w_a, w_b, w_gatecols, b_gatecols, n_fox, n_ml, tm_rows,
                                    float(dh) ** -0.5 * LOG2E)

    f_p = cumsum_time(gates, bp, tp)
    f_p3 = f_p.reshape(bp, tp, LANES)
    g_p3 = gates[:mp].reshape(bp, tp, LANES)
    f_row = lambda f3: jnp.swapaxes(f3[:, :, :n_fox], 1, 2)[:, :, None, :]
    heads = fox_attention(p16, col16[SEG_FQ], p32, col32[SEG_FK], col32[SEG_FV], 0, bp, tp,
                          f_row(f_p3), n_fox, dh, xn)
    hist_p = jnp.zeros((bp, conv_w - 1, half), F32)
    heads, c_p, n_p, m_p = mlstm_heads(
        p32, col32[SEG_QK], col32[SEG_MO], p16, col16[SEG_MV], gates, 0, bp, tp, f_p,
        jnp.swapaxes(g_p3, 1, 2), jnp.swapaxes(f_p3, 1, 2), hist_p, w_conv[l],
        jnp.zeros((bp, n_ml, dv, dk), F32), jnp.zeros((bp, n_ml, dk), F32), jnp.zeros((bp, n_ml), F32),
        g_mlstm_head[l], i_off, f_off, heads, half)

    g_s3 = gates[mp:].reshape(bs, ts, LANES)
    cache_pad = jnp.pad(cache_fox_logf[l].astype(F32), ((0, 0), (0, 0), (0, LANES - n_fox)))
    logf_s = jnp.concatenate([cache_pad, g_s3], axis=1)
    f_s3 = cumsum_time(logf_s.reshape(bs * (past + ts), LANES), bs, past + ts).reshape(bs, past + ts, LANES)
    heads = fox_attention(p16, col16[SEG_FQ], p32, col32[SEG_FK], col32[SEG_FV], mp // ts, bs, ts,
                          f_row(f_s3), n_fox, dh, heads,
                          cache_fox_k[l].reshape(bs, past, half), cache_fox_v[l].reshape(bs, past, half))
    fn_s = cumsum_time(g_s3.reshape(ms, LANES), bs, ts)
    heads, c_s, n_s, m_s = mlstm_heads(
        p32, col32[SEG_QK], col32[SEG_MO], p16, col16[SEG_MV], gates, mp // ts, bs, ts, fn_s,
        jnp.swapaxes(g_s3, 1, 2), jnp.swapaxes(fn_s.reshape(bs, ts, LANES), 1, 2), state_mlstm_conv[l], w_conv[l],
        state_mlstm_C[l], state_mlstm_n[l], state_mlstm_m[l], g_mlstm_head[l], i_off, f_off, heads, half)

    x1 = out_projection(heads, w_out[l].astype(BF16), xp2, xs2, tm_rows)
    y_p, y_s = moe_layer(x1, mp, g_norm_ffn[l], w_router, b_router, w_exp_gate[l], w_exp_up[l], w_exp_down[l],
                         g_norm_final, n_groups, n_experts // n_groups)

    def new_state(rows, b, t, g3, c_new, n_new, m_new, conv_hist):
        blk = p32[rows]
        fk = blk[:, col32[SEG_FK]:col32[SEG_FK] + half].reshape(b, t, n_fox, dh)
        fv = blk[:, col32[SEG_FV]:col32[SEG_FV] + half].reshape(b, t, n_fox, dh)
        qk_tail = blk[:, col32[SEG_QK]:col32[SEG_QK] + half].reshape(b, t, half)[:, t - min(t, conv_w - 1):]
        qk_hist = jnp.concatenate([conv_hist.astype(F32), qk_tail], axis=1)[:, -(conv_w - 1):]
        return tuple(a[None] for a in (fk, fv, g3[:, :, :n_fox], c_new, n_new, m_new, qk_hist))

    st_p = new_state(slice(0, mp), bp, tp, g_p3, c_p, n_p, m_p, hist_p)
    st_s = new_state(slice(mp, mp + ms), bs, ts, g_s3, c_s, n_s, m_s, state_mlstm_conv[l])
    return (y_p.reshape(bp, tp, d), y_s.reshape(bs, ts, d)) + st_p + st_s
```

```python
import functools
import math

import jax
import jax.numpy as jnp
from jax import lax
from jax.experimental import pallas as pl
from jax.experimental.pallas import tpu as pltpu

F32 = jnp.float32
BF16 = jnp.bfloat16
EPS = 1e-6
LANES = 128
V7X_VMEM_LIMIT = 56 * 1024 * 1024
TOP_K = 2
MLSTM_CHUNK = 256
DMA_ISSUE_UNROLL = 8
COMBINE_COLS = 256
DOWN_CHUNK = 2048
HIGHEST = lax.Precision.HIGHEST
NEG_INF = float("-inf")
LOG2E = math.log2(math.e)


def _params(*sem):
    return pltpu.CompilerParams(dimension_semantics=sem, vmem_limit_bytes=V7X_VMEM_LIMIT)


def _tile(n, pref):
    if n <= pref:
        return n
    for t in range(pref - pref % 8, 7, -8):
        if n % t == 0:
            return t
    return n


def _log_sigmoid(x):
    return jnp.minimum(x, 0.0) - jnp.log1p(jnp.exp(-jnp.abs(x)))


def _split_maps(n_first):
    first = lambda i, *_: (jnp.minimum(i, n_first - 1), 0)
    second = lambda i, *_: (jnp.maximum(i - n_first, 0), 0)
    return first, second


def _rmsnorm_kernel(xa_ref, xb_ref, g_ref, o_ref, *, n_first):
    def emit(x_ref):
        x = x_ref[...]
        y = x * lax.rsqrt(jnp.mean(x * x, axis=-1, keepdims=True) + EPS)
        o_ref[...] = (y * g_ref[...]).astype(o_ref.dtype)

    @pl.when(pl.program_id(0) < n_first)
    def _():
        emit(xa_ref)

    @pl.when(pl.program_id(0) >= n_first)
    def _():
        emit(xb_ref)


def rmsnorm_cast(xa, xb, g, out_dtype):
    (ma, d), mb = xa.shape, xb.shape[0]
    tm = _tile(math.gcd(ma, mb), 256)
    first, second = _split_maps(ma // tm)
    return pl.pallas_call(
        functools.partial(_rmsnorm_kernel, n_first=ma // tm),
        grid=((ma + mb) // tm,),
        in_specs=[pl.BlockSpec((tm, d), first), pl.BlockSpec((tm, d), second),
                  pl.BlockSpec((1, d), lambda i: (0, 0))],
        out_specs=pl.BlockSpec((tm, d), lambda i: (i, 0)),
        out_shape=jax.ShapeDtypeStruct((ma + mb, d), out_dtype),
        compiler_params=_params("arbitrary"),
        name="rmsnorm_cast",
    )(xa, xb, g.reshape(1, d))


SEG_FQ, SEG_FK, SEG_FV, SEG_QK, SEG_MV, SEG_MO = range(6)
P32_SEGS = (SEG_FK, SEG_FV, SEG_QK, SEG_MO)
P16_SEGS = (SEG_FQ, SEG_MV)


def _inproj_kernel(x_ref, wa_ref, wb_ref, ws_ref, b_ref, p32_ref, p16_ref, gate_ref, *, nseg, n_fox, n_ml,
                   q_scale):
    j = pl.program_id(1)
    seg = j // nseg

    def emit(w_ref):
        acc = jnp.dot(x_ref[...], w_ref[...], preferred_element_type=F32)
        is16 = (seg == SEG_FQ) | (seg == SEG_MV)

        @pl.when(is16)
        def _():
            p16_ref[...] = (acc * jnp.where(seg == SEG_FQ, q_scale, 1.0)).astype(p16_ref.dtype)

        @pl.when(jnp.logical_not(is16))
        def _():
            p32_ref[...] = acc

    @pl.when(seg < 3)
    def _():
        emit(wa_ref)

    @pl.when(seg >= 3)
    def _():
        emit(wb_ref)

    @pl.when(j == 0)
    def _():
        p = jnp.dot(x_ref[...], ws_ref[...], preferred_element_type=F32) + b_ref[...]
        lane = lax.broadcasted_iota(jnp.int32, p.shape, 1)
        is_id = (lane >= n_fox) & (lane < n_fox + n_ml)
        gate_ref[...] = jnp.where(is_id, p, _log_sigmoid(p))


def _held_block(j, nseg, segs):
    seg = j // nseg
    blk = jnp.int32(0)
    for k, s in enumerate(segs):
        here = k * nseg + (j - s * nseg)
        done = (k + 1) * nseg - 1
        blk = jnp.where(seg == s, here, jnp.where(seg > s, done, blk))
    return blk


def in_projection(xn, w_a, w_b, w_gate, b_gate, n_fox, n_ml, tm, q_scale):
    m, d = xn.shape
    w = w_a.shape[1] // 3
    tn = _tile(w, 512)
    nseg = w // tn
    return pl.pallas_call(
        functools.partial(_inproj_kernel, nseg=nseg, n_fox=n_fox, n_ml=n_ml, q_scale=q_scale),
        grid=(pl.cdiv(m, tm), 6 * nseg),
        in_specs=[
            pl.BlockSpec((tm, d), lambda i, j: (i, 0)),
            pl.BlockSpec((d, tn), lambda i, j: (0, jnp.minimum(j, 3 * nseg - 1))),
            pl.BlockSpec((d, tn), lambda i, j: (0, jnp.maximum(j - 3 * nseg, 0))),
            pl.BlockSpec((d, LANES), lambda i, j: (0, 0)),
            pl.BlockSpec((1, LANES), lambda i, j: (0, 0)),
        ],
        out_specs=[
            pl.BlockSpec((tm, tn), lambda i, j: (i, _held_block(j, nseg, P32_SEGS))),
            pl.BlockSpec((tm, tn), lambda i, j: (i, _held_block(j, nseg, P16_SEGS))),
            pl.BlockSpec((tm, LANES), lambda i, j: (i, 0)),
        ],
        out_shape=[
            jax.ShapeDtypeStruct((m, len(P32_SEGS) * w), F32),
            jax.ShapeDtypeStruct((m, len(P16_SEGS) * w), BF16),
            jax.ShapeDtypeStruct((m, LANES), F32),
        ],
        compiler_params=_params("arbitrary", "arbitrary"),
        name="in_projection",
    )(xn, w_a, w_b, w_gate, b_gate)


def _cumsum_kernel(g_ref, f_ref, *, chunk):
    s = g_ref.shape[0]
    r = lax.broadcasted_iota(jnp.int32, (chunk, chunk), 0)
    c = lax.broadcasted_iota(jnp.int32, (chunk, chunk), 1)
    tri = (c <= r).astype(F32)
    carry = jnp.zeros((1, g_ref.shape[1]), F32)
    for k in range(s // chunk):
        blk = g_ref[k * chunk:(k + 1) * chunk, :]
        loc = jnp.dot(tri, blk, precision=HIGHEST, preferred_element_type=F32)
        f_ref[k * chunk:(k + 1) * chunk, :] = loc + carry
        carry = carry + loc[chunk - 1:chunk, :]


def cumsum_time(g, n_seq, s):
    n = g.shape[1]
    chunk = next(c for c in (256, 128, 64, 32, 16, 8) if s % c == 0)
    return pl.pallas_call(
        functools.partial(_cumsum_kernel, chunk=chunk),
        grid=(n_seq,),
        in_specs=[pl.BlockSpec((s, n), lambda i: (i, 0))],
        out_specs=pl.BlockSpec((s, n), lambda i: (i, 0)),
        out_shape=jax.ShapeDtypeStruct((n_seq * s, n), F32),
        compiler_params=_params("parallel"),
        name="cumsum_time",
    )(g)


def _fox_kernel(*refs, past, t, tq):
    o_ref = refs[-1]
    if past:
        q_ref, k_ref, v_ref, pk_ref, pv_ref, frow_ref = refs[:-2]
    else:
        q_ref, k_ref, v_ref, frow_ref = refs[:-2]
    dh = q_ref.shape[1]
    fk_all = frow_ref[...] * LOG2E
    lane = lax.broadcasted_iota(jnp.int32, (1, dh), 1)
    ones_col = jnp.where(lane == 0, 1.0, 0.0).astype(BF16)

    den_from_matmul = not past

    def with_ones(v):
        if not den_from_matmul:
            return v.astype(BF16)
        return jnp.concatenate([v.astype(BF16), jnp.broadcast_to(ones_col, v.shape)], axis=1)

    kb = k_ref[...].astype(BF16)
    vb = with_ones(v_ref[...])
    dn_t = (((1,), (1,)), ((), ()))
    row = lax.broadcasted_iota(jnp.int32, (tq, tq), 0)
    col = lax.broadcasted_iota(jnp.int32, (tq, tq), 1)
    diag_mask = col <= row

    def scores(qi):
        q = q_ref[qi * tq:(qi + 1) * tq, :]
        lo = qi * tq
        parts = []
        if past:
            parts.append((pk_ref[...].astype(BF16), with_ones(pv_ref[...]), fk_all[:, 0:past], False))
        if qi:
            parts.append((kb[0:lo, :], vb[0:lo, :], fk_all[:, past:past + lo], False))
        parts.append((kb[lo:lo + tq, :], vb[lo:lo + tq, :], fk_all[:, past + lo:past + lo + tq], True))
        logits = []
        for kp, _, fk, masked in parts:
            lg = lax.dot_general(q, kp, dn_t, preferred_element_type=F32) - fk
            logits.append(jnp.where(diag_mask, lg, NEG_INF) if masked else lg)
        mx = functools.reduce(jnp.maximum, [jnp.max(lg, axis=1, keepdims=True) for lg in logits])
        return [vp for _, vp, _, _ in parts], logits, mx

    def finish(qi, values, logits, mx):
        acc = 0.0
        den = 0.0
        for vp, lg in zip(values, logits):
            p = jnp.exp2(lg - mx)
            if not den_from_matmul:
                den = den + jnp.sum(p, axis=1, keepdims=True)
            acc = acc + jnp.dot(p.astype(BF16), vp, preferred_element_type=F32)
        if den_from_matmul:
            acc, den = acc[:, :dh], acc[:, dh:dh + 1]
        o_ref[qi * tq:(qi + 1) * tq, :] = (acc / den).astype(o_ref.dtype)

    n_tiles = t // tq
    pending = scores(0)
    for qi in range(n_tiles):
        upcoming = scores(qi + 1) if qi + 1 < n_tiles else None
        finish(qi, *pending)
        pending = upcoming


def fox_attention(p16, q_col, p32, k_col, v_col, row_blk, n_seq, t, f_row, n_heads, dh, heads,
                  past_k=None, past_v=None):
    past = 0 if past_k is None else past_k.shape[1]
    tq = _tile(t, 256)

    def head_blk(off):
        return pl.BlockSpec((t, dh), lambda i, h: (row_blk + i, off // dh + h))

    in_specs = [head_blk(q_col), head_blk(k_col), head_blk(v_col)]
    args = [p16, p32, p32]
    if past:
        past_blk = pl.BlockSpec((None, past, dh), lambda i, h: (i, 0, h))
        in_specs += [past_blk, past_blk]
        args += [past_k, past_v]
    in_specs += [
        pl.BlockSpec((None, None, 1, past + t), lambda i, h: (i, h, 0, 0)),
        pl.BlockSpec(memory_space=pl.ANY),
    ]
    args += [f_row, heads]
    return pl.pallas_call(
        functools.partial(_fox_kernel, past=past, t=t, tq=tq),
        grid=(n_seq, n_heads),
        in_specs=in_specs,
        out_specs=head_blk(0),
        out_shape=jax.ShapeDtypeStruct(heads.shape, heads.dtype),
        input_output_aliases={len(args) - 1: 0},
        compiler_params=_params("parallel", "parallel"),
        name="fox_attention",
    )(*args)


def _mlstm_kernel(*refs, t, chunk, conv_w, i_off, f_off, k_scale):
    refs = refs[:16] + refs[17:]
    (qraw_ref, kraw_ref, hq_ref, hk_ref, wq_ref, wk_ref, v_ref, og_ref, g_ref, f_ref,
     irow_ref, frow_ref, c0_ref, n0_ref, m0_ref, gh_ref,
     out_ref, c_ref, n_ref, m_ref,
     histq_ref, histk_ref, qc_ref, kc_ref) = refs
    h = pl.program_id(1)
    hpad = hq_ref.shape[0]

    def conv_silu(raw_ref, hist_in_ref, w_ref, hist_ref, scale, dst_ref):
        hist_ref[0:hpad, :] = hist_in_ref[...]
        hist_ref[hpad:hpad + t, :] = raw_ref[...]
        y = None
        for j in range(conv_w):
            start = hpad - (conv_w - 1) + j
            term = hist_ref[start:start + t, :] * w_ref[j:j + 1, :]
            y = term if y is None else y + term
        y = y * jax.nn.sigmoid(y)
        if scale != 1.0:
            y = y * scale
        dst_ref[...] = y.astype(dst_ref.dtype)

    conv_silu(qraw_ref, hq_ref, wq_ref, histq_ref, 1.0, qc_ref)
    conv_silu(kraw_ref, hk_ref, wk_ref, histk_ref, k_scale, kc_ref)

    gh = gh_ref[...]
    rr = lax.broadcasted_iota(jnp.int32, (chunk, chunk), 0)
    cc = lax.broadcasted_iota(jnp.int32, (chunk, chunk), 1)
    causal = cc <= rr
    lane = lax.broadcasted_iota(jnp.int32, (chunk, LANES), 1)

    def body(c, f_prev):
        r0 = pl.multiple_of(c * chunk, chunk)
        rows = pl.ds(r0, chunk)
        i_col = jnp.sum(jnp.where(lane == i_off + h, g_ref[rows, :], 0.0), axis=1, keepdims=True)
        b_col = jnp.sum(jnp.where(lane == f_off + h, f_ref[rows, :], 0.0), axis=1, keepdims=True) - f_prev
        i_row = irow_ref[pl.ds(c, 1), :]
        b_row = frow_ref[pl.ds(c, 1), :] - f_prev
        m_prev = m_ref[:, 0:1]
        q = qc_ref[rows, :]
        k = kc_ref[rows, :]
        v = v_ref[rows, :]
        cmat = c_ref[...]
        nvec = n_ref[...]

        d = jnp.where(causal, b_col - b_row + i_row, NEG_INF)
        inter = b_col + m_prev
        m_t = jnp.maximum(inter, jnp.max(d, axis=1, keepdims=True))
        w_inter = jnp.exp(inter - m_t)
        qk = lax.dot_general(q, k, (((1,), (1,)), ((), ())), preferred_element_type=F32)
        s = qk * jnp.exp(d - m_t)
        qc_state = lax.dot_general(q, cmat.astype(BF16), (((1,), (1,)), ((), ())),
                                   preferred_element_type=F32)
        num = w_inter * qc_state + jnp.dot(s.astype(BF16), v, preferred_element_type=F32)
        qn = jnp.sum(q.astype(F32) * nvec, axis=1, keepdims=True)
        den = w_inter * qn + jnp.sum(s, axis=1, keepdims=True)
        hval = num / jnp.maximum(jnp.abs(den), jnp.exp(-m_t))

        hn = hval * lax.rsqrt(jnp.mean(hval * hval, axis=1, keepdims=True) + EPS)
        out = jax.nn.sigmoid(og_ref[rows, :]) * hn * gh
        out_ref[rows, :] = out.astype(out_ref.dtype)

        b_last = b_col[chunk - 1:chunk, :]
        dec_row = b_last - b_row + i_row
        dec_col = b_last - b_col + i_col
        m_new = jnp.maximum(b_last + m_prev, jnp.max(dec_row, axis=1, keepdims=True))
        a = jnp.exp(b_last + m_prev - m_new)
        w_col = jnp.exp(dec_col - m_new)
        vw = (v.astype(F32) * w_col).astype(BF16)
        c_ref[...] = a * cmat + lax.dot_general(vw, k, (((0,), (0,)), ((), ())),
                                                preferred_element_type=F32)
        n_ref[...] = a * nvec + jnp.sum(w_col * k.astype(F32), axis=0, keepdims=True)
        m_ref[...] = jnp.broadcast_to(m_new, m_ref.shape)
        return f_prev + b_last

    c_ref[...] = c0_ref[...]
    n_ref[...] = n0_ref[...]
    m_ref[...] = m0_ref[...]
    lax.fori_loop(0, t // chunk, body, jnp.zeros((1, 1), F32), unroll=True)


def mlstm_heads(p32, qk_col, og_col, p16, v_col, gates, row_blk, n_seq, t, f_cum, gates_t, f_cum_t,
                conv_hist, w_conv, c0, n0, m0, g_head, i_off, f_off, heads, out_col):
    b = n_seq
    _, n_h, dv, dk = c0.shape
    qb, ogb, vb = qk_col // dk, og_col // dv, v_col // dv
    conv_w = w_conv.shape[0]
    chunk = _tile(t, MLSTM_CHUNK)
    nch = t // chunk
    hpad = 8
    hist = jnp.pad(conv_hist, ((0, 0), (hpad - (conv_w - 1), 0), (0, 0)))
    gt = gates_t.reshape(b, LANES, nch, chunk)
    ft = f_cum_t.reshape(b, LANES, nch, chunk)
    n0r = n0.reshape(b, n_h, 1, dk)
    m0r = jnp.broadcast_to(m0[:, :, None, None], (b, n_h, 1, LANES))
    ghr = g_head.reshape(n_h, 1, dv)

    rowcol = lambda width, off: pl.BlockSpec((t, width), lambda i, h: (row_blk + i, off + h))
    hcol = lambda off: pl.BlockSpec((None, hpad, dk), lambda i, h: (i, 0, off + h))
    in_specs = [
        rowcol(dk, qb), rowcol(dk, qb + n_h), hcol(0), hcol(n_h),
        pl.BlockSpec((conv_w, dk), lambda i, h: (0, h)),
        pl.BlockSpec((conv_w, dk), lambda i, h: (0, n_h + h)),
        rowcol(dv, vb), rowcol(dv, ogb),
        pl.BlockSpec((t, LANES), lambda i, h: (row_blk + i, 0)),
        pl.BlockSpec((t, LANES), lambda i, h: (i, 0)),
        pl.BlockSpec((None, None, nch, chunk), lambda i, h: (i, i_off + h, 0, 0)),
        pl.BlockSpec((None, None, nch, chunk), lambda i, h: (i, f_off + h, 0, 0)),
        pl.BlockSpec((None, None, dv, dk), lambda i, h: (i, h, 0, 0)),
        pl.BlockSpec((None, None, 1, dk), lambda i, h: (i, h, 0, 0)),
        pl.BlockSpec((None, None, 1, LANES), lambda i, h: (i, h, 0, 0)),
        pl.BlockSpec((None, 1, dv), lambda i, h: (h, 0, 0)),
    ]
    in_specs.append(pl.BlockSpec(memory_space=pl.ANY))
    args = [p32, p32, hist, hist, w_conv, w_conv, p16, p32, gates, f_cum, gt, ft, c0, n0r, m0r, ghr, heads]
    out_col = out_col // dv
    out_specs = [
        pl.BlockSpec((t, dv), lambda i, h: (row_blk + i, out_col + h)),
        pl.BlockSpec((None, None, dv, dk), lambda i, h: (i, h, 0, 0)),
        pl.BlockSpec((None, None, 1, dk), lambda i, h: (i, h, 0, 0)),
        pl.BlockSpec((None, None, 1, LANES), lambda i, h: (i, h, 0, 0)),
    ]
    out_shape = [
        jax.ShapeDtypeStruct(heads.shape, heads.dtype),
        jax.ShapeDtypeStruct((b, n_h, dv, dk), F32),
        jax.ShapeDtypeStruct((b, n_h, 1, dk), F32),
        jax.ShapeDtypeStruct((b, n_h, 1, LANES), F32),
    ]
    out, c_new, n_new, m_new = pl.pallas_call(
        functools.partial(_mlstm_kernel, t=t, chunk=chunk, conv_w=conv_w, i_off=i_off, f_off=f_off,
                          k_scale=float(dk) ** -0.5),
        grid=(b, n_h),
        in_specs=in_specs,
        out_specs=out_specs,
        out_shape=out_shape,
        input_output_aliases={len(args) - 1: 0},
        scratch_shapes=[
            pltpu.VMEM((t + hpad, dk), F32), pltpu.VMEM((t + hpad, dk), F32),
            pltpu.VMEM((t, dk), BF16), pltpu.VMEM((t, dk), BF16),
        ],
        compiler_params=_params("parallel", "parallel"),
        name="mlstm_heads",
    )(*args)
    return out, c_new, n_new[:, :, 0, :], m_new[:, :, 0, 0]


def _outproj_kernel(a_ref, w_ref, xa_ref, xb_ref, o_ref, *, n_first):
    acc = jnp.dot(a_ref[...], w_ref[...], preferred_element_type=F32)

    @pl.when(pl.program_id(0) < n_first)
    def _():
        o_ref[...] = xa_ref[...] + acc

    @pl.when(pl.program_id(0) >= n_first)
    def _():
        o_ref[...] = xb_ref[...] + acc


def out_projection(heads, w_out, xa, xb, tm):
    (ma, d), mb = xa.shape, xb.shape[0]
    tn = _tile(d, 512)
    assert ma % tm == 0
    n_first = ma // tm
    first_j = lambda i, j: (jnp.minimum(i, n_first - 1), j)
    second_j = lambda i, j: (jnp.maximum(i - n_first, 0), j)
    return pl.pallas_call(
        functools.partial(_outproj_kernel, n_first=n_first),
        grid=(pl.cdiv(ma + mb, tm), d // tn),
        in_specs=[
            pl.BlockSpec((tm, d), lambda i, j: (i, 0)),
            pl.BlockSpec((d, tn), lambda i, j: (0, j)),
            pl.BlockSpec((tm, tn), first_j), pl.BlockSpec((tm, tn), second_j),
        ],
        out_specs=pl.BlockSpec((tm, tn), lambda i, j: (i, j)),
        out_shape=jax.ShapeDtypeStruct((ma + mb, d), F32),
        compiler_params=_params("arbitrary", "arbitrary"),
        name="out_projection",
    )(heads, w_out, xa, xb)


def _pack_bf16_pairs(x):
    half = x.shape[1] // 2
    bits = lax.bitcast_convert_type(x.astype(BF16).astype(F32), jnp.uint32)
    return (bits[:, :half] >> 16) | (bits[:, half:] & jnp.uint32(0xFFFF0000))


def _unpack_pairs_f32(w):
    lo = lax.bitcast_convert_type(w << 16, F32)
    hi = lax.bitcast_convert_type(w & jnp.uint32(0xFFFF0000), F32)
    return lo, hi


def _unpack_bf16_pairs(w):
    lo, hi = _unpack_pairs_f32(w)
    return lo.astype(BF16), hi.astype(BF16)


def _router_kernel(x_ref, g_ref, whi_ref, wlo_ref, b_ref, xn_ref, eid_ref, gate_ref, *, n_groups, per_group):
    x = x_ref[...]
    xn = x * lax.rsqrt(jnp.mean(x * x, axis=-1, keepdims=True) + EPS) * g_ref[...]
    xn_ref[...] = _pack_bf16_pairs(xn)
    x_hi = xn.astype(BF16)
    x_lo = (xn - x_hi.astype(F32)).astype(BF16)
    logits = (jnp.dot(x_hi, whi_ref[...], preferred_element_type=F32)
              + jnp.dot(x_lo, whi_ref[...], preferred_element_type=F32)
              + jnp.dot(x_hi, wlo_ref[...], preferred_element_type=F32)) + b_ref[...]
    lane = lax.broadcasted_iota(jnp.int32, logits.shape, 1).astype(F32)
    big = float(LANES)
    is_g = lane < n_groups
    gl = jnp.where(is_g, logits, NEG_INF)
    gmax = jnp.max(gl, axis=1, keepdims=True)
    g_idx = jnp.min(jnp.where(gl == gmax, lane, big), axis=1, keepdims=True)
    g_prob = 1.0 / jnp.sum(jnp.where(is_g, jnp.exp(logits - gmax), 0.0), axis=1, keepdims=True)
    lo = n_groups + per_group * g_idx
    el = jnp.where((lane >= lo) & (lane < lo + per_group), logits, NEG_INF)
    e1 = jnp.max(el, axis=1, keepdims=True)
    i1 = jnp.min(jnp.where(el == e1, lane, big), axis=1, keepdims=True)
    el2 = jnp.where(lane == i1, NEG_INF, el)
    e2 = jnp.max(el2, axis=1, keepdims=True)
    i2 = jnp.min(jnp.where(el2 == e2, lane, big), axis=1, keepdims=True)
    r = jnp.exp(e2 - e1)
    w1 = g_prob / (1.0 + r)
    w2 = g_prob * r / (1.0 + r)
    eid = jnp.where(lane == 0.0, i1 - n_groups, jnp.where(lane == 1.0, i2 - n_groups, 0.0))
    eid_ref[...] = eid.astype(jnp.int32)
    gate_ref[...] = jnp.where(lane == 0.0, w1, jnp.where(lane == 1.0, w2, 0.0))


def router(x, g, w_router, b_router, n_groups, per_group):
    m, d = x.shape
    tm = _tile(m, 256)
    w_hi = w_router.astype(BF16)
    w_lo = (w_router - w_hi.astype(F32)).astype(BF16)
    return pl.pallas_call(
        functools.partial(_router_kernel, n_groups=n_groups, per_group=per_group),
        grid=(m // tm,),
        in_specs=[
            pl.BlockSpec((tm, d), lambda i: (i, 0)),
            pl.BlockSpec((1, d), lambda i: (0, 0)),
            pl.BlockSpec((d, LANES), lambda i: (0, 0)),
            pl.BlockSpec((d, LANES), lambda i: (0, 0)),
            pl.BlockSpec((1, LANES), lambda i: (0, 0)),
        ],
        out_specs=[
            pl.BlockSpec((tm, d // 2), lambda i: (i, 0)),
            pl.BlockSpec((tm, LANES), lambda i: (i, 0)),
            pl.BlockSpec((tm, LANES), lambda i: (i, 0)),
        ],
        out_shape=[
            jax.ShapeDtypeStruct((m, d // 2), jnp.uint32),
            jax.ShapeDtypeStruct((m, LANES), jnp.int32),
            jax.ShapeDtypeStruct((m, LANES), F32),
        ],
        compiler_params=_params("parallel"),
        name="router",
    )(x, g.reshape(1, d), w_hi, w_lo, b_router)


def _gather_kernel(nrows_ref, idx0_ref, idx1_ref, src_ref, o_ref, buf_ref, sem_ref, *, tm):
    step = pl.program_id(0)
    nsteps = pl.num_programs(0)
    used = lambda s: s * tm < nrows_ref[0]

    def issue(idx_ref, slot):
        def body(r, carry):
            for q, row in enumerate((r, r + tm // 2)):
                tok = idx_ref[0, row]
                pltpu.make_async_copy(src_ref.at[pl.ds(tok, 1), :], buf_ref.at[slot, pl.ds(row, 1), :],
                                      sem_ref.at[slot]).start(priority=q)
            return carry
        lax.fori_loop(0, tm // 2, body, 0, unroll=DMA_ISSUE_UNROLL // 2)

    @pl.when(step == 0)
    def _():
        issue(idx0_ref, 0)

    @pl.when((step + 1 < nsteps) & used(step + 1))
    def _():
        issue(idx1_ref, (step + 1) % 2)

    @pl.when(used(step))
    def _():
        slot = step % 2
        pltpu.make_async_copy(src_ref.at[pl.ds(0, tm), :], buf_ref.at[slot], sem_ref.at[slot]).wait()
        half = buf_ref.shape[2]
        lo, hi = _unpack_bf16_pairs(buf_ref[slot])
        o_ref[:, :half] = lo
        o_ref[:, half:] = hi

    @pl.when(jnp.logical_not(used(step)))
    def _():
        o_ref[...] = jnp.zeros_like(o_ref)


def _smem_tiles(n_steps, width):
    cur = pl.BlockSpec((None, 1, width), lambda i: (i, 0, 0), memory_space=pltpu.SMEM)
    nxt = pl.BlockSpec((None, 1, width), lambda i: (jnp.minimum(i + 1, n_steps - 1), 0, 0),
                       memory_space=pltpu.SMEM)
    return cur, nxt


def gather_rows(src, idx, n_rows, tm):
    n, half = src.shape
    d = 2 * half
    a = idx.shape[0]
    n_steps = a // tm
    cur, nxt = _smem_tiles(n_steps, tm)
    idx3 = idx.reshape(n_steps, 1, tm)
    return pl.pallas_call(
        functools.partial(_gather_kernel, tm=tm),
        grid=(n_steps,),
        in_specs=[pl.BlockSpec(memory_space=pltpu.SMEM), cur, nxt, pl.BlockSpec(memory_space=pl.ANY)],
        out_specs=pl.BlockSpec((tm, d), lambda i: (i, 0)),
        scratch_shapes=[pltpu.VMEM((2, tm, half), src.dtype), pltpu.SemaphoreType.DMA((2,))],
        out_shape=jax.ShapeDtypeStruct((a, d), BF16),
        compiler_params=_params("arbitrary"),
        name="gather_rows",
    )(n_rows, idx3, idx3, src)


def _expert_up_kernel(te_ref, nv_ref, x_ref, wg_ref, wu_ref, h_ref):
    t = pl.program_id(1)

    @pl.when(t < nv_ref[0])
    def _():
        x = x_ref[...]
        g = jnp.dot(x, wg_ref[...].astype(BF16), preferred_element_type=F32)
        u = jnp.dot(x, wu_ref[...].astype(BF16), preferred_element_type=F32)
        h_ref[...] = (g * jax.nn.sigmoid(g) * u).astype(h_ref.dtype)

    @pl.when(t >= nv_ref[0])
    def _():
        h_ref[...] = jnp.zeros_like(h_ref)


def expert_up(xg, w_gate, w_up, tile_expert, n_valid, tm):
    a, d = xg.shape
    _, _, f = w_gate.shape
    tn = _tile(f, 512)
    return pl.pallas_call(
        _expert_up_kernel,
        grid_spec=pltpu.PrefetchScalarGridSpec(
            num_scalar_prefetch=2,
            grid=(f // tn, a // tm),
            in_specs=[
                pl.BlockSpec((tm, d), lambda c, t, te, nv: (t, 0)),
                pl.BlockSpec((None, d, tn), lambda c, t, te, nv: (te[t], 0, c)),
                pl.BlockSpec((None, d, tn), lambda c, t, te, nv: (te[t], 0, c)),
            ],
            out_specs=pl.BlockSpec((tm, tn), lambda c, t, te, nv: (t, c)),
        ),
        out_shape=jax.ShapeDtypeStruct((a, f), BF16),
        compiler_params=_params("arbitrary", "arbitrary"),
        name="expert_up",
    )(tile_expert, n_valid, xg, w_gate, w_up)


def _expert_down_kernel(te_ref, nv_ref, h_ref, wd_ref, y_ref):
    t = pl.program_id(1)

    @pl.when(t < nv_ref[0])
    def _():
        y = jnp.dot(h_ref[...], wd_ref[...].astype(BF16), preferred_element_type=F32)
        y_ref[...] = _pack_bf16_pairs(y)

    @pl.when(t >= nv_ref[0])
    def _():
        y_ref[...] = jnp.zeros_like(y_ref)


def expert_down(hg, w_down, tile_expert, n_valid, tm):
    a, f = hg.shape
    _, _, d = w_down.shape
    tn = _tile(d, DOWN_CHUNK)
    return pl.pallas_call(
        _expert_down_kernel,
        grid_spec=pltpu.PrefetchScalarGridSpec(
            num_scalar_prefetch=2,
            grid=(d // tn, a // tm),
            in_specs=[
                pl.BlockSpec((tm, f), lambda c, t, te, nv: (t, 0)),
                pl.BlockSpec((None, f, tn), lambda c, t, te, nv: (te[t], 0, c)),
            ],
            out_specs=pl.BlockSpec((tm, tn // 2), lambda c, t, te, nv: (t, c)),
        ),
        out_shape=jax.ShapeDtypeStruct((a, d // 2), jnp.uint32),
        compiler_params=_params("arbitrary", "arbitrary"),
        name="expert_down",
    )(tile_expert, n_valid, hg, w_down)


def _combine_kernel(pos0_ref, pos1_ref, x_ref, gate_ref, g_ref, y_ref, oa_ref, ob_ref, buf_ref, sem_ref,
                    xs_ref, *, tm, n_first, chunk):
    step = pl.program_id(0)
    nsteps = pl.num_programs(0)

    def issue(pos_ref, slot):
        def body(r, carry):
            for k in range(TOP_K):
                p = pos_ref[0, r * TOP_K + k]
                pltpu.make_async_copy(y_ref.at[pl.ds(p, 1), :], buf_ref.at[slot, k, pl.ds(r, 1), :],
                                      sem_ref.at[slot]).start(priority=k)
            return carry
        lax.fori_loop(0, tm, body, 0, unroll=DMA_ISSUE_UNROLL)

    @pl.when(step == 0)
    def _():
        issue(pos0_ref, 0)

    @pl.when(step + 1 < nsteps)
    def _():
        issue(pos1_ref, (step + 1) % 2)

    slot = step % 2
    for k in range(TOP_K):
        pltpu.make_async_copy(y_ref.at[pl.ds(0, tm), :], buf_ref.at[slot, k], sem_ref.at[slot]).wait()
    gates = gate_ref[...]
    d = x_ref.shape[1]
    half_chunk = chunk // 2
    cw = min(COMBINE_COLS, half_chunk)
    ssq = jnp.zeros((tm, 1), F32)
    for pc in range(0, d // 2, cw):
        real_lo = (pc // half_chunk) * chunk + pc % half_chunk
        real_hi = real_lo + half_chunk
        x_lo = x_ref[:, real_lo:real_lo + cw]
        x_hi = x_ref[:, real_hi:real_hi + cw]
        for k in range(TOP_K):
            lo, hi = _unpack_pairs_f32(buf_ref[slot, k, :, pc:pc + cw])
            x_lo = x_lo + gates[:, k:k + 1] * lo
            x_hi = x_hi + gates[:, k:k + 1] * hi
        xs_ref[:, real_lo:real_lo + cw] = x_lo
        xs_ref[:, real_hi:real_hi + cw] = x_hi
        ssq = ssq + jnp.sum(x_lo * x_lo, axis=1, keepdims=True) + jnp.sum(x_hi * x_hi, axis=1, keepdims=True)
    inv = lax.rsqrt(ssq * (1.0 / d) + EPS)

    def emit(o_ref):
        for c in range(0, d, 2 * cw):
            o_ref[:, c:c + 2 * cw] = xs_ref[:, c:c + 2 * cw] * inv * g_ref[:, c:c + 2 * cw]

    @pl.when(step < n_first)
    def _():
        emit(oa_ref)

    @pl.when(step >= n_first)
    def _():
        emit(ob_ref)


def combine_norm(x, gates, pos, yg, g_final, m_first, tm):
    m, d = x.shape
    n_steps = m // tm
    n_first = m_first // tm
    cur, nxt = _smem_tiles(n_steps, tm * TOP_K)
    pos3 = pos.reshape(n_steps, 1, tm * TOP_K)
    first, second = _split_maps(n_first)
    return pl.pallas_call(
        functools.partial(_combine_kernel, tm=tm, n_first=n_first, chunk=_tile(d, DOWN_CHUNK)),
        grid=(n_steps,),
        in_specs=[
            cur, nxt,
            pl.BlockSpec((tm, d), lambda i: (i, 0)),
            pl.BlockSpec((tm, LANES), lambda i: (i, 0)),
            pl.BlockSpec((1, d), lambda i: (0, 0)),
            pl.BlockSpec(memory_space=pl.ANY),
        ],
        out_specs=[pl.BlockSpec((tm, d), first), pl.BlockSpec((tm, d), second)],
        scratch_shapes=[pltpu.VMEM((2, TOP_K, tm, d // 2), jnp.uint32), pltpu.SemaphoreType.DMA((2,)),
                        pltpu.VMEM((tm, d), F32)],
        out_shape=[jax.ShapeDtypeStruct((m_first, d), F32), jax.ShapeDtypeStruct((m - m_first, d), F32)],
        compiler_params=_params("arbitrary"),
        name="combine_norm",
    )(pos3, pos3, x, gates, g_final.reshape(1, d), yg)


def _dispatch_plan(eid, n_experts, tm):
    n = eid.shape[0]
    a = n * TOP_K
    e_flat = eid.reshape(a)
    onehot = (e_flat[:, None] == jnp.arange(n_experts, dtype=jnp.int32)[None, :]).astype(jnp.int32)
    csum = jnp.cumsum(onehot, axis=0)
    counts = csum[-1]
    rank = jnp.take_along_axis(csum, e_flat[:, None], axis=1)[:, 0] - 1
    padded = ((counts + tm - 1) // tm) * tm
    pad_end = jnp.cumsum(padded)
    pos = (pad_end - padded)[e_flat] + rank
    n_tiles = a // tm + n_experts
    src = (jnp.arange(n_tiles * tm, dtype=jnp.int32) % n).at[pos].set(jnp.arange(a, dtype=jnp.int32) // TOP_K)
    tile_end = pad_end // tm
    tile_ids = jnp.arange(n_tiles, dtype=jnp.int32)
    tile_expert = jnp.sum((tile_ids[:, None] >= tile_end[None, :]).astype(jnp.int32), axis=1)
    tile_expert = jnp.minimum(tile_expert, n_experts - 1)
    n_valid = tile_end[-1:].astype(jnp.int32)
    return pos.astype(jnp.int32), src, tile_expert, n_valid


def moe_layer(x, m_first, g_ffn, w_router, b_router, w_gate, w_up, w_down, g_final, n_groups, per_group):
    m, d = x.shape
    n_experts = n_groups * per_group
    xn, eid, gates = router(x, g_ffn, w_router, b_router, n_groups, per_group)
    a = m * TOP_K
    tm = _tile(a, min(512, max(64, a // n_experts)))
    pos, src, tile_expert, n_valid = _dispatch_plan(eid[:, :TOP_K], n_experts, tm)
    xg = gather_rows(xn, src, n_valid * tm, _tile(tm, 256))
    hg = expert_up(xg, w_gate, w_up, tile_expert, n_valid, tm)
    yg = expert_down(hg, w_down, tile_expert, n_valid, tm)
    return combine_norm(x, gates, pos, yg, g_final, m_first, _tile(math.gcd(m_first, m - m_first), 128))


def kernel(x_prompt, x_sample, cache_fox_k, cache_fox_v, cache_fox_logf, state_mlstm_C, state_mlstm_n, state_mlstm_m, state_mlstm_conv, g_norm_mix, w_in, b_fox_f, b_mlstm_i, b_mlstm_f, w_conv, g_mlstm_head, w_out, g_norm_ffn, w_router_group, b_router_group, w_router_expert, b_router_expert, w_exp_gate, w_exp_up, w_exp_down, g_norm_final):
    depth = w_in.shape[0]
    assert depth == 1, "the final norm is fused into the last layer's MoE combine; one layer supported"
    bp, tp, d = x_prompt.shape
    bs, ts, _ = x_sample.shape
    n_fox, dh = cache_fox_k.shape[-2:]
    n_ml, dv, dk = state_mlstm_C.shape[-3:]
    past = cache_fox_k.shape[2]
    n_groups = w_router_group.shape[-1]
    n_experts = w_router_expert.shape[-1]
    conv_w = w_conv.shape[1]
    half = d // 2
    mp, ms = bp * tp, bs * ts
    assert n_fox * dh == half and n_ml * dv == half and 2 * n_ml * dk == half
    assert n_fox + 2 * n_ml <= LANES and n_groups + n_experts <= LANES
    assert mp % ts == 0 and ts >= conv_w - 1, "sample sequences are addressed as row blocks after the prompt rows"
    l = 0
    sizes = (half, half, half, n_fox, half, half, half, n_ml, n_ml)
    offs = [0]
    for s in sizes:
        offs.append(offs[-1] + s)
    col = lambda i: w_in[l][:, offs[i]:offs[i + 1]]
    w_a = w_in[l][:, offs[0]:offs[3]].astype(BF16)
    w_b = w_in[l][:, offs[4]:offs[7]].astype(BF16)
    n_gate = n_fox + 2 * n_ml
    w_gatecols = jnp.pad(jnp.concatenate([col(3), col(7), col(8)], axis=1), ((0, 0), (0, LANES - n_gate))).astype(BF16)
    b_gatecols = jnp.pad(jnp.concatenate([b_fox_f[l], b_mlstm_i[l], b_mlstm_f[l]]).astype(F32),
                         (0, LANES - n_gate)).reshape(1, LANES)
    n_r = n_groups + n_experts
    w_router = jnp.pad(jnp.concatenate([w_router_group[l], w_router_expert[l]], axis=1), ((0, 0), (0, LANES - n_r)))
    b_router = jnp.pad(jnp.concatenate([b_router_group[l], b_router_expert[l]]).astype(F32),
                       (0, LANES - n_r)).reshape(1, LANES)
    i_off, f_off = n_fox, n_fox + n_ml
    col32 = {s: k * half for k, s in enumerate(P32_SEGS)}
    col16 = {s: k * half for k, s in enumerate(P16_SEGS)}

    xp2, xs2 = x_prompt.reshape(mp, d), x_sample.reshape(ms, d)
    tm_rows = _tile(mp, 1024)
    xn = rmsnorm_cast(xp2, xs2, g_norm_mix[l], BF16)
    tm_proj = next((tmc for tmc in range(1280, 767, -16) if (mp + ms) % tmc == 0), tm_rows)
    p32, p16, gates = in_projection(xn, w_a, w_b, w_gatecols, b_gatecols, n_fox, n_ml, tm_proj,
                                    float(dh) ** -0.5 * LOG2E)

    f_p = cumsum_time(gates, bp, tp)
    f_p3 = f_p.reshape(bp, tp, LANES)
    g_p3 = gates[:mp].reshape(bp, tp, LANES)
    f_row = lambda f3: jnp.swapaxes(f3[:, :, :n_fox], 1, 2)[:, :, None, :]
    heads = fox_attention(p16, col16[SEG_FQ], p32, col32[SEG_FK], col32[SEG_FV], 0, bp, tp,
                          f_row(f_p3), n_fox, dh, xn)
    hist_p = jnp.zeros((bp, conv_w - 1, half), F32)
    heads, c_p, n_p, m_p = mlstm_heads(
        p32, col32[SEG_QK], col32[SEG_MO], p16, col16[SEG_MV], gates, 0, bp, tp, f_p,
        jnp.swapaxes(g_p3, 1, 2), jnp.swapaxes(f_p3, 1, 2), hist_p, w_conv[l],
        jnp.zeros((bp, n_ml, dv, dk), F32), jnp.zeros((bp, n_ml, dk), F32), jnp.zeros((bp, n_ml), F32),
        g_mlstm_head[l], i_off, f_off, heads, half)

    g_s3 = gates[mp:].reshape(bs, ts, LANES)
    cache_pad = jnp.pad(cache_fox_logf[l].astype(F32), ((0, 0), (0, 0), (0, LANES - n_fox)))
    logf_s = jnp.concatenate([cache_pad, g_s3], axis=1)
    f_s3 = cumsum_time(logf_s.reshape(bs * (past + ts), LANES), bs, past + ts).reshape(bs, past + ts, LANES)
    heads = fox_attention(p16, col16[SEG_FQ], p32, col32[SEG_FK], col32[SEG_FV], mp // ts, bs, ts,
                          f_row(f_s3), n_fox, dh, heads,
                          cache_fox_k[l].reshape(bs, past, half), cache_fox_v[l].reshape(bs, past, half))
    fn_s = cumsum_time(g_s3.reshape(ms, LANES), bs, ts)
    heads, c_s, n_s, m_s = mlstm_heads(
        p32, col32[SEG_QK], col32[SEG_MO], p16, col16[SEG_MV], gates, mp // ts, bs, ts, fn_s,
        jnp.swapaxes(g_s3, 1, 2), jnp.swapaxes(fn_s.reshape(bs, ts, LANES), 1, 2), state_mlstm_conv[l], w_conv[l],
        state_mlstm_C[l], state_mlstm_n[l], state_mlstm_m[l], g_mlstm_head[l], i_off, f_off, heads, half)

    x1 = out_projection(heads, w_out[l].astype(BF16), xp2, xs2, tm_rows)
    y_p, y_s = moe_layer(x1, mp, g_norm_ffn[l], w_router, b_router, w_exp_gate[l], w_exp_up[l], w_exp_down[l],
                         g_norm_final, n_groups, n_experts // n_groups)

    def new_state(rows, b, t, g3, c_new, n_new, m_new, conv_hist):
        blk = p32[rows]
        fk = blk[:, col32[SEG_FK]:col32[SEG_FK] + half].reshape(b, t, n_fox, dh)
        fv = blk[:, col32[SEG_FV]:col32[SEG_FV] + half].reshape(b, t, n_fox, dh)
        qk_tail = blk[:, col32[SEG_QK]:col32[SEG_QK] + half].reshape(b, t, half)[:, t - min(t, conv_w - 1):]
        qk_hist = jnp.concatenate([conv_hist.astype(F32), qk_tail], axis=1)[:, -(conv_w - 1):]
        return tuple(a[None] for a in (fk, fv, g3[:, :, :n_fox], c_new, n_new, m_new, qk_hist))

    st_p = new_state(slice(0, mp), bp, tp, g_p3, c_p, n_p, m_p, hist_p)
    st_s = new_state(slice(mp, mp + ms), bs, ts, g_s3, c_s, n_s, m_s, state_mlstm_conv[l])
    return (y_p.reshape(bp, tp, d), y_s.reshape(bs, ts, d)) + st_p + st_s
```

```python
import functools
import math

import jax
import jax.numpy as jnp
from jax import lax
from jax.experimental import pallas as pl
from jax.experimental.pallas import tpu as pltpu

F32 = jnp.float32
BF16 = jnp.bfloat16
EPS = 1e-6
LANES = 128
V7X_VMEM_LIMIT = 56 * 1024 * 1024
TOP_K = 2
MLSTM_CHUNK = 256
DMA_ISSUE_UNROLL = 8
FOX_LOOKAHEAD = 1
COMBINE_COLS = 256
DOWN_CHUNK = 2048
HIGHEST = lax.Precision.HIGHEST
NEG_INF = float("-inf")
LOG2E = math.log2(math.e)


def _params(*sem):
    return pltpu.CompilerParams(dimension_semantics=sem, vmem_limit_bytes=V7X_VMEM_LIMIT)


def _tile(n, pref):
    if n <= pref:
        return n
    for t in range(pref - pref % 8, 7, -8):
        if n % t == 0:
            return t
    return n


def _log_sigmoid(x):
    return jnp.minimum(x, 0.0) - jnp.log1p(jnp.exp(-jnp.abs(x)))


def _split_maps(n_first):
    first = lambda i, *_: (jnp.minimum(i, n_first - 1), 0)
    second = lambda i, *_: (jnp.maximum(i - n_first, 0), 0)
    return first, second


def _rmsnorm_kernel(xa_ref, xb_ref, g_ref, o_ref, *, n_first):
    def emit(x_ref):
        x = x_ref[...]
        y = x * lax.rsqrt(jnp.mean(x * x, axis=-1, keepdims=True) + EPS)
        o_ref[...] = (y * g_ref[...]).astype(o_ref.dtype)

    @pl.when(pl.program_id(0) < n_first)
    def _():
        emit(xa_ref)

    @pl.when(pl.program_id(0) >= n_first)
    def _():
        emit(xb_ref)


def rmsnorm_cast(xa, xb, g, out_dtype):
    (ma, d), mb = xa.shape, xb.shape[0]
    tm = _tile(math.gcd(ma, mb), 256)
    first, second = _split_maps(ma // tm)
    return pl.pallas_call(
        functools.partial(_rmsnorm_kernel, n_first=ma // tm),
        grid=((ma + mb) // tm,),
        in_specs=[pl.BlockSpec((tm, d), first), pl.BlockSpec((tm, d), second),
                  pl.BlockSpec((1, d), lambda i: (0, 0))],
        out_specs=pl.BlockSpec((tm, d), lambda i: (i, 0)),
        out_shape=jax.ShapeDtypeStruct((ma + mb, d), out_dtype),
        compiler_params=_params("arbitrary"),
        name="rmsnorm_cast",
    )(xa, xb, g.reshape(1, d))


SEG_FQ, SEG_FK, SEG_FV, SEG_QK, SEG_MV, SEG_MO = range(6)
P32_SEGS = (SEG_FK, SEG_FV, SEG_QK, SEG_MO)
P16_SEGS = (SEG_FQ, SEG_MV)


def _inproj_kernel(x_ref, wa_ref, wb_ref, ws_ref, b_ref, p32_ref, p16_ref, gate_ref, *, nseg, n_fox, n_ml,
                   q_scale):
    j = pl.program_id(1)
    seg = j // nseg

    def emit(w_ref):
        acc = jnp.dot(x_ref[...], w_ref[...], preferred_element_type=F32)
        is16 = (seg == SEG_FQ) | (seg == SEG_MV)

        @pl.when(is16)
        def _():
            p16_ref[...] = (acc * jnp.where(seg == SEG_FQ, q_scale, 1.0)).astype(p16_ref.dtype)

        @pl.when(jnp.logical_not(is16))
        def _():
            p32_ref[...] = acc

    @pl.when(seg < 3)
    def _():
        emit(wa_ref)

    @pl.when(seg >= 3)
    def _():
        emit(wb_ref)

    @pl.when(j == 0)
    def _():
        p = jnp.dot(x_ref[...], ws_ref[...], preferred_element_type=F32) + b_ref[...]
        lane = lax.broadcasted_iota(jnp.int32, p.shape, 1)
        is_id = (lane >= n_fox) & (lane < n_fox + n_ml)
        gate_ref[...] = jnp.where(is_id, p, _log_sigmoid(p))


def _held_block(j, nseg, segs):
    seg = j // nseg
    blk = jnp.int32(0)
    for k, s in enumerate(segs):
        here = k * nseg + (j - s * nseg)
        done = (k + 1) * nseg - 1
        blk = jnp.where(seg == s, here, jnp.where(seg > s, done, blk))
    return blk


def in_projection(xn, w_a, w_b, w_gate, b_gate, n_fox, n_ml, tm, q_scale):
    m, d = xn.shape
    w = w_a.shape[1] // 3
    tn = _tile(w, 512)
    nseg = w // tn
    return pl.pallas_call(
        functools.partial(_inproj_kernel, nseg=nseg, n_fox=n_fox, n_ml=n_ml, q_scale=q_scale),
        grid=(pl.cdiv(m, tm), 6 * nseg),
        in_specs=[
            pl.BlockSpec((tm, d), lambda i, j: (i, 0)),
            pl.BlockSpec((d, tn), lambda i, j: (0, jnp.minimum(j, 3 * nseg - 1))),
            pl.BlockSpec((d, tn), lambda i, j: (0, jnp.maximum(j - 3 * nseg, 0))),
            pl.BlockSpec((d, LANES), lambda i, j: (0, 0)),
            pl.BlockSpec((1, LANES), lambda i, j: (0, 0)),
        ],
        out_specs=[
            pl.BlockSpec((tm, tn), lambda i, j: (i, _held_block(j, nseg, P32_SEGS))),
            pl.BlockSpec((tm, tn), lambda i, j: (i, _held_block(j, nseg, P16_SEGS))),
            pl.BlockSpec((tm, LANES), lambda i, j: (i, 0)),
        ],
        out_shape=[
            jax.ShapeDtypeStruct((m, len(P32_SEGS) * w), F32),
            jax.ShapeDtypeStruct((m, len(P16_SEGS) * w), BF16),
            jax.ShapeDtypeStruct((m, LANES), F32),
        ],
        compiler_params=_params("arbitrary", "arbitrary"),
        name="in_projection",
    )(xn, w_a, w_b, w_gate, b_gate)


def _cumsum_kernel(g_ref, f_ref, *, chunk):
    s = g_ref.shape[0]
    r = lax.broadcasted_iota(jnp.int32, (chunk, chunk), 0)
    c = lax.broadcasted_iota(jnp.int32, (chunk, chunk), 1)
    tri = (c <= r).astype(F32)
    carry = jnp.zeros((1, g_ref.shape[1]), F32)
    for k in range(s // chunk):
        blk = g_ref[k * chunk:(k + 1) * chunk, :]
        loc = jnp.dot(tri, blk, precision=HIGHEST, preferred_element_type=F32)
        f_ref[k * chunk:(k + 1) * chunk, :] = loc + carry
        carry = carry + loc[chunk - 1:chunk, :]


def cumsum_time(g, n_seq, s):
    n = g.shape[1]
    chunk = next(c for c in (256, 128, 64, 32, 16, 8) if s % c == 0)
    return pl.pallas_call(
        functools.partial(_cumsum_kernel, chunk=chunk),
        grid=(n_seq,),
        in_specs=[pl.BlockSpec((s, n), lambda i: (i, 0))],
        out_specs=pl.BlockSpec((s, n), lambda i: (i, 0)),
        out_shape=jax.ShapeDtypeStruct((n_seq * s, n), F32),
        compiler_params=_params("parallel"),
        name="cumsum_time",
    )(g)


def _fox_kernel(*refs, past, t, tq):
    o_ref = refs[-1]
    if past:
        q_ref, k_ref, v_ref, pk_ref, pv_ref, frow_ref = refs[:-2]
    else:
        q_ref, k_ref, v_ref, frow_ref = refs[:-2]
    dh = q_ref.shape[1]
    fk_all = frow_ref[...] * LOG2E
    lane = lax.broadcasted_iota(jnp.int32, (1, dh), 1)
    ones_col = jnp.where(lane == 0, 1.0, 0.0).astype(BF16)

    den_from_matmul = not past

    def with_ones(v):
        if not den_from_matmul:
            return v.astype(BF16)
        return jnp.concatenate([v.astype(BF16), jnp.broadcast_to(ones_col, v.shape)], axis=1)

    kb = k_ref[...].astype(BF16)
    vb = with_ones(v_ref[...])
    dn_t = (((1,), (1,)), ((), ()))
    row = lax.broadcasted_iota(jnp.int32, (tq, tq), 0)
    col = lax.broadcasted_iota(jnp.int32, (tq, tq), 1)
    diag_mask = col <= row

    def scores(qi):
        q = q_ref[qi * tq:(qi + 1) * tq, :]
        lo = qi * tq
        parts = []
        if past:
            parts.append((pk_ref[...].astype(BF16), with_ones(pv_ref[...]), fk_all[:, 0:past], False))
        if qi:
            parts.append((kb[0:lo, :], vb[0:lo, :], fk_all[:, past:past + lo], False))
        parts.append((kb[lo:lo + tq, :], vb[lo:lo + tq, :], fk_all[:, past + lo:past + lo + tq], True))
        logits = []
        for kp, _, fk, masked in parts:
            lg = lax.dot_general(q, kp, dn_t, preferred_element_type=F32) - fk
            logits.append(jnp.where(diag_mask, lg, NEG_INF) if masked else lg)
        mx = functools.reduce(jnp.maximum, [jnp.max(lg, axis=1, keepdims=True) for lg in logits])
        return [vp for _, vp, _, _ in parts], logits, mx

    def finish(qi, values, logits, mx):
        acc = 0.0
        den = 0.0
        for vp, lg in zip(values, logits):
            p = jnp.exp2(lg - mx)
            if not den_from_matmul:
                den = den + jnp.sum(p, axis=1, keepdims=True)
            acc = acc + jnp.dot(p.astype(BF16), vp, preferred_element_type=F32)
        if den_from_matmul:
            acc, den = acc[:, :dh], acc[:, dh:dh + 1]
        o_ref[qi * tq:(qi + 1) * tq, :] = (acc / den).astype(o_ref.dtype)

    n_tiles = t // tq
    queue = [scores(qi) for qi in range(min(FOX_LOOKAHEAD, n_tiles))]
    for qi in range(n_tiles):
        if qi + FOX_LOOKAHEAD < n_tiles:
            queue.append(scores(qi + FOX_LOOKAHEAD))
        finish(qi, *queue.pop(0))


def fox_attention(p16, q_col, p32, k_col, v_col, row_blk, n_seq, t, f_row, n_heads, dh, heads,
                  past_k=None, past_v=None):
    past = 0 if past_k is None else past_k.shape[1]
    tq = _tile(t, 256)

    def head_blk(off):
        return pl.BlockSpec((t, dh), lambda i, h: (row_blk + i, off // dh + h))

    in_specs = [head_blk(q_col), head_blk(k_col), head_blk(v_col)]
    args = [p16, p32, p32]
    if past:
        past_blk = pl.BlockSpec((None, past, dh), lambda i, h: (i, 0, h))
        in_specs += [past_blk, past_blk]
        args += [past_k, past_v]
    in_specs += [
        pl.BlockSpec((None, None, 1, past + t), lambda i, h: (i, h, 0, 0)),
        pl.BlockSpec(memory_space=pl.ANY),
    ]
    args += [f_row, heads]
    return pl.pallas_call(
        functools.partial(_fox_kernel, past=past, t=t, tq=tq),
        grid=(n_seq, n_heads),
        in_specs=in_specs,
        out_specs=head_blk(0),
        out_shape=jax.ShapeDtypeStruct(heads.shape, heads.dtype),
        input_output_aliases={len(args) - 1: 0},
        compiler_params=_params("parallel", "parallel"),
        name="fox_attention",
    )(*args)


def _mlstm_kernel(*refs, t, chunk, conv_w, i_off, f_off, k_scale):
    refs = refs[:16] + refs[17:]
    (qraw_ref, kraw_ref, hq_ref, hk_ref, wq_ref, wk_ref, v_ref, og_ref, g_ref, f_ref,
     irow_ref, frow_ref, c0_ref, n0_ref, m0_ref, gh_ref,
     out_ref, c_ref, n_ref, m_ref,
     histq_ref, histk_ref, qc_ref, kc_ref) = refs
    h = pl.program_id(1)
    hpad = hq_ref.shape[0]

    def conv_silu(raw_ref, hist_in_ref, w_ref, hist_ref, scale, dst_ref):
        hist_ref[0:hpad, :] = hist_in_ref[...]
        hist_ref[hpad:hpad + t, :] = raw_ref[...]
        y = None
        for j in range(conv_w):
            start = hpad - (conv_w - 1) + j
            term = hist_ref[start:start + t, :] * w_ref[j:j + 1, :]
            y = term if y is None else y + term
        y = y * jax.nn.sigmoid(y)
        if scale != 1.0:
            y = y * scale
        dst_ref[...] = y.astype(dst_ref.dtype)

    conv_silu(qraw_ref, hq_ref, wq_ref, histq_ref, 1.0, qc_ref)
    conv_silu(kraw_ref, hk_ref, wk_ref, histk_ref, k_scale, kc_ref)

    gh = gh_ref[...]
    rr = lax.broadcasted_iota(jnp.int32, (chunk, chunk), 0)
    cc = lax.broadcasted_iota(jnp.int32, (chunk, chunk), 1)
    causal = cc <= rr
    lane = lax.broadcasted_iota(jnp.int32, (chunk, LANES), 1)

    chunks = range(t // chunk)
    rows = [slice(c * chunk, (c + 1) * chunk) for c in chunks]
    dn_t = (((1,), (1,)), ((), ()))

    f_col = [jnp.sum(jnp.where(lane == f_off + h, f_ref[rows[c], :], 0.0), axis=1, keepdims=True)
             for c in chunks]
    i_col = [jnp.sum(jnp.where(lane == i_off + h, g_ref[rows[c], :], 0.0), axis=1, keepdims=True)
             for c in chunks]
    i_row = [irow_ref[c:c + 1, :] for c in chunks]
    f_start = [jnp.zeros((1, 1), F32)] + [f_col[c][chunk - 1:chunk, :] for c in chunks[:-1]]
    b_col = [f_col[c] - f_start[c] for c in chunks]
    b_row = [frow_ref[c:c + 1, :] - f_start[c] for c in chunks]
    b_last = [b_col[c][chunk - 1:chunk, :] for c in chunks]
    dec_max = [jnp.max(b_last[c] - b_row[c] + i_row[c], axis=1, keepdims=True) for c in chunks]
    m_in, m_out = [], []
    m_prev = m0_ref[:, 0:1]
    for c in chunks:
        m_in.append(m_prev)
        m_prev = jnp.maximum(b_last[c] + m_prev, dec_max[c])
        m_out.append(m_prev)

    d = [jnp.where(causal, b_col[c] - b_row[c] + i_row[c], NEG_INF) for c in chunks]
    d_max = [jnp.max(d[c], axis=1, keepdims=True) for c in chunks]
    inter = [b_col[c] + m_in[c] for c in chunks]
    m_t = [jnp.maximum(inter[c], d_max[c]) for c in chunks]
    w_inter = [jnp.exp(inter[c] - m_t[c]) for c in chunks]
    qk = [lax.dot_general(qc_ref[rows[c], :], kc_ref[rows[c], :], dn_t, preferred_element_type=F32)
          for c in chunks]
    s = [qk[c] * jnp.exp(d[c] - m_t[c]) for c in chunks]
    sv = [jnp.dot(s[c].astype(BF16), v_ref[rows[c], :], preferred_element_type=F32) for c in chunks]
    s_sum = [jnp.sum(s[c], axis=1, keepdims=True) for c in chunks]
    a = [jnp.exp(b_last[c] + m_in[c] - m_out[c]) for c in chunks]
    kw = [jnp.exp(b_last[c] - b_col[c] + i_col[c] - m_out[c]) * kc_ref[rows[c], :].astype(F32) for c in chunks]
    c_add = [lax.dot_general(v_ref[rows[c], :], kw[c].astype(BF16), (((0,), (0,)), ((), ())),
                             preferred_element_type=F32) for c in chunks]
    n_add = [jnp.sum(kw[c], axis=0, keepdims=True) for c in chunks]
    cmat, nvec = c0_ref[...], n0_ref[...]
    for c in chunks:
        q = qc_ref[rows[c], :]
        num = w_inter[c] * lax.dot_general(q, cmat.astype(BF16), dn_t, preferred_element_type=F32) + sv[c]
        den = w_inter[c] * jnp.sum(q.astype(F32) * nvec, axis=1, keepdims=True) + s_sum[c]
        hval = num / jnp.maximum(jnp.abs(den), jnp.exp(-m_t[c]))
        hn = hval * lax.rsqrt(jnp.mean(hval * hval, axis=1, keepdims=True) + EPS)
        out = jax.nn.sigmoid(og_ref[rows[c], :]) * hn * gh
        out_ref[rows[c], :] = out.astype(out_ref.dtype)
        cmat = a[c] * cmat + c_add[c]
        nvec = a[c] * nvec + n_add[c]
    c_ref[...] = cmat
    n_ref[...] = nvec
    m_ref[...] = jnp.broadcast_to(m_prev, m_ref.shape)


def mlstm_heads(p32, qk_col, og_col, p16, v_col, gates, row_blk, n_seq, t, f_cum, gates_t, f_cum_t,
                conv_hist, w_conv, c0, n0, m0, g_head, i_off, f_off, heads, out_col):
    b = n_seq
    _, n_h, dv, dk = c0.shape
    qb, ogb, vb = qk_col // dk, og_col // dv, v_col // dv
    conv_w = w_conv.shape[0]
    chunk = _tile(t, MLSTM_CHUNK)
    nch = t // chunk
    hpad = 8
    hist = jnp.pad(conv_hist, ((0, 0), (hpad - (conv_w - 1), 0), (0, 0)))
    gt = gates_t.reshape(b, LANES, nch, chunk)
    ft = f_cum_t.reshape(b, LANES, nch, chunk)
    n0r = n0.reshape(b, n_h, 1, dk)
    m0r = jnp.broadcast_to(m0[:, :, None, None], (b, n_h, 1, LANES))
    ghr = g_head.reshape(n_h, 1, dv)

    rowcol = lambda width, off: pl.BlockSpec((t, width), lambda i, h: (row_blk + i, off + h))
    hcol = lambda off: pl.BlockSpec((None, hpad, dk), lambda i, h: (i, 0, off + h))
    in_specs = [
        rowcol(dk, qb), rowcol(dk, qb + n_h), hcol(0), hcol(n_h),
        pl.BlockSpec((conv_w, dk), lambda i, h: (0, h)),
        pl.BlockSpec((conv_w, dk), lambda i, h: (0, n_h + h)),
        rowcol(dv, vb), rowcol(dv, ogb),
        pl.BlockSpec((t, LANES), lambda i, h: (row_blk + i, 0)),
        pl.BlockSpec((t, LANES), lambda i, h: (i, 0)),
        pl.BlockSpec((None, None, nch, chunk), lambda i, h: (i, i_off + h, 0, 0)),
        pl.BlockSpec((None, None, nch, chunk), lambda i, h: (i, f_off + h, 0, 0)),
        pl.BlockSpec((None, None, dv, dk), lambda i, h: (i, h, 0, 0)),
        pl.BlockSpec((None, None, 1, dk), lambda i, h: (i, h, 0, 0)),
        pl.BlockSpec((None, None, 1, LANES), lambda i, h: (i, h, 0, 0)),
        pl.BlockSpec((None, 1, dv), lambda i, h: (h, 0, 0)),
    ]
    in_specs.append(pl.BlockSpec(memory_space=pl.ANY))
    args = [p32, p32, hist, hist, w_conv, w_conv, p16, p32, gates, f_cum, gt, ft, c0, n0r, m0r, ghr, heads]
    out_col = out_col // dv
    out_specs = [
        pl.BlockSpec((t, dv), lambda i, h: (row_blk + i, out_col + h)),
        pl.BlockSpec((None, None, dv, dk), lambda i, h: (i, h, 0, 0)),
        pl.BlockSpec((None, None, 1, dk), lambda i, h: (i, h, 0, 0)),
        pl.BlockSpec((None, None, 1, LANES), lambda i, h: (i, h, 0, 0)),
    ]
    out_shape = [
        jax.ShapeDtypeStruct(heads.shape, heads.dtype),
        jax.ShapeDtypeStruct((b, n_h, dv, dk), F32),
        jax.ShapeDtypeStruct((b, n_h, 1, dk), F32),
        jax.ShapeDtypeStruct((b, n_h, 1, LANES), F32),
    ]
    out, c_new, n_new, m_new = pl.pallas_call(
        functools.partial(_mlstm_kernel, t=t, chunk=chunk, conv_w=conv_w, i_off=i_off, f_off=f_off,
                          k_scale=float(dk) ** -0.5),
        grid=(b, n_h),
        in_specs=in_specs,
        out_specs=out_specs,
        out_shape=out_shape,
        input_output_aliases={len(args) - 1: 0},
        scratch_shapes=[
            pltpu.VMEM((t + hpad, dk), F32), pltpu.VMEM((t + hpad, dk), F32),
            pltpu.VMEM((t, dk), BF16), pltpu.VMEM((t, dk), BF16),
        ],
        compiler_params=_params("parallel", "parallel"),
        name="mlstm_heads",
    )(*args)
    return out, c_new, n_new[:, :, 0, :], m_new[:, :, 0, 0]


def _outproj_kernel(a_ref, w_ref, xa_ref, xb_ref, o_ref, *, n_first):
    acc = jnp.dot(a_ref[...], w_ref[...], preferred_element_type=F32)

    @pl.when(pl.program_id(0) < n_first)
    def _():
        o_ref[...] = xa_ref[...] + acc

    @pl.when(pl.program_id(0) >= n_first)
    def _():
        o_ref[...] = xb_ref[...] + acc


def out_projection(heads, w_out, xa, xb, tm):
    (ma, d), mb = xa.shape, xb.shape[0]
    tn = _tile(d, 512)
    assert ma % tm == 0
    n_first = ma // tm
    first_j = lambda i, j: (jnp.minimum(i, n_first - 1), j)
    second_j = lambda i, j: (jnp.maximum(i - n_first, 0), j)
    return pl.pallas_call(
        functools.partial(_outproj_kernel, n_first=n_first),
        grid=(pl.cdiv(ma + mb, tm), d // tn),
        in_specs=[
            pl.BlockSpec((tm, d), lambda i, j: (i, 0)),
            pl.BlockSpec((d, tn), lambda i, j: (0, j)),
            pl.BlockSpec((tm, tn), first_j), pl.BlockSpec((tm, tn), second_j),
        ],
        out_specs=pl.BlockSpec((tm, tn), lambda i, j: (i, j)),
        out_shape=jax.ShapeDtypeStruct((ma + mb, d), F32),
        compiler_params=_params("arbitrary", "arbitrary"),
        name="out_projection",
    )(heads, w_out, xa, xb)


def _pack_bf16_pairs(x):
    half = x.shape[1] // 2
    bits = lax.bitcast_convert_type(x.astype(BF16).astype(F32), jnp.uint32)
    return (bits[:, :half] >> 16) | (bits[:, half:] & jnp.uint32(0xFFFF0000))


def _unpack_pairs_f32(w):
    lo = lax.bitcast_convert_type(w << 16, F32)
    hi = lax.bitcast_convert_type(w & jnp.uint32(0xFFFF0000), F32)
    return lo, hi


def _unpack_bf16_pairs(w):
    lo, hi = _unpack_pairs_f32(w)
    return lo.astype(BF16), hi.astype(BF16)


def _router_kernel(x_ref, g_ref, whi_ref, wlo_ref, b_ref, xn_ref, eid_ref, gate_ref, *, n_groups, per_group):
    x = x_ref[...]
    xn = x * lax.rsqrt(jnp.mean(x * x, axis=-1, keepdims=True) + EPS) * g_ref[...]
    xn_ref[...] = _pack_bf16_pairs(xn)
    x_hi = xn.astype(BF16)
    x_lo = (xn - x_hi.astype(F32)).astype(BF16)
    logits = (jnp.dot(x_hi, whi_ref[...], preferred_element_type=F32)
              + jnp.dot(x_lo, whi_ref[...], preferred_element_type=F32)
              + jnp.dot(x_hi, wlo_ref[...], preferred_element_type=F32)) + b_ref[...]
    lane = lax.broadcasted_iota(jnp.int32, logits.shape, 1).astype(F32)
    big = float(LANES)
    is_g = lane < n_groups
    gl = jnp.where(is_g, logits, NEG_INF)
    gmax = jnp.max(gl, axis=1, keepdims=True)
    g_idx = jnp.min(jnp.where(gl == gmax, lane, big), axis=1, keepdims=True)
    g_prob = 1.0 / jnp.sum(jnp.where(is_g, jnp.exp(logits - gmax), 0.0), axis=1, keepdims=True)
    lo = n_groups + per_group * g_idx
    el = jnp.where((lane >= lo) & (lane < lo + per_group), logits, NEG_INF)
    e1 = jnp.max(el, axis=1, keepdims=True)
    i1 = jnp.min(jnp.where(el == e1, lane, big), axis=1, keepdims=True)
    el2 = jnp.where(lane == i1, NEG_INF, el)
    e2 = jnp.max(el2, axis=1, keepdims=True)
    i2 = jnp.min(jnp.where(el2 == e2, lane, big), axis=1, keepdims=True)
    r = jnp.exp(e2 - e1)
    w1 = g_prob / (1.0 + r)
    w2 = g_prob * r / (1.0 + r)
    eid = jnp.where(lane == 0.0, i1 - n_groups, jnp.where(lane == 1.0, i2 - n_groups, 0.0))
    eid_ref[...] = eid.astype(jnp.int32)
    gate_ref[...] = jnp.where(lane == 0.0, w1, jnp.where(lane == 1.0, w2, 0.0))


def router(x, g, w_router, b_router, n_groups, per_group):
    m, d = x.shape
    tm = _tile(m, 256)
    w_hi = w_router.astype(BF16)
    w_lo = (w_router - w_hi.astype(F32)).astype(BF16)
    return pl.pallas_call(
        functools.partial(_router_kernel, n_groups=n_groups, per_group=per_group),
        grid=(m // tm,),
        in_specs=[
            pl.BlockSpec((tm, d), lambda i: (i, 0)),
            pl.BlockSpec((1, d), lambda i: (0, 0)),
            pl.BlockSpec((d, LANES), lambda i: (0, 0)),
            pl.BlockSpec((d, LANES), lambda i: (0, 0)),
            pl.BlockSpec((1, LANES), lambda i: (0, 0)),
        ],
        out_specs=[
            pl.BlockSpec((tm, d // 2), lambda i: (i, 0)),
            pl.BlockSpec((tm, LANES), lambda i: (i, 0)),
            pl.BlockSpec((tm, LANES), lambda i: (i, 0)),
        ],
        out_shape=[
            jax.ShapeDtypeStruct((m, d // 2), jnp.uint32),
            jax.ShapeDtypeStruct((m, LANES), jnp.int32),
            jax.ShapeDtypeStruct((m, LANES), F32),
        ],
        compiler_params=_params("parallel"),
        name="router",
    )(x, g.reshape(1, d), w_hi, w_lo, b_router)


def _gather_kernel(nrows_ref, idx0_ref, idx1_ref, src_ref, o_ref, buf_ref, sem_ref, *, tm):
    step = pl.program_id(0)
    nsteps = pl.num_programs(0)
    used = lambda s: s * tm < nrows_ref[0]

    def issue(idx_ref, slot):
        def body(r, carry):
            for q, row in enumerate((r, r + tm // 2)):
                tok = idx_ref[0, row]
                pltpu.make_async_copy(src_ref.at[pl.ds(tok, 1), :], buf_ref.at[slot, pl.ds(row, 1), :],
                                      sem_ref.at[slot]).start(priority=q)
            return carry
        lax.fori_loop(0, tm // 2, body, 0, unroll=DMA_ISSUE_UNROLL // 2)

    @pl.when(step == 0)
    def _():
        issue(idx0_ref, 0)

    @pl.when((step + 1 < nsteps) & used(step + 1))
    def _():
        issue(idx1_ref, (step + 1) % 2)

    @pl.when(used(step))
    def _():
        slot = step % 2
        pltpu.make_async_copy(src_ref.at[pl.ds(0, tm), :], buf_ref.at[slot], sem_ref.at[slot]).wait()
        half = buf_ref.shape[2]
        lo, hi = _unpack_bf16_pairs(buf_ref[slot])
        o_ref[:, :half] = lo
        o_ref[:, half:] = hi

    @pl.when(jnp.logical_not(used(step)))
    def _():
        o_ref[...] = jnp.zeros_like(o_ref)


def _smem_tiles(n_steps, width):
    cur = pl.BlockSpec((None, 1, width), lambda i: (i, 0, 0), memory_space=pltpu.SMEM)
    nxt = pl.BlockSpec((None, 1, width), lambda i: (jnp.minimum(i + 1, n_steps - 1), 0, 0),
                       memory_space=pltpu.SMEM)
    return cur, nxt


def gather_rows(src, idx, n_rows, tm):
    n, half = src.shape
    d = 2 * half
    a = idx.shape[0]
    n_steps = a // tm
    cur, nxt = _smem_tiles(n_steps, tm)
    idx3 = idx.reshape(n_steps, 1, tm)
    return pl.pallas_call(
        functools.partial(_gather_kernel, tm=tm),
        grid=(n_steps,),
        in_specs=[pl.BlockSpec(memory_space=pltpu.SMEM), cur, nxt, pl.BlockSpec(memory_space=pl.ANY)],
        out_specs=pl.BlockSpec((tm, d), lambda i: (i, 0)),
        scratch_shapes=[pltpu.VMEM((2, tm, half), src.dtype), pltpu.SemaphoreType.DMA((2,))],
        out_shape=jax.ShapeDtypeStruct((a, d), BF16),
        compiler_params=_params("arbitrary"),
        name="gather_rows",
    )(n_rows, idx3, idx3, src)


def _expert_up_kernel(te_ref, nv_ref, x_ref, wg_ref, wu_ref, h_ref):
    t = pl.program_id(1)

    @pl.when(t < nv_ref[0])
    def _():
        x = x_ref[...]
        g = jnp.dot(x, wg_ref[...].astype(BF16), preferred_element_type=F32)
        u = jnp.dot(x, wu_ref[...].astype(BF16), preferred_element_type=F32)
        h_ref[...] = (g * jax.nn.sigmoid(g) * u).astype(h_ref.dtype)

    @pl.when(t >= nv_ref[0])
    def _():
        h_ref[...] = jnp.zeros_like(h_ref)


def expert_up(xg, w_gate, w_up, tile_expert, n_valid, tm):
    a, d = xg.shape
    _, _, f = w_gate.shape
    tn = _tile(f, 512)
    return pl.pallas_call(
        _expert_up_kernel,
        grid_spec=pltpu.PrefetchScalarGridSpec(
            num_scalar_prefetch=2,
            grid=(f // tn, a // tm),
            in_specs=[
                pl.BlockSpec((tm, d), lambda c, t, te, nv: (t, 0)),
                pl.BlockSpec((None, d, tn), lambda c, t, te, nv: (te[t], 0, c)),
                pl.BlockSpec((None, d, tn), lambda c, t, te, nv: (te[t], 0, c)),
            ],
            out_specs=pl.BlockSpec((tm, tn), lambda c, t, te, nv: (t, c)),
        ),
        out_shape=jax.ShapeDtypeStruct((a, f), BF16),
        compiler_params=_params("arbitrary", "arbitrary"),
        name="expert_up",
    )(tile_expert, n_valid, xg, w_gate, w_up)


def _expert_down_kernel(te_ref, nv_ref, h_ref, wd_ref, y_ref):
    t = pl.program_id(1)

    @pl.when(t < nv_ref[0])
    def _():
        y = jnp.dot(h_ref[...], wd_ref[...].astype(BF16), preferred_element_type=F32)
        y_ref[...] = _pack_bf16_pairs(y)

    @pl.when(t >= nv_ref[0])
    def _():
        y_ref[...] = jnp.zeros_like(y_ref)


def expert_down(hg, w_down, tile_expert, n_valid, tm):
    a, f = hg.shape
    _, _, d = w_down.shape
    tn = _tile(d, DOWN_CHUNK)
    return pl.pallas_call(
        _expert_down_kernel,
        grid_spec=pltpu.PrefetchScalarGridSpec(
            num_scalar_prefetch=2,
            grid=(d // tn, a // tm),
            in_specs=[
                pl.BlockSpec((tm, f), lambda c, t, te, nv: (t, 0)),
                pl.BlockSpec((None, f, tn), lambda c, t, te, nv: (te[t], 0, c)),
            ],
            out_specs=pl.BlockSpec((tm, tn // 2), lambda c, t, te, nv: (t, c)),
        ),
        out_shape=jax.ShapeDtypeStruct((a, d // 2), jnp.uint32),
        compiler_params=_params("arbitrary", "arbitrary"),
        name="expert_down",
    )(tile_expert, n_valid, hg, w_down)


def _combine_kernel(pos0_ref, pos1_ref, x_ref, gate_ref, g_ref, y_ref, oa_ref, ob_ref, buf_ref, sem_ref,
                    xs_ref, *, tm, n_first, chunk):
    step = pl.program_id(0)
    nsteps = pl.num_programs(0)

    def issue(pos_ref, slot):
        def body(r, carry):
            for k in range(TOP_K):
                p = pos_ref[0, r * TOP_K + k]
                pltpu.make_async_copy(y_ref.at[pl.ds(p, 1), :], buf_ref.at[slot, k, pl.ds(r, 1), :],
                                      sem_ref.at[slot]).start(priority=k)
            return carry
        lax.fori_loop(0, tm, body, 0, unroll=DMA_ISSUE_UNROLL)

    @pl.when(step == 0)
    def _():
        issue(pos0_ref, 0)

    @pl.when(step + 1 < nsteps)
    def _():
        issue(pos1_ref, (step + 1) % 2)

    slot = step % 2
    for k in range(TOP_K):
        pltpu.make_async_copy(y_ref.at[pl.ds(0, tm), :], buf_ref.at[slot, k], sem_ref.at[slot]).wait()
    gates = gate_ref[...]
    d = x_ref.shape[1]
    half_chunk = chunk // 2
    cw = min(COMBINE_COLS, half_chunk)
    ssq = jnp.zeros((tm, 1), F32)
    for pc in range(0, d // 2, cw):
        real_lo = (pc // half_chunk) * chunk + pc % half_chunk
        real_hi = real_lo + half_chunk
        x_lo = x_ref[:, real_lo:real_lo + cw]
        x_hi = x_ref[:, real_hi:real_hi + cw]
        for k in range(TOP_K):
            lo, hi = _unpack_pairs_f32(buf_ref[slot, k, :, pc:pc + cw])
            x_lo = x_lo + gates[:, k:k + 1] * lo
            x_hi = x_hi + gates[:, k:k + 1] * hi
        xs_ref[:, real_lo:real_lo + cw] = x_lo
        xs_ref[:, real_hi:real_hi + cw] = x_hi
        ssq = ssq + jnp.sum(x_lo * x_lo, axis=1, keepdims=True) + jnp.sum(x_hi * x_hi, axis=1, keepdims=True)
    inv = lax.rsqrt(ssq * (1.0 / d) + EPS)

    def emit(o_ref):
        for c in range(0, d, 2 * cw):
            o_ref[:, c:c + 2 * cw] = xs_ref[:, c:c + 2 * cw] * inv * g_ref[:, c:c + 2 * cw]

    @pl.when(step < n_first)
    def _():
        emit(oa_ref)

    @pl.when(step >= n_first)
    def _():
        emit(ob_ref)


def combine_norm(x, gates, pos, yg, g_final, m_first, tm):
    m, d = x.shape
    n_steps = m // tm
    n_first = m_first // tm
    cur, nxt = _smem_tiles(n_steps, tm * TOP_K)
    pos3 = pos.reshape(n_steps, 1, tm * TOP_K)
    first, second = _split_maps(n_first)
    return pl.pallas_call(
        functools.partial(_combine_kernel, tm=tm, n_first=n_first, chunk=_tile(d, DOWN_CHUNK)),
        grid=(n_steps,),
        in_specs=[
            cur, nxt,
            pl.BlockSpec((tm, d), lambda i: (i, 0)),
            pl.BlockSpec((tm, LANES), lambda i: (i, 0)),
            pl.BlockSpec((1, d), lambda i: (0, 0)),
            pl.BlockSpec(memory_space=pl.ANY),
        ],
        out_specs=[pl.BlockSpec((tm, d), first), pl.BlockSpec((tm, d), second)],
        scratch_shapes=[pltpu.VMEM((2, TOP_K, tm, d // 2), jnp.uint32), pltpu.SemaphoreType.DMA((2,)),
                        pltpu.VMEM((tm, d), F32)],
        out_shape=[jax.ShapeDtypeStruct((m_first, d), F32), jax.ShapeDtypeStruct((m - m_first, d), F32)],
        compiler_params=_params("arbitrary"),
        name="combine_norm",
    )(pos3, pos3, x, gates, g_final.reshape(1, d), yg)


def _dispatch_plan(eid, n_experts, tm):
    n = eid.shape[0]
    a = n * TOP_K
    e_flat = eid.reshape(a)
    onehot = (e_flat[:, None] == jnp.arange(n_experts, dtype=jnp.int32)[None, :]).astype(jnp.int32)
    csum = jnp.cumsum(onehot, axis=0)
    counts = csum[-1]
    rank = jnp.take_along_axis(csum, e_flat[:, None], axis=1)[:, 0] - 1
    padded = ((counts + tm - 1) // tm) * tm
    pad_end = jnp.cumsum(padded)
    pos = (pad_end - padded)[e_flat] + rank
    n_tiles = a // tm + n_experts
    src = (jnp.arange(n_tiles * tm, dtype=jnp.int32) % n).at[pos].set(jnp.arange(a, dtype=jnp.int32) // TOP_K)
    tile_end = pad_end // tm
    tile_ids = jnp.arange(n_tiles, dtype=jnp.int32)
    tile_expert = jnp.sum((tile_ids[:, None] >= tile_end[None, :]).astype(jnp.int32), axis=1)
    tile_expert = jnp.minimum(tile_expert, n_experts - 1)
    n_valid = tile_end[-1:].astype(jnp.int32)
    return pos.astype(jnp.int32), src, tile_expert, n_valid


def moe_layer(x, m_first, g_ffn, w_router, b_router, w_gate, w_up, w_down, g_final, n_groups, per_group):
    m, d = x.shape
    n_experts = n_groups * per_group
    xn, eid, gates = router(x, g_ffn, w_router, b_router, n_groups, per_group)
    a = m * TOP_K
    tm = _tile(a, min(512, max(64, a // n_experts)))
    pos, src, tile_expert, n_valid = _dispatch_plan(eid[:, :TOP_K], n_experts, tm)
    xg = gather_rows(xn, src, n_valid * tm, _tile(tm, 256))
    hg = expert_up(xg, w_gate, w_up, tile_expert, n_valid, tm)
    yg = expert_down(hg, w_down, tile_expert, n_valid, tm)
    return combine_norm(x, gates, pos, yg, g_final, m_first, _tile(math.gcd(m_first, m - m_first), 128))


def kernel(x_prompt, x_sample, cache_fox_k, cache_fox_v, cache_fox_logf, state_mlstm_C, state_mlstm_n, state_mlstm_m, state_mlstm_conv, g_norm_mix, w_in, b_fox_f, b_mlstm_i, b_mlstm_f, w_conv, g_mlstm_head, w_out, g_norm_ffn, w_router_group, b_router_group, w_router_expert, b_router_expert, w_exp_gate, w_exp_up, w_exp_down, g_norm_final):
    depth = w_in.shape[0]
    assert depth == 1, "the final norm is fused into the last layer's MoE combine; one layer supported"
    bp, tp, d = x_prompt.shape
    bs, ts, _ = x_sample.shape
    n_fox, dh = cache_fox_k.shape[-2:]
    n_ml, dv, dk = state_mlstm_C.shape[-3:]
    past = cache_fox_k.shape[2]
    n_groups = w_router_group.shape[-1]
    n_experts = w_router_expert.shape[-1]
    conv_w = w_conv.shape[1]
    half = d // 2
    mp, ms = bp * tp, bs * ts
    assert n_fox * dh == half and n_ml * dv == half and 2 * n_ml * dk == half
    assert n_fox + 2 * n_ml <= LANES and n_groups + n_experts <= LANES
    assert mp % ts == 0 and ts >= conv_w - 1, "sample sequences are addressed as row blocks after the prompt rows"
    l = 0
    sizes = (half, half, half, n_fox, half, half, half, n_ml, n_ml)
    offs = [0]
    for s in sizes:
        offs.append(offs[-1] + s)
    col = lambda i: w_in[l][:, offs[i]:offs[i + 1]]
    w_a = w_in[l][:, offs[0]:offs[3]].astype(BF16)
    w_b = w_in[l][:, offs[4]:offs[7]].astype(BF16)
    n_gate = n_fox + 2 * n_ml
    w_gatecols = jnp.pad(jnp.concatenate([col(3), col(7), col(8)], axis=1), ((0, 0), (0, LANES - n_gate))).astype(BF16)
    b_gatecols = jnp.pad(jnp.concatenate([b_fox_f[l], b_mlstm_i[l], b_mlstm_f[l]]).astype(F32),
                         (0, LANES - n_gate)).reshape(1, LANES)
    n_r = n_groups + n_experts
    w_router = jnp.pad(jnp.concatenate([w_router_group[l], w_router_expert[l]], axis=1), ((0, 0), (0, LANES - n_r)))
    b_router = jnp.pad(jnp.concatenate([b_router_group[l], b_router_expert[l]]).astype(F32),
                       (0, LANES - n_r)).reshape(1, LANES)
    i_off, f_off = n_fox, n_fox + n_ml
    col32 = {s: k * half for k, s in enumerate(P32_SEGS)}
    col16 = {s: k * half for k, s in enumerate(P16_SEGS)}

    xp2, xs2 = x_prompt.reshape(mp, d), x_sample.reshape(ms, d)
    tm_rows = _tile(mp, 1024)
    xn = rmsnorm_cast(xp2, xs2, g_norm_mix[l], BF16)
    tm_proj = next((tmc for tmc in range(1280, 767, -16) if (mp + ms) % tmc == 0), tm_rows)
    p32, p16, gates = in_projection(xn, w_a, w_b, w_gatecols, b_gatecols, n_fox, n_ml, tm_proj,
                                    float(dh) ** -0.5 * LOG2E)

    f_p = cumsum_time(gates, bp, tp)
    f_p3 = f_p.reshape(bp, tp, LANES)
    g_p3 = gates[:mp].reshape(bp, tp, LANES)
    f_row = lambda f3: jnp.swapaxes(f3[:, :, :n_fox], 1, 2)[:, :, None, :]
    heads = fox_attention(p16, col16[SEG_FQ], p32, col32[SEG_FK], col32[SEG_FV], 0, bp, tp,
                          f_row(f_p3), n_fox, dh, xn)
    hist_p = jnp.zeros((bp, conv_w - 1, half), F32)
    heads, c_p, n_p, m_p = mlstm_heads(
        p32, col32[SEG_QK], col32[SEG_MO], p16, col16[SEG_MV], gates, 0, bp, tp, f_p,
        jnp.swapaxes(g_p3, 1, 2), jnp.swapaxes(f_p3, 1, 2), hist_p, w_conv[l],
        jnp.zeros((bp, n_ml, dv, dk), F32), jnp.zeros((bp, n_ml, dk), F32), jnp.zeros((bp, n_ml), F32),
        g_mlstm_head[l], i_off, f_off, heads, half)

    g_s3 = gates[mp:].reshape(bs, ts, LANES)
    cache_pad = jnp.pad(cache_fox_logf[l].astype(F32), ((0, 0), (0, 0), (0, LANES - n_fox)))
    logf_s = jnp.concatenate([cache_pad, g_s3], axis=1)
    f_s3 = cumsum_time(logf_s.reshape(bs * (past + ts), LANES), bs, past + ts).reshape(bs, past + ts, LANES)
    heads = fox_attention(p16, col16[SEG_FQ], p32, col32[SEG_FK], col32[SEG_FV], mp // ts, bs, ts,
                          f_row(f_s3), n_fox, dh, heads,
                          cache_fox_k[l].reshape(bs, past, half), cache_fox_v[l].reshape(bs, past, half))
    fn_s = cumsum_time(g_s3.reshape(ms, LANES), bs, ts)
    heads, c_s, n_s, m_s = mlstm_heads(
        p32, col32[SEG_QK], col32[SEG_MO], p16, col16[SEG_MV], gates, mp // ts, bs, ts, fn_s,
        jnp.swapaxes(g_s3, 1, 2), jnp.swapaxes(fn_s.reshape(bs, ts, LANES), 1, 2), state_mlstm_conv[l], w_conv[l],
        state_mlstm_C[l], state_mlstm_n[l], state_mlstm_m[l], g_mlstm_head[l], i_off, f_off, heads, half)

    x1 = out_projection(heads, w_out[l].astype(BF16), xp2, xs2, tm_rows)
    y_p, y_s = moe_layer(x1, mp, g_norm_ffn[l], w_router, b_router, w_exp_gate[l], w_exp_up[l], w_exp_down[l],
                         g_norm_final, n_groups, n_experts // n_groups)

    def new_state(rows, b, t, g3, c_new, n_new, m_new, conv_hist):
        blk = p32[rows]
        fk = blk[:, col32[SEG_FK]:col32[SEG_FK] + half].reshape(b, t, n_fox, dh)
        fv = blk[:, col32[SEG_FV]:col32[SEG_FV] + half].reshape(b, t, n_fox, dh)
        qk_tail = blk[:, col32[SEG_QK]:col32[SEG_QK] + half].reshape(b, t, half)[:, t - min(t, conv_w - 1):]
        qk_hist = jnp.concatenate([conv_hist.astype(F32), qk_tail], axis=1)[:, -(conv_w - 1):]
        return tuple(a[None] for a in (fk, fv, g3[:, :, :n_fox], c_new, n_new, m_new, qk_hist))

    st_p = new_state(slice(0, mp), bp, tp, g_p3, c_p, n_p, m_p, hist_p)
    st_s = new_state(slice(mp, mp + ms), bs, ts, g_s3, c_s, n_s, m_s, state_mlstm_conv[l])
    return (y_p.reshape(bp, tp, d), y_s.reshape(bs, ts, d)) + st_p + st_s
```

```python
import functools
import math

import jax
import jax.numpy as jnp
from jax import lax
from jax.experimental import pallas as pl
from jax.experimental.pallas import tpu as pltpu

F32 = jnp.float32
BF16 = jnp.bfloat16
EPS = 1e-6
LANES = 128
V7X_VMEM_LIMIT = 56 * 1024 * 1024
TOP_K = 2
MLSTM_CHUNK = 256
DMA_ISSUE_UNROLL = 8
FOX_LOOKAHEAD = 4
COMBINE_COLS = 256
DOWN_CHUNK = 2048
HIGHEST = lax.Precision.HIGHEST
NEG_INF = float("-inf")
LOG2E = math.log2(math.e)


def _params(*sem):
    return pltpu.CompilerParams(dimension_semantics=sem, vmem_limit_bytes=V7X_VMEM_LIMIT)


def _tile(n, pref):
    if n <= pref:
        return n
    for t in range(pref - pref % 8, 7, -8):
        if n % t == 0:
            return t
    return n


def _log_sigmoid(x):
    return jnp.minimum(x, 0.0) - jnp.log1p(jnp.exp(-jnp.abs(x)))


def _split_maps(n_first):
    first = lambda i, *_: (jnp.minimum(i, n_first - 1), 0)
    second = lambda i, *_: (jnp.maximum(i - n_first, 0), 0)
    return first, second


def _rmsnorm_kernel(xa_ref, xb_ref, g_ref, o_ref, *, n_first):
    def emit(x_ref):
        x = x_ref[...]
        y = x * lax.rsqrt(jnp.mean(x * x, axis=-1, keepdims=True) + EPS)
        o_ref[...] = (y * g_ref[...]).astype(o_ref.dtype)

    @pl.when(pl.program_id(0) < n_first)
    def _():
        emit(xa_ref)

    @pl.when(pl.program_id(0) >= n_first)
    def _():
        emit(xb_ref)


def rmsnorm_cast(xa, xb, g, out_dtype):
    (ma, d), mb = xa.shape, xb.shape[0]
    tm = _tile(math.gcd(ma, mb), 256)
    first, second = _split_maps(ma // tm)
    return pl.pallas_call(
        functools.partial(_rmsnorm_kernel, n_first=ma // tm),
        grid=((ma + mb) // tm,),
        in_specs=[pl.BlockSpec((tm, d), first), pl.BlockSpec((tm, d), second),
                  pl.BlockSpec((1, d), lambda i: (0, 0))],
        out_specs=pl.BlockSpec((tm, d), lambda i: (i, 0)),
        out_shape=jax.ShapeDtypeStruct((ma + mb, d), out_dtype),
        compiler_params=_params("arbitrary"),
        name="rmsnorm_cast",
    )(xa, xb, g.reshape(1, d))


SEG_FQ, SEG_FK, SEG_FV, SEG_QK, SEG_MV, SEG_MO = range(6)
P32_SEGS = (SEG_FK, SEG_FV, SEG_QK, SEG_MO)
P16_SEGS = (SEG_FQ, SEG_MV)


def _inproj_kernel(x_ref, wa_ref, wb_ref, ws_ref, b_ref, p32_ref, p16_ref, gate_ref, *, nseg, n_fox, n_ml,
                   q_scale):
    j = pl.program_id(1)
    seg = j // nseg

    def emit(w_ref):
        acc = jnp.dot(x_ref[...], w_ref[...], preferred_element_type=F32)
        is16 = (seg == SEG_FQ) | (seg == SEG_MV)

        @pl.when(is16)
        def _():
            p16_ref[...] = (acc * jnp.where(seg == SEG_FQ, q_scale, 1.0)).astype(p16_ref.dtype)

        @pl.when(jnp.logical_not(is16))
        def _():
            p32_ref[...] = acc

    @pl.when(seg < 3)
    def _():
        emit(wa_ref)

    @pl.when(seg >= 3)
    def _():
        emit(wb_ref)

    @pl.when(j == 0)
    def _():
        p = jnp.dot(x_ref[...], ws_ref[...], preferred_element_type=F32) + b_ref[...]
        lane = lax.broadcasted_iota(jnp.int32, p.shape, 1)
        is_id = (lane >= n_fox) & (lane < n_fox + n_ml)
        gate_ref[...] = jnp.where(is_id, p, _log_sigmoid(p))


def _held_block(j, nseg, segs):
    seg = j // nseg
    blk = jnp.int32(0)
    for k, s in enumerate(segs):
        here = k * nseg + (j - s * nseg)
        done = (k + 1) * nseg - 1
        blk = jnp.where(seg == s, here, jnp.where(seg > s, done, blk))
    return blk


def in_projection(xn, w_a, w_b, w_gate, b_gate, n_fox, n_ml, tm, q_scale):
    m, d = xn.shape
    w = w_a.shape[1] // 3
    tn = _tile(w, 512)
    nseg = w // tn
    return pl.pallas_call(
        functools.partial(_inproj_kernel, nseg=nseg, n_fox=n_fox, n_ml=n_ml, q_scale=q_scale),
        grid=(pl.cdiv(m, tm), 6 * nseg),
        in_specs=[
            pl.BlockSpec((tm, d), lambda i, j: (i, 0)),
            pl.BlockSpec((d, tn), lambda i, j: (0, jnp.minimum(j, 3 * nseg - 1))),
            pl.BlockSpec((d, tn), lambda i, j: (0, jnp.maximum(j - 3 * nseg, 0))),
            pl.BlockSpec((d, LANES), lambda i, j: (0, 0)),
            pl.BlockSpec((1, LANES), lambda i, j: (0, 0)),
        ],
        out_specs=[
            pl.BlockSpec((tm, tn), lambda i, j: (i, _held_block(j, nseg, P32_SEGS))),
            pl.BlockSpec((tm, tn), lambda i, j: (i, _held_block(j, nseg, P16_SEGS))),
            pl.BlockSpec((tm, LANES), lambda i, j: (i, 0)),
        ],
        out_shape=[
            jax.ShapeDtypeStruct((m, len(P32_SEGS) * w), F32),
            jax.ShapeDtypeStruct((m, len(P16_SEGS) * w), BF16),
            jax.ShapeDtypeStruct((m, LANES), F32),
        ],
        compiler_params=_params("arbitrary", "arbitrary"),
        name="in_projection",
    )(xn, w_a, w_b, w_gate, b_gate)


def _cumsum_kernel(g_ref, f_ref, *, chunk):
    s = g_ref.shape[0]
    r = lax.broadcasted_iota(jnp.int32, (chunk, chunk), 0)
    c = lax.broadcasted_iota(jnp.int32, (chunk, chunk), 1)
    tri = (c <= r).astype(F32)
    carry = jnp.zeros((1, g_ref.shape[1]), F32)
    for k in range(s // chunk):
        blk = g_ref[k * chunk:(k + 1) * chunk, :]
        loc = jnp.dot(tri, blk, precision=HIGHEST, preferred_element_type=F32)
        f_ref[k * chunk:(k + 1) * chunk, :] = loc + carry
        carry = carry + loc[chunk - 1:chunk, :]


def cumsum_time(g, n_seq, s):
    n = g.shape[1]
    chunk = next(c for c in (256, 128, 64, 32, 16, 8) if s % c == 0)
    return pl.pallas_call(
        functools.partial(_cumsum_kernel, chunk=chunk),
        grid=(n_seq,),
        in_specs=[pl.BlockSpec((s, n), lambda i: (i, 0))],
        out_specs=pl.BlockSpec((s, n), lambda i: (i, 0)),
        out_shape=jax.ShapeDtypeStruct((n_seq * s, n), F32),
        compiler_params=_params("parallel"),
        name="cumsum_time",
    )(g)


def _fox_kernel(*refs, past, t, tq, dh):
    o_all = refs[-1]
    if past:
        q_all, k_all, v_all, pk_all, pv_all, frow_all = refs[:-2]
    else:
        q_all, k_all, v_all, frow_all = refs[:-2]
    for g in range(q_all.shape[1] // dh):
        cols = slice(g * dh, (g + 1) * dh)
        one_head = functools.partial(_fox_head, past=past, t=t, tq=tq, dh=dh)
        if past:
            one_head(q_all.at[:, cols], k_all.at[:, cols], v_all.at[:, cols], pk_all.at[:, g, :],
                     pv_all.at[:, g, :], frow_all.at[g], o_all.at[:, cols])
        else:
            one_head(q_all.at[:, cols], k_all.at[:, cols], v_all.at[:, cols], None, None,
                     frow_all.at[g], o_all.at[:, cols])


def _fox_head(q_ref, k_ref, v_ref, pk_ref, pv_ref, frow_ref, o_ref, *, past, t, tq, dh):
    fk_all = frow_ref[...] * LOG2E
    lane = lax.broadcasted_iota(jnp.int32, (1, dh), 1)
    ones_col = jnp.where(lane == 0, 1.0, 0.0).astype(BF16)

    den_from_matmul = not past

    def with_ones(v):
        if not den_from_matmul:
            return v.astype(BF16)
        return jnp.concatenate([v.astype(BF16), jnp.broadcast_to(ones_col, v.shape)], axis=1)

    kb = k_ref[...].astype(BF16)
    vb = with_ones(v_ref[...])
    dn_t = (((1,), (1,)), ((), ()))
    row = lax.broadcasted_iota(jnp.int32, (tq, tq), 0)
    col = lax.broadcasted_iota(jnp.int32, (tq, tq), 1)
    diag_mask = col <= row

    def scores(qi):
        q = q_ref[qi * tq:(qi + 1) * tq, :]
        lo = qi * tq
        parts = []
        if past:
            parts.append((pk_ref[...].astype(BF16), with_ones(pv_ref[...]), fk_all[:, 0:past], False))
        if qi:
            parts.append((kb[0:lo, :], vb[0:lo, :], fk_all[:, past:past + lo], False))
        parts.append((kb[lo:lo + tq, :], vb[lo:lo + tq, :], fk_all[:, past + lo:past + lo + tq], True))
        logits = []
        for kp, _, fk, masked in parts:
            lg = lax.dot_general(q, kp, dn_t, preferred_element_type=F32) - fk
            logits.append(jnp.where(diag_mask, lg, NEG_INF) if masked else lg)
        mx = functools.reduce(jnp.maximum, [jnp.max(lg, axis=1, keepdims=True) for lg in logits])
        return [vp for _, vp, _, _ in parts], logits, mx

    def finish(qi, values, logits, mx):
        acc = 0.0
        den = 0.0
        for vp, lg in zip(values, logits):
            p = jnp.exp2(lg - mx)
            if not den_from_matmul:
                den = den + jnp.sum(p, axis=1, keepdims=True)
            acc = acc + jnp.dot(p.astype(BF16), vp, preferred_element_type=F32)
        if den_from_matmul:
            acc, den = acc[:, :dh], acc[:, dh:dh + 1]
        o_ref[qi * tq:(qi + 1) * tq, :] = (acc / den).astype(o_ref.dtype)

    n_tiles = t // tq
    queue = [scores(qi) for qi in range(min(FOX_LOOKAHEAD, n_tiles))]
    for qi in range(n_tiles):
        if qi + FOX_LOOKAHEAD < n_tiles:
            queue.append(scores(qi + FOX_LOOKAHEAD))
        finish(qi, *queue.pop(0))


def fox_attention(p16, q_col, p32, k_col, v_col, row_blk, n_seq, t, f_row, n_heads, dh, heads,
                  past_k=None, past_v=None):
    past = 0 if past_k is None else past_k.shape[1]
    tq = _tile(t, 256)
    grp = n_heads if past else 1
    width = grp * dh

    def head_blk(off):
        return pl.BlockSpec((t, width), lambda i, h: (row_blk + i, off // width + h))

    in_specs = [head_blk(q_col), head_blk(k_col), head_blk(v_col)]
    args = [p16, p32, p32]
    if past:
        past_blk = pl.BlockSpec((None, past, n_heads, dh), lambda i, h: (i, 0, 0, 0))
        in_specs += [past_blk, past_blk]
        args += [past_k, past_v]
    in_specs += [
        pl.BlockSpec((None, grp, 1, past + t), lambda i, h: (i, h, 0, 0)),
        pl.BlockSpec(memory_space=pl.ANY),
    ]
    args += [f_row, heads]
    return pl.pallas_call(
        functools.partial(_fox_kernel, past=past, t=t, tq=tq, dh=dh),
        grid=(n_seq, n_heads // grp),
        in_specs=in_specs,
        out_specs=head_blk(0),
        out_shape=jax.ShapeDtypeStruct(heads.shape, heads.dtype),
        input_output_aliases={len(args) - 1: 0},
        compiler_params=_params("parallel", "parallel"),
        name="fox_attention",
    )(*args)


def _mlstm_kernel(*refs, t, chunk, conv_w, i_off, f_off, k_scale):
    refs = refs[:16] + refs[17:]
    (qraw_ref, kraw_ref, hq_ref, hk_ref, wq_ref, wk_ref, v_ref, og_ref, g_ref, f_ref,
     irow_ref, frow_ref, c0_ref, n0_ref, m0_ref, gh_ref,
     out_ref, c_ref, n_ref, m_ref,
     histq_ref, histk_ref, qc_ref, kc_ref) = refs
    h = pl.program_id(1)
    hpad = hq_ref.shape[0]

    def conv_silu(raw_ref, hist_in_ref, w_ref, hist_ref, scale, dst_ref):
        hist_ref[0:hpad, :] = hist_in_ref[...]
        hist_ref[hpad:hpad + t, :] = raw_ref[...]
        y = None
        for j in range(conv_w):
            start = hpad - (conv_w - 1) + j
            term = hist_ref[start:start + t, :] * w_ref[j:j + 1, :]
            y = term if y is None else y + term
        y = y * jax.nn.sigmoid(y)
        if scale != 1.0:
            y = y * scale
        dst_ref[...] = y.astype(dst_ref.dtype)

    conv_silu(qraw_ref, hq_ref, wq_ref, histq_ref, 1.0, qc_ref)
    conv_silu(kraw_ref, hk_ref, wk_ref, histk_ref, k_scale, kc_ref)

    gh = gh_ref[...]
    rr = lax.broadcasted_iota(jnp.int32, (chunk, chunk), 0)
    cc = lax.broadcasted_iota(jnp.int32, (chunk, chunk), 1)
    causal = cc <= rr
    lane = lax.broadcasted_iota(jnp.int32, (chunk, LANES), 1)

    chunks = range(t // chunk)
    rows = [slice(c * chunk, (c + 1) * chunk) for c in chunks]
    dn_t = (((1,), (1,)), ((), ()))

    f_col = [jnp.sum(jnp.where(lane == f_off + h, f_ref[rows[c], :], 0.0), axis=1, keepdims=True)
             for c in chunks]
    i_col = [jnp.sum(jnp.where(lane == i_off + h, g_ref[rows[c], :], 0.0), axis=1, keepdims=True)
             for c in chunks]
    i_row = [irow_ref[c:c + 1, :] for c in chunks]
    f_start = [jnp.zeros((1, 1), F32)] + [f_col[c][chunk - 1:chunk, :] for c in chunks[:-1]]
    b_col = [f_col[c] - f_start[c] for c in chunks]
    b_row = [frow_ref[c:c + 1, :] - f_start[c] for c in chunks]
    b_last = [b_col[c][chunk - 1:chunk, :] for c in chunks]
    dec_max = [jnp.max(b_last[c] - b_row[c] + i_row[c], axis=1, keepdims=True) for c in chunks]
    m_in, m_out = [], []
    m_prev = m0_ref[:, 0:1]
    for c in chunks:
        m_in.append(m_prev)
        m_prev = jnp.maximum(b_last[c] + m_prev, dec_max[c])
        m_out.append(m_prev)

    d = [jnp.where(causal, b_col[c] - b_row[c] + i_row[c], NEG_INF) for c in chunks]
    d_max = [jnp.max(d[c], axis=1, keepdims=True) for c in chunks]
    inter = [b_col[c] + m_in[c] for c in chunks]
    m_t = [jnp.maximum(inter[c], d_max[c]) for c in chunks]
    w_inter = [jnp.exp(inter[c] - m_t[c]) for c in chunks]
    qk = [lax.dot_general(qc_ref[rows[c], :], kc_ref[rows[c], :], dn_t, preferred_element_type=F32)
          for c in chunks]
    s = [qk[c] * jnp.exp(d[c] - m_t[c]) for c in chunks]
    sv = [jnp.dot(s[c].astype(BF16), v_ref[rows[c], :], preferred_element_type=F32) for c in chunks]
    s_sum = [jnp.sum(s[c], axis=1, keepdims=True) for c in chunks]
    a = [jnp.exp(b_last[c] + m_in[c] - m_out[c]) for c in chunks]
    kw = [jnp.exp(b_last[c] - b_col[c] + i_col[c] - m_out[c]) * kc_ref[rows[c], :].astype(F32) for c in chunks]
    c_add = [lax.dot_general(v_ref[rows[c], :], kw[c].astype(BF16), (((0,), (0,)), ((), ())),
                             preferred_element_type=F32) for c in chunks]
    n_add = [jnp.sum(kw[c], axis=0, keepdims=True) for c in chunks]
    cmat, nvec = c0_ref[...], n0_ref[...]
    for c in chunks:
        q = qc_ref[rows[c], :]
        num = w_inter[c] * lax.dot_general(q, cmat.astype(BF16), dn_t, preferred_element_type=F32) + sv[c]
        den = w_inter[c] * jnp.sum(q.astype(F32) * nvec, axis=1, keepdims=True) + s_sum[c]
        hval = num / jnp.maximum(jnp.abs(den), jnp.exp(-m_t[c]))
        hn = hval * lax.rsqrt(jnp.mean(hval * hval, axis=1, keepdims=True) + EPS)
        out = jax.nn.sigmoid(og_ref[rows[c], :]) * hn * gh
        out_ref[rows[c], :] = out.astype(out_ref.dtype)
        cmat = a[c] * cmat + c_add[c]
        nvec = a[c] * nvec + n_add[c]
    c_ref[...] = cmat
    n_ref[...] = nvec
    m_ref[...] = jnp.broadcast_to(m_prev, m_ref.shape)


def mlstm_heads(p32, qk_col, og_col, p16, v_col, gates, row_blk, n_seq, t, f_cum, gates_t, f_cum_t,
                conv_hist, w_conv, c0, n0, m0, g_head, i_off, f_off, heads, out_col):
    b = n_seq
    _, n_h, dv, dk = c0.shape
    qb, ogb, vb = qk_col // dk, og_col // dv, v_col // dv
    conv_w = w_conv.shape[0]
    chunk = _tile(t, MLSTM_CHUNK)
    nch = t // chunk
    hpad = 8
    hist = jnp.pad(conv_hist, ((0, 0), (hpad - (conv_w - 1), 0), (0, 0)))
    gt = gates_t.reshape(b, LANES, nch, chunk)
    ft = f_cum_t.reshape(b, LANES, nch, chunk)
    n0r = n0.reshape(b, n_h, 1, dk)
    m0r = jnp.broadcast_to(m0[:, :, None, None], (b, n_h, 1, LANES))
    ghr = g_head.reshape(n_h, 1, dv)

    rowcol = lambda width, off: pl.BlockSpec((t, width), lambda i, h: (row_blk + i, off + h))
    hcol = lambda off: pl.BlockSpec((None, hpad, dk), lambda i, h: (i, 0, off + h))
    in_specs = [
        rowcol(dk, qb), rowcol(dk, qb + n_h), hcol(0), hcol(n_h),
        pl.BlockSpec((conv_w, dk), lambda i, h: (0, h)),
        pl.BlockSpec((conv_w, dk), lambda i, h: (0, n_h + h)),
        rowcol(dv, vb), rowcol(dv, ogb),
        pl.BlockSpec((t, LANES), lambda i, h: (row_blk + i, 0)),
        pl.BlockSpec((t, LANES), lambda i, h: (i, 0)),
        pl.BlockSpec((None, None, nch, chunk), lambda i, h: (i, i_off + h, 0, 0)),
        pl.BlockSpec((None, None, nch, chunk), lambda i, h: (i, f_off + h, 0, 0)),
        pl.BlockSpec((None, None, dv, dk), lambda i, h: (i, h, 0, 0)),
        pl.BlockSpec((None, None, 1, dk), lambda i, h: (i, h, 0, 0)),
        pl.BlockSpec((None, None, 1, LANES), lambda i, h: (i, h, 0, 0)),
        pl.BlockSpec((None, 1, dv), lambda i, h: (h, 0, 0)),
    ]
    in_specs.append(pl.BlockSpec(memory_space=pl.ANY))
    args = [p32, p32, hist, hist, w_conv, w_conv, p16, p32, gates, f_cum, gt, ft, c0, n0r, m0r, ghr, heads]
    out_col = out_col // dv
    out_specs = [
        pl.BlockSpec((t, dv), lambda i, h: (row_blk + i, out_col + h)),
        pl.BlockSpec((None, None, dv, dk), lambda i, h: (i, h, 0, 0)),
        pl.BlockSpec((None, None, 1, dk), lambda i, h: (i, h, 0, 0)),
        pl.BlockSpec((None, None, 1, LANES), lambda i, h: (i, h, 0, 0)),
    ]
    out_shape = [
        jax.ShapeDtypeStruct(heads.shape, heads.dtype),
        jax.ShapeDtypeStruct((b, n_h, dv, dk), F32),
        jax.ShapeDtypeStruct((b, n_h, 1, dk), F32),
        jax.ShapeDtypeStruct((b, n_h, 1, LANES), F32),
    ]
    out, c_new, n_new, m_new = pl.pallas_call(
        functools.partial(_mlstm_kernel, t=t, chunk=chunk, conv_w=conv_w, i_off=i_off, f_off=f_off,
                          k_scale=float(dk) ** -0.5),
        grid=(b, n_h),
        in_specs=in_specs,
        out_specs=out_specs,
        out_shape=out_shape,
        input_output_aliases={len(args) - 1: 0},
        scratch_shapes=[
            pltpu.VMEM((t + hpad, dk), F32), pltpu.VMEM((t + hpad, dk), F32),
            pltpu.VMEM((t, dk), BF16), pltpu.VMEM((t, dk), BF16),
        ],
        compiler_params=_params("parallel", "parallel"),
        name="mlstm_heads",
    )(*args)
    return out, c_new, n_new[:, :, 0, :], m_new[:, :, 0, 0]


def _outproj_kernel(a_ref, w_ref, xa_ref, xb_ref, o_ref, *, n_first):
    acc = jnp.dot(a_ref[...], w_ref[...], preferred_element_type=F32)

    @pl.when(pl.program_id(0) < n_first)
    def _():
        o_ref[...] = xa_ref[...] + acc

    @pl.when(pl.program_id(0) >= n_first)
    def _():
        o_ref[...] = xb_ref[...] + acc


def out_projection(heads, w_out, xa, xb, tm):
    (ma, d), mb = xa.shape, xb.shape[0]
    tn = _tile(d, 512)
    assert ma % tm == 0
    n_first = ma // tm
    first_j = lambda i, j: (jnp.minimum(i, n_first - 1), j)
    second_j = lambda i, j: (jnp.maximum(i - n_first, 0), j)
    return pl.pallas_call(
        functools.partial(_outproj_kernel, n_first=n_first),
        grid=(pl.cdiv(ma + mb, tm), d // tn),
        in_specs=[
            pl.BlockSpec((tm, d), lambda i, j: (i, 0)),
            pl.BlockSpec((d, tn), lambda i, j: (0, j)),
            pl.BlockSpec((tm, tn), first_j), pl.BlockSpec((tm, tn), second_j),
        ],
        out_specs=pl.BlockSpec((tm, tn), lambda i, j: (i, j)),
        out_shape=jax.ShapeDtypeStruct((ma + mb, d), F32),
        compiler_params=_params("arbitrary", "arbitrary"),
        name="out_projection",
    )(heads, w_out, xa, xb)


def _pack_bf16_pairs(x):
    half = x.shape[1] // 2
    bits = lax.bitcast_convert_type(x.astype(BF16).astype(F32), jnp.uint32)
    return (bits[:, :half] >> 16) | (bits[:, half:] & jnp.uint32(0xFFFF0000))


def _unpack_pairs_f32(w):
    lo = lax.bitcast_convert_type(w << 16, F32)
    hi = lax.bitcast_convert_type(w & jnp.uint32(0xFFFF0000), F32)
    return lo, hi


def _unpack_bf16_pairs(w):
    lo, hi = _unpack_pairs_f32(w)
    return lo.astype(BF16), hi.astype(BF16)


def _router_kernel(x_ref, g_ref, whi_ref, wlo_ref, b_ref, xn_ref, eid_ref, gate_ref, *, n_groups, per_group):
    x = x_ref[...]
    xn = x * lax.rsqrt(jnp.mean(x * x, axis=-1, keepdims=True) + EPS) * g_ref[...]
    xn_ref[...] = _pack_bf16_pairs(xn)
    x_hi = xn.astype(BF16)
    x_lo = (xn - x_hi.astype(F32)).astype(BF16)
    logits = (jnp.dot(x_hi, whi_ref[...], preferred_element_type=F32)
              + jnp.dot(x_lo, whi_ref[...], preferred_element_type=F32)
              + jnp.dot(x_hi, wlo_ref[...], preferred_element_type=F32)) + b_ref[...]
    lane = lax.broadcasted_iota(jnp.int32, logits.shape, 1).astype(F32)
    big = float(LANES)
    is_g = lane < n_groups
    gl = jnp.where(is_g, logits, NEG_INF)
    gmax = jnp.max(gl, axis=1, keepdims=True)
    g_idx = jnp.min(jnp.where(gl == gmax, lane, big), axis=1, keepdims=True)
    g_prob = 1.0 / jnp.sum(jnp.where(is_g, jnp.exp(logits - gmax), 0.0), axis=1, keepdims=True)
    lo = n_groups + per_group * g_idx
    el = jnp.where((lane >= lo) & (lane < lo + per_group), logits, NEG_INF)
    e1 = jnp.max(el, axis=1, keepdims=True)
    i1 = jnp.min(jnp.where(el == e1, lane, big), axis=1, keepdims=True)
    el2 = jnp.where(lane == i1, NEG_INF, el)
    e2 = jnp.max(el2, axis=1, keepdims=True)
    i2 = jnp.min(jnp.where(el2 == e2, lane, big), axis=1, keepdims=True)
    r = jnp.exp(e2 - e1)
    w1 = g_prob / (1.0 + r)
    w2 = g_prob * r / (1.0 + r)
    eid = jnp.where(lane == 0.0, i1 - n_groups, jnp.where(lane == 1.0, i2 - n_groups, 0.0))
    eid_ref[...] = eid.astype(jnp.int32)
    gate_ref[...] = jnp.where(lane == 0.0, w1, jnp.where(lane == 1.0, w2, 0.0))


def router(x, g, w_router, b_router, n_groups, per_group):
    m, d = x.shape
    tm = _tile(m, 256)
    w_hi = w_router.astype(BF16)
    w_lo = (w_router - w_hi.astype(F32)).astype(BF16)
    return pl.pallas_call(
        functools.partial(_router_kernel, n_groups=n_groups, per_group=per_group),
        grid=(m // tm,),
        in_specs=[
            pl.BlockSpec((tm, d), lambda i: (i, 0)),
            pl.BlockSpec((1, d), lambda i: (0, 0)),
            pl.BlockSpec((d, LANES), lambda i: (0, 0)),
            pl.BlockSpec((d, LANES), lambda i: (0, 0)),
            pl.BlockSpec((1, LANES), lambda i: (0, 0)),
        ],
        out_specs=[
            pl.BlockSpec((tm, d // 2), lambda i: (i, 0)),
            pl.BlockSpec((tm, LANES), lambda i: (i, 0)),
            pl.BlockSpec((tm, LANES), lambda i: (i, 0)),
        ],
        out_shape=[
            jax.ShapeDtypeStruct((m, d // 2), jnp.uint32),
            jax.ShapeDtypeStruct((m, LANES), jnp.int32),
            jax.ShapeDtypeStruct((m, LANES), F32),
        ],
        compiler_params=_params("parallel"),
        name="router",
    )(x, g.reshape(1, d), w_hi, w_lo, b_router)


def _gather_kernel(nrows_ref, idx0_ref, idx1_ref, src_ref, o_ref, buf_ref, sem_ref, *, tm):
    step = pl.program_id(0)
    nsteps = pl.num_programs(0)
    used = lambda s: s * tm < nrows_ref[0]

    def issue(idx_ref, slot):
        def body(r, carry):
            for q, row in enumerate((r, r + tm // 2)):
                tok = idx_ref[0, row]
                pltpu.make_async_copy(src_ref.at[pl.ds(tok, 1), :], buf_ref.at[slot, pl.ds(row, 1), :],
                                      sem_ref.at[slot]).start(priority=q)
            return carry
        lax.fori_loop(0, tm // 2, body, 0, unroll=DMA_ISSUE_UNROLL // 2)

    @pl.when(step == 0)
    def _():
        issue(idx0_ref, 0)

    @pl.when((step + 1 < nsteps) & used(step + 1))
    def _():
        issue(idx1_ref, (step + 1) % 2)

    @pl.when(used(step))
    def _():
        slot = step % 2
        pltpu.make_async_copy(src_ref.at[pl.ds(0, tm), :], buf_ref.at[slot], sem_ref.at[slot]).wait()
        half = buf_ref.shape[2]
        lo, hi = _unpack_bf16_pairs(buf_ref[slot])
        o_ref[:, :half] = lo
        o_ref[:, half:] = hi

    @pl.when(jnp.logical_not(used(step)))
    def _():
        o_ref[...] = jnp.zeros_like(o_ref)


def _smem_tiles(n_steps, width):
    cur = pl.BlockSpec((None, 1, width), lambda i: (i, 0, 0), memory_space=pltpu.SMEM)
    nxt = pl.BlockSpec((None, 1, width), lambda i: (jnp.minimum(i + 1, n_steps - 1), 0, 0),
                       memory_space=pltpu.SMEM)
    return cur, nxt


def gather_rows(src, idx, n_rows, tm):
    n, half = src.shape
    d = 2 * half
    a = idx.shape[0]
    n_steps = a // tm
    cur, nxt = _smem_tiles(n_steps, tm)
    idx3 = idx.reshape(n_steps, 1, tm)
    return pl.pallas_call(
        functools.partial(_gather_kernel, tm=tm),
        grid=(n_steps,),
        in_specs=[pl.BlockSpec(memory_space=pltpu.SMEM), cur, nxt, pl.BlockSpec(memory_space=pl.ANY)],
        out_specs=pl.BlockSpec((tm, d), lambda i: (i, 0)),
        scratch_shapes=[pltpu.VMEM((2, tm, half), src.dtype), pltpu.SemaphoreType.DMA((2,))],
        out_shape=jax.ShapeDtypeStruct((a, d), BF16),
        compiler_params=_params("arbitrary"),
        name="gather_rows",
    )(n_rows, idx3, idx3, src)


def _expert_up_kernel(te_ref, nv_ref, x_ref, wg_ref, wu_ref, h_ref):
    t = pl.program_id(1)

    @pl.when(t < nv_ref[0])
    def _():
        x = x_ref[...]
        g = jnp.dot(x, wg_ref[...].astype(BF16), preferred_element_type=F32)
        u = jnp.dot(x, wu_ref[...].astype(BF16), preferred_element_type=F32)
        h_ref[...] = (g * jax.nn.sigmoid(g) * u).astype(h_ref.dtype)

    @pl.when(t >= nv_ref[0])
    def _():
        h_ref[...] = jnp.zeros_like(h_ref)


def expert_up(xg, w_gate, w_up, tile_expert, n_valid, tm):
    a, d = xg.shape
    _, _, f = w_gate.shape
    tn = _tile(f, 512)
    return pl.pallas_call(
        _expert_up_kernel,
        grid_spec=pltpu.PrefetchScalarGridSpec(
            num_scalar_prefetch=2,
            grid=(f // tn, a // tm),
            in_specs=[
                pl.BlockSpec((tm, d), lambda c, t, te, nv: (t, 0)),
                pl.BlockSpec((None, d, tn), lambda c, t, te, nv: (te[t], 0, c)),
                pl.BlockSpec((None, d, tn), lambda c, t, te, nv: (te[t], 0, c)),
            ],
            out_specs=pl.BlockSpec((tm, tn), lambda c, t, te, nv: (t, c)),
        ),
        out_shape=jax.ShapeDtypeStruct((a, f), BF16),
        compiler_params=_params("arbitrary", "arbitrary"),
        name="expert_up",
    )(tile_expert, n_valid, xg, w_gate, w_up)


def _expert_down_kernel(te_ref, nv_ref, h_ref, wd_ref, y_ref):
    t = pl.program_id(1)

    @pl.when(t < nv_ref[0])
    def _():
        y = jnp.dot(h_ref[...], wd_ref[...].astype(BF16), preferred_element_type=F32)
        y_ref[...] = _pack_bf16_pairs(y)

    @pl.when(t >= nv_ref[0])
    def _():
        y_ref[...] = jnp.zeros_like(y_ref)


def expert_down(hg, w_down, tile_expert, n_valid, tm):
    a, f = hg.shape
    _, _, d = w_down.shape
    tn = _tile(d, DOWN_CHUNK)
    return pl.pallas_call(
        _expert_down_kernel,
        grid_spec=pltpu.PrefetchScalarGridSpec(
            num_scalar_prefetch=2,
            grid=(d // tn, a // tm),
            in_specs=[
                pl.BlockSpec((tm, f), lambda c, t, te, nv: (t, 0)),
                pl.BlockSpec((None, f, tn), lambda c, t, te, nv: (te[t], 0, c)),
            ],
            out_specs=pl.BlockSpec((tm, tn // 2), lambda c, t, te, nv: (t, c)),
        ),
        out_shape=jax.ShapeDtypeStruct((a, d // 2), jnp.uint32),
        compiler_params=_params("arbitrary", "arbitrary"),
        name="expert_down",
    )(tile_expert, n_valid, hg, w_down)


def _combine_kernel(pos0_ref, pos1_ref, x_ref, gate_ref, g_ref, y_ref, oa_ref, ob_ref, buf_ref, sem_ref,
                    xs_ref, *, tm, n_first, chunk):
    step = pl.program_id(0)
    nsteps = pl.num_programs(0)

    def issue(pos_ref, slot):
        def body(r, carry):
            for k in range(TOP_K):
                p = pos_ref[0, r * TOP_K + k]
                pltpu.make_async_copy(y_ref.at[pl.ds(p, 1), :], buf_ref.at[slot, k, pl.ds(r, 1), :],
                                      sem_ref.at[slot]).start(priority=k)
            return carry
        lax.fori_loop(0, tm, body, 0, unroll=DMA_ISSUE_UNROLL)

    @pl.when(step == 0)
    def _():
        issue(pos0_ref, 0)

    @pl.when(step + 1 < nsteps)
    def _():
        issue(pos1_ref, (step + 1) % 2)

    slot = step % 2
    for k in range(TOP_K):
        pltpu.make_async_copy(y_ref.at[pl.ds(0, tm), :], buf_ref.at[slot, k], sem_ref.at[slot]).wait()
    gates = gate_ref[...]
    d = x_ref.shape[1]
    half_chunk = chunk // 2
    cw = min(COMBINE_COLS, half_chunk)
    ssq = jnp.zeros((tm, 1), F32)
    for pc in range(0, d // 2, cw):
        real_lo = (pc // half_chunk) * chunk + pc % half_chunk
        real_hi = real_lo + half_chunk
        x_lo = x_ref[:, real_lo:real_lo + cw]
        x_hi = x_ref[:, real_hi:real_hi + cw]
        for k in range(TOP_K):
            lo, hi = _unpack_pairs_f32(buf_ref[slot, k, :, pc:pc + cw])
            x_lo = x_lo + gates[:, k:k + 1] * lo
            x_hi = x_hi + gates[:, k:k + 1] * hi
        xs_ref[:, real_lo:real_lo + cw] = x_lo
        xs_ref[:, real_hi:real_hi + cw] = x_hi
        ssq = ssq + jnp.sum(x_lo * x_lo, axis=1, keepdims=True) + jnp.sum(x_hi * x_hi, axis=1, keepdims=True)
    inv = lax.rsqrt(ssq * (1.0 / d) + EPS)

    def emit(o_ref):
        for c in range(0, d, 2 * cw):
            o_ref[:, c:c + 2 * cw] = xs_ref[:, c:c + 2 * cw] * inv * g_ref[:, c:c + 2 * cw]

    @pl.when(step < n_first)
    def _():
        emit(oa_ref)

    @pl.when(step >= n_first)
    def _():
        emit(ob_ref)


def combine_norm(x, gates, pos, yg, g_final, m_first, tm):
    m, d = x.shape
    n_steps = m // tm
    n_first = m_first // tm
    cur, nxt = _smem_tiles(n_steps, tm * TOP_K)
    pos3 = pos.reshape(n_steps, 1, tm * TOP_K)
    first, second = _split_maps(n_first)
    return pl.pallas_call(
        functools.partial(_combine_kernel, tm=tm, n_first=n_first, chunk=_tile(d, DOWN_CHUNK)),
        grid=(n_steps,),
        in_specs=[
            cur, nxt,
            pl.BlockSpec((tm, d), lambda i: (i, 0)),
            pl.BlockSpec((tm, LANES), lambda i: (i, 0)),
            pl.BlockSpec((1, d), lambda i: (0, 0)),
            pl.BlockSpec(memory_space=pl.ANY),
        ],
        out_specs=[pl.BlockSpec((tm, d), first), pl.BlockSpec((tm, d), second)],
        scratch_shapes=[pltpu.VMEM((2, TOP_K, tm, d // 2), jnp.uint32), pltpu.SemaphoreType.DMA((2,)),
                        pltpu.VMEM((tm, d), F32)],
        out_shape=[jax.ShapeDtypeStruct((m_first, d), F32), jax.ShapeDtypeStruct((m - m_first, d), F32)],
        compiler_params=_params("arbitrary"),
        name="combine_norm",
    )(pos3, pos3, x, gates, g_final.reshape(1, d), yg)


def _dispatch_plan(eid, n_experts, tm):
    n = eid.shape[0]
    a = n * TOP_K
    e_flat = eid.reshape(a)
    onehot = (e_flat[:, None] == jnp.arange(n_experts, dtype=jnp.int32)[None, :]).astype(jnp.int32)
    csum = jnp.cumsum(onehot, axis=0)
    counts = csum[-1]
    rank = jnp.take_along_axis(csum, e_flat[:, None], axis=1)[:, 0] - 1
    padded = ((counts + tm - 1) // tm) * tm
    pad_end = jnp.cumsum(padded)
    pos = (pad_end - padded)[e_flat] + rank
    n_tiles = a // tm + n_experts
    src = (jnp.arange(n_tiles * tm, dtype=jnp.int32) % n).at[pos].set(jnp.arange(a, dtype=jnp.int32) // TOP_K)
    tile_end = pad_end // tm
    tile_ids = jnp.arange(n_tiles, dtype=jnp.int32)
    tile_expert = jnp.sum((tile_ids[:, None] >= tile_end[None, :]).astype(jnp.int32), axis=1)
    tile_expert = jnp.minimum(tile_expert, n_experts - 1)
    n_valid = tile_end[-1:].astype(jnp.int32)
    return pos.astype(jnp.int32), src, tile_expert, n_valid


def moe_layer(x, m_first, g_ffn, w_router, b_router, w_gate, w_up, w_down, g_final, n_groups, per_group):
    m, d = x.shape
    n_experts = n_groups * per_group
    xn, eid, gates = router(x, g_ffn, w_router, b_router, n_groups, per_group)
    a = m * TOP_K
    tm = _tile(a, min(512, max(64, a // n_experts)))
    pos, src, tile_expert, n_valid = _dispatch_plan(eid[:, :TOP_K], n_experts, tm)
    xg = gather_rows(xn, src, n_valid * tm, _tile(tm, 256))
    hg = expert_up(xg, w_gate, w_up, tile_expert, n_valid, tm)
    yg = expert_down(hg, w_down, tile_expert, n_valid, tm)
    return combine_norm(x, gates, pos, yg, g_final, m_first, _tile(math.gcd(m_first, m - m_first), 128))


def kernel(x_prompt, x_sample, cache_fox_k, cache_fox_v, cache_fox_logf, state_mlstm_C, state_mlstm_n, state_mlstm_m, state_mlstm_conv, g_norm_mix, w_in, b_fox_f, b_mlstm_i, b_mlstm_f, w_conv, g_mlstm_head, w_out, g_norm_ffn, w_router_group, b_router_group, w_router_expert, b_router_expert, w_exp_gate, w_exp_up, w_exp_down, g_norm_final):
    depth = w_in.shape[0]
    assert depth == 1, "the final norm is fused into the last layer's MoE combine; one layer supported"
    bp, tp, d = x_prompt.shape
    bs, ts, _ = x_sample.shape
    n_fox, dh = cache_fox_k.shape[-2:]
    n_ml, dv, dk = state_mlstm_C.shape[-3:]
    past = cache_fox_k.shape[2]
    n_groups = w_router_group.shape[-1]
    n_experts = w_router_expert.shape[-1]
    conv_w = w_conv.shape[1]
    half = d // 2
    mp, ms = bp * tp, bs * ts
    assert n_fox * dh == half and n_ml * dv == half and 2 * n_ml * dk == half
    assert n_fox + 2 * n_ml <= LANES and n_groups + n_experts <= LANES
    assert mp % ts == 0 and ts >= conv_w - 1, "sample sequences are addressed as row blocks after the prompt rows"
    l = 0
    sizes = (half, half, half, n_fox, half, half, half, n_ml, n_ml)
    offs = [0]
    for s in sizes:
        offs.append(offs[-1] + s)
    col = lambda i: w_in[l][:, offs[i]:offs[i + 1]]
    w_a = w_in[l][:, offs[0]:offs[3]].astype(BF16)
    w_b = w_in[l][:, offs[4]:offs[7]].astype(BF16)
    n_gate = n_fox + 2 * n_ml
    w_gatecols = jnp.pad(jnp.concatenate([col(3), col(7), col(8)], axis=1), ((0, 0), (0, LANES - n_gate))).astype(BF16)
    b_gatecols = jnp.pad(jnp.concatenate([b_fox_f[l], b_mlstm_i[l], b_mlstm_f[l]]).astype(F32),
                         (0, LANES - n_gate)).reshape(1, LANES)
    n_r = n_groups + n_experts
    w_router = jnp.pad(jnp.concatenate([w_router_group[l], w_router_expert[l]], axis=1), ((0, 0), (0, LANES - n_r)))
    b_router = jnp.pad(jnp.concatenate([b_router_group[l], b_router_expert[l]]).astype(F32),
                       (0, LANES - n_r)).reshape(1, LANES)
    i_off, f_off = n_fox, n_fox + n_ml
    col32 = {s: k * half for k, s in enumerate(P32_SEGS)}
    col16 = {s: k * half for k, s in enumerate(P16_SEGS)}

    xp2, xs2 = x_prompt.reshape(mp, d), x_sample.reshape(ms, d)
    tm_rows = _tile(mp, 1024)
    xn = rmsnorm_cast(xp2, xs2, g_norm_mix[l], BF16)
    tm_proj = next((tmc for tmc in range(1280, 767, -16) if (mp + ms) % tmc == 0), tm_rows)
    p32, p16, gates = in_projection(xn, w_a, w_b, w_gatecols, b_gatecols, n_fox, n_ml, tm_proj,
                                    float(dh) ** -0.5 * LOG2E)

    f_p = cumsum_time(gates, bp, tp)
    f_p3 = f_p.reshape(bp, tp, LANES)
    g_p3 = gates[:mp].reshape(bp, tp, LANES)
    f_row = lambda f3: jnp.swapaxes(f3[:, :, :n_fox], 1, 2)[:, :, None, :]
    heads = fox_attention(p16, col16[SEG_FQ], p32, col32[SEG_FK], col32[SEG_FV], 0, bp, tp,
                          f_row(f_p3), n_fox, dh, xn)
    hist_p = jnp.zeros((bp, conv_w - 1, half), F32)
    heads, c_p, n_p, m_p = mlstm_heads(
        p32, col32[SEG_QK], col32[SEG_MO], p16, col16[SEG_MV], gates, 0, bp, tp, f_p,
        jnp.swapaxes(g_p3, 1, 2), jnp.swapaxes(f_p3, 1, 2), hist_p, w_conv[l],
        jnp.zeros((bp, n_ml, dv, dk), F32), jnp.zeros((bp, n_ml, dk), F32), jnp.zeros((bp, n_ml), F32),
        g_mlstm_head[l], i_off, f_off, heads, half)

    g_s3 = gates[mp:].reshape(bs, ts, LANES)
    cache_pad = jnp.pad(cache_fox_logf[l].astype(F32), ((0, 0), (0, 0), (0, LANES - n_fox)))
    logf_s = jnp.concatenate([cache_pad, g_s3], axis=1)
    f_s3 = cumsum_time(logf_s.reshape(bs * (past + ts), LANES), bs, past + ts).reshape(bs, past + ts, LANES)
    heads = fox_attention(p16, col16[SEG_FQ], p32, col32[SEG_FK], col32[SEG_FV], mp // ts, bs, ts,
                          f_row(f_s3), n_fox, dh, heads, cache_fox_k[l], cache_fox_v[l])
    fn_s = cumsum_time(g_s3.reshape(ms, LANES), bs, ts)
    heads, c_s, n_s, m_s = mlstm_heads(
        p32, col32[SEG_QK], col32[SEG_MO], p16, col16[SEG_MV], gates, mp // ts, bs, ts, fn_s,
        jnp.swapaxes(g_s3, 1, 2), jnp.swapaxes(fn_s.reshape(bs, ts, LANES), 1, 2), state_mlstm_conv[l], w_conv[l],
        state_mlstm_C[l], state_mlstm_n[l], state_mlstm_m[l], g_mlstm_head[l], i_off, f_off, heads, half)

    x1 = out_projection(heads, w_out[l].astype(BF16), xp2, xs2, tm_rows)
    y_p, y_s = moe_layer(x1, mp, g_norm_ffn[l], w_router, b_router, w_exp_gate[l], w_exp_up[l], w_exp_down[l],
                         g_norm_final, n_groups, n_experts // n_groups)

    def new_state(rows, b, t, g3, c_new, n_new, m_new, conv_hist):
        blk = p32[rows]
        fk = blk[:, col32[SEG_FK]:col32[SEG_FK] + half].reshape(b, t, n_fox, dh)
        fv = blk[:, col32[SEG_FV]:col32[SEG_FV] + half].reshape(b, t, n_fox, dh)
        qk_tail = blk[:, col32[SEG_QK]:col32[SEG_QK] + half].reshape(b, t, half)[:, t - min(t, conv_w - 1):]
        qk_hist = jnp.concatenate([conv_hist.astype(F32), qk_tail], axis=1)[:, -(conv_w - 1):]
        return tuple(a[None] for a in (fk, fv, g3[:, :, :n_fox], c_new, n_new, m_new, qk_hist))

    st_p = new_state(slice(0, mp), bp, tp, g_p3, c_p, n_p, m_p, hist_p)
    st_s = new_state(slice(mp, mp + ms), bs, ts, g_s3, c_s, n_s, m_s, state_mlstm_conv[l])
    return (y_p.reshape(bp, tp, d), y_s.reshape(bs, ts, d)) + st_p + st_s
```

```python
import functools
import math

import jax
import jax.numpy as jnp
from jax import lax
from jax.experimental import pallas as pl
from jax.experimental.pallas import tpu as pltpu

F32 = jnp.float32
BF16 = jnp.bfloat16
EPS = 1e-6
LANES = 128
V7X_VMEM_LIMIT = 56 * 1024 * 1024
TOP_K = 2
MLSTM_CHUNK = 256
DMA_ISSUE_UNROLL = 8
FOX_LOOKAHEAD = 4
COMBINE_COLS = 256
DOWN_CHUNK = 2048
HIGHEST = lax.Precision.HIGHEST
NEG_INF = float("-inf")
LOG2E = math.log2(math.e)


def _params(*sem):
    return pltpu.CompilerParams(dimension_semantics=sem, vmem_limit_bytes=V7X_VMEM_LIMIT)


def _tile(n, pref):
    if n <= pref:
        return n
    for t in range(pref - pref % 8, 7, -8):
        if n % t == 0:
            return t
    return n


def _log_sigmoid(x):
    return jnp.minimum(x, 0.0) - jnp.log1p(jnp.exp(-jnp.abs(x)))


def _split_maps(n_first):
    first = lambda i, *_: (jnp.minimum(i, n_first - 1), 0)
    second = lambda i, *_: (jnp.maximum(i - n_first, 0), 0)
    return first, second


def _rmsnorm_kernel(xa_ref, xb_ref, g_ref, o_ref, *, n_first):
    def emit(x_ref):
        x = x_ref[...]
        y = x * lax.rsqrt(jnp.mean(x * x, axis=-1, keepdims=True) + EPS)
        o_ref[...] = (y * g_ref[...]).astype(o_ref.dtype)

    @pl.when(pl.program_id(0) < n_first)
    def _():
        emit(xa_ref)

    @pl.when(pl.program_id(0) >= n_first)
    def _():
        emit(xb_ref)


def rmsnorm_cast(xa, xb, g, out_dtype):
    (ma, d), mb = xa.shape, xb.shape[0]
    tm = _tile(math.gcd(ma, mb), 256)
    first, second = _split_maps(ma // tm)
    return pl.pallas_call(
        functools.partial(_rmsnorm_kernel, n_first=ma // tm),
        grid=((ma + mb) // tm,),
        in_specs=[pl.BlockSpec((tm, d), first), pl.BlockSpec((tm, d), second),
                  pl.BlockSpec((1, d), lambda i: (0, 0))],
        out_specs=pl.BlockSpec((tm, d), lambda i: (i, 0)),
        out_shape=jax.ShapeDtypeStruct((ma + mb, d), out_dtype),
        compiler_params=_params("arbitrary"),
        name="rmsnorm_cast",
    )(xa, xb, g.reshape(1, d))


SEG_FQ, SEG_FK, SEG_FV, SEG_QK, SEG_MV, SEG_MO = range(6)
P32_SEGS = (SEG_FK, SEG_FV, SEG_QK, SEG_MO)
P16_SEGS = (SEG_FQ, SEG_MV)


def _inproj_kernel(x_ref, wa_ref, wb_ref, ws_ref, b_ref, p32_ref, p16_ref, gate_ref, *, nseg, n_fox, n_ml,
                   q_scale):
    j = pl.program_id(1)
    seg = j // nseg

    def emit(w_ref):
        acc = jnp.dot(x_ref[...], w_ref[...], preferred_element_type=F32)
        is16 = (seg == SEG_FQ) | (seg == SEG_MV)

        @pl.when(is16)
        def _():
            p16_ref[...] = (acc * jnp.where(seg == SEG_FQ, q_scale, 1.0)).astype(p16_ref.dtype)

        @pl.when(jnp.logical_not(is16))
        def _():
            p32_ref[...] = acc

    @pl.when(seg < 3)
    def _():
        emit(wa_ref)

    @pl.when(seg >= 3)
    def _():
        emit(wb_ref)

    @pl.when(j == 0)
    def _():
        p = jnp.dot(x_ref[...], ws_ref[...], preferred_element_type=F32) + b_ref[...]
        lane = lax.broadcasted_iota(jnp.int32, p.shape, 1)
        is_id = (lane >= n_fox) & (lane < n_fox + n_ml)
        gate_ref[...] = jnp.where(is_id, p, _log_sigmoid(p))


def _held_block(j, nseg, segs):
    seg = j // nseg
    blk = jnp.int32(0)
    for k, s in enumerate(segs):
        here = k * nseg + (j - s * nseg)
        done = (k + 1) * nseg - 1
        blk = jnp.where(seg == s, here, jnp.where(seg > s, done, blk))
    return blk


def in_projection(xn, w_a, w_b, w_gate, b_gate, n_fox, n_ml, tm, q_scale):
    m, d = xn.shape
    w = w_a.shape[1] // 3
    tn = _tile(w, 512)
    nseg = w // tn
    return pl.pallas_call(
        functools.partial(_inproj_kernel, nseg=nseg, n_fox=n_fox, n_ml=n_ml, q_scale=q_scale),
        grid=(pl.cdiv(m, tm), 6 * nseg),
        in_specs=[
            pl.BlockSpec((tm, d), lambda i, j: (i, 0)),
            pl.BlockSpec((d, tn), lambda i, j: (0, jnp.minimum(j, 3 * nseg - 1))),
            pl.BlockSpec((d, tn), lambda i, j: (0, jnp.maximum(j - 3 * nseg, 0))),
            pl.BlockSpec((d, LANES), lambda i, j: (0, 0)),
            pl.BlockSpec((1, LANES), lambda i, j: (0, 0)),
        ],
        out_specs=[
            pl.BlockSpec((tm, tn), lambda i, j: (i, _held_block(j, nseg, P32_SEGS))),
            pl.BlockSpec((tm, tn), lambda i, j: (i, _held_block(j, nseg, P16_SEGS))),
            pl.BlockSpec((tm, LANES), lambda i, j: (i, 0)),
        ],
        out_shape=[
            jax.ShapeDtypeStruct((m, len(P32_SEGS) * w), F32),
            jax.ShapeDtypeStruct((m, len(P16_SEGS) * w), BF16),
            jax.ShapeDtypeStruct((m, LANES), F32),
        ],
        compiler_params=_params("arbitrary", "arbitrary"),
        name="in_projection",
    )(xn, w_a, w_b, w_gate, b_gate)


def _cumsum_kernel(g_ref, f_ref, *, chunk):
    s = g_ref.shape[0]
    r = lax.broadcasted_iota(jnp.int32, (chunk, chunk), 0)
    c = lax.broadcasted_iota(jnp.int32, (chunk, chunk), 1)
    tri = (c <= r).astype(F32)
    carry = jnp.zeros((1, g_ref.shape[1]), F32)
    for k in range(s // chunk):
        blk = g_ref[k * chunk:(k + 1) * chunk, :]
        loc = jnp.dot(tri, blk, precision=HIGHEST, preferred_element_type=F32)
        f_ref[k * chunk:(k + 1) * chunk, :] = loc + carry
        carry = carry + loc[chunk - 1:chunk, :]


def cumsum_time(g, n_seq, s):
    n = g.shape[1]
    chunk = next(c for c in (256, 128, 64, 32, 16, 8) if s % c == 0)
    return pl.pallas_call(
        functools.partial(_cumsum_kernel, chunk=chunk),
        grid=(n_seq,),
        in_specs=[pl.BlockSpec((s, n), lambda i: (i, 0))],
        out_specs=pl.BlockSpec((s, n), lambda i: (i, 0)),
        out_shape=jax.ShapeDtypeStruct((n_seq * s, n), F32),
        compiler_params=_params("parallel"),
        name="cumsum_time",
    )(g)


def _fox_kernel(*refs, past, t, tq, dh):
    o_all = refs[-1]
    if past:
        q_all, k_all, v_all, pk_all, pv_all, frow_all = refs[:-2]
    else:
        q_all, k_all, v_all, frow_all = refs[:-2]
    for g in range(q_all.shape[1] // dh):
        cols = slice(g * dh, (g + 1) * dh)
        one_head = functools.partial(_fox_head, past=past, t=t, tq=tq, dh=dh)
        if past:
            one_head(q_all.at[:, cols], k_all.at[:, cols], v_all.at[:, cols], pk_all.at[:, g, :],
                     pv_all.at[:, g, :], frow_all.at[g], o_all.at[:, cols])
        else:
            one_head(q_all.at[:, cols], k_all.at[:, cols], v_all.at[:, cols], None, None,
                     frow_all.at[g], o_all.at[:, cols])


def _fox_head(q_ref, k_ref, v_ref, pk_ref, pv_ref, frow_ref, o_ref, *, past, t, tq, dh):
    fk_all = frow_ref[...] * LOG2E
    lane = lax.broadcasted_iota(jnp.int32, (1, dh), 1)
    ones_col = jnp.where(lane == 0, 1.0, 0.0).astype(BF16)

    den_from_matmul = not past

    def with_ones(v):
        if not den_from_matmul:
            return v.astype(BF16)
        return jnp.concatenate([v.astype(BF16), jnp.broadcast_to(ones_col, v.shape)], axis=1)

    kb = k_ref[...].astype(BF16)
    vb = with_ones(v_ref[...])
    dn_t = (((1,), (1,)), ((), ()))
    row = lax.broadcasted_iota(jnp.int32, (tq, tq), 0)
    col = lax.broadcasted_iota(jnp.int32, (tq, tq), 1)
    diag_mask = col <= row

    def scores(qi):
        q = q_ref[qi * tq:(qi + 1) * tq, :]
        lo = qi * tq
        parts = []
        if past:
            parts.append((pk_ref[...].astype(BF16), with_ones(pv_ref[...]), fk_all[:, 0:past], False))
        if qi:
            parts.append((kb[0:lo, :], vb[0:lo, :], fk_all[:, past:past + lo], False))
        parts.append((kb[lo:lo + tq, :], vb[lo:lo + tq, :], fk_all[:, past + lo:past + lo + tq], True))
        logits = []
        for kp, _, fk, masked in parts:
            lg = lax.dot_general(q, kp, dn_t, preferred_element_type=F32) - fk
            logits.append(jnp.where(diag_mask, lg, NEG_INF) if masked else lg)
        mx = functools.reduce(jnp.maximum, [jnp.max(lg, axis=1, keepdims=True) for lg in logits])
        return [vp for _, vp, _, _ in parts], logits, mx

    def finish(qi, values, logits, mx):
        acc = 0.0
        den = 0.0
        for vp, lg in zip(values, logits):
            p = jnp.exp2(lg - mx)
            if not den_from_matmul:
                den = den + jnp.sum(p, axis=1, keepdims=True)
            acc = acc + jnp.dot(p.astype(BF16), vp, preferred_element_type=F32)
        if den_from_matmul:
            acc, den = acc[:, :dh], acc[:, dh:dh + 1]
        o_ref[qi * tq:(qi + 1) * tq, :] = (acc / den).astype(o_ref.dtype)

    n_tiles = t // tq
    queue = [scores(qi) for qi in range(min(FOX_LOOKAHEAD, n_tiles))]
    for qi in range(n_tiles):
        if qi + FOX_LOOKAHEAD < n_tiles:
            queue.append(scores(qi + FOX_LOOKAHEAD))
        finish(qi, *queue.pop(0))


def fox_attention(p16, q_col, p32, k_col, v_col, row_blk, n_seq, t, f_row, n_heads, dh, heads,
                  past_k=None, past_v=None):
    past = 0 if past_k is None else past_k.shape[1]
    tq = _tile(t, 256)
    grp = n_heads if past else 1
    width = grp * dh

    def head_blk(off):
        return pl.BlockSpec((t, width), lambda i, h: (row_blk + i, off // width + h))

    in_specs = [head_blk(q_col), head_blk(k_col), head_blk(v_col)]
    args = [p16, p32, p32]
    if past:
        past_blk = pl.BlockSpec((None, past, n_heads, dh), lambda i, h: (i, 0, 0, 0))
        in_specs += [past_blk, past_blk]
        args += [past_k, past_v]
    in_specs += [
        pl.BlockSpec((None, grp, 1, past + t), lambda i, h: (i, h, 0, 0)),
        pl.BlockSpec(memory_space=pl.ANY),
    ]
    args += [f_row, heads]
    return pl.pallas_call(
        functools.partial(_fox_kernel, past=past, t=t, tq=tq, dh=dh),
        grid=(n_seq, n_heads // grp),
        in_specs=in_specs,
        out_specs=head_blk(0),
        out_shape=jax.ShapeDtypeStruct(heads.shape, heads.dtype),
        input_output_aliases={len(args) - 1: 0},
        compiler_params=_params("parallel", "parallel"),
        name="fox_attention",
    )(*args)


def _mlstm_kernel(*refs, t, chunk, conv_w, i_off, f_off, k_scale):
    refs = refs[:16] + refs[17:]
    (qraw_ref, kraw_ref, hq_ref, hk_ref, wq_ref, wk_ref, v_ref, og_ref, g_ref, f_ref,
     irow_ref, frow_ref, c0_ref, n0_ref, m0_ref, gh_ref,
     out_ref, c_ref, n_ref, m_ref,
     histq_ref, histk_ref, qc_ref, kc_ref) = refs
    h = pl.program_id(1)
    hpad = hq_ref.shape[0]

    def conv_silu(raw_ref, hist_in_ref, w_ref, hist_ref, scale, dst_ref):
        hist_ref[0:hpad, :] = hist_in_ref[...]
        hist_ref[hpad:hpad + t, :] = raw_ref[...]
        y = None
        for j in range(conv_w):
            start = hpad - (conv_w - 1) + j
            term = hist_ref[start:start + t, :] * w_ref[j:j + 1, :]
            y = term if y is None else y + term
        y = y * jax.nn.sigmoid(y)
        if scale != 1.0:
            y = y * scale
        dst_ref[...] = y.astype(dst_ref.dtype)

    conv_silu(qraw_ref, hq_ref, wq_ref, histq_ref, 1.0, qc_ref)
    conv_silu(kraw_ref, hk_ref, wk_ref, histk_ref, k_scale, kc_ref)

    gh = gh_ref[...]
    rr = lax.broadcasted_iota(jnp.int32, (chunk, chunk), 0)
    cc = lax.broadcasted_iota(jnp.int32, (chunk, chunk), 1)
    causal = cc <= rr
    lane = lax.broadcasted_iota(jnp.int32, (chunk, LANES), 1)

    chunks = range(t // chunk)
    rows = [slice(c * chunk, (c + 1) * chunk) for c in chunks]
    dn_t = (((1,), (1,)), ((), ()))

    f_col = [jnp.sum(jnp.where(lane == f_off + h, f_ref[rows[c], :], 0.0), axis=1, keepdims=True)
             for c in chunks]
    i_col = [jnp.sum(jnp.where(lane == i_off + h, g_ref[rows[c], :], 0.0), axis=1, keepdims=True)
             for c in chunks]
    i_row = [irow_ref[c:c + 1, :] for c in chunks]
    f_start = [jnp.zeros((1, 1), F32)] + [f_col[c][chunk - 1:chunk, :] for c in chunks[:-1]]
    b_col = [f_col[c] - f_start[c] for c in chunks]
    b_row = [frow_ref[c:c + 1, :] - f_start[c] for c in chunks]
    b_last = [b_col[c][chunk - 1:chunk, :] for c in chunks]
    dec_max = [jnp.max(b_last[c] - b_row[c] + i_row[c], axis=1, keepdims=True) for c in chunks]
    m_in, m_out = [], []
    m_prev = m0_ref[:, 0:1]
    for c in chunks:
        m_in.append(m_prev)
        m_prev = jnp.maximum(b_last[c] + m_prev, dec_max[c])
        m_out.append(m_prev)

    d = [jnp.where(causal, b_col[c] - b_row[c] + i_row[c], NEG_INF) for c in chunks]
    d_max = [jnp.max(d[c], axis=1, keepdims=True) for c in chunks]
    inter = [b_col[c] + m_in[c] for c in chunks]
    m_t = [jnp.maximum(inter[c], d_max[c]) for c in chunks]
    w_inter = [jnp.exp(inter[c] - m_t[c]) for c in chunks]
    qk = [lax.dot_general(qc_ref[rows[c], :], kc_ref[rows[c], :], dn_t, preferred_element_type=F32)
          for c in chunks]
    s = [qk[c] * jnp.exp(d[c] - m_t[c]) for c in chunks]
    sv = [jnp.dot(s[c].astype(BF16), v_ref[rows[c], :], preferred_element_type=F32) for c in chunks]
    s_sum = [jnp.sum(s[c], axis=1, keepdims=True) for c in chunks]
    a = [jnp.exp(b_last[c] + m_in[c] - m_out[c]) for c in chunks]
    kw = [jnp.exp(b_last[c] - b_col[c] + i_col[c] - m_out[c]) * kc_ref[rows[c], :].astype(F32) for c in chunks]
    c_add = [lax.dot_general(v_ref[rows[c], :], kw[c].astype(BF16), (((0,), (0,)), ((), ())),
                             preferred_element_type=F32) for c in chunks]
    n_add = [jnp.sum(kw[c], axis=0, keepdims=True) for c in chunks]
    cmat, nvec = c0_ref[...], n0_ref[...]
    for c in chunks:
        q = qc_ref[rows[c], :]
        num = w_inter[c] * lax.dot_general(q, cmat.astype(BF16), dn_t, preferred_element_type=F32) + sv[c]
        den = w_inter[c] * jnp.sum(q.astype(F32) * nvec, axis=1, keepdims=True) + s_sum[c]
        hval = num / jnp.maximum(jnp.abs(den), jnp.exp(-m_t[c]))
        hn = hval * lax.rsqrt(jnp.mean(hval * hval, axis=1, keepdims=True) + EPS)
        out = jax.nn.sigmoid(og_ref[rows[c], :]) * hn * gh
        out_ref[rows[c], :] = out.astype(out_ref.dtype)
        cmat = a[c] * cmat + c_add[c]
        nvec = a[c] * nvec + n_add[c]
    c_ref[...] = cmat
    n_ref[...] = nvec
    m_ref[...] = jnp.broadcast_to(m_prev, m_ref.shape)


def mlstm_heads(p32, qk_col, og_col, p16, v_col, gates, row_blk, n_seq, t, f_cum, gates_t, f_cum_t,
                conv_hist, w_conv, c0, n0, m0, g_head, i_off, f_off, heads, out_col):
    b = n_seq
    _, n_h, dv, dk = c0.shape
    qb, ogb, vb = qk_col // dk, og_col // dv, v_col // dv
    conv_w = w_conv.shape[0]
    chunk = _tile(t, MLSTM_CHUNK)
    nch = t // chunk
    hpad = 8
    hist = jnp.pad(conv_hist, ((0, 0), (hpad - (conv_w - 1), 0), (0, 0)))
    gt = gates_t.reshape(b, LANES, nch, chunk)
    ft = f_cum_t.reshape(b, LANES, nch, chunk)
    n0r = n0.reshape(b, n_h, 1, dk)
    m0r = jnp.broadcast_to(m0[:, :, None, None], (b, n_h, 1, LANES))
    ghr = g_head.reshape(n_h, 1, dv)

    rowcol = lambda width, off: pl.BlockSpec((t, width), lambda i, h: (row_blk + i, off + h))
    hcol = lambda off: pl.BlockSpec((None, hpad, dk), lambda i, h: (i, 0, off + h))
    in_specs = [
        rowcol(dk, qb), rowcol(dk, qb + n_h), hcol(0), hcol(n_h),
        pl.BlockSpec((conv_w, dk), lambda i, h: (0, h)),
        pl.BlockSpec((conv_w, dk), lambda i, h: (0, n_h + h)),
        rowcol(dv, vb), rowcol(dv, ogb),
        pl.BlockSpec((t, LANES), lambda i, h: (row_blk + i, 0)),
        pl.BlockSpec((t, LANES), lambda i, h: (i, 0)),
        pl.BlockSpec((None, None, nch, chunk), lambda i, h: (i, i_off + h, 0, 0)),
        pl.BlockSpec((None, None, nch, chunk), lambda i, h: (i, f_off + h, 0, 0)),
        pl.BlockSpec((None, None, dv, dk), lambda i, h: (i, h, 0, 0)),
        pl.BlockSpec((None, None, 1, dk), lambda i, h: (i, h, 0, 0)),
        pl.BlockSpec((None, None, 1, LANES), lambda i, h: (i, h, 0, 0)),
        pl.BlockSpec((None, 1, dv), lambda i, h: (h, 0, 0)),
    ]
    in_specs.append(pl.BlockSpec(memory_space=pl.ANY))
    args = [p32, p32, hist, hist, w_conv, w_conv, p16, p32, gates, f_cum, gt, ft, c0, n0r, m0r, ghr, heads]
    out_col = out_col // dv
    out_specs = [
        pl.BlockSpec((t, dv), lambda i, h: (row_blk + i, out_col + h)),
        pl.BlockSpec((None, None, dv, dk), lambda i, h: (i, h, 0, 0)),
        pl.BlockSpec((None, None, 1, dk), lambda i, h: (i, h, 0, 0)),
        pl.BlockSpec((None, None, 1, LANES), lambda i, h: (i, h, 0, 0)),
    ]
    out_shape = [
        jax.ShapeDtypeStruct(heads.shape, heads.dtype),
        jax.ShapeDtypeStruct((b, n_h, dv, dk), F32),
        jax.ShapeDtypeStruct((b, n_h, 1, dk), F32),
        jax.ShapeDtypeStruct((b, n_h, 1, LANES), F32),
    ]
    out, c_new, n_new, m_new = pl.pallas_call(
        functools.partial(_mlstm_kernel, t=t, chunk=chunk, conv_w=conv_w, i_off=i_off, f_off=f_off,
                          k_scale=float(dk) ** -0.5),
        grid=(b, n_h),
        in_specs=in_specs,
        out_specs=out_specs,
        out_shape=out_shape,
        input_output_aliases={len(args) - 1: 0},
        scratch_shapes=[
            pltpu.VMEM((t + hpad, dk), F32), pltpu.VMEM((t + hpad, dk), F32),
            pltpu.VMEM((t, dk), BF16), pltpu.VMEM((t, dk), BF16),
        ],
        compiler_params=_params("parallel", "parallel"),
        name="mlstm_heads",
    )(*args)
    return out, c_new, n_new[:, :, 0, :], m_new[:, :, 0, 0]


def _outproj_kernel(a_ref, w_ref, xa_ref, xb_ref, o_ref, *, n_first, split):
    acc = jnp.dot(a_ref[...], w_ref[...], preferred_element_type=F32)

    @pl.when(pl.program_id(0) < n_first)
    def _():
        o_ref[...] = xa_ref[...] + acc

    @pl.when(pl.program_id(0) >= n_first)
    def _():
        if split:
            o_ref[:split, :] = xa_ref[:split, :] + acc[:split, :]
        o_ref[split:, :] = xb_ref[...] + acc[split:, :]


def out_projection(heads, w_out, xa, xb, tm):
    (ma, d), mb = xa.shape, xb.shape[0]
    tn = _tile(d, 512)
    n_first, split = ma // tm, ma % tm
    assert (ma + mb) == (n_first + 1) * tm and split + mb == tm
    return pl.pallas_call(
        functools.partial(_outproj_kernel, n_first=n_first, split=split),
        grid=(n_first + 1, d // tn),
        in_specs=[
            pl.BlockSpec((tm, d), lambda i, j: (i, 0)),
            pl.BlockSpec((d, tn), lambda i, j: (0, j)),
            pl.BlockSpec((tm, tn), lambda i, j: (i, j)),
            pl.BlockSpec((mb, tn), lambda i, j: (0, j)),
        ],
        out_specs=pl.BlockSpec((tm, tn), lambda i, j: (i, j)),
        out_shape=jax.ShapeDtypeStruct((ma + mb, d), F32),
        compiler_params=_params("arbitrary", "arbitrary"),
        name="out_projection",
    )(heads, w_out, xa, xb)


def _pack_bf16_pairs(x):
    half = x.shape[1] // 2
    bits = lax.bitcast_convert_type(x.astype(BF16).astype(F32), jnp.uint32)
    return (bits[:, :half] >> 16) | (bits[:, half:] & jnp.uint32(0xFFFF0000))


def _unpack_pairs_f32(w):
    lo = lax.bitcast_convert_type(w << 16, F32)
    hi = lax.bitcast_convert_type(w & jnp.uint32(0xFFFF0000), F32)
    return lo, hi


def _unpack_bf16_pairs(w):
    lo, hi = _unpack_pairs_f32(w)
    return lo.astype(BF16), hi.astype(BF16)


def _router_kernel(x_ref, g_ref, whi_ref, wlo_ref, b_ref, xn_ref, eid_ref, gate_ref, *, n_groups, per_group):
    x = x_ref[...]
    xn = x * lax.rsqrt(jnp.mean(x * x, axis=-1, keepdims=True) + EPS) * g_ref[...]
    xn_ref[...] = _pack_bf16_pairs(xn)
    x_hi = xn.astype(BF16)
    x_lo = (xn - x_hi.astype(F32)).astype(BF16)
    logits = (jnp.dot(x_hi, whi_ref[...], preferred_element_type=F32)
              + jnp.dot(x_lo, whi_ref[...], preferred_element_type=F32)
              + jnp.dot(x_hi, wlo_ref[...], preferred_element_type=F32)) + b_ref[...]
    lane = lax.broadcasted_iota(jnp.int32, logits.shape, 1).astype(F32)
    big = float(LANES)
    is_g = lane < n_groups
    gl = jnp.where(is_g, logits, NEG_INF)
    gmax = jnp.max(gl, axis=1, keepdims=True)
    g_idx = jnp.min(jnp.where(gl == gmax, lane, big), axis=1, keepdims=True)
    g_prob = 1.0 / jnp.sum(jnp.where(is_g, jnp.exp(logits - gmax), 0.0), axis=1, keepdims=True)
    lo = n_groups + per_group * g_idx
    el = jnp.where((lane >= lo) & (lane < lo + per_group), logits, NEG_INF)
    e1 = jnp.max(el, axis=1, keepdims=True)
    i1 = jnp.min(jnp.where(el == e1, lane, big), axis=1, keepdims=True)
    el2 = jnp.where(lane == i1, NEG_INF, el)
    e2 = jnp.max(el2, axis=1, keepdims=True)
    i2 = jnp.min(jnp.where(el2 == e2, lane, big), axis=1, keepdims=True)
    r = jnp.exp(e2 - e1)
    w1 = g_prob / (1.0 + r)
    w2 = g_prob * r / (1.0 + r)
    eid = jnp.where(lane == 0.0, i1 - n_groups, jnp.where(lane == 1.0, i2 - n_groups, 0.0))
    eid_ref[...] = eid.astype(jnp.int32)
    gate_ref[...] = jnp.where(lane == 0.0, w1, jnp.where(lane == 1.0, w2, 0.0))


def router(x, g, w_router, b_router, n_groups, per_group):
    m, d = x.shape
    tm = _tile(m, 256)
    w_hi = w_router.astype(BF16)
    w_lo = (w_router - w_hi.astype(F32)).astype(BF16)
    return pl.pallas_call(
        functools.partial(_router_kernel, n_groups=n_groups, per_group=per_group),
        grid=(m // tm,),
        in_specs=[
            pl.BlockSpec((tm, d), lambda i: (i, 0)),
            pl.BlockSpec((1, d), lambda i: (0, 0)),
            pl.BlockSpec((d, LANES), lambda i: (0, 0)),
            pl.BlockSpec((d, LANES), lambda i: (0, 0)),
            pl.BlockSpec((1, LANES), lambda i: (0, 0)),
        ],
        out_specs=[
            pl.BlockSpec((tm, d // 2), lambda i: (i, 0)),
            pl.BlockSpec((tm, LANES), lambda i: (i, 0)),
            pl.BlockSpec((tm, LANES), lambda i: (i, 0)),
        ],
        out_shape=[
            jax.ShapeDtypeStruct((m, d // 2), jnp.uint32),
            jax.ShapeDtypeStruct((m, LANES), jnp.int32),
            jax.ShapeDtypeStruct((m, LANES), F32),
        ],
        compiler_params=_params("parallel"),
        name="router",
    )(x, g.reshape(1, d), w_hi, w_lo, b_router)


def _gather_kernel(nrows_ref, idx0_ref, idx1_ref, src_ref, o_ref, buf_ref, sem_ref, *, tm):
    step = pl.program_id(0)
    nsteps = pl.num_programs(0)
    used = lambda s: s * tm < nrows_ref[0]

    def issue(idx_ref, slot):
        def body(r, carry):
            for q, row in enumerate((r, r + tm // 2)):
                tok = idx_ref[0, row]
                pltpu.make_async_copy(src_ref.at[pl.ds(tok, 1), :], buf_ref.at[slot, pl.ds(row, 1), :],
                                      sem_ref.at[slot]).start(priority=q)
            return carry
        lax.fori_loop(0, tm // 2, body, 0, unroll=DMA_ISSUE_UNROLL // 2)

    @pl.when(step == 0)
    def _():
        issue(idx0_ref, 0)

    @pl.when((step + 1 < nsteps) & used(step + 1))
    def _():
        issue(idx1_ref, (step + 1) % 2)

    @pl.when(used(step))
    def _():
        slot = step % 2
        pltpu.make_async_copy(src_ref.at[pl.ds(0, tm), :], buf_ref.at[slot], sem_ref.at[slot]).wait()
        half = buf_ref.shape[2]
        lo, hi = _unpack_bf16_pairs(buf_ref[slot])
        o_ref[:, :half] = lo
        o_ref[:, half:] = hi

    @pl.when(jnp.logical_not(used(step)))
    def _():
        o_ref[...] = jnp.zeros_like(o_ref)


def _smem_tiles(n_steps, width):
    cur = pl.BlockSpec((None, 1, width), lambda i: (i, 0, 0), memory_space=pltpu.SMEM)
    nxt = pl.BlockSpec((None, 1, width), lambda i: (jnp.minimum(i + 1, n_steps - 1), 0, 0),
                       memory_space=pltpu.SMEM)
    return cur, nxt


def gather_rows(src, idx, n_rows, tm):
    n, half = src.shape
    d = 2 * half
    a = idx.shape[0]
    n_steps = a // tm
    cur, nxt = _smem_tiles(n_steps, tm)
    idx3 = idx.reshape(n_steps, 1, tm)
    return pl.pallas_call(
        functools.partial(_gather_kernel, tm=tm),
        grid=(n_steps,),
        in_specs=[pl.BlockSpec(memory_space=pltpu.SMEM), cur, nxt, pl.BlockSpec(memory_space=pl.ANY)],
        out_specs=pl.BlockSpec((tm, d), lambda i: (i, 0)),
        scratch_shapes=[pltpu.VMEM((2, tm, half), src.dtype), pltpu.SemaphoreType.DMA((2,))],
        out_shape=jax.ShapeDtypeStruct((a, d), BF16),
        compiler_params=_params("arbitrary"),
        name="gather_rows",
    )(n_rows, idx3, idx3, src)


def _expert_up_kernel(te_ref, nv_ref, x_ref, wg_ref, wu_ref, h_ref):
    t = pl.program_id(1)

    @pl.when(t < nv_ref[0])
    def _():
        x = x_ref[...]
        g = jnp.dot(x, wg_ref[...].astype(BF16), preferred_element_type=F32)
        u = jnp.dot(x, wu_ref[...].astype(BF16), preferred_element_type=F32)
        h_ref[...] = (g * jax.nn.sigmoid(g) * u).astype(h_ref.dtype)

    @pl.when(t >= nv_ref[0])
    def _():
        h_ref[...] = jnp.zeros_like(h_ref)


def expert_up(xg, w_gate, w_up, tile_expert, n_valid, tm):
    a, d = xg.shape
    _, _, f = w_gate.shape
    tn = _tile(f, 512)
    return pl.pallas_call(
        _expert_up_kernel,
        grid_spec=pltpu.PrefetchScalarGridSpec(
            num_scalar_prefetch=2,
            grid=(f // tn, a // tm),
            in_specs=[
                pl.BlockSpec((tm, d), lambda c, t, te, nv: (t, 0)),
                pl.BlockSpec((None, d, tn), lambda c, t, te, nv: (te[t], 0, c)),
                pl.BlockSpec((None, d, tn), lambda c, t, te, nv: (te[t], 0, c)),
            ],
            out_specs=pl.BlockSpec((tm, tn), lambda c, t, te, nv: (t, c)),
        ),
        out_shape=jax.ShapeDtypeStruct((a, f), BF16),
        compiler_params=_params("arbitrary", "arbitrary"),
        name="expert_up",
    )(tile_expert, n_valid, xg, w_gate, w_up)


def _expert_down_kernel(te_ref, nv_ref, h_ref, wd_ref, y_ref):
    t = pl.program_id(1)

    @pl.when(t < nv_ref[0])
    def _():
        y = jnp.dot(h_ref[...], wd_ref[...].astype(BF16), preferred_element_type=F32)
        y_ref[...] = _pack_bf16_pairs(y)

    @pl.when(t >= nv_ref[0])
    def _():
        y_ref[...] = jnp.zeros_like(y_ref)


def expert_down(hg, w_down, tile_expert, n_valid, tm):
    a, f = hg.shape
    _, _, d = w_down.shape
    tn = _tile(d, DOWN_CHUNK)
    return pl.pallas_call(
        _expert_down_kernel,
        grid_spec=pltpu.PrefetchScalarGridSpec(
            num_scalar_prefetch=2,
            grid=(d // tn, a // tm),
            in_specs=[
                pl.BlockSpec((tm, f), lambda c, t, te, nv: (t, 0)),
                pl.BlockSpec((None, f, tn), lambda c, t, te, nv: (te[t], 0, c)),
            ],
            out_specs=pl.BlockSpec((tm, tn // 2), lambda c, t, te, nv: (t, c)),
        ),
        out_shape=jax.ShapeDtypeStruct((a, d // 2), jnp.uint32),
        compiler_params=_params("arbitrary", "arbitrary"),
        name="expert_down",
    )(tile_expert, n_valid, hg, w_down)


def _combine_kernel(pos0_ref, pos1_ref, x_ref, gate_ref, g_ref, y_ref, oa_ref, ob_ref, buf_ref, sem_ref,
                    xs_ref, *, tm, n_first, chunk):
    step = pl.program_id(0)
    nsteps = pl.num_programs(0)

    def issue(pos_ref, slot):
        def body(r, carry):
            for k in range(TOP_K):
                p = pos_ref[0, r * TOP_K + k]
                pltpu.make_async_copy(y_ref.at[pl.ds(p, 1), :], buf_ref.at[slot, k, pl.ds(r, 1), :],
                                      sem_ref.at[slot]).start(priority=k)
            return carry
        lax.fori_loop(0, tm, body, 0, unroll=DMA_ISSUE_UNROLL)

    @pl.when(step == 0)
    def _():
        issue(pos0_ref, 0)

    @pl.when(step + 1 < nsteps)
    def _():
        issue(pos1_ref, (step + 1) % 2)

    slot = step % 2
    for k in range(TOP_K):
        pltpu.make_async_copy(y_ref.at[pl.ds(0, tm), :], buf_ref.at[slot, k], sem_ref.at[slot]).wait()
    gates = gate_ref[...]
    d = x_ref.shape[1]
    half_chunk = chunk // 2
    cw = min(COMBINE_COLS, half_chunk)
    ssq = jnp.zeros((tm, 1), F32)
    for pc in range(0, d // 2, cw):
        real_lo = (pc // half_chunk) * chunk + pc % half_chunk
        real_hi = real_lo + half_chunk
        x_lo = x_ref[:, real_lo:real_lo + cw]
        x_hi = x_ref[:, real_hi:real_hi + cw]
        for k in range(TOP_K):
            lo, hi = _unpack_pairs_f32(buf_ref[slot, k, :, pc:pc + cw])
            x_lo = x_lo + gates[:, k:k + 1] * lo
            x_hi = x_hi + gates[:, k:k + 1] * hi
        xs_ref[:, real_lo:real_lo + cw] = x_lo
        xs_ref[:, real_hi:real_hi + cw] = x_hi
        ssq = ssq + jnp.sum(x_lo * x_lo, axis=1, keepdims=True) + jnp.sum(x_hi * x_hi, axis=1, keepdims=True)
    inv = lax.rsqrt(ssq * (1.0 / d) + EPS)

    def emit(o_ref):
        for c in range(0, d, 2 * cw):
            o_ref[:, c:c + 2 * cw] = xs_ref[:, c:c + 2 * cw] * inv * g_ref[:, c:c + 2 * cw]

    @pl.when(step < n_first)
    def _():
        emit(oa_ref)

    @pl.when(step >= n_first)
    def _():
        emit(ob_ref)


def combine_norm(x, gates, pos, yg, g_final, m_first, tm):
    m, d = x.shape
    n_steps = m // tm
    n_first = m_first // tm
    cur, nxt = _smem_tiles(n_steps, tm * TOP_K)
    pos3 = pos.reshape(n_steps, 1, tm * TOP_K)
    first, second = _split_maps(n_first)
    return pl.pallas_call(
        functools.partial(_combine_kernel, tm=tm, n_first=n_first, chunk=_tile(d, DOWN_CHUNK)),
        grid=(n_steps,),
        in_specs=[
            cur, nxt,
            pl.BlockSpec((tm, d), lambda i: (i, 0)),
            pl.BlockSpec((tm, LANES), lambda i: (i, 0)),
            pl.BlockSpec((1, d), lambda i: (0, 0)),
            pl.BlockSpec(memory_space=pl.ANY),
        ],
        out_specs=[pl.BlockSpec((tm, d), first), pl.BlockSpec((tm, d), second)],
        scratch_shapes=[pltpu.VMEM((2, TOP_K, tm, d // 2), jnp.uint32), pltpu.SemaphoreType.DMA((2,)),
                        pltpu.VMEM((tm, d), F32)],
        out_shape=[jax.ShapeDtypeStruct((m_first, d), F32), jax.ShapeDtypeStruct((m - m_first, d), F32)],
        compiler_params=_params("arbitrary"),
        name="combine_norm",
    )(pos3, pos3, x, gates, g_final.reshape(1, d), yg)


def _dispatch_plan(eid, n_experts, tm):
    n = eid.shape[0]
    a = n * TOP_K
    e_flat = eid.reshape(a)
    onehot = (e_flat[:, None] == jnp.arange(n_experts, dtype=jnp.int32)[None, :]).astype(jnp.int32)
    csum = jnp.cumsum(onehot, axis=0)
    counts = csum[-1]
    rank = jnp.take_along_axis(csum, e_flat[:, None], axis=1)[:, 0] - 1
    padded = ((counts + tm - 1) // tm) * tm
    pad_end = jnp.cumsum(padded)
    pos = (pad_end - padded)[e_flat] + rank
    n_tiles = a // tm + n_experts
    src = (jnp.arange(n_tiles * tm, dtype=jnp.int32) % n).at[pos].set(jnp.arange(a, dtype=jnp.int32) // TOP_K)
    tile_end = pad_end // tm
    tile_ids = jnp.arange(n_tiles, dtype=jnp.int32)
    tile_expert = jnp.sum((tile_ids[:, None] >= tile_end[None, :]).astype(jnp.int32), axis=1)
    tile_expert = jnp.minimum(tile_expert, n_experts - 1)
    n_valid = tile_end[-1:].astype(jnp.int32)
    return pos.astype(jnp.int32), src, tile_expert, n_valid


def moe_layer(x, m_first, g_ffn, w_router, b_router, w_gate, w_up, w_down, g_final, n_groups, per_group):
    m, d = x.shape
    n_experts = n_groups * per_group
    xn, eid, gates = router(x, g_ffn, w_router, b_router, n_groups, per_group)
    a = m * TOP_K
    tm = _tile(a, min(512, max(64, a // n_experts)))
    pos, src, tile_expert, n_valid = _dispatch_plan(eid[:, :TOP_K], n_experts, tm)
    xg = gather_rows(xn, src, n_valid * tm, _tile(tm, 256))
    hg = expert_up(xg, w_gate, w_up, tile_expert, n_valid, tm)
    yg = expert_down(hg, w_down, tile_expert, n_valid, tm)
    return combine_norm(x, gates, pos, yg, g_final, m_first, _tile(math.gcd(m_first, m - m_first), 128))


def kernel(x_prompt, x_sample, cache_fox_k, cache_fox_v, cache_fox_logf, state_mlstm_C, state_mlstm_n, state_mlstm_m, state_mlstm_conv, g_norm_mix, w_in, b_fox_f, b_mlstm_i, b_mlstm_f, w_conv, g_mlstm_head, w_out, g_norm_ffn, w_router_group, b_router_group, w_router_expert, b_router_expert, w_exp_gate, w_exp_up, w_exp_down, g_norm_final):
    depth = w_in.shape[0]
    assert depth == 1, "the final norm is fused into the last layer's MoE combine; one layer supported"
    bp, tp, d = x_prompt.shape
    bs, ts, _ = x_sample.shape
    n_fox, dh = cache_fox_k.shape[-2:]
    n_ml, dv, dk = state_mlstm_C.shape[-3:]
    past = cache_fox_k.shape[2]
    n_groups = w_router_group.shape[-1]
    n_experts = w_router_expert.shape[-1]
    conv_w = w_conv.shape[1]
    half = d // 2
    mp, ms = bp * tp, bs * ts
    assert n_fox * dh == half and n_ml * dv == half and 2 * n_ml * dk == half
    assert n_fox + 2 * n_ml <= LANES and n_groups + n_experts <= LANES
    assert mp % ts == 0 and ts >= conv_w - 1, "sample sequences are addressed as row blocks after the prompt rows"
    l = 0
    sizes = (half, half, half, n_fox, half, half, half, n_ml, n_ml)
    offs = [0]
    for s in sizes:
        offs.append(offs[-1] + s)
    col = lambda i: w_in[l][:, offs[i]:offs[i + 1]]
    w_a = w_in[l][:, offs[0]:offs[3]].astype(BF16)
    w_b = w_in[l][:, offs[4]:offs[7]].astype(BF16)
    n_gate = n_fox + 2 * n_ml
    w_gatecols = jnp.pad(jnp.concatenate([col(3), col(7), col(8)], axis=1), ((0, 0), (0, LANES - n_gate))).astype(BF16)
    b_gatecols = jnp.pad(jnp.concatenate([b_fox_f[l], b_mlstm_i[l], b_mlstm_f[l]]).astype(F32),
                         (0, LANES - n_gate)).reshape(1, LANES)
    n_r = n_groups + n_experts
    w_router = jnp.pad(jnp.concatenate([w_router_group[l], w_router_expert[l]], axis=1), ((0, 0), (0, LANES - n_r)))
    b_router = jnp.pad(jnp.concatenate([b_router_group[l], b_router_expert[l]]).astype(F32),
                       (0, LANES - n_r)).reshape(1, LANES)
    i_off, f_off = n_fox, n_fox + n_ml
    col32 = {s: k * half for k, s in enumerate(P32_SEGS)}
    col16 = {s: k * half for k, s in enumerate(P16_SEGS)}

    xp2, xs2 = x_prompt.reshape(mp, d), x_sample.reshape(ms, d)
    tm_rows = _tile(mp, 1024)
    xn = rmsnorm_cast(xp2, xs2, g_norm_mix[l], BF16)
    tm_proj = next((tmc for tmc in range(1280, 767, -16) if (mp + ms) % tmc == 0), tm_rows)
    p32, p16, gates = in_projection(xn, w_a, w_b, w_gatecols, b_gatecols, n_fox, n_ml, tm_proj,
                                    float(dh) ** -0.5 * LOG2E)

    f_p = cumsum_time(gates, bp, tp)
    f_p3 = f_p.reshape(bp, tp, LANES)
    g_p3 = gates[:mp].reshape(bp, tp, LANES)
    f_row = lambda f3: jnp.swapaxes(f3[:, :, :n_fox], 1, 2)[:, :, None, :]
    heads = fox_attention(p16, col16[SEG_FQ], p32, col32[SEG_FK], col32[SEG_FV], 0, bp, tp,
                          f_row(f_p3), n_fox, dh, xn)
    hist_p = jnp.zeros((bp, conv_w - 1, half), F32)
    heads, c_p, n_p, m_p = mlstm_heads(
        p32, col32[SEG_QK], col32[SEG_MO], p16, col16[SEG_MV], gates, 0, bp, tp, f_p,
        jnp.swapaxes(g_p3, 1, 2), jnp.swapaxes(f_p3, 1, 2), hist_p, w_conv[l],
        jnp.zeros((bp, n_ml, dv, dk), F32), jnp.zeros((bp, n_ml, dk), F32), jnp.zeros((bp, n_ml), F32),
        g_mlstm_head[l], i_off, f_off, heads, half)

    g_s3 = gates[mp:].reshape(bs, ts, LANES)
    cache_pad = jnp.pad(cache_fox_logf[l].astype(F32), ((0, 0), (0, 0), (0, LANES - n_fox)))
    logf_s = jnp.concatenate([cache_pad, g_s3], axis=1)
    f_s3 = cumsum_time(logf_s.reshape(bs * (past + ts), LANES), bs, past + ts).reshape(bs, past + ts, LANES)
    heads = fox_attention(p16, col16[SEG_FQ], p32, col32[SEG_FK], col32[SEG_FV], mp // ts, bs, ts,
                          f_row(f_s3), n_fox, dh, heads, cache_fox_k[l], cache_fox_v[l])
    fn_s = cumsum_time(g_s3.reshape(ms, LANES), bs, ts)
    heads, c_s, n_s, m_s = mlstm_heads(
        p32, col32[SEG_QK], col32[SEG_MO], p16, col16[SEG_MV], gates, mp // ts, bs, ts, fn_s,
        jnp.swapaxes(g_s3, 1, 2), jnp.swapaxes(fn_s.reshape(bs, ts, LANES), 1, 2), state_mlstm_conv[l], w_conv[l],
        state_mlstm_C[l], state_mlstm_n[l], state_mlstm_m[l], g_mlstm_head[l], i_off, f_off, heads, half)

    x1 = out_projection(heads, w_out[l].astype(BF16), xp2, xs2, tm_proj)
    y_p, y_s = moe_layer(x1, mp, g_norm_ffn[l], w_router, b_router, w_exp_gate[l], w_exp_up[l], w_exp_down[l],
                         g_norm_final, n_groups, n_experts // n_groups)

    def new_state(rows, b, t, g3, c_new, n_new, m_new, conv_hist):
        blk = p32[rows]
        fk = blk[:, col32[SEG_FK]:col32[SEG_FK] + half].reshape(b, t, n_fox, dh)
        fv = blk[:, col32[SEG_FV]:col32[SEG_FV] + half].reshape(b, t, n_fox, dh)
        qk_tail = blk[:, col32[SEG_QK]:col32[SEG_QK] + half].reshape(b, t, half)[:, t - min(t, conv_w - 1):]
        qk_hist = jnp.concatenate([conv_hist.astype(F32), qk_tail], axis=1)[:, -(conv_w - 1):]
        return tuple(a[None] for a in (fk, fv, g3[:, :, :n_fox], c_new, n_new, m_new, qk_hist))

    st_p = new_state(slice(0, mp), bp, tp, g_p3, c_p, n_p, m_p, hist_p)
    st_s = new_state(slice(mp, mp + ms), bs, ts, g_s3, c_s, n_s, m_s, state_mlstm_conv[l])
    return (y_p.reshape(bp, tp, d), y_s.reshape(bs, ts, d)) + st_p + st_s
```

```python
import functools
import math

import jax
import jax.numpy as jnp
from jax import lax
from jax.experimental import pallas as pl
from jax.experimental.pallas import tpu as pltpu

F32 = jnp.float32
BF16 = jnp.bfloat16
EPS = 1e-6
LANES = 128
V7X_VMEM_LIMIT = 56 * 1024 * 1024
TOP_K = 2
MLSTM_CHUNK = 256
DMA_ISSUE_UNROLL = 8
PROJ_TILE_MAX, PROJ_TILE_MIN = 1280, 768
FOX_LOOKAHEAD = 8
COMBINE_COLS = 256
DOWN_CHUNK = 2048
HIGHEST = lax.Precision.HIGHEST
NEG_INF = float("-inf")
LOG2E = math.log2(math.e)


def _params(*sem):
    return pltpu.CompilerParams(dimension_semantics=sem, vmem_limit_bytes=V7X_VMEM_LIMIT)


def _tile(n, pref):
    if n <= pref:
        return n
    for t in range(pref - pref % 8, 7, -8):
        if n % t == 0:
            return t
    return n


def _log_sigmoid(x):
    return jnp.minimum(x, 0.0) - jnp.log1p(jnp.exp(-jnp.abs(x)))


def _split_maps(n_first):
    first = lambda i, *_: (jnp.minimum(i, n_first - 1), 0)
    second = lambda i, *_: (jnp.maximum(i - n_first, 0), 0)
    return first, second


def _rmsnorm_kernel(xa_ref, xb_ref, g_ref, o_ref, *, n_first):
    def emit(x_ref):
        x = x_ref[...]
        y = x * lax.rsqrt(jnp.mean(x * x, axis=-1, keepdims=True) + EPS)
        o_ref[...] = (y * g_ref[...]).astype(o_ref.dtype)

    @pl.when(pl.program_id(0) < n_first)
    def _():
        emit(xa_ref)

    @pl.when(pl.program_id(0) >= n_first)
    def _():
        emit(xb_ref)


def rmsnorm_cast(xa, xb, g, out_dtype):
    (ma, d), mb = xa.shape, xb.shape[0]
    tm = _tile(math.gcd(ma, mb), 256)
    first, second = _split_maps(ma // tm)
    return pl.pallas_call(
        functools.partial(_rmsnorm_kernel, n_first=ma // tm),
        grid=((ma + mb) // tm,),
        in_specs=[pl.BlockSpec((tm, d), first), pl.BlockSpec((tm, d), second),
                  pl.BlockSpec((1, d), lambda i: (0, 0))],
        out_specs=pl.BlockSpec((tm, d), lambda i: (i, 0)),
        out_shape=jax.ShapeDtypeStruct((ma + mb, d), out_dtype),
        compiler_params=_params("arbitrary"),
        name="rmsnorm_cast",
    )(xa, xb, g.reshape(1, d))


SEG_FQ, SEG_FK, SEG_FV, SEG_QK, SEG_MV, SEG_MO = range(6)
P32_SEGS = (SEG_FK, SEG_FV, SEG_QK, SEG_MO)
P16_SEGS = (SEG_FQ, SEG_MV)


def _inproj_kernel(x_ref, wa_ref, wb_ref, ws_ref, b_ref, p32_ref, p16_ref, gate_ref, *, nseg, n_fox, n_ml,
                   q_scale):
    j = pl.program_id(1)
    seg = j // nseg

    def emit(w_ref):
        acc = jnp.dot(x_ref[...], w_ref[...], preferred_element_type=F32)
        is16 = (seg == SEG_FQ) | (seg == SEG_MV)

        @pl.when(is16)
        def _():
            p16_ref[...] = (acc * jnp.where(seg == SEG_FQ, q_scale, 1.0)).astype(p16_ref.dtype)

        @pl.when(jnp.logical_not(is16))
        def _():
            p32_ref[...] = acc

    @pl.when(seg < 3)
    def _():
        emit(wa_ref)

    @pl.when(seg >= 3)
    def _():
        emit(wb_ref)

    @pl.when(j == 0)
    def _():
        p = jnp.dot(x_ref[...], ws_ref[...], preferred_element_type=F32) + b_ref[...]
        lane = lax.broadcasted_iota(jnp.int32, p.shape, 1)
        is_id = (lane >= n_fox) & (lane < n_fox + n_ml)
        gate_ref[...] = jnp.where(is_id, p, _log_sigmoid(p))


def _held_block(j, nseg, segs):
    seg = j // nseg
    blk = jnp.int32(0)
    for k, s in enumerate(segs):
        here = k * nseg + (j - s * nseg)
        done = (k + 1) * nseg - 1
        blk = jnp.where(seg == s, here, jnp.where(seg > s, done, blk))
    return blk


def in_projection(xn, w_a, w_b, w_gate, b_gate, n_fox, n_ml, tm, q_scale):
    m, d = xn.shape
    w = w_a.shape[1] // 3
    tn = _tile(w, 512)
    nseg = w // tn
    return pl.pallas_call(
        functools.partial(_inproj_kernel, nseg=nseg, n_fox=n_fox, n_ml=n_ml, q_scale=q_scale),
        grid=(pl.cdiv(m, tm), 6 * nseg),
        in_specs=[
            pl.BlockSpec((tm, d), lambda i, j: (i, 0)),
            pl.BlockSpec((d, tn), lambda i, j: (0, jnp.minimum(j, 3 * nseg - 1))),
            pl.BlockSpec((d, tn), lambda i, j: (0, jnp.maximum(j - 3 * nseg, 0))),
            pl.BlockSpec((d, LANES), lambda i, j: (0, 0)),
            pl.BlockSpec((1, LANES), lambda i, j: (0, 0)),
        ],
        out_specs=[
            pl.BlockSpec((tm, tn), lambda i, j: (i, _held_block(j, nseg, P32_SEGS))),
            pl.BlockSpec((tm, tn), lambda i, j: (i, _held_block(j, nseg, P16_SEGS))),
            pl.BlockSpec((tm, LANES), lambda i, j: (i, 0)),
        ],
        out_shape=[
            jax.ShapeDtypeStruct((m, len(P32_SEGS) * w), F32),
            jax.ShapeDtypeStruct((m, len(P16_SEGS) * w), BF16),
            jax.ShapeDtypeStruct((m, LANES), F32),
        ],
        compiler_params=_params("arbitrary", "arbitrary"),
        name="in_projection",
    )(xn, w_a, w_b, w_gate, b_gate)


def _cumsum_kernel(g_ref, f_ref, *, chunk):
    s = g_ref.shape[0]
    r = lax.broadcasted_iota(jnp.int32, (chunk, chunk), 0)
    c = lax.broadcasted_iota(jnp.int32, (chunk, chunk), 1)
    tri = (c <= r).astype(F32)
    carry = jnp.zeros((1, g_ref.shape[1]), F32)
    for k in range(s // chunk):
        blk = g_ref[k * chunk:(k + 1) * chunk, :]
        loc = jnp.dot(tri, blk, precision=HIGHEST, preferred_element_type=F32)
        f_ref[k * chunk:(k + 1) * chunk, :] = loc + carry
        carry = carry + loc[chunk - 1:chunk, :]


def cumsum_time(g, n_seq, s):
    n = g.shape[1]
    chunk = next(c for c in (256, 128, 64, 32, 16, 8) if s % c == 0)
    return pl.pallas_call(
        functools.partial(_cumsum_kernel, chunk=chunk),
        grid=(n_seq,),
        in_specs=[pl.BlockSpec((s, n), lambda i: (i, 0))],
        out_specs=pl.BlockSpec((s, n), lambda i: (i, 0)),
        out_shape=jax.ShapeDtypeStruct((n_seq * s, n), F32),
        compiler_params=_params("parallel"),
        name="cumsum_time",
    )(g)


def _fox_kernel(*refs, past, t, tq, dh):
    o_all = refs[-1]
    if past:
        q_all, k_all, v_all, pk_all, pv_all, frow_all = refs[:-2]
    else:
        q_all, k_all, v_all, frow_all = refs[:-2]
    for g in range(q_all.shape[1] // dh):
        cols = slice(g * dh, (g + 1) * dh)
        one_head = functools.partial(_fox_head, past=past, t=t, tq=tq, dh=dh)
        if past:
            one_head(q_all.at[:, cols], k_all.at[:, cols], v_all.at[:, cols], pk_all.at[:, g, :],
                     pv_all.at[:, g, :], frow_all.at[g], o_all.at[:, cols])
        else:
            one_head(q_all.at[:, cols], k_all.at[:, cols], v_all.at[:, cols], None, None,
                     frow_all.at[g], o_all.at[:, cols])


def _fox_head(q_ref, k_ref, v_ref, pk_ref, pv_ref, frow_ref, o_ref, *, past, t, tq, dh):
    fk_all = frow_ref[...] * LOG2E
    lane = lax.broadcasted_iota(jnp.int32, (1, dh), 1)
    ones_col = jnp.where(lane == 0, 1.0, 0.0).astype(BF16)

    den_from_matmul = not past

    def with_ones(v):
        if not den_from_matmul:
            return v.astype(BF16)
        return jnp.concatenate([v.astype(BF16), jnp.broadcast_to(ones_col, v.shape)], axis=1)

    kb = k_ref[...].astype(BF16)
    vb = with_ones(v_ref[...])
    dn_t = (((1,), (1,)), ((), ()))
    row = lax.broadcasted_iota(jnp.int32, (tq, tq), 0)
    col = lax.broadcasted_iota(jnp.int32, (tq, tq), 1)
    diag_mask = col <= row

    def scores(qi):
        q = q_ref[qi * tq:(qi + 1) * tq, :]
        lo = qi * tq
        parts = []
        if past:
            parts.append((pk_ref[...].astype(BF16), with_ones(pv_ref[...]), fk_all[:, 0:past], False))
        if qi:
            parts.append((kb[0:lo, :], vb[0:lo, :], fk_all[:, past:past + lo], False))
        parts.append((kb[lo:lo + tq, :], vb[lo:lo + tq, :], fk_all[:, past + lo:past + lo + tq], True))
        logits = []
        for kp, _, fk, masked in parts:
            lg = lax.dot_general(q, kp, dn_t, preferred_element_type=F32) - fk
            logits.append(jnp.where(diag_mask, lg, NEG_INF) if masked else lg)
        mx = functools.reduce(jnp.maximum, [jnp.max(lg, axis=1, keepdims=True) for lg in logits])
        return [vp for _, vp, _, _ in parts], logits, mx

    def finish(qi, values, logits, mx):
        acc = 0.0
        den = 0.0
        for vp, lg in zip(values, logits):
            p = jnp.exp2(lg - mx)
            if not den_from_matmul:
                den = den + jnp.sum(p, axis=1, keepdims=True)
            acc = acc + jnp.dot(p.astype(BF16), vp, preferred_element_type=F32)
        if den_from_matmul:
            acc, den = acc[:, :dh], acc[:, dh:dh + 1]
        o_ref[qi * tq:(qi + 1) * tq, :] = (acc / den).astype(o_ref.dtype)

    n_tiles = t // tq
    queue = [scores(qi) for qi in range(min(FOX_LOOKAHEAD, n_tiles))]
    for qi in range(n_tiles):
        if qi + FOX_LOOKAHEAD < n_tiles:
            queue.append(scores(qi + FOX_LOOKAHEAD))
        finish(qi, *queue.pop(0))


def fox_attention(p16, q_col, p32, k_col, v_col, row_blk, n_seq, t, f_row, n_heads, dh, heads,
                  past_k=None, past_v=None):
    past = 0 if past_k is None else past_k.shape[1]
    tq = _tile(t, 256)
    grp = n_heads if past else 1
    width = grp * dh

    def head_blk(off):
        return pl.BlockSpec((t, width), lambda i, h: (row_blk + i, off // width + h))

    in_specs = [head_blk(q_col), head_blk(k_col), head_blk(v_col)]
    args = [p16, p32, p32]
    if past:
        past_blk = pl.BlockSpec((None, past, n_heads, dh), lambda i, h: (i, 0, 0, 0))
        in_specs += [past_blk, past_blk]
        args += [past_k, past_v]
    in_specs += [
        pl.BlockSpec((None, grp, 1, past + t), lambda i, h: (i, h, 0, 0)),
        pl.BlockSpec(memory_space=pl.ANY),
    ]
    args += [f_row, heads]
    return pl.pallas_call(
        functools.partial(_fox_kernel, past=past, t=t, tq=tq, dh=dh),
        grid=(n_seq, n_heads // grp),
        in_specs=in_specs,
        out_specs=head_blk(0),
        out_shape=jax.ShapeDtypeStruct(heads.shape, heads.dtype),
        input_output_aliases={len(args) - 1: 0},
        compiler_params=_params("parallel", "parallel"),
        name="fox_attention",
    )(*args)


def _mlstm_kernel(*refs, t, chunk, conv_w, i_off, f_off, k_scale):
    refs = refs[:16] + refs[17:]
    (qraw_ref, kraw_ref, hq_ref, hk_ref, wq_ref, wk_ref, v_ref, og_ref, g_ref, f_ref,
     irow_ref, frow_ref, c0_ref, n0_ref, m0_ref, gh_ref,
     out_ref, c_ref, n_ref, m_ref,
     histq_ref, histk_ref, qc_ref, kc_ref) = refs
    h = pl.program_id(1)
    hpad = hq_ref.shape[0]

    def conv_silu(raw_ref, hist_in_ref, w_ref, hist_ref, scale, dst_ref):
        hist_ref[0:hpad, :] = hist_in_ref[...]
        hist_ref[hpad:hpad + t, :] = raw_ref[...]
        y = None
        for j in range(conv_w):
            start = hpad - (conv_w - 1) + j
            term = hist_ref[start:start + t, :] * w_ref[j:j + 1, :]
            y = term if y is None else y + term
        y = y * jax.nn.sigmoid(y)
        if scale != 1.0:
            y = y * scale
        dst_ref[...] = y.astype(dst_ref.dtype)

    conv_silu(qraw_ref, hq_ref, wq_ref, histq_ref, 1.0, qc_ref)
    conv_silu(kraw_ref, hk_ref, wk_ref, histk_ref, k_scale, kc_ref)

    gh = gh_ref[...]
    rr = lax.broadcasted_iota(jnp.int32, (chunk, chunk), 0)
    cc = lax.broadcasted_iota(jnp.int32, (chunk, chunk), 1)
    causal = cc <= rr
    lane = lax.broadcasted_iota(jnp.int32, (chunk, LANES), 1)

    chunks = range(t // chunk)
    rows = [slice(c * chunk, (c + 1) * chunk) for c in chunks]
    dn_t = (((1,), (1,)), ((), ()))

    f_col = [jnp.sum(jnp.where(lane == f_off + h, f_ref[rows[c], :], 0.0), axis=1, keepdims=True)
             for c in chunks]
    i_col = [jnp.sum(jnp.where(lane == i_off + h, g_ref[rows[c], :], 0.0), axis=1, keepdims=True)
             for c in chunks]
    i_row = [irow_ref[c:c + 1, :] for c in chunks]
    f_start = [jnp.zeros((1, 1), F32)] + [f_col[c][chunk - 1:chunk, :] for c in chunks[:-1]]
    b_col = [f_col[c] - f_start[c] for c in chunks]
    b_row = [frow_ref[c:c + 1, :] - f_start[c] for c in chunks]
    b_last = [b_col[c][chunk - 1:chunk, :] for c in chunks]
    dec_max = [jnp.max(b_last[c] - b_row[c] + i_row[c], axis=1, keepdims=True) for c in chunks]
    m_in, m_out = [], []
    m_prev = m0_ref[:, 0:1]
    for c in chunks:
        m_in.append(m_prev)
        m_prev = jnp.maximum(b_last[c] + m_prev, dec_max[c])
        m_out.append(m_prev)

    d = [jnp.where(causal, b_col[c] - b_row[c] + i_row[c], NEG_INF) for c in chunks]
    d_max = [jnp.max(d[c], axis=1, keepdims=True) for c in chunks]
    inter = [b_col[c] + m_in[c] for c in chunks]
    m_t = [jnp.maximum(inter[c], d_max[c]) for c in chunks]
    w_inter = [jnp.exp(inter[c] - m_t[c]) for c in chunks]
    qk = [lax.dot_general(qc_ref[rows[c], :], kc_ref[rows[c], :], dn_t, preferred_element_type=F32)
          for c in chunks]
    s = [qk[c] * jnp.exp(d[c] - m_t[c]) for c in chunks]
    sv = [jnp.dot(s[c].astype(BF16), v_ref[rows[c], :], preferred_element_type=F32) for c in chunks]
    s_sum = [jnp.sum(s[c], axis=1, keepdims=True) for c in chunks]
    a = [jnp.exp(b_last[c] + m_in[c] - m_out[c]) for c in chunks]
    kw = [jnp.exp(b_last[c] - b_col[c] + i_col[c] - m_out[c]) * kc_ref[rows[c], :].astype(F32) for c in chunks]
    c_add = [lax.dot_general(v_ref[rows[c], :], kw[c].astype(BF16), (((0,), (0,)), ((), ())),
                             preferred_element_type=F32) for c in chunks]
    n_add = [jnp.sum(kw[c], axis=0, keepdims=True) for c in chunks]
    cmat, nvec = c0_ref[...], n0_ref[...]
    for c in chunks:
        q = qc_ref[rows[c], :]
        num = w_inter[c] * lax.dot_general(q, cmat.astype(BF16), dn_t, preferred_element_type=F32) + sv[c]
        den = w_inter[c] * jnp.sum(q.astype(F32) * nvec, axis=1, keepdims=True) + s_sum[c]
        hval = num / jnp.maximum(jnp.abs(den), jnp.exp(-m_t[c]))
        hn = hval * lax.rsqrt(jnp.mean(hval * hval, axis=1, keepdims=True) + EPS)
        out = jax.nn.sigmoid(og_ref[rows[c], :]) * hn * gh
        out_ref[rows[c], :] = out.astype(out_ref.dtype)
        cmat = a[c] * cmat + c_add[c]
        nvec = a[c] * nvec + n_add[c]
    c_ref[...] = cmat
    n_ref[...] = nvec
    m_ref[...] = jnp.broadcast_to(m_prev, m_ref.shape)


def mlstm_heads(p32, qk_col, og_col, p16, v_col, gates, row_blk, n_seq, t, f_cum, gates_t, f_cum_t,
                conv_hist, w_conv, c0, n0, m0, g_head, i_off, f_off, heads, out_col):
    b = n_seq
    _, n_h, dv, dk = c0.shape
    qb, ogb, vb = qk_col // dk, og_col // dv, v_col // dv
    conv_w = w_conv.shape[0]
    chunk = _tile(t, MLSTM_CHUNK)
    nch = t // chunk
    hpad = 8
    hist = jnp.pad(conv_hist, ((0, 0), (hpad - (conv_w - 1), 0), (0, 0)))
    gt = gates_t.reshape(b, LANES, nch, chunk)
    ft = f_cum_t.reshape(b, LANES, nch, chunk)
    n0r = n0.reshape(b, n_h, 1, dk)
    m0r = jnp.broadcast_to(m0[:, :, None, None], (b, n_h, 1, LANES))
    ghr = g_head.reshape(n_h, 1, dv)

    rowcol = lambda width, off: pl.BlockSpec((t, width), lambda i, h: (row_blk + i, off + h))
    hcol = lambda off: pl.BlockSpec((None, hpad, dk), lambda i, h: (i, 0, off + h))
    in_specs = [
        rowcol(dk, qb), rowcol(dk, qb + n_h), hcol(0), hcol(n_h),
        pl.BlockSpec((conv_w, dk), lambda i, h: (0, h)),
        pl.BlockSpec((conv_w, dk), lambda i, h: (0, n_h + h)),
        rowcol(dv, vb), rowcol(dv, ogb),
        pl.BlockSpec((t, LANES), lambda i, h: (row_blk + i, 0)),
        pl.BlockSpec((t, LANES), lambda i, h: (i, 0)),
        pl.BlockSpec((None, None, nch, chunk), lambda i, h: (i, i_off + h, 0, 0)),
        pl.BlockSpec((None, None, nch, chunk), lambda i, h: (i, f_off + h, 0, 0)),
        pl.BlockSpec((None, None, dv, dk), lambda i, h: (i, h, 0, 0)),
        pl.BlockSpec((None, None, 1, dk), lambda i, h: (i, h, 0, 0)),
        pl.BlockSpec((None, None, 1, LANES), lambda i, h: (i, h, 0, 0)),
        pl.BlockSpec((None, 1, dv), lambda i, h: (h, 0, 0)),
    ]
    in_specs.append(pl.BlockSpec(memory_space=pl.ANY))
    args = [p32, p32, hist, hist, w_conv, w_conv, p16, p32, gates, f_cum, gt, ft, c0, n0r, m0r, ghr, heads]
    out_col = out_col // dv
    out_specs = [
        pl.BlockSpec((t, dv), lambda i, h: (row_blk + i, out_col + h)),
        pl.BlockSpec((None, None, dv, dk), lambda i, h: (i, h, 0, 0)),
        pl.BlockSpec((None, None, 1, dk), lambda i, h: (i, h, 0, 0)),
        pl.BlockSpec((None, None, 1, LANES), lambda i, h: (i, h, 0, 0)),
    ]
    out_shape = [
        jax.ShapeDtypeStruct(heads.shape, heads.dtype),
        jax.ShapeDtypeStruct((b, n_h, dv, dk), F32),
        jax.ShapeDtypeStruct((b, n_h, 1, dk), F32),
        jax.ShapeDtypeStruct((b, n_h, 1, LANES), F32),
    ]
    out, c_new, n_new, m_new = pl.pallas_call(
        functools.partial(_mlstm_kernel, t=t, chunk=chunk, conv_w=conv_w, i_off=i_off, f_off=f_off,
                          k_scale=float(dk) ** -0.5),
        grid=(b, n_h),
        in_specs=in_specs,
        out_specs=out_specs,
        out_shape=out_shape,
        input_output_aliases={len(args) - 1: 0},
        scratch_shapes=[
            pltpu.VMEM((t + hpad, dk), F32), pltpu.VMEM((t + hpad, dk), F32),
            pltpu.VMEM((t, dk), BF16), pltpu.VMEM((t, dk), BF16),
        ],
        compiler_params=_params("parallel", "parallel"),
        name="mlstm_heads",
    )(*args)
    return out, c_new, n_new[:, :, 0, :], m_new[:, :, 0, 0]


def _outproj_kernel(a_ref, w_ref, xa_ref, xb_ref, o_ref, *, n_first, split):
    acc = jnp.dot(a_ref[...], w_ref[...], preferred_element_type=F32)

    @pl.when(pl.program_id(0) < n_first)
    def _():
        o_ref[...] = xa_ref[...] + acc

    @pl.when(pl.program_id(0) >= n_first)
    def _():
        if split:
            o_ref[:split, :] = xa_ref[:split, :] + acc[:split, :]
        o_ref[split:, :] = xb_ref[...] + acc[split:, :]


def out_projection(heads, w_out, xa, xb, tm):
    (ma, d), mb = xa.shape, xb.shape[0]
    tn = _tile(d, 512)
    n_first, split = ma // tm, ma % tm
    assert (ma + mb) == (n_first + 1) * tm and split + mb == tm
    return pl.pallas_call(
        functools.partial(_outproj_kernel, n_first=n_first, split=split),
        grid=(n_first + 1, d // tn),
        in_specs=[
            pl.BlockSpec((tm, d), lambda i, j: (i, 0)),
            pl.BlockSpec((d, tn), lambda i, j: (0, j)),
            pl.BlockSpec((tm, tn), lambda i, j: (i, j)),
            pl.BlockSpec((mb, tn), lambda i, j: (0, j)),
        ],
        out_specs=pl.BlockSpec((tm, tn), lambda i, j: (i, j)),
        out_shape=jax.ShapeDtypeStruct((ma + mb, d), F32),
        compiler_params=_params("arbitrary", "arbitrary"),
        name="out_projection",
    )(heads, w_out, xa, xb)


def _pack_bf16_pairs(x):
    half = x.shape[1] // 2
    bits = lax.bitcast_convert_type(x.astype(BF16).astype(F32), jnp.uint32)
    return (bits[:, :half] >> 16) | (bits[:, half:] & jnp.uint32(0xFFFF0000))


def _unpack_pairs_f32(w):
    lo = lax.bitcast_convert_type(w << 16, F32)
    hi = lax.bitcast_convert_type(w & jnp.uint32(0xFFFF0000), F32)
    return lo, hi


def _unpack_bf16_pairs(w):
    lo, hi = _unpack_pairs_f32(w)
    return lo.astype(BF16), hi.astype(BF16)


def _router_kernel(x_ref, g_ref, whi_ref, wlo_ref, b_ref, xn_ref, eid_ref, gate_ref, *, n_groups, per_group):
    x = x_ref[...]
    xn = x * lax.rsqrt(jnp.mean(x * x, axis=-1, keepdims=True) + EPS) * g_ref[...]
    xn_ref[...] = _pack_bf16_pairs(xn)
    x_hi = xn.astype(BF16)
    x_lo = (xn - x_hi.astype(F32)).astype(BF16)
    logits = (jnp.dot(x_hi, whi_ref[...], preferred_element_type=F32)
              + jnp.dot(x_lo, whi_ref[...], preferred_element_type=F32)
              + jnp.dot(x_hi, wlo_ref[...], preferred_element_type=F32)) + b_ref[...]
    lane = lax.broadcasted_iota(jnp.int32, logits.shape, 1).astype(F32)
    big = float(LANES)
    is_g = lane < n_groups
    gl = jnp.where(is_g, logits, NEG_INF)
    gmax = jnp.max(gl, axis=1, keepdims=True)
    g_idx = jnp.min(jnp.where(gl == gmax, lane, big), axis=1, keepdims=True)
    g_prob = 1.0 / jnp.sum(jnp.where(is_g, jnp.exp(logits - gmax), 0.0), axis=1, keepdims=True)
    lo = n_groups + per_group * g_idx
    el = jnp.where((lane >= lo) & (lane < lo + per_group), logits, NEG_INF)
    e1 = jnp.max(el, axis=1, keepdims=True)
    i1 = jnp.min(jnp.where(el == e1, lane, big), axis=1, keepdims=True)
    el2 = jnp.where(lane == i1, NEG_INF, el)
    e2 = jnp.max(el2, axis=1, keepdims=True)
    i2 = jnp.min(jnp.where(el2 == e2, lane, big), axis=1, keepdims=True)
    r = jnp.exp(e2 - e1)
    w1 = g_prob / (1.0 + r)
    w2 = g_prob * r / (1.0 + r)
    eid = jnp.where(lane == 0.0, i1 - n_groups, jnp.where(lane == 1.0, i2 - n_groups, 0.0))
    eid_ref[...] = eid.astype(jnp.int32)
    gate_ref[...] = jnp.where(lane == 0.0, w1, jnp.where(lane == 1.0, w2, 0.0))


def router(x, g, w_router, b_router, n_groups, per_group):
    m, d = x.shape
    tm = _tile(m, 256)
    w_hi = w_router.astype(BF16)
    w_lo = (w_router - w_hi.astype(F32)).astype(BF16)
    return pl.pallas_call(
        functools.partial(_router_kernel, n_groups=n_groups, per_group=per_group),
        grid=(m // tm,),
        in_specs=[
            pl.BlockSpec((tm, d), lambda i: (i, 0)),
            pl.BlockSpec((1, d), lambda i: (0, 0)),
            pl.BlockSpec((d, LANES), lambda i: (0, 0)),
            pl.BlockSpec((d, LANES), lambda i: (0, 0)),
            pl.BlockSpec((1, LANES), lambda i: (0, 0)),
        ],
        out_specs=[
            pl.BlockSpec((tm, d // 2), lambda i: (i, 0)),
            pl.BlockSpec((tm, LANES), lambda i: (i, 0)),
            pl.BlockSpec((tm, LANES), lambda i: (i, 0)),
        ],
        out_shape=[
            jax.ShapeDtypeStruct((m, d // 2), jnp.uint32),
            jax.ShapeDtypeStruct((m, LANES), jnp.int32),
            jax.ShapeDtypeStruct((m, LANES), F32),
        ],
        compiler_params=_params("parallel"),
        name="router",
    )(x, g.reshape(1, d), w_hi, w_lo, b_router)


def _gather_kernel(nrows_ref, idx0_ref, idx1_ref, src_ref, o_ref, buf_ref, sem_ref, *, tm):
    step = pl.program_id(0)
    nsteps = pl.num_programs(0)
    used = lambda s: s * tm < nrows_ref[0]

    def issue(idx_ref, slot):
        def body(r, carry):
            for q, row in enumerate((r, r + tm // 2)):
                tok = idx_ref[0, row]
                pltpu.make_async_copy(src_ref.at[pl.ds(tok, 1), :], buf_ref.at[slot, pl.ds(row, 1), :],
                                      sem_ref.at[slot]).start(priority=q)
            return carry
        lax.fori_loop(0, tm // 2, body, 0, unroll=DMA_ISSUE_UNROLL // 2)

    @pl.when(step == 0)
    def _():
        issue(idx0_ref, 0)

    @pl.when((step + 1 < nsteps) & used(step + 1))
    def _():
        issue(idx1_ref, (step + 1) % 2)

    @pl.when(used(step))
    def _():
        slot = step % 2
        pltpu.make_async_copy(src_ref.at[pl.ds(0, tm), :], buf_ref.at[slot], sem_ref.at[slot]).wait()
        half = buf_ref.shape[2]
        lo, hi = _unpack_bf16_pairs(buf_ref[slot])
        o_ref[:, :half] = lo
        o_ref[:, half:] = hi

    @pl.when(jnp.logical_not(used(step)))
    def _():
        o_ref[...] = jnp.zeros_like(o_ref)


def _smem_tiles(n_steps, width):
    cur = pl.BlockSpec((None, 1, width), lambda i: (i, 0, 0), memory_space=pltpu.SMEM)
    nxt = pl.BlockSpec((None, 1, width), lambda i: (jnp.minimum(i + 1, n_steps - 1), 0, 0),
                       memory_space=pltpu.SMEM)
    return cur, nxt


def gather_rows(src, idx, n_rows, tm):
    n, half = src.shape
    d = 2 * half
    a = idx.shape[0]
    n_steps = a // tm
    cur, nxt = _smem_tiles(n_steps, tm)
    idx3 = idx.reshape(n_steps, 1, tm)
    return pl.pallas_call(
        functools.partial(_gather_kernel, tm=tm),
        grid=(n_steps,),
        in_specs=[pl.BlockSpec(memory_space=pltpu.SMEM), cur, nxt, pl.BlockSpec(memory_space=pl.ANY)],
        out_specs=pl.BlockSpec((tm, d), lambda i: (i, 0)),
        scratch_shapes=[pltpu.VMEM((2, tm, half), src.dtype), pltpu.SemaphoreType.DMA((2,))],
        out_shape=jax.ShapeDtypeStruct((a, d), BF16),
        compiler_params=_params("arbitrary"),
        name="gather_rows",
    )(n_rows, idx3, idx3, src)


def _expert_up_kernel(te_ref, nv_ref, x_ref, wg_ref, wu_ref, h_ref):
    t = pl.program_id(1)

    @pl.when(t < nv_ref[0])
    def _():
        x = x_ref[...]
        g = jnp.dot(x, wg_ref[...].astype(BF16), preferred_element_type=F32)
        u = jnp.dot(x, wu_ref[...].astype(BF16), preferred_element_type=F32)
        h_ref[...] = (g * jax.nn.sigmoid(g) * u).astype(h_ref.dtype)

    @pl.when(t >= nv_ref[0])
    def _():
        h_ref[...] = jnp.zeros_like(h_ref)


def expert_up(xg, w_gate, w_up, tile_expert, n_valid, tm):
    a, d = xg.shape
    _, _, f = w_gate.shape
    tn = _tile(f, 512)
    return pl.pallas_call(
        _expert_up_kernel,
        grid_spec=pltpu.PrefetchScalarGridSpec(
            num_scalar_prefetch=2,
            grid=(f // tn, a // tm),
            in_specs=[
                pl.BlockSpec((tm, d), lambda c, t, te, nv: (t, 0)),
                pl.BlockSpec((None, d, tn), lambda c, t, te, nv: (te[t], 0, c)),
                pl.BlockSpec((None, d, tn), lambda c, t, te, nv: (te[t], 0, c)),
            ],
            out_specs=pl.BlockSpec((tm, tn), lambda c, t, te, nv: (t, c)),
        ),
        out_shape=jax.ShapeDtypeStruct((a, f), BF16),
        compiler_params=_params("arbitrary", "arbitrary"),
        name="expert_up",
    )(tile_expert, n_valid, xg, w_gate, w_up)


def _expert_down_kernel(te_ref, nv_ref, h_ref, wd_ref, y_ref):
    t = pl.program_id(1)

    @pl.when(t < nv_ref[0])
    def _():
        y = jnp.dot(h_ref[...], wd_ref[...].astype(BF16), preferred_element_type=F32)
        y_ref[...] = _pack_bf16_pairs(y)

    @pl.when(t >= nv_ref[0])
    def _():
        y_ref[...] = jnp.zeros_like(y_ref)


def expert_down(hg, w_down, tile_expert, n_valid, tm):
    a, f = hg.shape
    _, _, d = w_down.shape
    tn = _tile(d, DOWN_CHUNK)
    return pl.pallas_call(
        _expert_down_kernel,
        grid_spec=pltpu.PrefetchScalarGridSpec(
            num_scalar_prefetch=2,
            grid=(d // tn, a // tm),
            in_specs=[
                pl.BlockSpec((tm, f), lambda c, t, te, nv: (t, 0)),
                pl.BlockSpec((None, f, tn), lambda c, t, te, nv: (te[t], 0, c)),
            ],
            out_specs=pl.BlockSpec((tm, tn // 2), lambda c, t, te, nv: (t, c)),
        ),
        out_shape=jax.ShapeDtypeStruct((a, d // 2), jnp.uint32),
        compiler_params=_params("arbitrary", "arbitrary"),
        name="expert_down",
    )(tile_expert, n_valid, hg, w_down)


def _combine_kernel(pos0_ref, pos1_ref, x_ref, gate_ref, g_ref, y_ref, oa_ref, ob_ref, buf_ref, sem_ref,
                    xs_ref, *, tm, n_first, chunk):
    step = pl.program_id(0)
    nsteps = pl.num_programs(0)

    def issue(pos_ref, slot):
        def body(r, carry):
            for k in range(TOP_K):
                p = pos_ref[0, r * TOP_K + k]
                pltpu.make_async_copy(y_ref.at[pl.ds(p, 1), :], buf_ref.at[slot, k, pl.ds(r, 1), :],
                                      sem_ref.at[slot]).start(priority=k)
            return carry
        lax.fori_loop(0, tm, body, 0, unroll=DMA_ISSUE_UNROLL)

    @pl.when(step == 0)
    def _():
        issue(pos0_ref, 0)

    @pl.when(step + 1 < nsteps)
    def _():
        issue(pos1_ref, (step + 1) % 2)

    slot = step % 2
    for k in range(TOP_K):
        pltpu.make_async_copy(y_ref.at[pl.ds(0, tm), :], buf_ref.at[slot, k], sem_ref.at[slot]).wait()
    gates = gate_ref[...]
    d = x_ref.shape[1]
    half_chunk = chunk // 2
    cw = min(COMBINE_COLS, half_chunk)
    ssq = jnp.zeros((tm, 1), F32)
    for pc in range(0, d // 2, cw):
        real_lo = (pc // half_chunk) * chunk + pc % half_chunk
        real_hi = real_lo + half_chunk
        x_lo = x_ref[:, real_lo:real_lo + cw]
        x_hi = x_ref[:, real_hi:real_hi + cw]
        for k in range(TOP_K):
            lo, hi = _unpack_pairs_f32(buf_ref[slot, k, :, pc:pc + cw])
            x_lo = x_lo + gates[:, k:k + 1] * lo
            x_hi = x_hi + gates[:, k:k + 1] * hi
        xs_ref[:, real_lo:real_lo + cw] = x_lo
        xs_ref[:, real_hi:real_hi + cw] = x_hi
        ssq = ssq + jnp.sum(x_lo * x_lo, axis=1, keepdims=True) + jnp.sum(x_hi * x_hi, axis=1, keepdims=True)
    inv = lax.rsqrt(ssq * (1.0 / d) + EPS)

    def emit(o_ref):
        for c in range(0, d, 2 * cw):
            o_ref[:, c:c + 2 * cw] = xs_ref[:, c:c + 2 * cw] * inv * g_ref[:, c:c + 2 * cw]

    @pl.when(step < n_first)
    def _():
        emit(oa_ref)

    @pl.when(step >= n_first)
    def _():
        emit(ob_ref)


def combine_norm(x, gates, pos, yg, g_final, m_first, tm):
    m, d = x.shape
    n_steps = m // tm
    n_first = m_first // tm
    cur, nxt = _smem_tiles(n_steps, tm * TOP_K)
    pos3 = pos.reshape(n_steps, 1, tm * TOP_K)
    first, second = _split_maps(n_first)
    return pl.pallas_call(
        functools.partial(_combine_kernel, tm=tm, n_first=n_first, chunk=_tile(d, DOWN_CHUNK)),
        grid=(n_steps,),
        in_specs=[
            cur, nxt,
            pl.BlockSpec((tm, d), lambda i: (i, 0)),
            pl.BlockSpec((tm, LANES), lambda i: (i, 0)),
            pl.BlockSpec((1, d), lambda i: (0, 0)),
            pl.BlockSpec(memory_space=pl.ANY),
        ],
        out_specs=[pl.BlockSpec((tm, d), first), pl.BlockSpec((tm, d), second)],
        scratch_shapes=[pltpu.VMEM((2, TOP_K, tm, d // 2), jnp.uint32), pltpu.SemaphoreType.DMA((2,)),
                        pltpu.VMEM((tm, d), F32)],
        out_shape=[jax.ShapeDtypeStruct((m_first, d), F32), jax.ShapeDtypeStruct((m - m_first, d), F32)],
        compiler_params=_params("arbitrary"),
        name="combine_norm",
    )(pos3, pos3, x, gates, g_final.reshape(1, d), yg)


def _dispatch_plan(eid, n_experts, tm):
    n = eid.shape[0]
    a = n * TOP_K
    e_flat = eid.reshape(a)
    onehot = (e_flat[:, None] == jnp.arange(n_experts, dtype=jnp.int32)[None, :]).astype(jnp.int32)
    csum = jnp.cumsum(onehot, axis=0)
    counts = csum[-1]
    rank = jnp.take_along_axis(csum, e_flat[:, None], axis=1)[:, 0] - 1
    padded = ((counts + tm - 1) // tm) * tm
    pad_end = jnp.cumsum(padded)
    pos = (pad_end - padded)[e_flat] + rank
    n_tiles = a // tm + n_experts
    src = (jnp.arange(n_tiles * tm, dtype=jnp.int32) % n).at[pos].set(jnp.arange(a, dtype=jnp.int32) // TOP_K)
    tile_end = pad_end // tm
    tile_ids = jnp.arange(n_tiles, dtype=jnp.int32)
    tile_expert = jnp.sum((tile_ids[:, None] >= tile_end[None, :]).astype(jnp.int32), axis=1)
    tile_expert = jnp.minimum(tile_expert, n_experts - 1)
    n_valid = tile_end[-1:].astype(jnp.int32)
    return pos.astype(jnp.int32), src, tile_expert, n_valid


def moe_layer(x, m_first, g_ffn, w_router, b_router, w_gate, w_up, w_down, g_final, n_groups, per_group):
    m, d = x.shape
    n_experts = n_groups * per_group
    xn, eid, gates = router(x, g_ffn, w_router, b_router, n_groups, per_group)
    a = m * TOP_K
    tm = _tile(a, min(512, max(64, a // n_experts)))
    pos, src, tile_expert, n_valid = _dispatch_plan(eid[:, :TOP_K], n_experts, tm)
    xg = gather_rows(xn, src, n_valid * tm, _tile(tm, 256))
    hg = expert_up(xg, w_gate, w_up, tile_expert, n_valid, tm)
    yg = expert_down(hg, w_down, tile_expert, n_valid, tm)
    return combine_norm(x, gates, pos, yg, g_final, m_first, _tile(math.gcd(m_first, m - m_first), 128))


def kernel(x_prompt, x_sample, cache_fox_k, cache_fox_v, cache_fox_logf, state_mlstm_C, state_mlstm_n, state_mlstm_m, state_mlstm_conv, g_norm_mix, w_in, b_fox_f, b_mlstm_i, b_mlstm_f, w_conv, g_mlstm_head, w_out, g_norm_ffn, w_router_group, b_router_group, w_router_expert, b_router_expert, w_exp_gate, w_exp_up, w_exp_down, g_norm_final):
    depth = w_in.shape[0]
    assert depth == 1, "the final norm is fused into the last layer's MoE combine; one layer supported"
    bp, tp, d = x_prompt.shape
    bs, ts, _ = x_sample.shape
    n_fox, dh = cache_fox_k.shape[-2:]
    n_ml, dv, dk = state_mlstm_C.shape[-3:]
    past = cache_fox_k.shape[2]
    n_groups = w_router_group.shape[-1]
    n_experts = w_router_expert.shape[-1]
    conv_w = w_conv.shape[1]
    half = d // 2
    mp, ms = bp * tp, bs * ts
    assert n_fox * dh == half and n_ml * dv == half and 2 * n_ml * dk == half
    assert n_fox + 2 * n_ml <= LANES and n_groups + n_experts <= LANES
    assert mp % ts == 0 and ts >= conv_w - 1, "sample sequences are addressed as row blocks after the prompt rows"
    l = 0
    sizes = (half, half, half, n_fox, half, half, half, n_ml, n_ml)
    offs = [0]
    for s in sizes:
        offs.append(offs[-1] + s)
    col = lambda i: w_in[l][:, offs[i]:offs[i + 1]]
    w_a = w_in[l][:, offs[0]:offs[3]].astype(BF16)
    w_b = w_in[l][:, offs[4]:offs[7]].astype(BF16)
    n_gate = n_fox + 2 * n_ml
    w_gatecols = jnp.pad(jnp.concatenate([col(3), col(7), col(8)], axis=1), ((0, 0), (0, LANES - n_gate))).astype(BF16)
    b_gatecols = jnp.pad(jnp.concatenate([b_fox_f[l], b_mlstm_i[l], b_mlstm_f[l]]).astype(F32),
                         (0, LANES - n_gate)).reshape(1, LANES)
    n_r = n_groups + n_experts
    w_router = jnp.pad(jnp.concatenate([w_router_group[l], w_router_expert[l]], axis=1), ((0, 0), (0, LANES - n_r)))
    b_router = jnp.pad(jnp.concatenate([b_router_group[l], b_router_expert[l]]).astype(F32),
                       (0, LANES - n_r)).reshape(1, LANES)
    i_off, f_off = n_fox, n_fox + n_ml
    col32 = {s: k * half for k, s in enumerate(P32_SEGS)}
    col16 = {s: k * half for k, s in enumerate(P16_SEGS)}

    xp2, xs2 = x_prompt.reshape(mp, d), x_sample.reshape(ms, d)
    xn = rmsnorm_cast(xp2, xs2, g_norm_mix[l], BF16)
    tm_proj = next((tmc for tmc in range(PROJ_TILE_MAX, PROJ_TILE_MIN - 1, -16) if (mp + ms) % tmc == 0),
                   _tile(mp, 1024))
    p32, p16, gates = in_projection(xn, w_a, w_b, w_gatecols, b_gatecols, n_fox, n_ml, tm_proj,
                                    float(dh) ** -0.5 * LOG2E)

    f_p = cumsum_time(gates, bp, tp)
    f_p3 = f_p.reshape(bp, tp, LANES)
    g_p3 = gates[:mp].reshape(bp, tp, LANES)
    f_row = lambda f3: jnp.swapaxes(f3[:, :, :n_fox], 1, 2)[:, :, None, :]
    heads = fox_attention(p16, col16[SEG_FQ], p32, col32[SEG_FK], col32[SEG_FV], 0, bp, tp,
                          f_row(f_p3), n_fox, dh, xn)
    hist_p = jnp.zeros((bp, conv_w - 1, half), F32)
    heads, c_p, n_p, m_p = mlstm_heads(
        p32, col32[SEG_QK], col32[SEG_MO], p16, col16[SEG_MV], gates, 0, bp, tp, f_p,
        jnp.swapaxes(g_p3, 1, 2), jnp.swapaxes(f_p3, 1, 2), hist_p, w_conv[l],
        jnp.zeros((bp, n_ml, dv, dk), F32), jnp.zeros((bp, n_ml, dk), F32), jnp.zeros((bp, n_ml), F32),
        g_mlstm_head[l], i_off, f_off, heads, half)

    g_s3 = gates[mp:].reshape(bs, ts, LANES)
    cache_pad = jnp.pad(cache_fox_logf[l].astype(F32), ((0, 0), (0, 0), (0, LANES - n_fox)))
    logf_s = jnp.concatenate([cache_pad, g_s3], axis=1)
    f_s3 = cumsum_time(logf_s.reshape(bs * (past + ts), LANES), bs, past + ts).reshape(bs, past + ts, LANES)
    heads = fox_attention(p16, col16[SEG_FQ], p32, col32[SEG_FK], col32[SEG_FV], mp // ts, bs, ts,
                          f_row(f_s3), n_fox, dh, heads, cache_fox_k[l], cache_fox_v[l])
    fn_s = cumsum_time(g_s3.reshape(ms, LANES), bs, ts)
    heads, c_s, n_s, m_s = mlstm_heads(
        p32, col32[SEG_QK], col32[SEG_MO], p16, col16[SEG_MV], gates, mp // ts, bs, ts, fn_s,
        jnp.swapaxes(g_s3, 1, 2), jnp.swapaxes(fn_s.reshape(bs, ts, LANES), 1, 2), state_mlstm_conv[l], w_conv[l],
        state_mlstm_C[l], state_mlstm_n[l], state_mlstm_m[l], g_mlstm_head[l], i_off, f_off, heads, half)

    x1 = out_projection(heads, w_out[l].astype(BF16), xp2, xs2, tm_proj)
    y_p, y_s = moe_layer(x1, mp, g_norm_ffn[l], w_router, b_router, w_exp_gate[l], w_exp_up[l], w_exp_down[l],
                         g_norm_final, n_groups, n_experts // n_groups)

    def new_state(rows, b, t, g3, c_new, n_new, m_new, conv_hist):
        blk = p32[rows]
        fk = blk[:, col32[SEG_FK]:col32[SEG_FK] + half].reshape(b, t, n_fox, dh)
        fv = blk[:, col32[SEG_FV]:col32[SEG_FV] + half].reshape(b, t, n_fox, dh)
        qk_tail = blk[:, col32[SEG_QK]:col32[SEG_QK] + half].reshape(b, t, half)[:, t - min(t, conv_w - 1):]
        qk_hist = jnp.concatenate([conv_hist.astype(F32), qk_tail], axis=1)[:, -(conv_w - 1):]
        return tuple(a[None] for a in (fk, fv, g3[:, :, :n_fox], c_new, n_new, m_new, qk_hist))

    st_p = new_state(slice(0, mp), bp, tp, g_p3, c_p, n_p, m_p, hist_p)
    st_s = new_state(slice(mp, mp + ms), bs, ts, g_s3, c_s, n_s, m_s, state_mlstm_conv[l])
    return (y_p.reshape(bp, tp, d), y_s.reshape(bs, ts, d)) + st_p + st_s
```
